```python
import math
import jax
import jax.numpy as jnp
from jax import lax
import numpy as np

D_MODEL = 1024
BATCH = 8
SEQ = 2048
DEPTH = 2
DEC_BATCH = 128
DEC_SEQ = 1
PAST_LEN = 16384
PAGE_SIZE = 128

N_EVEN = (DEPTH + 1) // 2
N_ODD = DEPTH // 2
SSM_D_INNER = D_MODEL
SSM_HEAD_DIM = 64
SSM_HEADS = SSM_D_INNER // SSM_HEAD_DIM
SSM_GROUPS = 2
SSM_STATE = 64
SSM_CONV = 4
SSM_CHUNK = 128
SSM_CONV_DIM = SSM_D_INNER + 2 * SSM_GROUPS * SSM_STATE
HG_HEADS = 8
HG_DIM = 128
HG_WIDTH = HG_HEADS * HG_DIM
HG_CHUNK = 16
AB_IN = SSM_D_INNER + SSM_CONV_DIM + SSM_HEADS + 4 * HG_WIDTH
AB_OUT = SSM_D_INNER + HG_WIDTH
S5_GROUP = 16
S5_GROUPS = D_MODEL // S5_GROUP
S5_STATE = 64
FFN_DIM = 2816
FFN_CONV = 3
EPS = 1e-6
F32 = jnp.float32

kernel_name = 'hybrid_ssd_hgrn2_s5_convffn_step'


def rmsnorm(x, w):
    xf = x.astype(F32)
    y = xf * lax.rsqrt(jnp.mean(xf * xf, axis=-1, keepdims=True) + EPS)
    return (y * w.astype(F32)).astype(x.dtype)


def causal_dwconv(x, buf, w, b):
    k_w = w.shape[0]
    n = x.shape[1]
    xp = jnp.concatenate([buf.astype(x.dtype), x], axis=1)
    y = b + xp[:, 0:n] * w[0]
    for k in range(1, k_w):
        y = y + xp[:, k:k + n] * w[k]
    return y, xp[:, xp.shape[1] - (k_w - 1):]


def to_chunks(a, c):
    b, n = a.shape[0], a.shape[1]
    return jnp.swapaxes(a.reshape((b, n // c, c) + a.shape[2:]), 0, 1)


def from_chunks(a):
    a = jnp.swapaxes(a, 0, 1)
    return a.reshape((a.shape[0], a.shape[1] * a.shape[2]) + a.shape[3:])


def ssd_scan(x, dt, a, bm, cm, h0):
    b, n, nh, p = x.shape
    g, ns = bm.shape[2], bm.shape[3]
    r = nh // g
    c = math.gcd(n, SSM_CHUNK)
    mask = jnp.tril(jnp.ones((c, c), dtype=bool))[None, :, :, None, None]
    a = a.reshape(g, r)

    def step(h, inp):
        xc, dtc, bc, cc = inp
        cum = jnp.cumsum(dtc * a, axis=1)
        seg = jnp.exp(jnp.where(mask, cum[:, :, None] - cum[:, None], -jnp.inf))
        cb = jnp.einsum('btgn,bsgn->bgts', cc, bc)
        wts = jnp.einsum('bgts,btsgr,bsgr->bgrts', cb, seg, dtc)
        y = jnp.einsum('bgrts,bsgrp->btgrp', wts, xc)
        y = y + jnp.einsum('btgn,bgrpn,btgr->btgrp', cc, h, jnp.exp(cum))
        last = cum[:, -1]
        to_end = jnp.exp(last[:, None] - cum) * dtc
        h = jnp.exp(last)[..., None, None] * h + jnp.einsum('bsgr,bsgn,bsgrp->bgrpn', to_end, bc, xc)
        return h, y

    xs = to_chunks(x.reshape(b, n, g, r, p), c)
    dts = to_chunks(dt.reshape(b, n, g, r), c)
    h, ys = lax.scan(step, h0.reshape(b, g, r, p, ns), (xs, dts, to_chunks(bm, c), to_chunks(cm, c)))
    return from_chunks(ys).reshape(b, n, nh, p), h.reshape(b, nh, p, ns)


def hgrn2_scan(q, logf, k, v, s0):
    n = q.shape[1]
    c = math.gcd(n, HG_CHUNK)
    mask = jnp.tril(jnp.ones((c, c), dtype=bool))[None, :, :, None, None]

    def step(s, inp):
        qc, gc, kc, vc = inp
        cum = jnp.cumsum(gc, axis=1)
        dec = jnp.exp(jnp.where(mask, cum[:, :, None] - cum[:, None], -jnp.inf))
        att = jnp.einsum('bthk,btshk,bshk->bhts', qc, dec, kc)
        o = jnp.einsum('bhts,bshv->bthv', att, vc) + jnp.einsum('bthk,bhkv->bthv', qc * jnp.exp(cum), s)
        last = cum[:, -1]
        s = jnp.exp(last)[..., None] * s + jnp.einsum('bshk,bshv->bhkv', jnp.exp(last[:, None] - cum) * kc, vc)
        return s, o

    s, o = lax.scan(step, s0, (to_chunks(q, c), to_chunks(logf, c), to_chunks(k, c), to_chunks(v, c)))
    return from_chunks(o), s


def complex_affine_combine(e1, e2):
    a1r, a1i, b1r, b1i = e1
    a2r, a2i, b2r, b2i = e2
    return (a2r * a1r - a2i * a1i,
            a2r * a1i + a2i * a1r,
            a2r * b1r - a2i * b1i + b2r,
            a2r * b1i + a2i * b1r + b2i)


def mixer_ab(h, conv_buf, ssm0, hg0, lb, in_w, conv_w, conv_b, dt_bias, a_log, d_skip, ssm_norm_w, hg_norm_w, out_w):
    b, n, _ = h.shape
    proj = h @ in_w
    cuts = np.cumsum([SSM_D_INNER, SSM_CONV_DIM, SSM_HEADS, HG_WIDTH, HG_WIDTH, HG_WIDTH]).tolist()
    z, xbc, dt_raw, q, f_raw, i_in, g = jnp.split(proj, cuts, axis=-1)
    xbc, conv_new = causal_dwconv(xbc, conv_buf, conv_w, conv_b)
    xbc = jax.nn.silu(xbc.astype(F32))
    xs, bm, cm = jnp.split(xbc, [SSM_D_INNER, SSM_D_INNER + SSM_GROUPS * SSM_STATE], axis=-1)
    xs = xs.reshape(b, n, SSM_HEADS, SSM_HEAD_DIM)
    bm = bm.reshape(b, n, SSM_GROUPS, SSM_STATE)
    cm = cm.reshape(b, n, SSM_GROUPS, SSM_STATE)
    dt = jax.nn.softplus(dt_raw.astype(F32) + dt_bias.astype(F32))
    a = -jnp.exp(a_log.astype(F32))
    y, ssm_new = ssd_scan(xs, dt, a, bm, cm, ssm0.astype(F32))
    y = y + d_skip.astype(F32)[:, None] * xs
    y = rmsnorm(y.reshape(b, n, SSM_D_INNER) * jax.nn.silu(z.astype(F32)), ssm_norm_w)
    f_raw = f_raw.astype(F32)
    logf = jnp.log(lb + (1.0 - lb) * jax.nn.sigmoid(f_raw))
    k = (1.0 - lb) * jax.nn.sigmoid(-f_raw)
    hs = (b, n, HG_HEADS, HG_DIM)
    qh = jax.nn.silu(q.astype(F32)).reshape(hs)
    o, hg_new = hgrn2_scan(qh, logf.reshape(hs), k.reshape(hs), i_in.astype(F32).reshape(hs), hg0.astype(F32))
    o = rmsnorm(o, hg_norm_w) * jax.nn.silu(g.astype(F32).reshape(hs))
    mixed = jnp.concatenate([y, o.reshape(b, n, HG_WIDTH)], axis=-1).astype(h.dtype)
    return mixed @ out_w, conv_new, ssm_new, hg_new


def mixer_c(h, s_re0, s_im0, lam_re, lam_im, log_step, b_re, b_im, c_re, c_im, d_skip, glu_w):
    b, n, _ = h.shape
    u = h.astype(F32).reshape(b, n, S5_GROUPS, S5_GROUP)
    lam_re = lam_re.astype(F32)
    lam_im = lam_im.astype(F32)
    step = jnp.exp(log_step.astype(F32))[:, None]
    mag = jnp.exp(lam_re * step)
    ab_re = mag * jnp.cos(lam_im * step)
    ab_im = mag * jnp.sin(lam_im * step)
    den = lam_re * lam_re + lam_im * lam_im
    nr = ab_re - 1.0
    coef_re = (nr * lam_re + ab_im * lam_im) / den
    coef_im = (ab_im * lam_re - nr * lam_im) / den
    b_re = b_re.astype(F32)
    b_im = b_im.astype(F32)
    bb_re = coef_re[..., None] * b_re - coef_im[..., None] * b_im
    bb_im = coef_re[..., None] * b_im + coef_im[..., None] * b_re
    bu_re = jnp.einsum('blgc,gnc->blgn', u, bb_re)
    bu_im = jnp.einsum('blgc,gnc->blgn', u, bb_im)
    s_re0 = s_re0.astype(F32)
    s_im0 = s_im0.astype(F32)
    bu_re = bu_re.at[:, 0].add(ab_re * s_re0 - ab_im * s_im0)
    bu_im = bu_im.at[:, 0].add(ab_re * s_im0 + ab_im * s_re0)
    a_re = jnp.broadcast_to(ab_re, bu_re.shape)
    a_im = jnp.broadcast_to(ab_im, bu_im.shape)
    _, _, x_re, x_im = lax.associative_scan(complex_affine_combine, (a_re, a_im, bu_re, bu_im), axis=1)
    y = (jnp.einsum('gcn,blgn->blgc', c_re.astype(F32), x_re)
         - jnp.einsum('gcn,blgn->blgc', c_im.astype(F32), x_im)
         + d_skip.astype(F32).reshape(S5_GROUPS, S5_GROUP) * u)
    zg = jax.nn.gelu(y.reshape(b, n, D_MODEL)).astype(h.dtype) @ glu_w
    val, gate = jnp.split(zg, 2, axis=-1)
    return val * jax.nn.sigmoid(gate), x_re[:, -1], x_im[:, -1]


def conv_ffn(h, buf, up_w, conv_w, conv_b, down_w):
    up = h @ up_w
    up, buf_new = causal_dwconv(up, buf, conv_w, conv_b)
    val, gate = jnp.split(up, 2, axis=-1)
    return (val * jax.nn.silu(gate)) @ down_w, buf_new


def trunk(x, ssm0, sconv0, hg0, s5r0, s5i0, fconv0, w):
    lbs = jnp.cumsum(jax.nn.softmax(w['hgrn_lb'].astype(F32), axis=0), axis=0)
    n_ssm, n_sconv, n_hg, n_s5r, n_s5i, n_fconv = [], [], [], [], [], []
    for l in range(DEPTH):
        e = l // 2
        hn = rmsnorm(x, w['norm_mix_pre'][l])
        if l % 2 == 0:
            m, sc, ss, sh = mixer_ab(hn, sconv0[e], ssm0[e], hg0[e], lbs[e],
                                     w['ab_in_w'][e], w['ssm_conv_w'][e], w['ssm_conv_b'][e],
                                     w['ssm_dt_bias'][e], w['ssm_a_log'][e], w['ssm_d'][e],
                                     w['ssm_norm_w'][e], w['hgrn_norm_w'][e], w['ab_out_w'][e])
            n_sconv.append(sc.astype(sconv0.dtype))
            n_ssm.append(ss.astype(ssm0.dtype))
            n_hg.append(sh.astype(hg0.dtype))
        else:
            m, sr, si = mixer_c(hn, s5r0[e], s5i0[e], w['s5_lam_re'][e], w['s5_lam_im'][e],
                                w['s5_log_step'][e], w['s5_b_re'][e], w['s5_b_im'][e],
                                w['s5_c_re'][e], w['s5_c_im'][e], w['s5_d'][e], w['s5_glu_w'][e])
            n_s5r.append(sr.astype(s5r0.dtype))
            n_s5i.append(si.astype(s5i0.dtype))
        x = x + rmsnorm(m, w['norm_mix_post'][l])
        hn = rmsnorm(x, w['norm_ffn_pre'][l])
        f, fc = conv_ffn(hn, fconv0[l], w['ffn_up_w'][l], w['ffn_conv_w'][l], w['ffn_conv_b'][l], w['ffn_down_w'][l])
        n_fconv.append(fc.astype(fconv0.dtype))
        x = x + rmsnorm(f, w['norm_ffn_post'][l])
    return x, (jnp.stack(n_ssm), jnp.stack(n_sconv), jnp.stack(n_hg),
               jnp.stack(n_s5r), jnp.stack(n_s5i), jnp.stack(n_fconv))


def setup_inputs(seed: int = 0) -> dict:
    key = jax.random.key(seed)
    ks = iter(jax.random.split(key, 64))

    def nrm(shape, scale):
        return jax.random.normal(next(ks), shape, F32) * scale

    def gain(shape):
        return 1.0 + nrm(shape, 0.05)

    def log_uniform(shape, lo, hi):
        return jax.random.uniform(next(ks), shape, F32, minval=math.log(lo), maxval=math.log(hi))

    x_prompt = nrm((BATCH, SEQ, D_MODEL), 1.0)
    x_sample = nrm((DEC_BATCH, DEC_SEQ, D_MODEL), 1.0)
    state_ssm = nrm((N_EVEN, DEC_BATCH, SSM_HEADS, SSM_HEAD_DIM, SSM_STATE), 0.5)
    state_ssm_conv = nrm((N_EVEN, DEC_BATCH, SSM_CONV - 1, SSM_CONV_DIM), 1.0)
    state_hgrn = nrm((N_EVEN, DEC_BATCH, HG_HEADS, HG_DIM, HG_DIM), 0.5)
    state_s5_re = nrm((N_ODD, DEC_BATCH, S5_GROUPS, S5_STATE), 0.3)
    state_s5_im = nrm((N_ODD, DEC_BATCH, S5_GROUPS, S5_STATE), 0.3)
    state_ffn_conv = nrm((DEPTH, DEC_BATCH, FFN_CONV - 1, 2 * FFN_DIM), 1.0)
    norm_mix_pre = gain((DEPTH, D_MODEL))
    norm_mix_post = gain((DEPTH, D_MODEL))
    norm_ffn_pre = gain((DEPTH, D_MODEL))
    norm_ffn_post = gain((DEPTH, D_MODEL))
    ab_in_w = nrm((N_EVEN, D_MODEL, AB_IN), D_MODEL ** -0.5)
    ssm_conv_w = nrm((N_EVEN, SSM_CONV, SSM_CONV_DIM), SSM_CONV ** -0.5)
    ssm_conv_b = nrm((N_EVEN, SSM_CONV_DIM), 0.02)
    dt0 = jnp.exp(log_uniform((N_EVEN, SSM_HEADS), 1e-3, 1e-1))
    ssm_dt_bias = dt0 + jnp.log(-jnp.expm1(-dt0))
    ssm_a_log = jnp.log(jax.random.uniform(next(ks), (N_EVEN, SSM_HEADS), F32, minval=1.0, maxval=16.0))
    ssm_d = gain((N_EVEN, SSM_HEADS))
    ssm_norm_w = gain((N_EVEN, SSM_D_INNER))
    hgrn_lb = nrm((N_EVEN + 1, HG_WIDTH), 1.0)
    hgrn_norm_w = gain((N_EVEN, HG_DIM))
    ab_out_w = nrm((N_EVEN, AB_OUT, D_MODEL), AB_OUT ** -0.5)
    s5_lam_re = -0.5 + nrm((N_ODD, S5_GROUPS, S5_STATE), 0.01)
    s5_lam_im = jnp.pi * jnp.arange(S5_STATE, dtype=F32) + nrm((N_ODD, S5_GROUPS, S5_STATE), 0.01)
    s5_log_step = log_uniform((N_ODD, S5_GROUPS), 1e-3, 1e-1)
    s5_b_re = nrm((N_ODD, S5_GROUPS, S5_STATE, S5_GROUP), (2 * S5_GROUP) ** -0.5)
    s5_b_im = nrm((N_ODD, S5_GROUPS, S5_STATE, S5_GROUP), (2 * S5_GROUP) ** -0.5)
    s5_c_re = nrm((N_ODD, S5_GROUPS, S5_GROUP, S5_STATE), (2 * S5_STATE) ** -0.5)
    s5_c_im = nrm((N_ODD, S5_GROUPS, S5_GROUP, S5_STATE), (2 * S5_STATE) ** -0.5)
    s5_d = nrm((N_ODD, D_MODEL), 1.0)
    s5_glu_w = nrm((N_ODD, D_MODEL, 2 * D_MODEL), D_MODEL ** -0.5)
    ffn_up_w = nrm((DEPTH, D_MODEL, 2 * FFN_DIM), D_MODEL ** -0.5)
    ffn_conv_w = nrm((DEPTH, FFN_CONV, 2 * FFN_DIM), FFN_CONV ** -0.5)
    ffn_conv_b = nrm((DEPTH, 2 * FFN_DIM), 0.02)
    ffn_down_w = nrm((DEPTH, FFN_DIM, D_MODEL), FFN_DIM ** -0.5)
    return {'x_prompt': x_prompt, 'x_sample': x_sample,
            'state_ssm': state_ssm, 'state_ssm_conv': state_ssm_conv, 'state_hgrn': state_hgrn,
            'state_s5_re': state_s5_re, 'state_s5_im': state_s5_im, 'state_ffn_conv': state_ffn_conv,
            'norm_mix_pre': norm_mix_pre, 'norm_mix_post': norm_mix_post,
            'norm_ffn_pre': norm_ffn_pre, 'norm_ffn_post': norm_ffn_post,
            'ab_in_w': ab_in_w, 'ssm_conv_w': ssm_conv_w, 'ssm_conv_b': ssm_conv_b,
            'ssm_dt_bias': ssm_dt_bias, 'ssm_a_log': ssm_a_log, 'ssm_d': ssm_d, 'ssm_norm_w': ssm_norm_w,
            'hgrn_lb': hgrn_lb, 'hgrn_norm_w': hgrn_norm_w, 'ab_out_w': ab_out_w,
            's5_lam_re': s5_lam_re, 's5_lam_im': s5_lam_im, 's5_log_step': s5_log_step,
            's5_b_re': s5_b_re, 's5_b_im': s5_b_im, 's5_c_re': s5_c_re, 's5_c_im': s5_c_im,
            's5_d': s5_d, 's5_glu_w': s5_glu_w,
            'ffn_up_w': ffn_up_w, 'ffn_conv_w': ffn_conv_w, 'ffn_conv_b': ffn_conv_b, 'ffn_down_w': ffn_down_w}


def reference(x_prompt, x_sample, state_ssm, state_ssm_conv, state_hgrn, state_s5_re, state_s5_im, state_ffn_conv,
              norm_mix_pre, norm_mix_post, norm_ffn_pre, norm_ffn_post,
              ab_in_w, ssm_conv_w, ssm_conv_b, ssm_dt_bias, ssm_a_log, ssm_d, ssm_norm_w,
              hgrn_lb, hgrn_norm_w, ab_out_w,
              s5_lam_re, s5_lam_im, s5_log_step, s5_b_re, s5_b_im, s5_c_re, s5_c_im, s5_d, s5_glu_w,
              ffn_up_w, ffn_conv_w, ffn_conv_b, ffn_down_w):
    w = dict(norm_mix_pre=norm_mix_pre, norm_mix_post=norm_mix_post,
             norm_ffn_pre=norm_ffn_pre, norm_ffn_post=norm_ffn_post,
             ab_in_w=ab_in_w, ssm_conv_w=ssm_conv_w, ssm_conv_b=ssm_conv_b, ssm_dt_bias=ssm_dt_bias,
             ssm_a_log=ssm_a_log, ssm_d=ssm_d, ssm_norm_w=ssm_norm_w,
             hgrn_lb=hgrn_lb, hgrn_norm_w=hgrn_norm_w, ab_out_w=ab_out_w,
             s5_lam_re=s5_lam_re, s5_lam_im=s5_lam_im, s5_log_step=s5_log_step,
             s5_b_re=s5_b_re, s5_b_im=s5_b_im, s5_c_re=s5_c_re, s5_c_im=s5_c_im, s5_d=s5_d, s5_glu_w=s5_glu_w,
             ffn_up_w=ffn_up_w, ffn_conv_w=ffn_conv_w, ffn_conv_b=ffn_conv_b, ffn_down_w=ffn_down_w)
    pd = x_prompt.dtype
    y_prompt, (p_ssm, p_sconv, p_hg, p_s5r, p_s5i, p_fconv) = trunk(
        x_prompt,
        jnp.zeros((N_EVEN, BATCH, SSM_HEADS, SSM_HEAD_DIM, SSM_STATE), pd),
        jnp.zeros((N_EVEN, BATCH, SSM_CONV - 1, SSM_CONV_DIM), pd),
        jnp.zeros((N_EVEN, BATCH, HG_HEADS, HG_DIM, HG_DIM), pd),
        jnp.zeros((N_ODD, BATCH, S5_GROUPS, S5_STATE), pd),
        jnp.zeros((N_ODD, BATCH, S5_GROUPS, S5_STATE), pd),
        jnp.zeros((DEPTH, BATCH, FFN_CONV - 1, 2 * FFN_DIM), pd),
        w)
    y_sample, (s_ssm, s_sconv, s_hg, s_s5r, s_s5i, s_fconv) = trunk(
        x_sample, state_ssm, state_ssm_conv, state_hgrn, state_s5_re, state_s5_im, state_ffn_conv, w)
    return (y_prompt, y_sample, p_ssm, p_sconv, p_hg, p_s5r, p_s5i, p_fconv,
            s_ssm, s_sconv, s_hg, s_s5r, s_s5i, s_fconv)
```

```python
import functools
import math

import jax
import jax.numpy as jnp
from jax import lax
from jax.experimental import pallas as pl
from jax.experimental.pallas import tpu as pltpu

F32 = jnp.float32
BF16 = jnp.bfloat16
EPS = 1e-6

LANES = 128
SUBLANES = 8
VMEM_LIMIT_BYTES = 56 * 1024 * 1024

SSM_HEAD_DIM = 64
SSM_GROUPS = 2
SSM_STATE = 64
SSM_CONV = 4
SSM_CHUNK = 128
HG_DIM = 128
HG_CHUNK = 128
S5_GROUP = 16
S5_STATE = 64
S5_GROUPS_PER_BLOCK = LANES // S5_GROUP
S5_BLOCK_STATE = S5_GROUPS_PER_BLOCK * S5_STATE
S5_SEGMENTS = SUBLANES
FFN_CONV = 3


def _params(sem):
    return pltpu.CompilerParams(dimension_semantics=sem, vmem_limit_bytes=VMEM_LIMIT_BYTES)


def _resident(shape):
    nd = len(shape)
    return pl.BlockSpec(shape, lambda *_: (0,) * nd, pipeline_mode=pl.Buffered(1))


def _rms(x, w):
    return x * lax.rsqrt(jnp.mean(x * x, axis=-1, keepdims=True) + EPS) * w


def _silu(x):
    return x * jax.nn.sigmoid(x)


def _dot(a, b):
    return jnp.dot(a, b, preferred_element_type=F32)


def _dot_nt(a, b):
    return lax.dot_general(a, b, (((1,), (1,)), ((), ())), preferred_element_type=F32)


def _dot_tn(a, b):
    return lax.dot_general(a, b, (((0,), (0,)), ((), ())), preferred_element_type=F32)


def _norm_matmul_kernel(n_out, x_ref, g_ref, *refs):
    w_refs, o_refs = refs[:n_out], refs[n_out:]
    xn = _rms(x_ref[...], g_ref[...]).astype(BF16)
    for w_ref, o_ref in zip(w_refs, o_refs):
        n = w_ref.shape[1]
        for c0 in range(0, n, 512):
            c1 = min(n, c0 + 512)
            o_ref[:, c0:c1] = _dot(xn, w_ref[:, c0:c1])


def norm_matmul(x, g, ws, tm):
    t, d = x.shape
    kern = functools.partial(_norm_matmul_kernel, len(ws))
    return pl.pallas_call(
        kern,
        grid=(t // tm,),
        in_specs=[pl.BlockSpec((tm, d), lambda i: (i, 0)), _resident((1, d))]
        + [_resident(w.shape) for w in ws],
        out_specs=[pl.BlockSpec((tm, w.shape[1]), lambda i: (i, 0)) for w in ws],
        out_shape=[jax.ShapeDtypeStruct((t, w.shape[1]), F32) for w in ws],
        compiler_params=_params(("parallel",)),
        name="norm_matmul",
    )(x, g, *ws)


def _rmsnorm_kernel(x_ref, g_ref, o_ref):
    o_ref[...] = _rms(x_ref[...], g_ref[...])


def rmsnorm(x, g, tm):
    t, d = x.shape
    return pl.pallas_call(
        _rmsnorm_kernel,
        grid=(t // tm,),
        in_specs=[pl.BlockSpec((tm, d), lambda i: (i, 0)), _resident((1, d))],
        out_specs=pl.BlockSpec((tm, d), lambda i: (i, 0)),
        out_shape=jax.ShapeDtypeStruct((t, d), F32),
        compiler_params=_params(("parallel",)),
        name="rmsnorm",
    )(x, g)


def _matmul_norm_residual_kernel(n_in, glu, *refs):
    a_refs, w_refs = refs[:n_in], refs[n_in:2 * n_in]
    g_ref, x_ref, o_ref = refs[2 * n_in:]
    acc = None
    for a_ref, w_ref in zip(a_refs, w_refs):
        part = _dot(a_ref[...].astype(BF16), w_ref[...])
        acc = part if acc is None else acc + part
    if glu:
        d = acc.shape[1] // 2
        acc = acc[:, :d] * jax.nn.sigmoid(acc[:, d:])
    o_ref[...] = x_ref[...] + _rms(acc, g_ref[...])


def matmul_norm_residual(a_list, w_list, g, x, tm, glu=False):
    t, d = x.shape
    kern = functools.partial(_matmul_norm_residual_kernel, len(a_list), glu)
    return pl.pallas_call(
        kern,
        grid=(t // tm,),
        in_specs=[pl.BlockSpec((tm, a.shape[1]), lambda i: (i, 0)) for a in a_list]
        + [_resident(w.shape) for w in w_list]
        + [_resident((1, d)), pl.BlockSpec((tm, d), lambda i: (i, 0))],
        out_specs=pl.BlockSpec((tm, d), lambda i: (i, 0)),
        out_shape=jax.ShapeDtypeStruct((t, d), F32),
        compiler_params=_params(("parallel",)),
        name="matmul_norm_residual",
    )(*a_list, *w_list, g, x)


FFN_CHUNK = 256


def _ffn_tail(up_ref, xm2_of, xm1_of, cw_ref, cb_ref, dw_ref, g_ref, x_ref, o_ref):
    f = dw_ref.shape[0]
    acc = jnp.zeros(o_ref.shape, F32)
    for c0 in range(0, f, FFN_CHUNK):
        c1 = c0 + FFN_CHUNK
        halves = []
        for off in (0, f):
            a, b = c0 + off, c1 + off
            halves.append(cb_ref[:, a:b] + xm2_of(a, b) * cw_ref[0:1, a:b]
                          + xm1_of(a, b) * cw_ref[1:2, a:b] + up_ref[:, a:b] * cw_ref[2:3, a:b])
        act = (halves[0] * _silu(halves[1])).astype(BF16)
        acc = acc + _dot(act, dw_ref[c0:c1, :])
    o_ref[...] = x_ref[...] + _rms(acc, g_ref[...])


def _ffn_seq_kernel(tiles_per_seq, up_ref, halo_ref, cw_ref, cb_ref, dw_ref, g_ref, x_ref, o_ref):
    tm = up_ref.shape[0]
    not_first = jnp.where(pl.program_id(0) % tiles_per_seq != 0, 1.0, 0.0)
    row = lax.broadcasted_iota(jnp.int32, (tm, FFN_CHUNK), 0)

    def shifted(k):
        def of(a, b):
            cur = pltpu.roll(up_ref[:, a:b], k, axis=0)
            for r in range(k):
                prev = halo_ref[SUBLANES - k + r:SUBLANES - k + r + 1, a:b] * not_first
                cur = jnp.where(row == r, prev, cur)
            return cur
        return of

    _ffn_tail(up_ref, shifted(2), shifted(1), cw_ref, cb_ref, dw_ref, g_ref, x_ref, o_ref)


def ffn_tail_seq(up, conv_w, conv_b, down_w, g, x, tm, seq):
    t, d = x.shape
    f2 = up.shape[1]
    hb = tm // SUBLANES
    kern = functools.partial(_ffn_seq_kernel, seq // tm)
    return pl.pallas_call(
        kern,
        grid=(t // tm,),
        in_specs=[pl.BlockSpec((tm, f2), lambda i: (i, 0)),
                  pl.BlockSpec((SUBLANES, f2), lambda i: (jnp.maximum(i * hb - 1, 0), 0)),
                  _resident(conv_w.shape), _resident((1, f2)), _resident(down_w.shape),
                  _resident((1, d)), pl.BlockSpec((tm, d), lambda i: (i, 0))],
        out_specs=pl.BlockSpec((tm, d), lambda i: (i, 0)),
        out_shape=jax.ShapeDtypeStruct((t, d), F32),
        compiler_params=_params(("parallel",)),
        name="ffn_tail_seq",
    )(up, up, conv_w, conv_b, down_w, g, x)


def _ffn_step_kernel(up_ref, xm2_ref, xm1_ref, cw_ref, cb_ref, dw_ref, g_ref, x_ref, o_ref):
    _ffn_tail(up_ref, lambda a, b: xm2_ref[:, a:b], lambda a, b: xm1_ref[:, a:b],
              cw_ref, cb_ref, dw_ref, g_ref, x_ref, o_ref)


def ffn_tail_step(up, xm2, xm1, conv_w, conv_b, down_w, g, x):
    t, d = x.shape
    f2 = up.shape[1]
    return pl.pallas_call(
        _ffn_step_kernel,
        grid=(1,),
        in_specs=[_resident((t, f2))] * 3
        + [_resident(conv_w.shape), _resident((1, f2)), _resident(down_w.shape),
           _resident((1, d)), _resident((t, d))],
        out_specs=pl.BlockSpec((t, d), lambda i: (0, 0)),
        out_shape=jax.ShapeDtypeStruct((t, d), F32),
        compiler_params=_params(("arbitrary",)),
        name="ffn_tail_step",
    )(up, xm2, xm1, conv_w, conv_b, down_w, g, x)


def _ssd_gate_norm(y, xs, z, dsk_ref, nw_ref):
    y = y + dsk_ref[...] * xs
    return _rms(y * _silu(z), nw_ref[...])


def _ssd_prompt_kernel(z_ref, xbc_ref, dt_ref, cw_ref, cb_ref, dtb_ref, alog_ref, dsk_ref, nw_ref,
                       y_ref, hout_ref, hist_sc, h_sc, xp_sc, y_sc):
    c = pl.program_id(1)
    nchunks = pl.num_programs(1)
    cl = xbc_ref.shape[0]
    nheads, hp, hn = h_sc.shape
    d_inner = nheads * hp
    heads_per_group = nheads // SSM_GROUPS

    @pl.when(c == 0)
    def _():
        hist_sc[...] = jnp.zeros_like(hist_sc)
        h_sc[...] = jnp.zeros_like(h_sc)

    raw = xbc_ref[...]
    xp_sc[0:SUBLANES, :] = hist_sc[...]
    xp_sc[SUBLANES:, :] = raw
    hist_sc[...] = raw[cl - SUBLANES:, :]
    conv = cb_ref[...]
    for k in range(SSM_CONV):
        conv = conv + xp_sc[pl.ds(SUBLANES - (SSM_CONV - 1) + k, cl), :] * cw_ref[k:k + 1, :]
    act = _silu(conv)
    xs = act[:, :d_inner]
    bm = act[:, d_inner:d_inner + SSM_GROUPS * SSM_STATE]
    cm = act[:, d_inner + SSM_GROUPS * SSM_STATE:]

    dt = jax.nn.softplus(dt_ref[...] + dtb_ref[...])
    da = dt * (-jnp.exp(alog_ref[...]))
    ti = lax.broadcasted_iota(jnp.int32, (cl, cl), 0)
    si = lax.broadcasted_iota(jnp.int32, (cl, cl), 1)
    tril = ti >= si
    cum_col = jnp.dot(tril.astype(F32), da, preferred_element_type=F32,
                      precision=lax.Precision.HIGHEST)
    cum_row = cum_col.T
    dt_row = dt.T

    cbs = []
    for g in range(SSM_GROUPS):
        sl = slice(g * SSM_STATE, (g + 1) * SSM_STATE)
        cbs.append(_dot_nt(cm[:, sl].astype(BF16), bm[:, sl].astype(BF16)))

    for h in range(nheads):
        g = h // heads_per_group
        sl = slice(g * SSM_STATE, (g + 1) * SSM_STATE)
        bm_g = bm[:, sl].astype(BF16)
        cm_g = cm[:, sl].astype(BF16)
        cc = cum_col[:, h:h + 1]
        cr = cum_row[h:h + 1, :]
        last = cum_col[cl - 1:cl, h:h + 1]
        seg = jnp.exp(jnp.where(tril, cc - cr, -jnp.inf))
        wts = cbs[g] * seg * dt_row[h:h + 1, :]
        xh = xs[:, h * hp:(h + 1) * hp]
        hst = h_sc[h]
        yh = _dot(wts.astype(BF16), xh.astype(BF16))
        yh = yh + _dot_nt(cm_g, hst.astype(BF16)) * jnp.exp(cc)
        to_end = jnp.exp(last - cc) * dt[:, h:h + 1]
        h_sc[h] = jnp.exp(last) * hst + _dot_tn((xh * to_end).astype(BF16), bm_g)
        y_sc[:, h * hp:(h + 1) * hp] = yh

    y_ref[...] = _ssd_gate_norm(y_sc[...], xs, z_ref[...], dsk_ref, nw_ref)

    @pl.when(c == nchunks - 1)
    def _():
        hout_ref[0] = h_sc[...]


def ssd_prompt(z, xbc, dt, conv_w, conv_b, dt_bias, a_log, d_skip, norm_w, batch, seq):
    t, d_inner = z.shape
    conv_dim = xbc.shape[1]
    nheads = d_inner // SSM_HEAD_DIM
    cl = SSM_CHUNK
    nchunks = seq // cl
    row = lambda b, c: (b * nchunks + c, 0)
    return pl.pallas_call(
        _ssd_prompt_kernel,
        grid=(batch, nchunks),
        in_specs=[pl.BlockSpec((cl, d_inner), row), pl.BlockSpec((cl, conv_dim), row),
                  pl.BlockSpec((cl, LANES), row),
                  _resident(conv_w.shape), _resident((1, conv_dim)), _resident((1, LANES)),
                  _resident((1, LANES)), _resident((1, d_inner)), _resident((1, d_inner))],
        out_specs=[pl.BlockSpec((cl, d_inner), row),
                   pl.BlockSpec((1, nheads, SSM_HEAD_DIM, SSM_STATE), lambda b, c: (b, 0, 0, 0))],
        out_shape=[jax.ShapeDtypeStruct((t, d_inner), F32),
                   jax.ShapeDtypeStruct((batch, nheads, SSM_HEAD_DIM, SSM_STATE), F32)],
        scratch_shapes=[pltpu.VMEM((SUBLANES, conv_dim), F32),
                        pltpu.VMEM((nheads, SSM_HEAD_DIM, SSM_STATE), F32),
                        pltpu.VMEM((cl + SUBLANES, conv_dim), F32),
                        pltpu.VMEM((cl, d_inner), F32)],
        compiler_params=_params(("arbitrary", "arbitrary")),
        name="ssd_prompt",
    )(z, xbc, dt, conv_w, conv_b, dt_bias, a_log, d_skip, norm_w)


def _ssd_step_kernel(z_ref, xnew_ref, h0_ref, h1_ref, h2_ref, dt_ref, cw_ref, cb_ref, dtb_ref, alog_ref,
                     dsk_ref, nw_ref, st_ref, y_ref, stout_ref,
                     xs_sc, xst_sc, bmt_sc, cmt_sc, dtt_sc, dat_sc, yt_sc):
    h = pl.program_id(0)
    nheads = pl.num_programs(0)
    d_inner = xs_sc.shape[1]
    hp, hn = SSM_HEAD_DIM, SSM_STATE
    heads_per_group = d_inner // hp // SSM_GROUPS

    @pl.when(h == 0)
    def _():
        conv = (cb_ref[...] + h0_ref[...] * cw_ref[0:1, :] + h1_ref[...] * cw_ref[1:2, :]
                + h2_ref[...] * cw_ref[2:3, :] + xnew_ref[...] * cw_ref[3:4, :])
        act = _silu(conv)
        xs = act[:, :d_inner]
        dt = jax.nn.softplus(dt_ref[...] + dtb_ref[...])
        xs_sc[...] = xs
        xst_sc[...] = xs.T
        bmt_sc[...] = act[:, d_inner:d_inner + SSM_GROUPS * hn].T
        cmt_sc[...] = act[:, d_inner + SSM_GROUPS * hn:].T
        dtt_sc[...] = dt.T
        dat_sc[...] = jnp.exp(dt * (-jnp.exp(alog_ref[...]))).T

    g = h // heads_per_group
    bmt = bmt_sc[pl.ds(pl.multiple_of(g * hn, hn), hn), :]
    cmt = cmt_sc[pl.ds(pl.multiple_of(g * hn, hn), hn), :]
    da = dat_sc[pl.ds(h, 1), :]
    dtx = xst_sc[pl.ds(pl.multiple_of(h * hp, hp), hp), :] * dtt_sc[pl.ds(h, 1), :]
    rows_per_tile = LANES // hn
    for j in range(hp // rows_per_tile):
        tile = st_ref[:, j * LANES:(j + 1) * LANES].T
        news = []
        for r in range(rows_per_tile):
            p = j * rows_per_tile + r
            new = da * tile[r * hn:(r + 1) * hn, :] + dtx[p:p + 1, :] * bmt
            yt_sc[pl.ds(h * hp + p, 1), :] = jnp.sum(new * cmt, axis=0, keepdims=True)
            news.append(new)
        stout_ref[:, j * LANES:(j + 1) * LANES] = jnp.concatenate(news, axis=0).T

    @pl.when(h == nheads - 1)
    def _():
        y_ref[...] = _ssd_gate_norm(yt_sc[...].T, xs_sc[...], z_ref[...], dsk_ref, nw_ref)


def ssd_step(z, xnew, hist, dt, conv_w, conv_b, dt_bias, a_log, d_skip, norm_w, state):
    b, d_inner = z.shape
    conv_dim = xnew.shape[1]
    nheads = d_inner // SSM_HEAD_DIM
    per_head = SSM_HEAD_DIM * SSM_STATE
    full = lambda shape: pl.BlockSpec(shape, lambda h: (0, 0))
    y, st = pl.pallas_call(
        _ssd_step_kernel,
        grid=(nheads,),
        in_specs=[full((b, d_inner))] + [full((b, conv_dim))] * 4 + [full((b, LANES))]
        + [full(conv_w.shape), full((1, conv_dim)), full((1, LANES)), full((1, LANES)),
           full((1, d_inner)), full((1, d_inner)),
           pl.BlockSpec((b, per_head), lambda h: (0, h))],
        out_specs=[full((b, d_inner)), pl.BlockSpec((b, per_head), lambda h: (0, h))],
        out_shape=[jax.ShapeDtypeStruct((b, d_inner), F32),
                   jax.ShapeDtypeStruct((b, nheads * per_head), F32)],
        scratch_shapes=[pltpu.VMEM((b, d_inner), F32), pltpu.VMEM((d_inner, b), F32),
                        pltpu.VMEM((SSM_GROUPS * SSM_STATE, b), F32),
                        pltpu.VMEM((SSM_GROUPS * SSM_STATE, b), F32),
                        pltpu.VMEM((LANES, b), F32), pltpu.VMEM((LANES, b), F32),
                        pltpu.VMEM((d_inner, b), F32)],
        compiler_params=_params(("arbitrary",)),
        name="ssd_step",
    )(z, xnew, hist[0], hist[1], hist[2], dt, conv_w, conv_b, dt_bias, a_log, d_skip, norm_w,
      state.reshape(b, nheads * per_head))
    return y, st.reshape(b, nheads, SSM_HEAD_DIM, SSM_STATE)


def _hgrn_lower_bound(lbp_ref, layer):
    raw = lbp_ref[...]
    e = jnp.exp(raw - jnp.max(raw, axis=0, keepdims=True))
    return jnp.sum(e[:layer + 1], axis=0, keepdims=True) / jnp.sum(e, axis=0, keepdims=True)


def _hgrn_gates(q_raw, f_raw, lb):
    q = _silu(q_raw)
    f = lb + (1.0 - lb) * jax.nn.sigmoid(f_raw)
    k = (1.0 - lb) * jax.nn.sigmoid(-f_raw)
    return q, f, k


def _hgrn_prompt_kernel(layer, q_ref, f_ref, i_ref, g_ref, lbp_ref, nw_ref, o_ref, sout_ref, st_sc):
    c = pl.program_id(2)
    nchunks = pl.num_programs(2)
    cl = q_ref.shape[0]

    @pl.when(c == 0)
    def _():
        st_sc[...] = jnp.zeros_like(st_sc)

    lb = _hgrn_lower_bound(lbp_ref, layer)
    q, f, k = _hgrn_gates(q_ref[...], f_ref[...], lb)
    v = i_ref[...]
    logf = jnp.log(f)

    row = lax.broadcasted_iota(jnp.int32, (cl, LANES), 0)
    cum = logf
    shift = 1
    while shift < cl:
        cum = cum + jnp.where(row >= shift, pltpu.roll(cum, shift, axis=0), 0.0)
        shift *= 2

    ti = lax.broadcasted_iota(jnp.int32, (cl, cl), 0)
    si = lax.broadcasted_iota(jnp.int32, (cl, cl), 1)
    tril = ti >= si
    att = jnp.where(ti == si, _dot_nt(q.astype(BF16), k.astype(BF16)), 0.0)
    last_of_block = cum
    blk = 1
    level = 0
    while blk < cl:
        odd = (row & blk) != 0
        expo = jnp.where(odd, cum - pltpu.roll(last_of_block, blk, axis=0), last_of_block - cum)
        e = jnp.exp(expo)
        pair = _dot_nt((q * e).astype(BF16), (k * e).astype(BF16))
        mask = tril & (((ti ^ si) >> level) == 1)
        att = jnp.where(mask, pair, att)
        last_of_block = jnp.where(odd, last_of_block, pltpu.roll(last_of_block, cl - blk, axis=0))
        blk *= 2
        level += 1
    cum_last = last_of_block

    st = st_sc[...]
    o = _dot(att.astype(BF16), v.astype(BF16)) + _dot_nt((q * jnp.exp(cum)).astype(BF16), st.astype(BF16))
    k_end = (k * jnp.exp(cum_last - cum)).astype(BF16)
    st_new = st * jnp.exp(cum_last) + _dot(v.T.astype(BF16), k_end)
    st_sc[...] = st_new
    o_ref[...] = _rms(o, nw_ref[...]) * _silu(g_ref[...])

    @pl.when(c == nchunks - 1)
    def _():
        sout_ref[0, 0] = st_new.T


def hgrn_prompt(qfig, lb_param, norm_w, layer, batch, seq):
    t, w4 = qfig.shape
    width = w4 // 4
    nheads = width // HG_DIM
    cl = HG_CHUNK
    nchunks = seq // cl
    nlb = lb_param.shape[0]

    def col(which):
        return lambda b, h, c: (b * nchunks + c, which * nheads + h)

    kern = functools.partial(_hgrn_prompt_kernel, layer)
    return pl.pallas_call(
        kern,
        grid=(batch, nheads, nchunks),
        in_specs=[pl.BlockSpec((cl, HG_DIM), col(j)) for j in range(4)]
        + [pl.BlockSpec((nlb, HG_DIM), lambda b, h, c: (0, h)),
           pl.BlockSpec((1, HG_DIM), lambda b, h, c: (0, 0))],
        out_specs=[pl.BlockSpec((cl, HG_DIM), lambda b, h, c: (b * nchunks + c, h)),
                   pl.BlockSpec((1, 1, HG_DIM, HG_DIM), lambda b, h, c: (b, h, 0, 0))],
        out_shape=[jax.ShapeDtypeStruct((t, width), F32),
                   jax.ShapeDtypeStruct((batch, nheads, HG_DIM, HG_DIM), F32)],
        scratch_shapes=[pltpu.VMEM((HG_DIM, HG_DIM), F32)],
        compiler_params=_params(("arbitrary", "arbitrary", "arbitrary")),
        name="hgrn_prompt",
    )(qfig, qfig, qfig, qfig, lb_param, norm_w)


HG_STEP_ROWS = 32


def _hgrn_step_kernel(layer, qfig_ref, lbp_ref, nw_ref, st_ref, o_ref, stout_ref,
                      qt_sc, ft_sc, kt_sc, vt_sc, ot_sc, acc_sc):
    h = pl.program_id(0)
    kc = pl.program_id(1)
    nheads = pl.num_programs(0)
    nkc = pl.num_programs(1)
    width = qt_sc.shape[0]

    @pl.when((h == 0) & (kc == 0))
    def _():
        raw = lbp_ref[...]
        e = jnp.exp(raw - jnp.max(raw, axis=0, keepdims=True))
        lb = jnp.sum(e[:layer + 1], axis=0, keepdims=True) / jnp.sum(e, axis=0, keepdims=True)
        q, f, k = _hgrn_gates(qfig_ref[:, 0:width], qfig_ref[:, width:2 * width], lb)
        qt_sc[...] = q.T
        ft_sc[...] = f.T
        kt_sc[...] = k.T
        vt_sc[...] = qfig_ref[:, 2 * width:3 * width].T

    @pl.when(kc == 0)
    def _():
        acc_sc[...] = jnp.zeros_like(acc_sc)

    base = pl.multiple_of(h * HG_DIM, HG_DIM)
    vt = vt_sc[pl.ds(base, HG_DIM), :]
    acc = acc_sc[...]
    for r in range(HG_STEP_ROWS):
        krow = base + kc * HG_STEP_ROWS + r
        new = (ft_sc[pl.ds(krow, 1), :] * st_ref[:, r * HG_DIM:(r + 1) * HG_DIM].T
               + kt_sc[pl.ds(krow, 1), :] * vt)
        acc = acc + new * qt_sc[pl.ds(krow, 1), :]
        stout_ref[:, r * HG_DIM:(r + 1) * HG_DIM] = new.T
    acc_sc[...] = acc

    @pl.when(kc == nkc - 1)
    def _():
        ot_sc[pl.ds(base, HG_DIM), :] = acc

    @pl.when((h == nheads - 1) & (kc == nkc - 1))
    def _():
        o = ot_sc[...].T
        for hh in range(width // HG_DIM):
            sl = slice(hh * HG_DIM, (hh + 1) * HG_DIM)
            o_ref[:, sl] = _rms(o[:, sl], nw_ref[...]) * _silu(qfig_ref[:, 3 * width + hh * HG_DIM:3 * width + (hh + 1) * HG_DIM])


def hgrn_step(qfig, lb_param, norm_w, layer, state):
    b, w4 = qfig.shape
    width = w4 // 4
    nheads = width // HG_DIM
    nkc = HG_DIM // HG_STEP_ROWS
    blk = HG_STEP_ROWS * HG_DIM
    kern = functools.partial(_hgrn_step_kernel, layer)
    full = lambda shape: pl.BlockSpec(shape, lambda h, kc: (0, 0))
    o, st = pl.pallas_call(
        kern,
        grid=(nheads, nkc),
        in_specs=[full((b, w4)), full(lb_param.shape), full((1, HG_DIM)),
                  pl.BlockSpec((b, blk), lambda h, kc: (0, h * nkc + kc))],
        out_specs=[full((b, width)), pl.BlockSpec((b, blk), lambda h, kc: (0, h * nkc + kc))],
        out_shape=[jax.ShapeDtypeStruct((b, width), F32),
                   jax.ShapeDtypeStruct((b, nheads * HG_DIM * HG_DIM), F32)],
        scratch_shapes=[pltpu.VMEM((width, b), F32)] * 5 + [pltpu.VMEM((HG_DIM, b), F32)],
        compiler_params=_params(("arbitrary", "arbitrary")),
        name="hgrn_step",
    )(qfig, lb_param, norm_w, state.reshape(b, nheads * HG_DIM * HG_DIM))
    return o, st.reshape(b, nheads, HG_DIM, HG_DIM)


def _s5_discretize_kernel(lre_ref, lim_ref, step_ref, bre_ref, bim_ref, are_ref, aim_ref, bbre_ref, bbim_ref):
    lre, lim = lre_ref[...], lim_ref[...]
    step = jnp.exp(step_ref[...])
    mag = jnp.exp(lre * step)
    are = mag * jnp.cos(lim * step)
    aim = mag * jnp.sin(lim * step)
    den = lre * lre + lim * lim
    nr = are - 1.0
    cre = (nr * lre + aim * lim) / den
    cim = (aim * lre - nr * lim) / den
    are_ref[...] = are
    aim_ref[...] = aim
    bre, bim = bre_ref[...], bim_ref[...]
    bbre_ref[...] = cre * bre - cim * bim
    bbim_ref[...] = cre * bim + cim * bre


def s5_discretize(lam_re, lam_im, log_step, b_re_t, b_im_t):
    g, n = lam_re.shape
    c = b_re_t.shape[1]
    return pl.pallas_call(
        _s5_discretize_kernel,
        out_shape=[jax.ShapeDtypeStruct((g, 1, n), F32)] * 2 + [jax.ShapeDtypeStruct((g, c, n), F32)] * 2,
        name="s5_discretize",
    )(lam_re.reshape(g, 1, n), lam_im.reshape(g, 1, n), log_step.reshape(g, 1, 1), b_re_t, b_im_t)


def _s5_powers_kernel(are_ref, aim_ref, pre_ref, pim_ref):
    n = pre_ref.shape[0]
    pre_ref[0:1, :] = are_ref[...]
    pim_ref[0:1, :] = aim_ref[...]
    m = 1
    while m < n:
        tre, tim = pre_ref[m - 1:m, :], pim_ref[m - 1:m, :]
        lre, lim = pre_ref[0:m, :], pim_ref[0:m, :]
        pre_ref[m:2 * m, :] = lre * tre - lim * tim
        pim_ref[m:2 * m, :] = lre * tim + lim * tre
        m *= 2


def s5_powers(a_re, a_im, n):
    width = a_re.shape[1]
    blk = S5_BLOCK_STATE
    return pl.pallas_call(
        _s5_powers_kernel,
        grid=(width // blk,),
        in_specs=[pl.BlockSpec((1, blk), lambda i: (0, i))] * 2,
        out_specs=[pl.BlockSpec((n, blk), lambda i: (0, i))] * 2,
        out_shape=[jax.ShapeDtypeStruct((n, width), F32)] * 2,
        compiler_params=_params(("parallel",)),
        name="s5_powers",
    )(a_re, a_im)


def _s5_output(xre, xim, u, cre_ref, cim_ref, d_ref):
    y = _dot(xre.astype(BF16), cre_ref[0]) - _dot(xim.astype(BF16), cim_ref[0]) + d_ref[...] * u
    return jax.nn.gelu(y)


S5_ROWS = 256


def _s5_prompt_kernel(u_ref, b_ref, cre_ref, cim_ref, d_ref, are_ref, aim_ref, pre_ref, pim_ref,
                      y_ref, sre_ref, sim_ref, x_sc):
    seq = u_ref.shape[0]
    ns = S5_BLOCK_STATE
    nt = ns // LANES
    seg = seq // S5_SEGMENTS

    def project(i, carry):
        r = pl.ds(pl.multiple_of(i * S5_ROWS, S5_ROWS), S5_ROWS)
        bu = _dot(u_ref[r, :].astype(BF16), b_ref[0])
        for k in range(2 * nt):
            x_sc[k, r, :] = bu[:, k * LANES:(k + 1) * LANES]
        return carry
    lax.fori_loop(0, seq // S5_ROWS, project, 0)

    def lane_tile(ref, k):
        return ref[:, k * LANES:(k + 1) * LANES]

    are = [jnp.broadcast_to(lane_tile(are_ref, k), (S5_SEGMENTS, LANES)) for k in range(nt)]
    aim = [jnp.broadcast_to(lane_tile(aim_ref, k), (S5_SEGMENTS, LANES)) for k in range(nt)]

    def scan(i, carry):
        rows = pl.ds(i, S5_SEGMENTS, stride=seg)
        new = []
        for k in range(nt):
            sre, sim = carry[k]
            nre = are[k] * sre - aim[k] * sim + x_sc[k, rows, :]
            nim = are[k] * sim + aim[k] * sre + x_sc[nt + k, rows, :]
            x_sc[k, rows, :] = nre
            x_sc[nt + k, rows, :] = nim
            new.append((nre, nim))
        return tuple(new)
    zero = jnp.zeros((S5_SEGMENTS, LANES), F32)
    ends = lax.fori_loop(0, seg, scan, tuple((zero, zero) for _ in range(nt)))

    for k in range(nt):
        ere, eim = ends[k]
        pre, pim = lane_tile(pre_ref, k), lane_tile(pim_ref, k)
        full_re, full_im = pre[seg - 1:seg, :], pim[seg - 1:seg, :]
        tre, tim = ere[0:1, :], eim[0:1, :]
        for j in range(1, S5_SEGMENTS):
            rows = slice(j * seg, (j + 1) * seg)
            x_sc[k, rows, :] = x_sc[k, rows, :] + pre * tre - pim * tim
            x_sc[nt + k, rows, :] = x_sc[nt + k, rows, :] + pre * tim + pim * tre
            tre, tim = (ere[j:j + 1, :] + full_re * tre - full_im * tim,
                        eim[j:j + 1, :] + full_re * tim + full_im * tre)
        sre_ref[0, :, k * LANES:(k + 1) * LANES] = tre
        sim_ref[0, :, k * LANES:(k + 1) * LANES] = tim

    def output(i, carry):
        r = pl.ds(pl.multiple_of(i * S5_ROWS, S5_ROWS), S5_ROWS)
        xre = jnp.concatenate([x_sc[k, r, :] for k in range(nt)], axis=1)
        xim = jnp.concatenate([x_sc[nt + k, r, :] for k in range(nt)], axis=1)
        y_ref[r, :] = _s5_output(xre, xim, u_ref[r, :], cre_ref, cim_ref, d_ref)
        return carry
    lax.fori_loop(0, seq // S5_ROWS, output, 0)


def s5_prompt(u, b_exp, c_re_exp, c_im_exp, d_skip, a_re, a_im, p_re, p_im, batch, seq):
    t, d = u.shape
    nblk = d // LANES
    ns = S5_BLOCK_STATE
    seg = seq // S5_SEGMENTS
    return pl.pallas_call(
        _s5_prompt_kernel,
        grid=(nblk, batch),
        in_specs=[pl.BlockSpec((seq, LANES), lambda j, b: (b, j)),
                  pl.BlockSpec((1, LANES, 2 * ns), lambda j, b: (j, 0, 0)),
                  pl.BlockSpec((1, ns, LANES), lambda j, b: (j, 0, 0)),
                  pl.BlockSpec((1, ns, LANES), lambda j, b: (j, 0, 0)),
                  pl.BlockSpec((1, LANES), lambda j, b: (0, j)),
                  pl.BlockSpec((1, ns), lambda j, b: (0, j)),
                  pl.BlockSpec((1, ns), lambda j, b: (0, j)),
                  pl.BlockSpec((seg, ns), lambda j, b: (0, j)),
                  pl.BlockSpec((seg, ns), lambda j, b: (0, j))],
        out_specs=[pl.BlockSpec((seq, LANES), lambda j, b: (b, j)),
                   pl.BlockSpec((1, 1, ns), lambda j, b: (b, 0, j)),
                   pl.BlockSpec((1, 1, ns), lambda j, b: (b, 0, j))],
        out_shape=[jax.ShapeDtypeStruct((t, d), F32),
                   jax.ShapeDtypeStruct((batch, 1, nblk * ns), F32),
                   jax.ShapeDtypeStruct((batch, 1, nblk * ns), F32)],
        scratch_shapes=[pltpu.VMEM((2 * ns // LANES, seq, LANES), F32)],
        compiler_params=_params(("arbitrary", "arbitrary")),
        name="s5_prompt",
    )(u, b_exp, c_re_exp, c_im_exp, d_skip, a_re, a_im, p_re, p_im)


def _s5_step_kernel(u_ref, b_ref, cre_ref, cim_ref, d_ref, are_ref, aim_ref, s0re_ref, s0im_ref,
                    y_ref, sre_ref, sim_ref):
    ns = S5_BLOCK_STATE
    u = u_ref[...]
    bu = _dot(u.astype(BF16), b_ref[0])
    are, aim = are_ref[...], aim_ref[...]
    s0re, s0im = s0re_ref[...], s0im_ref[...]
    xre = bu[:, 0:ns] + (are * s0re - aim * s0im)
    xim = bu[:, ns:2 * ns] + (are * s0im + aim * s0re)
    sre_ref[...] = xre
    sim_ref[...] = xim
    y_ref[...] = _s5_output(xre, xim, u, cre_ref, cim_ref, d_ref)


def s5_step(u, b_exp, c_re_exp, c_im_exp, d_skip, a_re, a_im, s0_re, s0_im):
    b, d = u.shape
    nblk = d // LANES
    ns = S5_BLOCK_STATE
    return pl.pallas_call(
        _s5_step_kernel,
        grid=(nblk,),
        in_specs=[pl.BlockSpec((b, LANES), lambda j: (0, j)),
                  pl.BlockSpec((1, LANES, 2 * ns), lambda j: (j, 0, 0)),
                  pl.BlockSpec((1, ns, LANES), lambda j: (j, 0, 0)),
                  pl.BlockSpec((1, ns, LANES), lambda j: (j, 0, 0)),
                  pl.BlockSpec((1, LANES), lambda j: (0, j)),
                  pl.BlockSpec((1, ns), lambda j: (0, j)),
                  pl.BlockSpec((1, ns), lambda j: (0, j)),
                  pl.BlockSpec((b, ns), lambda j: (0, j)),
                  pl.BlockSpec((b, ns), lambda j: (0, j))],
        out_specs=[pl.BlockSpec((b, LANES), lambda j: (0, j)),
                   pl.BlockSpec((b, ns), lambda j: (0, j)),
                   pl.BlockSpec((b, ns), lambda j: (0, j))],
        out_shape=[jax.ShapeDtypeStruct((b, d), F32),
                   jax.ShapeDtypeStruct((b, nblk * ns), F32),
                   jax.ShapeDtypeStruct((b, nblk * ns), F32)],
        compiler_params=_params(("parallel",)),
        name="s5_step",
    )(u, b_exp, c_re_exp, c_im_exp, d_skip, a_re, a_im, s0_re, s0_im)


def _block_diag(per_group):
    ngroups, r, c = per_group.shape
    p = S5_GROUPS_PER_BLOCK
    eye = jnp.eye(p, dtype=per_group.dtype)
    tiles = per_group.reshape(ngroups // p, p, r, c)
    return jnp.einsum('bjrc,jk->bjrkc', tiles, eye).reshape(ngroups // p, p * r, p * c)


def kernel(x_prompt, x_sample, state_ssm, state_ssm_conv, state_hgrn, state_s5_re, state_s5_im, state_ffn_conv, norm_mix_pre, norm_mix_post, norm_ffn_pre, norm_ffn_post, ab_in_w, ssm_conv_w, ssm_conv_b, ssm_dt_bias, ssm_a_log, ssm_d, ssm_norm_w, hgrn_lb, hgrn_norm_w, ab_out_w, s5_lam_re, s5_lam_im, s5_log_step, s5_b_re, s5_b_im, s5_c_re, s5_c_im, s5_d, s5_glu_w, ffn_up_w, ffn_conv_w, ffn_conv_b, ffn_down_w):
    batch, seq, d = x_prompt.shape
    dec_batch = x_sample.shape[0]
    depth = norm_mix_pre.shape[0]
    conv_dim = ssm_conv_w.shape[2]
    nheads = ssm_a_log.shape[1]
    d_inner = nheads * SSM_HEAD_DIM
    hg_width = hgrn_lb.shape[1]
    ffn_dim = ffn_down_w.shape[1]
    tm = 256

    def row(v):
        return v.reshape(1, -1)

    def pad_lanes(v):
        return jnp.pad(v, ((0, 0), (0, LANES - v.shape[1])))

    groups = [(x_prompt.reshape(batch * seq, d), True), (x_sample.reshape(dec_batch, d), False)]
    xs = [g[0] for g in groups]
    out = {'ssm': [[], []], 'sconv': [[], []], 'hg': [[], []], 's5r': [[], []], 's5i': [[], []], 'fconv': [[], []]}

    for l in range(depth):
        e = l // 2
        if l % 2 == 0:
            w_in = ab_in_w[e]
            o_xbc = d_inner
            o_dt = o_xbc + conv_dim
            o_q = o_dt + nheads
            w_z = w_in[:, :o_xbc].astype(BF16)
            w_xbc = w_in[:, o_xbc:o_dt].astype(BF16)
            w_dt = pad_lanes(w_in[:, o_dt:o_q]).astype(BF16)
            w_qfig = w_in[:, o_q:].astype(BF16)
            w_out = ab_out_w[e].astype(BF16)
            dt_bias = pad_lanes(row(ssm_dt_bias[e]))
            a_log = pad_lanes(row(ssm_a_log[e]))
            d_skip = row(jnp.repeat(ssm_d[e], SSM_HEAD_DIM))
            for gi, (_, is_prompt) in enumerate(groups):
                x = xs[gi]
                t = x.shape[0]
                z, xbc, dt, qfig = norm_matmul(x, row(norm_mix_pre[l]), [w_z, w_xbc, w_dt, w_qfig], min(tm, t))
                if is_prompt:
                    y, ssm_new = ssd_prompt(z, xbc, dt, ssm_conv_w[e], row(ssm_conv_b[e]), dt_bias, a_log,
                                            d_skip, row(ssm_norm_w[e]), batch, seq)
                    o, hg_new = hgrn_prompt(qfig, hgrn_lb, row(hgrn_norm_w[e]), e, batch, seq)
                    sconv_new = xbc.reshape(batch, seq, conv_dim)[:, seq - (SSM_CONV - 1):]
                else:
                    hist = [state_ssm_conv[e][:, k] for k in range(SSM_CONV - 1)]
                    y, ssm_new = ssd_step(z, xbc, hist, dt, ssm_conv_w[e], row(ssm_conv_b[e]), dt_bias, a_log,
                                          d_skip, row(ssm_norm_w[e]), state_ssm[e])
                    o, hg_new = hgrn_step(qfig, hgrn_lb, row(hgrn_norm_w[e]), e, state_hgrn[e])
                    sconv_new = jnp.stack(hist[1:] + [xbc], axis=1)
                out['ssm'][gi].append(ssm_new)
                out['hg'][gi].append(hg_new)
                out['sconv'][gi].append(sconv_new)
                xs[gi] = matmul_norm_residual([y, o], [w_out[:d_inner], w_out[d_inner:]],
                                              row(norm_mix_post[l]), x, min(tm, t))
        else:
            ngroups = s5_lam_re.shape[1]
            a_re, a_im, bb_re, bb_im = s5_discretize(
                s5_lam_re[e], s5_lam_im[e], s5_log_step[e],
                jnp.swapaxes(s5_b_re[e], 1, 2), jnp.swapaxes(s5_b_im[e], 1, 2))
            a_re = a_re.reshape(1, ngroups * S5_STATE)
            a_im = a_im.reshape(1, ngroups * S5_STATE)
            b_exp = jnp.concatenate([_block_diag(bb_re), _block_diag(bb_im)], axis=2).astype(BF16)
            c_re_exp = _block_diag(jnp.swapaxes(s5_c_re[e], 1, 2)).astype(BF16)
            c_im_exp = _block_diag(jnp.swapaxes(s5_c_im[e], 1, 2)).astype(BF16)
            glu_w = s5_glu_w[e].astype(BF16)
            for gi, (_, is_prompt) in enumerate(groups):
                x = xs[gi]
                t = x.shape[0]
                u = rmsnorm(x, row(norm_mix_pre[l]), min(tm, t))
                if is_prompt:
                    p_re, p_im = s5_powers(a_re, a_im, seq // S5_SEGMENTS)
                    yg, s_re, s_im = s5_prompt(u, b_exp, c_re_exp, c_im_exp, row(s5_d[e]), a_re, a_im,
                                               p_re, p_im, batch, seq)
                else:
                    yg, s_re, s_im = s5_step(u, b_exp, c_re_exp, c_im_exp, row(s5_d[e]), a_re, a_im,
                                             state_s5_re[e].reshape(t, -1), state_s5_im[e].reshape(t, -1))
                out['s5r'][gi].append(s_re.reshape(-1, ngroups, S5_STATE))
                out['s5i'][gi].append(s_im.reshape(-1, ngroups, S5_STATE))
                xs[gi] = matmul_norm_residual([yg], [glu_w], row(norm_mix_post[l]), x, min(tm, t), glu=True)

        up_w = ffn_up_w[l].astype(BF16)
        down_w = ffn_down_w[l].astype(BF16)
        for gi, (_, is_prompt) in enumerate(groups):
            x = xs[gi]
            t = x.shape[0]
            (up,) = norm_matmul(x, row(norm_ffn_pre[l]), [up_w], min(tm, t))
            if is_prompt:
                xs[gi] = ffn_tail_seq(up, ffn_conv_w[l], row(ffn_conv_b[l]), down_w, row(norm_ffn_post[l]), x, tm, seq)
                fconv_new = up.reshape(batch, seq, 2 * ffn_dim)[:, seq - (FFN_CONV - 1):]
            else:
                old = state_ffn_conv[l]
                xs[gi] = ffn_tail_step(up, old[:, 0], old[:, 1], ffn_conv_w[l], row(ffn_conv_b[l]), down_w,
                                       row(norm_ffn_post[l]), x)
                fconv_new = jnp.stack([old[:, 1], up], axis=1)
            out['fconv'][gi].append(fconv_new)

    y_prompt = xs[0].reshape(batch, seq, d)
    y_sample = xs[1].reshape(dec_batch, 1, d)
    states = []
    for gi in range(2):
        states += [jnp.stack(out[k][gi]) for k in ('ssm', 'sconv', 'hg', 's5r', 's5i', 'fconv')]
    return (y_prompt, y_sample, *states)
```

```python
import functools
import math

import jax
import jax.numpy as jnp
import numpy as np
from jax import lax
from jax.experimental import pallas as pl
from jax.experimental.pallas import tpu as pltpu

F32 = jnp.float32
BF16 = jnp.bfloat16
EPS = 1e-6

LANES = 128
SUBLANES = 8
VMEM_LIMIT_BYTES = 56 * 1024 * 1024

SSM_HEAD_DIM = 64
SSM_GROUPS = 2
SSM_STATE = 64
SSM_CONV = 4
SSM_CHUNK = 128
HG_DIM = 128
HG_CHUNK = 128
S5_GROUP = 16
S5_STATE = 64
S5_GROUPS_PER_BLOCK = LANES // S5_GROUP
S5_BLOCK_STATE = S5_GROUPS_PER_BLOCK * S5_STATE
S5_SEGMENTS = SUBLANES
FFN_CONV = 3


def _params(sem):
    return pltpu.CompilerParams(dimension_semantics=sem, vmem_limit_bytes=VMEM_LIMIT_BYTES)


def _resident(shape):
    nd = len(shape)
    return pl.BlockSpec(shape, lambda *_: (0,) * nd, pipeline_mode=pl.Buffered(1))


def _rms(x, w):
    return x * lax.rsqrt(jnp.mean(x * x, axis=-1, keepdims=True) + EPS) * w


def _silu(x):
    return x * jax.nn.sigmoid(x)


def _dot(a, b):
    return jnp.dot(a, b, preferred_element_type=F32)


def _dot_nt(a, b):
    return lax.dot_general(a, b, (((1,), (1,)), ((), ())), preferred_element_type=F32)


def _dot_tn(a, b):
    return lax.dot_general(a, b, (((0,), (0,)), ((), ())), preferred_element_type=F32)


def _norm_matmul_kernel(n_out, x_ref, g_ref, *refs):
    w_refs, o_refs = refs[:n_out], refs[n_out:]
    xn = _rms(x_ref[...], g_ref[...]).astype(BF16)
    for w_ref, o_ref in zip(w_refs, o_refs):
        n = w_ref.shape[1]
        for c0 in range(0, n, 512):
            c1 = min(n, c0 + 512)
            o_ref[:, c0:c1] = _dot(xn, w_ref[:, c0:c1])


def norm_matmul(x, g, ws, tm):
    t, d = x.shape
    kern = functools.partial(_norm_matmul_kernel, len(ws))
    return pl.pallas_call(
        kern,
        grid=(t // tm,),
        in_specs=[pl.BlockSpec((tm, d), lambda i: (i, 0)), _resident((1, d))]
        + [_resident(w.shape) for w in ws],
        out_specs=[pl.BlockSpec((tm, w.shape[1]), lambda i: (i, 0)) for w in ws],
        out_shape=[jax.ShapeDtypeStruct((t, w.shape[1]), F32) for w in ws],
        compiler_params=_params(("parallel",)),
        name="norm_matmul",
    )(x, g, *ws)


def _rmsnorm_kernel(x_ref, g_ref, o_ref):
    o_ref[...] = _rms(x_ref[...], g_ref[...]).reshape(o_ref.shape)


def _interleaved_spec(tm, width, nseg):
    return pl.BlockSpec((1, tm, width), lambda i: (i // nseg, 0, i % nseg))


def rmsnorm(x, g, tm, interleave=None):
    t, d = x.shape
    if interleave is None:
        out_spec, out_shape = pl.BlockSpec((tm, d), lambda i: (i, 0)), (t, d)
    else:
        out_spec, out_shape = _interleaved_spec(tm, d, interleave), (t // (tm * interleave), tm, interleave * d)
    out = pl.pallas_call(
        _rmsnorm_kernel,
        grid=(t // tm,),
        in_specs=[pl.BlockSpec((tm, d), lambda i: (i, 0)), _resident((1, d))],
        out_specs=out_spec,
        out_shape=jax.ShapeDtypeStruct(out_shape, F32),
        compiler_params=_params(("parallel",)),
        name="rmsnorm",
    )(x, g)
    return out.reshape(t, d)


def _matmul_norm_residual_kernel(n_in, glu, *refs):
    a_refs, w_refs = refs[:n_in], refs[n_in:2 * n_in]
    g_ref, x_ref, o_ref = refs[2 * n_in:]
    acc = None
    for a_ref, w_ref in zip(a_refs, w_refs):
        a = a_ref[...].reshape(a_ref.shape[-2:])
        part = _dot(a.astype(BF16), w_ref[...])
        acc = part if acc is None else acc + part
    if glu:
        d = acc.shape[1] // 2
        acc = acc[:, :d] * jax.nn.sigmoid(acc[:, d:])
    o_ref[...] = x_ref[...] + _rms(acc, g_ref[...])


def matmul_norm_residual(a_list, w_list, g, x, tm, glu=False, interleave=None):
    t, d = x.shape
    kern = functools.partial(_matmul_norm_residual_kernel, len(a_list), glu)
    if interleave is None:
        a_specs = [pl.BlockSpec((tm, a.shape[1]), lambda i: (i, 0)) for a in a_list]
    else:
        a_specs = [_interleaved_spec(tm, a.shape[1], interleave) for a in a_list]
        a_list = [a.reshape(t // (tm * interleave), tm, interleave * a.shape[1]) for a in a_list]
    return pl.pallas_call(
        kern,
        grid=(t // tm,),
        in_specs=a_specs
        + [_resident(w.shape) for w in w_list]
        + [_resident((1, d)), pl.BlockSpec((tm, d), lambda i: (i, 0))],
        out_specs=pl.BlockSpec((tm, d), lambda i: (i, 0)),
        out_shape=jax.ShapeDtypeStruct((t, d), F32),
        compiler_params=_params(("parallel",)),
        name="matmul_norm_residual",
    )(*a_list, *w_list, g, x)


FFN_CHUNK = 256


def _ffn_tail(up_ref, xm2_of, xm1_of, cw_ref, cb_ref, dw_ref, g_ref, x_ref, o_ref):
    f = dw_ref.shape[0]
    acc = jnp.zeros(o_ref.shape, F32)
    for c0 in range(0, f, FFN_CHUNK):
        c1 = c0 + FFN_CHUNK
        halves = []
        for off in (0, f):
            a, b = c0 + off, c1 + off
            halves.append(cb_ref[:, a:b] + xm2_of(a, b) * cw_ref[0:1, a:b]
                          + xm1_of(a, b) * cw_ref[1:2, a:b] + up_ref[:, a:b] * cw_ref[2:3, a:b])
        act = (halves[0] * _silu(halves[1])).astype(BF16)
        acc = acc + _dot(act, dw_ref[c0:c1, :])
    o_ref[...] = x_ref[...] + _rms(acc, g_ref[...])


def _ffn_seq_kernel(tiles_per_seq, up_ref, halo_ref, cw_ref, cb_ref, dw_ref, g_ref, x_ref, o_ref):
    tm = up_ref.shape[0]
    not_first = jnp.where(pl.program_id(0) % tiles_per_seq != 0, 1.0, 0.0)
    row = lax.broadcasted_iota(jnp.int32, (tm, FFN_CHUNK), 0)

    def shifted(k):
        def of(a, b):
            cur = pltpu.roll(up_ref[:, a:b], k, axis=0)
            for r in range(k):
                prev = halo_ref[SUBLANES - k + r:SUBLANES - k + r + 1, a:b] * not_first
                cur = jnp.where(row == r, prev, cur)
            return cur
        return of

    _ffn_tail(up_ref, shifted(2), shifted(1), cw_ref, cb_ref, dw_ref, g_ref, x_ref, o_ref)


def ffn_tail_seq(up, conv_w, conv_b, down_w, g, x, tm, seq):
    t, d = x.shape
    f2 = up.shape[1]
    hb = tm // SUBLANES
    kern = functools.partial(_ffn_seq_kernel, seq // tm)
    return pl.pallas_call(
        kern,
        grid=(t // tm,),
        in_specs=[pl.BlockSpec((tm, f2), lambda i: (i, 0)),
                  pl.BlockSpec((SUBLANES, f2), lambda i: (jnp.maximum(i * hb - 1, 0), 0)),
                  _resident(conv_w.shape), _resident((1, f2)), _resident(down_w.shape),
                  _resident((1, d)), pl.BlockSpec((tm, d), lambda i: (i, 0))],
        out_specs=pl.BlockSpec((tm, d), lambda i: (i, 0)),
        out_shape=jax.ShapeDtypeStruct((t, d), F32),
        compiler_params=_params(("parallel",)),
        name="ffn_tail_seq",
    )(up, up, conv_w, conv_b, down_w, g, x)


def _ffn_step_kernel(up_ref, xm2_ref, xm1_ref, cw_ref, cb_ref, dw_ref, g_ref, x_ref, o_ref):
    _ffn_tail(up_ref, lambda a, b: xm2_ref[:, a:b], lambda a, b: xm1_ref[:, a:b],
              cw_ref, cb_ref, dw_ref, g_ref, x_ref, o_ref)


def ffn_tail_step(up, xm2, xm1, conv_w, conv_b, down_w, g, x):
    t, d = x.shape
    f2 = up.shape[1]
    return pl.pallas_call(
        _ffn_step_kernel,
        grid=(1,),
        in_specs=[_resident((t, f2))] * 3
        + [_resident(conv_w.shape), _resident((1, f2)), _resident(down_w.shape),
           _resident((1, d)), _resident((t, d))],
        out_specs=pl.BlockSpec((t, d), lambda i: (0, 0)),
        out_shape=jax.ShapeDtypeStruct((t, d), F32),
        compiler_params=_params(("arbitrary",)),
        name="ffn_tail_step",
    )(up, xm2, xm1, conv_w, conv_b, down_w, g, x)


def _ssd_gate_norm(y, xs, z, dsk_ref, nw_ref):
    y = y + dsk_ref[...] * xs
    return _rms(y * _silu(z), nw_ref[...])


def _ssd_prompt_kernel(z_ref, xbc_ref, dt_ref, cw_ref, cb_ref, dtb_ref, alog_ref, dsk_ref, nw_ref,
                       y_ref, hout_ref, hist_sc, h_sc, xp_sc, y_sc):
    c = pl.program_id(1)
    nchunks = pl.num_programs(1)
    cl = xbc_ref.shape[0]
    nheads, hp, hn = h_sc.shape
    d_inner = nheads * hp
    heads_per_group = nheads // SSM_GROUPS

    @pl.when(c == 0)
    def _():
        hist_sc[...] = jnp.zeros_like(hist_sc)
        h_sc[...] = jnp.zeros_like(h_sc)

    raw = xbc_ref[...]
    xp_sc[0:SUBLANES, :] = hist_sc[...]
    xp_sc[SUBLANES:, :] = raw
    hist_sc[...] = raw[cl - SUBLANES:, :]
    conv = cb_ref[...]
    for k in range(SSM_CONV):
        conv = conv + xp_sc[pl.ds(SUBLANES - (SSM_CONV - 1) + k, cl), :] * cw_ref[k:k + 1, :]
    act = _silu(conv)
    xs = act[:, :d_inner]
    bm = act[:, d_inner:d_inner + SSM_GROUPS * SSM_STATE]
    cm = act[:, d_inner + SSM_GROUPS * SSM_STATE:]

    dt = jax.nn.softplus(dt_ref[...] + dtb_ref[...])
    da = dt * (-jnp.exp(alog_ref[...]))
    ti = lax.broadcasted_iota(jnp.int32, (cl, cl), 0)
    si = lax.broadcasted_iota(jnp.int32, (cl, cl), 1)
    tril = ti >= si
    cum_col = jnp.dot(tril.astype(F32), da, preferred_element_type=F32,
                      precision=lax.Precision.HIGHEST)
    cum_row = cum_col.T
    dt_row = dt.T

    cbs = []
    for g in range(SSM_GROUPS):
        sl = slice(g * SSM_STATE, (g + 1) * SSM_STATE)
        cbs.append(_dot_nt(cm[:, sl].astype(BF16), bm[:, sl].astype(BF16)))

    for h in range(nheads):
        g = h // heads_per_group
        sl = slice(g * SSM_STATE, (g + 1) * SSM_STATE)
        bm_g = bm[:, sl].astype(BF16)
        cm_g = cm[:, sl].astype(BF16)
        cc = cum_col[:, h:h + 1]
        cr = cum_row[h:h + 1, :]
        last = cum_col[cl - 1:cl, h:h + 1]
        seg = jnp.exp(jnp.where(tril, cc - cr, -jnp.inf))
        wts = cbs[g] * seg * dt_row[h:h + 1, :]
        xh = xs[:, h * hp:(h + 1) * hp]
        hst = h_sc[h]
        yh = _dot(wts.astype(BF16), xh.astype(BF16))
        yh = yh + _dot_nt(cm_g, hst.astype(BF16)) * jnp.exp(cc)
        to_end = jnp.exp(last - cc) * dt[:, h:h + 1]
        h_sc[h] = jnp.exp(last) * hst + _dot_tn((xh * to_end).astype(BF16), bm_g)
        y_sc[:, h * hp:(h + 1) * hp] = yh

    y_ref[...] = _ssd_gate_norm(y_sc[...], xs, z_ref[...], dsk_ref, nw_ref)

    @pl.when(c == nchunks - 1)
    def _():
        hout_ref[0] = h_sc[...]


def ssd_prompt(z, xbc, dt, conv_w, conv_b, dt_bias, a_log, d_skip, norm_w, batch, seq):
    t, d_inner = z.shape
    conv_dim = xbc.shape[1]
    nheads = d_inner // SSM_HEAD_DIM
    cl = SSM_CHUNK
    nchunks = seq // cl
    row = lambda b, c: (b * nchunks + c, 0)
    return pl.pallas_call(
        _ssd_prompt_kernel,
        grid=(batch, nchunks),
        in_specs=[pl.BlockSpec((cl, d_inner), row), pl.BlockSpec((cl, conv_dim), row),
                  pl.BlockSpec((cl, LANES), row),
                  _resident(conv_w.shape), _resident((1, conv_dim)), _resident((1, LANES)),
                  _resident((1, LANES)), _resident((1, d_inner)), _resident((1, d_inner))],
        out_specs=[pl.BlockSpec((cl, d_inner), row),
                   pl.BlockSpec((1, nheads, SSM_HEAD_DIM, SSM_STATE), lambda b, c: (b, 0, 0, 0))],
        out_shape=[jax.ShapeDtypeStruct((t, d_inner), F32),
                   jax.ShapeDtypeStruct((batch, nheads, SSM_HEAD_DIM, SSM_STATE), F32)],
        scratch_shapes=[pltpu.VMEM((SUBLANES, conv_dim), F32),
                        pltpu.VMEM((nheads, SSM_HEAD_DIM, SSM_STATE), F32),
                        pltpu.VMEM((cl + SUBLANES, conv_dim), F32),
                        pltpu.VMEM((cl, d_inner), F32)],
        compiler_params=_params(("arbitrary", "arbitrary")),
        name="ssd_prompt",
    )(z, xbc, dt, conv_w, conv_b, dt_bias, a_log, d_skip, norm_w)


def _ssd_step_kernel(z_ref, xnew_ref, h0_ref, h1_ref, h2_ref, dt_ref, cw_ref, cb_ref, dtb_ref, alog_ref,
                     dsk_ref, nw_ref, st_ref, y_ref, stout_ref,
                     xs_sc, xst_sc, bmt_sc, cmt_sc, dtt_sc, dat_sc, yt_sc):
    h = pl.program_id(0)
    nheads = pl.num_programs(0)
    d_inner = xs_sc.shape[1]
    hp, hn = SSM_HEAD_DIM, SSM_STATE
    heads_per_group = d_inner // hp // SSM_GROUPS

    @pl.when(h == 0)
    def _():
        conv = (cb_ref[...] + h0_ref[...] * cw_ref[0:1, :] + h1_ref[...] * cw_ref[1:2, :]
                + h2_ref[...] * cw_ref[2:3, :] + xnew_ref[...] * cw_ref[3:4, :])
        act = _silu(conv)
        xs = act[:, :d_inner]
        dt = jax.nn.softplus(dt_ref[...] + dtb_ref[...])
        xs_sc[...] = xs
        xst_sc[...] = xs.T
        bmt_sc[...] = act[:, d_inner:d_inner + SSM_GROUPS * hn].T
        cmt_sc[...] = act[:, d_inner + SSM_GROUPS * hn:].T
        dtt_sc[...] = dt.T
        dat_sc[...] = jnp.exp(dt * (-jnp.exp(alog_ref[...]))).T

    g = h // heads_per_group
    bmt = bmt_sc[pl.ds(pl.multiple_of(g * hn, hn), hn), :]
    cmt = cmt_sc[pl.ds(pl.multiple_of(g * hn, hn), hn), :]
    da = dat_sc[pl.ds(h, 1), :]
    dtx = xst_sc[pl.ds(pl.multiple_of(h * hp, hp), hp), :] * dtt_sc[pl.ds(h, 1), :]
    rows_per_tile = LANES // hn
    for j in range(hp // rows_per_tile):
        tile = st_ref[:, j * LANES:(j + 1) * LANES].T
        news = []
        for r in range(rows_per_tile):
            p = j * rows_per_tile + r
            new = da * tile[r * hn:(r + 1) * hn, :] + dtx[p:p + 1, :] * bmt
            yt_sc[pl.ds(h * hp + p, 1), :] = jnp.sum(new * cmt, axis=0, keepdims=True)
            news.append(new)
        stout_ref[:, j * LANES:(j + 1) * LANES] = jnp.concatenate(news, axis=0).T

    @pl.when(h == nheads - 1)
    def _():
        y_ref[...] = _ssd_gate_norm(yt_sc[...].T, xs_sc[...], z_ref[...], dsk_ref, nw_ref)


def ssd_step(z, xnew, hist, dt, conv_w, conv_b, dt_bias, a_log, d_skip, norm_w, state):
    b, d_inner = z.shape
    conv_dim = xnew.shape[1]
    nheads = d_inner // SSM_HEAD_DIM
    per_head = SSM_HEAD_DIM * SSM_STATE
    full = lambda shape: pl.BlockSpec(shape, lambda h: (0, 0))
    y, st = pl.pallas_call(
        _ssd_step_kernel,
        grid=(nheads,),
        in_specs=[full((b, d_inner))] + [full((b, conv_dim))] * 4 + [full((b, LANES))]
        + [full(conv_w.shape), full((1, conv_dim)), full((1, LANES)), full((1, LANES)),
           full((1, d_inner)), full((1, d_inner)),
           pl.BlockSpec((b, per_head), lambda h: (0, h))],
        out_specs=[full((b, d_inner)), pl.BlockSpec((b, per_head), lambda h: (0, h))],
        out_shape=[jax.ShapeDtypeStruct((b, d_inner), F32),
                   jax.ShapeDtypeStruct((b, nheads * per_head), F32)],
        scratch_shapes=[pltpu.VMEM((b, d_inner), F32), pltpu.VMEM((d_inner, b), F32),
                        pltpu.VMEM((SSM_GROUPS * SSM_STATE, b), F32),
                        pltpu.VMEM((SSM_GROUPS * SSM_STATE, b), F32),
                        pltpu.VMEM((LANES, b), F32), pltpu.VMEM((LANES, b), F32),
                        pltpu.VMEM((d_inner, b), F32)],
        compiler_params=_params(("arbitrary",)),
        name="ssd_step",
    )(z, xnew, hist[0], hist[1], hist[2], dt, conv_w, conv_b, dt_bias, a_log, d_skip, norm_w,
      state.reshape(b, nheads * per_head))
    return y, st.reshape(b, nheads, SSM_HEAD_DIM, SSM_STATE)


def _hgrn_lower_bound(lbp_ref, layer):
    raw = lbp_ref[...]
    e = jnp.exp(raw - jnp.max(raw, axis=0, keepdims=True))
    return jnp.sum(e[:layer + 1], axis=0, keepdims=True) / jnp.sum(e, axis=0, keepdims=True)


def _hgrn_gates(q_raw, f_raw, lb):
    q = _silu(q_raw)
    f = lb + (1.0 - lb) * jax.nn.sigmoid(f_raw)
    k = (1.0 - lb) * jax.nn.sigmoid(-f_raw)
    return q, f, k


def _pair_levels(cl):
    t = np.arange(cl)[:, None]
    s = np.arange(cl)[None, :]
    lvl = np.floor(np.log2(np.maximum(t ^ s, 1))).astype(np.int32)
    return np.where(t > s, lvl, np.where(t == s, -1, -2)).astype(np.int32)


def _hgrn_prompt_kernel(layer, q_ref, f_ref, i_ref, g_ref, lvl_ref, lbp_ref, nw_ref, o_ref, sout_ref, st_sc):
    c = pl.program_id(1)
    nchunks = pl.num_programs(1)
    cl = q_ref.shape[0]
    nheads = st_sc.shape[0]

    @pl.when(c == 0)
    def _():
        st_sc[...] = jnp.zeros_like(st_sc)

    lb_all = _hgrn_lower_bound(lbp_ref, layer)
    row = lax.broadcasted_iota(jnp.int32, (cl, LANES), 0)
    lvl = lvl_ref[...]

    for h in range(nheads):
        sl = slice(h * HG_DIM, (h + 1) * HG_DIM)
        q, f, k = _hgrn_gates(q_ref[:, sl], f_ref[:, sl], lb_all[:, sl])
        v = i_ref[:, sl]
        cum = jnp.log(f)
        shift = 1
        while shift < cl:
            cum = cum + jnp.where(row >= shift, pltpu.roll(cum, shift, axis=0), 0.0)
            shift *= 2

        att = jnp.where(lvl == -1, _dot_nt(q.astype(BF16), k.astype(BF16)), 0.0)
        last_of_block = cum
        blk = 1
        level = 0
        while blk < cl:
            odd = (row & blk) != 0
            expo = jnp.where(odd, cum - pltpu.roll(last_of_block, blk, axis=0), last_of_block - cum)
            scaled = (jnp.where(odd, q, k) * jnp.exp(expo)).astype(BF16)
            att = jnp.where(lvl == level, _dot_nt(scaled, scaled), att)
            last_of_block = jnp.where(odd, last_of_block, pltpu.roll(last_of_block, cl - blk, axis=0))
            blk *= 2
            level += 1
        cum_last = last_of_block

        st = st_sc[h]
        o = (_dot(att.astype(BF16), v.astype(BF16))
             + _dot_nt((q * jnp.exp(cum)).astype(BF16), st.astype(BF16)))
        k_end = (k * jnp.exp(cum_last - cum)).astype(BF16)
        st_sc[h] = st * jnp.exp(cum_last) + _dot(v.T.astype(BF16), k_end)
        o_ref[:, sl] = _rms(o, nw_ref[...]) * _silu(g_ref[:, sl])

    @pl.when(c == nchunks - 1)
    def _():
        for h in range(nheads):
            sout_ref[0, h] = st_sc[h].T


def hgrn_prompt(qfig, lb_param, norm_w, layer, batch, seq):
    t, w4 = qfig.shape
    width = w4 // 4
    nheads = width // HG_DIM
    cl = HG_CHUNK
    nchunks = seq // cl

    def col(which):
        return lambda b, c: (b * nchunks + c, which)

    kern = functools.partial(_hgrn_prompt_kernel, layer)
    return pl.pallas_call(
        kern,
        grid=(batch, nchunks),
        in_specs=[pl.BlockSpec((cl, width), col(j)) for j in range(4)]
        + [_resident((cl, cl)), _resident(lb_param.shape), _resident((1, HG_DIM))],
        out_specs=[pl.BlockSpec((cl, width), col(0)),
                   pl.BlockSpec((1, nheads, HG_DIM, HG_DIM), lambda b, c: (b, 0, 0, 0))],
        out_shape=[jax.ShapeDtypeStruct((t, width), F32),
                   jax.ShapeDtypeStruct((batch, nheads, HG_DIM, HG_DIM), F32)],
        scratch_shapes=[pltpu.VMEM((nheads, HG_DIM, HG_DIM), F32)],
        compiler_params=_params(("arbitrary", "arbitrary")),
        name="hgrn_prompt",
    )(qfig, qfig, qfig, qfig, jnp.asarray(_pair_levels(cl)), lb_param, norm_w)


HG_STEP_ROWS = 32


def _hgrn_step_kernel(layer, qfig_ref, lbp_ref, nw_ref, st_ref, o_ref, stout_ref,
                      qt_sc, ft_sc, kt_sc, vt_sc, ot_sc, acc_sc):
    h = pl.program_id(0)
    kc = pl.program_id(1)
    nheads = pl.num_programs(0)
    nkc = pl.num_programs(1)
    width = qt_sc.shape[0]

    @pl.when((h == 0) & (kc == 0))
    def _():
        raw = lbp_ref[...]
        e = jnp.exp(raw - jnp.max(raw, axis=0, keepdims=True))
        lb = jnp.sum(e[:layer + 1], axis=0, keepdims=True) / jnp.sum(e, axis=0, keepdims=True)
        q, f, k = _hgrn_gates(qfig_ref[:, 0:width], qfig_ref[:, width:2 * width], lb)
        qt_sc[...] = q.T
        ft_sc[...] = f.T
        kt_sc[...] = k.T
        vt_sc[...] = qfig_ref[:, 2 * width:3 * width].T

    @pl.when(kc == 0)
    def _():
        acc_sc[...] = jnp.zeros_like(acc_sc)

    base = pl.multiple_of(h * HG_DIM, HG_DIM)
    vt = vt_sc[pl.ds(base, HG_DIM), :]
    acc = acc_sc[...]
    for r in range(HG_STEP_ROWS):
        krow = base + kc * HG_STEP_ROWS + r
        new = (ft_sc[pl.ds(krow, 1), :] * st_ref[:, r * HG_DIM:(r + 1) * HG_DIM].T
               + kt_sc[pl.ds(krow, 1), :] * vt)
        acc = acc + new * qt_sc[pl.ds(krow, 1), :]
        stout_ref[:, r * HG_DIM:(r + 1) * HG_DIM] = new.T
    acc_sc[...] = acc

    @pl.when(kc == nkc - 1)
    def _():
        ot_sc[pl.ds(base, HG_DIM), :] = acc

    @pl.when((h == nheads - 1) & (kc == nkc - 1))
    def _():
        o = ot_sc[...].T
        for hh in range(width // HG_DIM):
            sl = slice(hh * HG_DIM, (hh + 1) * HG_DIM)
            o_ref[:, sl] = _rms(o[:, sl], nw_ref[...]) * _silu(qfig_ref[:, 3 * width + hh * HG_DIM:3 * width + (hh + 1) * HG_DIM])


def hgrn_step(qfig, lb_param, norm_w, layer, state):
    b, w4 = qfig.shape
    width = w4 // 4
    nheads = width // HG_DIM
    nkc = HG_DIM // HG_STEP_ROWS
    blk = HG_STEP_ROWS * HG_DIM
    kern = functools.partial(_hgrn_step_kernel, layer)
    full = lambda shape: pl.BlockSpec(shape, lambda h, kc: (0, 0))
    o, st = pl.pallas_call(
        kern,
        grid=(nheads, nkc),
        in_specs=[full((b, w4)), full(lb_param.shape), full((1, HG_DIM)),
                  pl.BlockSpec((b, blk), lambda h, kc: (0, h * nkc + kc))],
        out_specs=[full((b, width)), pl.BlockSpec((b, blk), lambda h, kc: (0, h * nkc + kc))],
        out_shape=[jax.ShapeDtypeStruct((b, width), F32),
                   jax.ShapeDtypeStruct((b, nheads * HG_DIM * HG_DIM), F32)],
        scratch_shapes=[pltpu.VMEM((width, b), F32)] * 5 + [pltpu.VMEM((HG_DIM, b), F32)],
        compiler_params=_params(("arbitrary", "arbitrary")),
        name="hgrn_step",
    )(qfig, lb_param, norm_w, state.reshape(b, nheads * HG_DIM * HG_DIM))
    return o, st.reshape(b, nheads, HG_DIM, HG_DIM)


def _s5_discretize_kernel(lre_ref, lim_ref, step_ref, bre_ref, bim_ref, are_ref, aim_ref, bbre_ref, bbim_ref):
    lre, lim = lre_ref[...], lim_ref[...]
    step = jnp.exp(step_ref[...])
    mag = jnp.exp(lre * step)
    are = mag * jnp.cos(lim * step)
    aim = mag * jnp.sin(lim * step)
    den = lre * lre + lim * lim
    nr = are - 1.0
    cre = (nr * lre + aim * lim) / den
    cim = (aim * lre - nr * lim) / den
    are_ref[...] = are
    aim_ref[...] = aim
    bre, bim = bre_ref[...], bim_ref[...]
    bbre_ref[...] = cre * bre - cim * bim
    bbim_ref[...] = cre * bim + cim * bre


def s5_discretize(lam_re, lam_im, log_step, b_re_t, b_im_t):
    g, n = lam_re.shape
    c = b_re_t.shape[1]
    return pl.pallas_call(
        _s5_discretize_kernel,
        out_shape=[jax.ShapeDtypeStruct((g, 1, n), F32)] * 2 + [jax.ShapeDtypeStruct((g, c, n), F32)] * 2,
        name="s5_discretize",
    )(lam_re.reshape(g, 1, n), lam_im.reshape(g, 1, n), log_step.reshape(g, 1, 1), b_re_t, b_im_t)


def _complex_power(re, im, n):
    out = None
    while n:
        if n & 1:
            out = (re, im) if out is None else (out[0] * re - out[1] * im, out[0] * im + out[1] * re)
        n >>= 1
        if n:
            re, im = re * re - im * im, 2.0 * re * im
    return out


def _s5_output(xre, xim, u, cre_ref, cim_ref, d_ref):
    y = _dot(xre.astype(BF16), cre_ref[0]) - _dot(xim.astype(BF16), cim_ref[0]) + d_ref[...] * u
    return jax.nn.gelu(y)


S5_ROWS = 256


S5_UNROLL = 4


def _s5_prompt_kernel(u_ref, b_ref, cre_ref, cim_ref, d_ref, are_ref, aim_ref,
                      y_ref, sre_ref, sim_ref, x_sc):
    nrows = u_ref.shape[0]
    ns = S5_BLOCK_STATE
    nt = ns // LANES
    nseg = S5_SEGMENTS
    seg = nrows // nseg

    def project(i, carry):
        r = pl.ds(pl.multiple_of(i * S5_ROWS, S5_ROWS), S5_ROWS)
        bu = _dot(u_ref[r, :].astype(BF16), b_ref[0])
        for k in range(2 * nt):
            x_sc[k, r, :] = bu[:, k * LANES:(k + 1) * LANES]
        return carry
    lax.fori_loop(0, nrows // S5_ROWS, project, 0)

    def lane_tile(ref, k):
        return ref[:, k * LANES:(k + 1) * LANES]

    are = [jnp.broadcast_to(lane_tile(are_ref, k), (nseg, LANES)) for k in range(nt)]
    aim = [jnp.broadcast_to(lane_tile(aim_ref, k), (nseg, LANES)) for k in range(nt)]

    def scan(keep, i, carry):
        rows = pl.ds(pl.multiple_of(i * nseg, nseg), nseg)
        new = []
        for k in range(nt):
            sre, sim = carry[k]
            nre = are[k] * sre - aim[k] * sim + x_sc[k, rows, :]
            nim = are[k] * sim + aim[k] * sre + x_sc[nt + k, rows, :]
            if keep:
                x_sc[k, rows, :] = nre
                x_sc[nt + k, rows, :] = nim
            new.append((nre, nim))
        return tuple(new)

    zero = jnp.zeros((nseg, LANES), F32)
    ends = lax.fori_loop(0, seg, functools.partial(scan, False),
                         tuple((zero, zero) for _ in range(nt)), unroll=S5_UNROLL)

    seg_row = lax.broadcasted_iota(jnp.int32, (nseg, LANES), 0)
    starts = []
    for k in range(nt):
        ere, eim = ends[k]
        full_re, full_im = _complex_power(lane_tile(are_ref, k), lane_tile(aim_ref, k), seg)
        tre = tim = jnp.zeros((1, LANES), F32)
        cre = cim = zero
        for j in range(nseg):
            cre = jnp.where(seg_row == j, tre, cre)
            cim = jnp.where(seg_row == j, tim, cim)
            tre, tim = (ere[j:j + 1, :] + full_re * tre - full_im * tim,
                        eim[j:j + 1, :] + full_re * tim + full_im * tre)
        starts.append((cre, cim))
        sre_ref[0, :, k * LANES:(k + 1) * LANES] = tre
        sim_ref[0, :, k * LANES:(k + 1) * LANES] = tim
    lax.fori_loop(0, seg, functools.partial(scan, True), tuple(starts), unroll=S5_UNROLL)

    def output(i, carry):
        r = pl.ds(pl.multiple_of(i * S5_ROWS, S5_ROWS), S5_ROWS)
        xre = jnp.concatenate([x_sc[k, r, :] for k in range(nt)], axis=1)
        xim = jnp.concatenate([x_sc[nt + k, r, :] for k in range(nt)], axis=1)
        y_ref[r, :] = _s5_output(xre, xim, u_ref[r, :], cre_ref, cim_ref, d_ref)
        return carry
    lax.fori_loop(0, nrows // S5_ROWS, output, 0)


def s5_prompt(u, b_exp, c_re_exp, c_im_exp, d_skip, a_re, a_im, batch, seq):
    t, d = u.shape
    nblk = d // LANES
    ns = S5_BLOCK_STATE
    return pl.pallas_call(
        _s5_prompt_kernel,
        grid=(nblk, batch),
        in_specs=[pl.BlockSpec((seq, LANES), lambda j, b: (b, j)),
                  pl.BlockSpec((1, LANES, 2 * ns), lambda j, b: (j, 0, 0)),
                  pl.BlockSpec((1, ns, LANES), lambda j, b: (j, 0, 0)),
                  pl.BlockSpec((1, ns, LANES), lambda j, b: (j, 0, 0)),
                  pl.BlockSpec((1, LANES), lambda j, b: (0, j)),
                  pl.BlockSpec((1, ns), lambda j, b: (0, j)),
                  pl.BlockSpec((1, ns), lambda j, b: (0, j))],
        out_specs=[pl.BlockSpec((seq, LANES), lambda j, b: (b, j)),
                   pl.BlockSpec((1, 1, ns), lambda j, b: (b, 0, j)),
                   pl.BlockSpec((1, 1, ns), lambda j, b: (b, 0, j))],
        out_shape=[jax.ShapeDtypeStruct((t, d), F32),
                   jax.ShapeDtypeStruct((batch, 1, nblk * ns), F32),
                   jax.ShapeDtypeStruct((batch, 1, nblk * ns), F32)],
        scratch_shapes=[pltpu.VMEM((2 * ns // LANES, seq, LANES), F32)],
        compiler_params=_params(("arbitrary", "arbitrary")),
        name="s5_prompt",
    )(u, b_exp, c_re_exp, c_im_exp, d_skip, a_re, a_im)


def _s5_step_kernel(u_ref, b_ref, cre_ref, cim_ref, d_ref, are_ref, aim_ref, s0re_ref, s0im_ref,
                    y_ref, sre_ref, sim_ref):
    ns = S5_BLOCK_STATE
    u = u_ref[...]
    bu = _dot(u.astype(BF16), b_ref[0])
    are, aim = are_ref[...], aim_ref[...]
    s0re, s0im = s0re_ref[...], s0im_ref[...]
    xre = bu[:, 0:ns] + (are * s0re - aim * s0im)
    xim = bu[:, ns:2 * ns] + (are * s0im + aim * s0re)
    sre_ref[...] = xre
    sim_ref[...] = xim
    y_ref[...] = _s5_output(xre, xim, u, cre_ref, cim_ref, d_ref)


def s5_step(u, b_exp, c_re_exp, c_im_exp, d_skip, a_re, a_im, s0_re, s0_im):
    b, d = u.shape
    nblk = d // LANES
    ns = S5_BLOCK_STATE
    return pl.pallas_call(
        _s5_step_kernel,
        grid=(nblk,),
        in_specs=[pl.BlockSpec((b, LANES), lambda j: (0, j)),
                  pl.BlockSpec((1, LANES, 2 * ns), lambda j: (j, 0, 0)),
                  pl.BlockSpec((1, ns, LANES), lambda j: (j, 0, 0)),
                  pl.BlockSpec((1, ns, LANES), lambda j: (j, 0, 0)),
                  pl.BlockSpec((1, LANES), lambda j: (0, j)),
                  pl.BlockSpec((1, ns), lambda j: (0, j)),
                  pl.BlockSpec((1, ns), lambda j: (0, j)),
                  pl.BlockSpec((b, ns), lambda j: (0, j)),
                  pl.BlockSpec((b, ns), lambda j: (0, j))],
        out_specs=[pl.BlockSpec((b, LANES), lambda j: (0, j)),
                   pl.BlockSpec((b, ns), lambda j: (0, j)),
                   pl.BlockSpec((b, ns), lambda j: (0, j))],
        out_shape=[jax.ShapeDtypeStruct((b, d), F32),
                   jax.ShapeDtypeStruct((b, nblk * ns), F32),
                   jax.ShapeDtypeStruct((b, nblk * ns), F32)],
        compiler_params=_params(("parallel",)),
        name="s5_step",
    )(u, b_exp, c_re_exp, c_im_exp, d_skip, a_re, a_im, s0_re, s0_im)


def _block_diag(per_group):
    ngroups, r, c = per_group.shape
    p = S5_GROUPS_PER_BLOCK
    eye = jnp.eye(p, dtype=per_group.dtype)
    tiles = per_group.reshape(ngroups // p, p, r, c)
    return jnp.einsum('bjrc,jk->bjrkc', tiles, eye).reshape(ngroups // p, p * r, p * c)


def kernel(x_prompt, x_sample, state_ssm, state_ssm_conv, state_hgrn, state_s5_re, state_s5_im, state_ffn_conv, norm_mix_pre, norm_mix_post, norm_ffn_pre, norm_ffn_post, ab_in_w, ssm_conv_w, ssm_conv_b, ssm_dt_bias, ssm_a_log, ssm_d, ssm_norm_w, hgrn_lb, hgrn_norm_w, ab_out_w, s5_lam_re, s5_lam_im, s5_log_step, s5_b_re, s5_b_im, s5_c_re, s5_c_im, s5_d, s5_glu_w, ffn_up_w, ffn_conv_w, ffn_conv_b, ffn_down_w):
    batch, seq, d = x_prompt.shape
    dec_batch = x_sample.shape[0]
    depth = norm_mix_pre.shape[0]
    conv_dim = ssm_conv_w.shape[2]
    nheads = ssm_a_log.shape[1]
    d_inner = nheads * SSM_HEAD_DIM
    hg_width = hgrn_lb.shape[1]
    ffn_dim = ffn_down_w.shape[1]
    tm = 256

    def row(v):
        return v.reshape(1, -1)

    def pad_lanes(v):
        return jnp.pad(v, ((0, 0), (0, LANES - v.shape[1])))

    groups = [(x_prompt.reshape(batch * seq, d), True), (x_sample.reshape(dec_batch, d), False)]
    xs = [g[0] for g in groups]
    out = {'ssm': [[], []], 'sconv': [[], []], 'hg': [[], []], 's5r': [[], []], 's5i': [[], []], 'fconv': [[], []]}

    for l in range(depth):
        e = l // 2
        if l % 2 == 0:
            w_in = ab_in_w[e]
            o_xbc = d_inner
            o_dt = o_xbc + conv_dim
            o_q = o_dt + nheads
            w_z = w_in[:, :o_xbc].astype(BF16)
            w_xbc = w_in[:, o_xbc:o_dt].astype(BF16)
            w_dt = pad_lanes(w_in[:, o_dt:o_q]).astype(BF16)
            w_qfig = w_in[:, o_q:].astype(BF16)
            w_out = ab_out_w[e].astype(BF16)
            dt_bias = pad_lanes(row(ssm_dt_bias[e]))
            a_log = pad_lanes(row(ssm_a_log[e]))
            d_skip = row(jnp.repeat(ssm_d[e], SSM_HEAD_DIM))
            for gi, (_, is_prompt) in enumerate(groups):
                x = xs[gi]
                t = x.shape[0]
                z, xbc, dt, qfig = norm_matmul(x, row(norm_mix_pre[l]), [w_z, w_xbc, w_dt, w_qfig], min(tm, t))
                if is_prompt:
                    y, ssm_new = ssd_prompt(z, xbc, dt, ssm_conv_w[e], row(ssm_conv_b[e]), dt_bias, a_log,
                                            d_skip, row(ssm_norm_w[e]), batch, seq)
                    o, hg_new = hgrn_prompt(qfig, hgrn_lb, row(hgrn_norm_w[e]), e, batch, seq)
                    sconv_new = xbc.reshape(batch, seq, conv_dim)[:, seq - (SSM_CONV - 1):]
                else:
                    hist = [state_ssm_conv[e][:, k] for k in range(SSM_CONV - 1)]
                    y, ssm_new = ssd_step(z, xbc, hist, dt, ssm_conv_w[e], row(ssm_conv_b[e]), dt_bias, a_log,
                                          d_skip, row(ssm_norm_w[e]), state_ssm[e])
                    o, hg_new = hgrn_step(qfig, hgrn_lb, row(hgrn_norm_w[e]), e, state_hgrn[e])
                    sconv_new = jnp.stack(hist[1:] + [xbc], axis=1)
                out['ssm'][gi].append(ssm_new)
                out['hg'][gi].append(hg_new)
                out['sconv'][gi].append(sconv_new)
                xs[gi] = matmul_norm_residual([y, o], [w_out[:d_inner], w_out[d_inner:]],
                                              row(norm_mix_post[l]), x, min(tm, t))
        else:
            ngroups = s5_lam_re.shape[1]
            a_re, a_im, bb_re, bb_im = s5_discretize(
                s5_lam_re[e], s5_lam_im[e], s5_log_step[e],
                jnp.swapaxes(s5_b_re[e], 1, 2), jnp.swapaxes(s5_b_im[e], 1, 2))
            a_re = a_re.reshape(1, ngroups * S5_STATE)
            a_im = a_im.reshape(1, ngroups * S5_STATE)
            b_exp = jnp.concatenate([_block_diag(bb_re), _block_diag(bb_im)], axis=2).astype(BF16)
            c_re_exp = _block_diag(jnp.swapaxes(s5_c_re[e], 1, 2)).astype(BF16)
            c_im_exp = _block_diag(jnp.swapaxes(s5_c_im[e], 1, 2)).astype(BF16)
            glu_w = s5_glu_w[e].astype(BF16)
            for gi, (_, is_prompt) in enumerate(groups):
                x = xs[gi]
                t = x.shape[0]
                if is_prompt:
                    seg = seq // S5_SEGMENTS
                    u = rmsnorm(x, row(norm_mix_pre[l]), seg, interleave=S5_SEGMENTS)
                    yg, s_re, s_im = s5_prompt(u, b_exp, c_re_exp, c_im_exp, row(s5_d[e]), a_re, a_im, batch, seq)
                    xs[gi] = matmul_norm_residual([yg], [glu_w], row(norm_mix_post[l]), x, seg, glu=True,
                                                  interleave=S5_SEGMENTS)
                else:
                    u = rmsnorm(x, row(norm_mix_pre[l]), t)
                    yg, s_re, s_im = s5_step(u, b_exp, c_re_exp, c_im_exp, row(s5_d[e]), a_re, a_im,
                                             state_s5_re[e].reshape(t, -1), state_s5_im[e].reshape(t, -1))
                    xs[gi] = matmul_norm_residual([yg], [glu_w], row(norm_mix_post[l]), x, t, glu=True)
                out['s5r'][gi].append(s_re.reshape(-1, ngroups, S5_STATE))
                out['s5i'][gi].append(s_im.reshape(-1, ngroups, S5_STATE))

        up_w = ffn_up_w[l].astype(BF16)
        down_w = ffn_down_w[l].astype(BF16)
        for gi, (_, is_prompt) in enumerate(groups):
            x = xs[gi]
            t = x.shape[0]
            (up,) = norm_matmul(x, row(norm_ffn_pre[l]), [up_w], min(tm, t))
            if is_prompt:
                xs[gi] = ffn_tail_seq(up, ffn_conv_w[l], row(ffn_conv_b[l]), down_w, row(norm_ffn_post[l]), x, tm, seq)
                fconv_new = up.reshape(batch, seq, 2 * ffn_dim)[:, seq - (FFN_CONV - 1):]
            else:
                old = state_ffn_conv[l]
                xs[gi] = ffn_tail_step(up, old[:, 0], old[:, 1], ffn_conv_w[l], row(ffn_conv_b[l]), down_w,
                                       row(norm_ffn_post[l]), x)
                fconv_new = jnp.stack([old[:, 1], up], axis=1)
            out['fconv'][gi].append(fconv_new)

    y_prompt = xs[0].reshape(batch, seq, d)
    y_sample = xs[1].reshape(dec_batch, 1, d)
    states = []
    for gi in range(2):
        states += [jnp.stack(out[k][gi]) for k in ('ssm', 'sconv', 'hg', 's5r', 's5i', 'fconv')]
    return (y_prompt, y_sample, *states)
```

```python
import functools
import math

import jax
import jax.numpy as jnp
import numpy as np
from jax import lax
from jax.experimental import pallas as pl
from jax.experimental.pallas import tpu as pltpu

F32 = jnp.float32
BF16 = jnp.bfloat16
EPS = 1e-6

LANES = 128
SUBLANES = 8
VMEM_LIMIT_BYTES = 56 * 1024 * 1024

SSM_HEAD_DIM = 64
SSM_GROUPS = 2
SSM_STATE = 64
SSM_CONV = 4
SSM_CHUNK = 128
HG_DIM = 128
HG_CHUNK = 128
S5_GROUP = 16
S5_STATE = 64
S5_GROUPS_PER_BLOCK = LANES // S5_GROUP
S5_BLOCK_STATE = S5_GROUPS_PER_BLOCK * S5_STATE
S5_SEGMENTS = SUBLANES
FFN_CONV = 3


def _params(sem):
    return pltpu.CompilerParams(dimension_semantics=sem, vmem_limit_bytes=VMEM_LIMIT_BYTES)


def _resident(shape):
    nd = len(shape)
    return pl.BlockSpec(shape, lambda *_: (0,) * nd, pipeline_mode=pl.Buffered(1))


def _rms(x, w):
    return x * lax.rsqrt(jnp.mean(x * x, axis=-1, keepdims=True) + EPS) * w


def _silu(x):
    return x * jax.nn.sigmoid(x)


def _dot(a, b):
    return jnp.dot(a, b, preferred_element_type=F32)


def _dot_nt(a, b):
    return lax.dot_general(a, b, (((1,), (1,)), ((), ())), preferred_element_type=F32)


def _dot_tn(a, b):
    return lax.dot_general(a, b, (((0,), (0,)), ((), ())), preferred_element_type=F32)


def _norm_matmul_kernel(n_out, x_ref, g_ref, *refs):
    w_refs, o_refs = refs[:n_out], refs[n_out:]
    xn = _rms(x_ref[...], g_ref[...]).astype(BF16)
    for w_ref, o_ref in zip(w_refs, o_refs):
        n = w_ref.shape[1]
        for c0 in range(0, n, 512):
            c1 = min(n, c0 + 512)
            o_ref[:, c0:c1] = _dot(xn, w_ref[:, c0:c1])


def norm_matmul(x, g, ws, tm):
    t, d = x.shape
    kern = functools.partial(_norm_matmul_kernel, len(ws))
    return pl.pallas_call(
        kern,
        grid=(t // tm,),
        in_specs=[pl.BlockSpec((tm, d), lambda i: (i, 0)), _resident((1, d))]
        + [_resident(w.shape) for w in ws],
        out_specs=[pl.BlockSpec((tm, w.shape[1]), lambda i: (i, 0)) for w in ws],
        out_shape=[jax.ShapeDtypeStruct((t, w.shape[1]), F32) for w in ws],
        compiler_params=_params(("parallel",)),
        name="norm_matmul",
    )(x, g, *ws)


def _rmsnorm_kernel(x_ref, g_ref, o_ref):
    o_ref[...] = _rms(x_ref[...], g_ref[...])


def rmsnorm(x, g, tm):
    t, d = x.shape
    return pl.pallas_call(
        _rmsnorm_kernel,
        grid=(t // tm,),
        in_specs=[pl.BlockSpec((tm, d), lambda i: (i, 0)), _resident((1, d))],
        out_specs=pl.BlockSpec((tm, d), lambda i: (i, 0)),
        out_shape=jax.ShapeDtypeStruct((t, d), F32),
        compiler_params=_params(("parallel",)),
        name="rmsnorm",
    )(x, g)


def _matmul_norm_residual_kernel(n_in, glu, *refs):
    a_refs, w_refs = refs[:n_in], refs[n_in:2 * n_in]
    g_ref, x_ref, o_ref = refs[2 * n_in:]
    acc = None
    for a_ref, w_ref in zip(a_refs, w_refs):
        part = _dot(a_ref[...].astype(BF16), w_ref[...])
        acc = part if acc is None else acc + part
    if glu:
        d = acc.shape[1] // 2
        acc = acc[:, :d] * jax.nn.sigmoid(acc[:, d:])
    o_ref[...] = x_ref[...] + _rms(acc, g_ref[...])


def matmul_norm_residual(a_list, w_list, g, x, tm, glu=False):
    t, d = x.shape
    kern = functools.partial(_matmul_norm_residual_kernel, len(a_list), glu)
    return pl.pallas_call(
        kern,
        grid=(t // tm,),
        in_specs=[pl.BlockSpec((tm, a.shape[1]), lambda i: (i, 0)) for a in a_list]
        + [_resident(w.shape) for w in w_list]
        + [_resident((1, d)), pl.BlockSpec((tm, d), lambda i: (i, 0))],
        out_specs=pl.BlockSpec((tm, d), lambda i: (i, 0)),
        out_shape=jax.ShapeDtypeStruct((t, d), F32),
        compiler_params=_params(("parallel",)),
        name="matmul_norm_residual",
    )(*a_list, *w_list, g, x)


FFN_CHUNK = 256


def _ffn_tail(up_ref, xm2_of, xm1_of, cw_ref, cb_ref, dw_ref, g_ref, x_ref, o_ref):
    f = dw_ref.shape[0]
    acc = jnp.zeros(o_ref.shape, F32)
    for c0 in range(0, f, FFN_CHUNK):
        c1 = c0 + FFN_CHUNK
        halves = []
        for off in (0, f):
            a, b = c0 + off, c1 + off
            halves.append(cb_ref[:, a:b] + xm2_of(a, b) * cw_ref[0:1, a:b]
                          + xm1_of(a, b) * cw_ref[1:2, a:b] + up_ref[:, a:b] * cw_ref[2:3, a:b])
        act = (halves[0] * _silu(halves[1])).astype(BF16)
        acc = acc + _dot(act, dw_ref[c0:c1, :])
    o_ref[...] = x_ref[...] + _rms(acc, g_ref[...])


def _ffn_seq_kernel(tiles_per_seq, x_ref, g1_ref, uw_ref, cw_ref, cb_ref, dw_ref, g2_ref,
                    o_ref, tail_ref, carry_sc):
    tm = x_ref.shape[0]
    f = dw_ref.shape[0]

    @pl.when(pl.program_id(0) % tiles_per_seq == 0)
    def _():
        carry_sc[...] = jnp.zeros_like(carry_sc)

    x = x_ref[...]
    xn = _rms(x, g1_ref[...]).astype(BF16)
    top_row = lax.broadcasted_iota(jnp.int32, (SUBLANES, FFN_CHUNK), 0)

    def earlier(up, prev, k):
        rolled = pltpu.roll(up, k, axis=0)
        top = jnp.where(top_row < k, pltpu.roll(prev, k, axis=0), rolled[:SUBLANES])
        return jnp.concatenate([top, rolled[SUBLANES:]], axis=0)

    def project(c0):
        return [_dot(xn, uw_ref[:, c0 + off:c0 + off + FFN_CHUNK]) for off in (0, f)]

    acc = jnp.zeros(o_ref.shape, F32)
    ups_next = project(0)
    act_prev = None
    for c0 in range(0, f, FFN_CHUNK):
        ups = ups_next
        if c0 + FFN_CHUNK < f:
            ups_next = project(c0 + FFN_CHUNK)
        if act_prev is not None:
            acc = acc + _dot(act_prev, dw_ref[c0 - FFN_CHUNK:c0, :])
        halves = []
        for up, off in zip(ups, (0, f)):
            a, b = c0 + off, c0 + off + FFN_CHUNK
            prev = carry_sc[:, a:b]
            carry_sc[:, a:b] = up[tm - SUBLANES:, :]
            tail_ref[0, :, a:b] = up[tm - SUBLANES:, :]
            halves.append(cb_ref[:, a:b] + earlier(up, prev, 2) * cw_ref[0:1, a:b]
                          + earlier(up, prev, 1) * cw_ref[1:2, a:b] + up * cw_ref[2:3, a:b])
        act_prev = (halves[0] * _silu(halves[1])).astype(BF16)
    acc = acc + _dot(act_prev, dw_ref[f - FFN_CHUNK:f, :])
    o_ref[...] = x + _rms(acc, g2_ref[...])


def ffn_seq(x, g_pre, up_w, conv_w, conv_b, down_w, g_post, tm, seq):
    t, d = x.shape
    f2 = up_w.shape[1]
    tiles_per_seq = seq // tm
    kern = functools.partial(_ffn_seq_kernel, tiles_per_seq)
    return pl.pallas_call(
        kern,
        grid=(t // tm,),
        in_specs=[pl.BlockSpec((tm, d), lambda i: (i, 0)), _resident((1, d)), _resident(up_w.shape),
                  _resident(conv_w.shape), _resident((1, f2)), _resident(down_w.shape), _resident((1, d))],
        out_specs=[pl.BlockSpec((tm, d), lambda i: (i, 0)),
                   pl.BlockSpec((1, SUBLANES, f2), lambda i: (i // tiles_per_seq, 0, 0))],
        out_shape=[jax.ShapeDtypeStruct((t, d), F32),
                   jax.ShapeDtypeStruct((t // seq, SUBLANES, f2), F32)],
        scratch_shapes=[pltpu.VMEM((SUBLANES, f2), F32)],
        compiler_params=_params(("arbitrary",)),
        name="ffn_seq",
    )(x, g_pre, up_w, conv_w, conv_b, down_w, g_post)


def _ffn_step_kernel(up_ref, xm2_ref, xm1_ref, cw_ref, cb_ref, dw_ref, g_ref, x_ref, o_ref):
    _ffn_tail(up_ref, lambda a, b: xm2_ref[:, a:b], lambda a, b: xm1_ref[:, a:b],
              cw_ref, cb_ref, dw_ref, g_ref, x_ref, o_ref)


def ffn_tail_step(up, xm2, xm1, conv_w, conv_b, down_w, g, x):
    t, d = x.shape
    f2 = up.shape[1]
    return pl.pallas_call(
        _ffn_step_kernel,
        grid=(1,),
        in_specs=[_resident((t, f2))] * 3
        + [_resident(conv_w.shape), _resident((1, f2)), _resident(down_w.shape),
           _resident((1, d)), _resident((t, d))],
        out_specs=pl.BlockSpec((t, d), lambda i: (0, 0)),
        out_shape=jax.ShapeDtypeStruct((t, d), F32),
        compiler_params=_params(("arbitrary",)),
        name="ffn_tail_step",
    )(up, xm2, xm1, conv_w, conv_b, down_w, g, x)


def _ssd_gate_norm(y, xs, z, dsk_ref, nw_ref):
    y = y + dsk_ref[...] * xs
    return _rms(y * _silu(z), nw_ref[...])


def _ssd_prompt_kernel(z_ref, xbc_ref, dt_ref, cw_ref, cb_ref, dtb_ref, alog_ref, dsk_ref, nw_ref,
                       y_ref, hout_ref, hist_sc, h_sc, xp_sc, y_sc):
    c = pl.program_id(1)
    nchunks = pl.num_programs(1)
    cl = xbc_ref.shape[0]
    nheads, hp, hn = h_sc.shape
    d_inner = nheads * hp
    heads_per_group = nheads // SSM_GROUPS

    @pl.when(c == 0)
    def _():
        hist_sc[...] = jnp.zeros_like(hist_sc)
        h_sc[...] = jnp.zeros_like(h_sc)

    raw = xbc_ref[...]
    xp_sc[0:SUBLANES, :] = hist_sc[...]
    xp_sc[SUBLANES:, :] = raw
    hist_sc[...] = raw[cl - SUBLANES:, :]
    conv = cb_ref[...]
    for k in range(SSM_CONV):
        conv = conv + xp_sc[pl.ds(SUBLANES - (SSM_CONV - 1) + k, cl), :] * cw_ref[k:k + 1, :]
    act = _silu(conv)
    xs = act[:, :d_inner]
    bm = act[:, d_inner:d_inner + SSM_GROUPS * SSM_STATE]
    cm = act[:, d_inner + SSM_GROUPS * SSM_STATE:]

    dt = jax.nn.softplus(dt_ref[...] + dtb_ref[...])
    da = dt * (-jnp.exp(alog_ref[...]))
    ti = lax.broadcasted_iota(jnp.int32, (cl, cl), 0)
    si = lax.broadcasted_iota(jnp.int32, (cl, cl), 1)
    tril = ti >= si
    cum_col = jnp.dot(tril.astype(F32), da, preferred_element_type=F32,
                      precision=lax.Precision.HIGHEST)
    cum_row = cum_col.T
    dt_row = dt.T

    cbs = []
    for g in range(SSM_GROUPS):
        sl = slice(g * SSM_STATE, (g + 1) * SSM_STATE)
        cbs.append(_dot_nt(cm[:, sl].astype(BF16), bm[:, sl].astype(BF16)))

    for h in range(nheads):
        g = h // heads_per_group
        sl = slice(g * SSM_STATE, (g + 1) * SSM_STATE)
        bm_g = bm[:, sl].astype(BF16)
        cm_g = cm[:, sl].astype(BF16)
        cc = cum_col[:, h:h + 1]
        cr = cum_row[h:h + 1, :]
        last = cum_col[cl - 1:cl, h:h + 1]
        seg = jnp.exp(jnp.where(tril, cc - cr, -jnp.inf))
        wts = cbs[g] * seg * dt_row[h:h + 1, :]
        xh = xs[:, h * hp:(h + 1) * hp]
        hst = h_sc[h]
        yh = _dot(wts.astype(BF16), xh.astype(BF16))
        yh = yh + _dot_nt(cm_g, hst.astype(BF16)) * jnp.exp(cc)
        to_end = jnp.exp(last - cc) * dt[:, h:h + 1]
        h_sc[h] = jnp.exp(last) * hst + _dot_tn((xh * to_end).astype(BF16), bm_g)
        y_sc[:, h * hp:(h + 1) * hp] = yh

    y_ref[...] = _ssd_gate_norm(y_sc[...], xs, z_ref[...], dsk_ref, nw_ref)

    @pl.when(c == nchunks - 1)
    def _():
        hout_ref[0] = h_sc[...]


def ssd_prompt(z, xbc, dt, conv_w, conv_b, dt_bias, a_log, d_skip, norm_w, batch, seq):
    t, d_inner = z.shape
    conv_dim = xbc.shape[1]
    nheads = d_inner // SSM_HEAD_DIM
    cl = SSM_CHUNK
    nchunks = seq // cl
    row = lambda b, c: (b * nchunks + c, 0)
    return pl.pallas_call(
        _ssd_prompt_kernel,
        grid=(batch, nchunks),
        in_specs=[pl.BlockSpec((cl, d_inner), row), pl.BlockSpec((cl, conv_dim), row),
                  pl.BlockSpec((cl, LANES), row),
                  _resident(conv_w.shape), _resident((1, conv_dim)), _resident((1, LANES)),
                  _resident((1, LANES)), _resident((1, d_inner)), _resident((1, d_inner))],
        out_specs=[pl.BlockSpec((cl, d_inner), row),
                   pl.BlockSpec((1, nheads, SSM_HEAD_DIM, SSM_STATE), lambda b, c: (b, 0, 0, 0))],
        out_shape=[jax.ShapeDtypeStruct((t, d_inner), F32),
                   jax.ShapeDtypeStruct((batch, nheads, SSM_HEAD_DIM, SSM_STATE), F32)],
        scratch_shapes=[pltpu.VMEM((SUBLANES, conv_dim), F32),
                        pltpu.VMEM((nheads, SSM_HEAD_DIM, SSM_STATE), F32),
                        pltpu.VMEM((cl + SUBLANES, conv_dim), F32),
                        pltpu.VMEM((cl, d_inner), F32)],
        compiler_params=_params(("arbitrary", "arbitrary")),
        name="ssd_prompt",
    )(z, xbc, dt, conv_w, conv_b, dt_bias, a_log, d_skip, norm_w)


def _ssd_step_kernel(z_ref, xnew_ref, h0_ref, h1_ref, h2_ref, dt_ref, cw_ref, cb_ref, dtb_ref, alog_ref,
                     dsk_ref, nw_ref, st_ref, y_ref, stout_ref,
                     xs_sc, xst_sc, bmt_sc, cmt_sc, dtt_sc, dat_sc, yt_sc):
    h = pl.program_id(0)
    nheads = pl.num_programs(0)
    d_inner = xs_sc.shape[1]
    hp, hn = SSM_HEAD_DIM, SSM_STATE
    heads_per_group = d_inner // hp // SSM_GROUPS

    @pl.when(h == 0)
    def _():
        conv = (cb_ref[...] + h0_ref[...] * cw_ref[0:1, :] + h1_ref[...] * cw_ref[1:2, :]
                + h2_ref[...] * cw_ref[2:3, :] + xnew_ref[...] * cw_ref[3:4, :])
        act = _silu(conv)
        xs = act[:, :d_inner]
        dt = jax.nn.softplus(dt_ref[...] + dtb_ref[...])
        xs_sc[...] = xs
        xst_sc[...] = xs.T
        bmt_sc[...] = act[:, d_inner:d_inner + SSM_GROUPS * hn].T
        cmt_sc[...] = act[:, d_inner + SSM_GROUPS * hn:].T
        dtt_sc[...] = dt.T
        dat_sc[...] = jnp.exp(dt * (-jnp.exp(alog_ref[...]))).T

    g = h // heads_per_group
    bmt = bmt_sc[pl.ds(pl.multiple_of(g * hn, hn), hn), :]
    cmt = cmt_sc[pl.ds(pl.multiple_of(g * hn, hn), hn), :]
    da = dat_sc[pl.ds(h, 1), :]
    dtx = xst_sc[pl.ds(pl.multiple_of(h * hp, hp), hp), :] * dtt_sc[pl.ds(h, 1), :]
    rows_per_tile = LANES // hn
    for j in range(hp // rows_per_tile):
        tile = st_ref[:, j * LANES:(j + 1) * LANES].T
        news = []
        for r in range(rows_per_tile):
            p = j * rows_per_tile + r
            new = da * tile[r * hn:(r + 1) * hn, :] + dtx[p:p + 1, :] * bmt
            yt_sc[pl.ds(h * hp + p, 1), :] = jnp.sum(new * cmt, axis=0, keepdims=True)
            news.append(new)
        stout_ref[:, j * LANES:(j + 1) * LANES] = jnp.concatenate(news, axis=0).T

    @pl.when(h == nheads - 1)
    def _():
        y_ref[...] = _ssd_gate_norm(yt_sc[...].T, xs_sc[...], z_ref[...], dsk_ref, nw_ref)


def ssd_step(z, xnew, hist, dt, conv_w, conv_b, dt_bias, a_log, d_skip, norm_w, state):
    b, d_inner = z.shape
    conv_dim = xnew.shape[1]
    nheads = d_inner // SSM_HEAD_DIM
    per_head = SSM_HEAD_DIM * SSM_STATE
    full = lambda shape: pl.BlockSpec(shape, lambda h: (0, 0))
    y, st = pl.pallas_call(
        _ssd_step_kernel,
        grid=(nheads,),
        in_specs=[full((b, d_inner))] + [full((b, conv_dim))] * 4 + [full((b, LANES))]
        + [full(conv_w.shape), full((1, conv_dim)), full((1, LANES)), full((1, LANES)),
           full((1, d_inner)), full((1, d_inner)),
           pl.BlockSpec((b, per_head), lambda h: (0, h))],
        out_specs=[full((b, d_inner)), pl.BlockSpec((b, per_head), lambda h: (0, h))],
        out_shape=[jax.ShapeDtypeStruct((b, d_inner), F32),
                   jax.ShapeDtypeStruct((b, nheads * per_head), F32)],
        scratch_shapes=[pltpu.VMEM((b, d_inner), F32), pltpu.VMEM((d_inner, b), F32),
                        pltpu.VMEM((SSM_GROUPS * SSM_STATE, b), F32),
                        pltpu.VMEM((SSM_GROUPS * SSM_STATE, b), F32),
                        pltpu.VMEM((LANES, b), F32), pltpu.VMEM((LANES, b), F32),
                        pltpu.VMEM((d_inner, b), F32)],
        compiler_params=_params(("arbitrary",)),
        name="ssd_step",
    )(z, xnew, hist[0], hist[1], hist[2], dt, conv_w, conv_b, dt_bias, a_log, d_skip, norm_w,
      state.reshape(b, nheads * per_head))
    return y, st.reshape(b, nheads, SSM_HEAD_DIM, SSM_STATE)


def _hgrn_lower_bound(lbp_ref, layer):
    raw = lbp_ref[...]
    e = jnp.exp(raw - jnp.max(raw, axis=0, keepdims=True))
    return jnp.sum(e[:layer + 1], axis=0, keepdims=True) / jnp.sum(e, axis=0, keepdims=True)


def _hgrn_gates(q_raw, f_raw, lb):
    q = _silu(q_raw)
    f = lb + (1.0 - lb) * jax.nn.sigmoid(f_raw)
    k = (1.0 - lb) * jax.nn.sigmoid(-f_raw)
    return q, f, k


def _pair_levels(cl):
    t = np.arange(cl)[:, None]
    s = np.arange(cl)[None, :]
    lvl = np.floor(np.log2(np.maximum(t ^ s, 1))).astype(np.int32)
    return np.where(t > s, lvl, np.where(t == s, -1, -2)).astype(np.int32)


def _hgrn_prompt_kernel(layer, q_ref, f_ref, i_ref, g_ref, lvl_ref, lbp_ref, nw_ref, o_ref, sout_ref, st_sc):
    c = pl.program_id(1)
    nchunks = pl.num_programs(1)
    cl = q_ref.shape[0]
    nheads = st_sc.shape[0]

    @pl.when(c == 0)
    def _():
        st_sc[...] = jnp.zeros_like(st_sc)

    lb_all = _hgrn_lower_bound(lbp_ref, layer)
    row = lax.broadcasted_iota(jnp.int32, (cl, LANES), 0)
    lvl = lvl_ref[...]

    for h in range(nheads):
        sl = slice(h * HG_DIM, (h + 1) * HG_DIM)
        q, f, k = _hgrn_gates(q_ref[:, sl], f_ref[:, sl], lb_all[:, sl])
        v = i_ref[:, sl]
        cum = jnp.log(f)
        shift = 1
        while shift < cl:
            cum = cum + jnp.where(row >= shift, pltpu.roll(cum, shift, axis=0), 0.0)
            shift *= 2

        att = jnp.where(lvl == -1, _dot_nt(q.astype(BF16), k.astype(BF16)), 0.0)
        last_of_block = cum
        blk = 1
        level = 0
        while blk < cl:
            odd = (row & blk) != 0
            expo = jnp.where(odd, cum - pltpu.roll(last_of_block, blk, axis=0), last_of_block - cum)
            scaled = (jnp.where(odd, q, k) * jnp.exp(expo)).astype(BF16)
            att = jnp.where(lvl == level, _dot_nt(scaled, scaled), att)
            last_of_block = jnp.where(odd, last_of_block, pltpu.roll(last_of_block, cl - blk, axis=0))
            blk *= 2
            level += 1
        cum_last = last_of_block

        st = st_sc[h]
        o = (_dot(att.astype(BF16), v.astype(BF16))
             + _dot_nt((q * jnp.exp(cum)).astype(BF16), st.astype(BF16)))
        k_end = (k * jnp.exp(cum_last - cum)).astype(BF16)
        st_sc[h] = st * jnp.exp(cum_last) + _dot(v.T.astype(BF16), k_end)
        o_ref[:, sl] = _rms(o, nw_ref[...]) * _silu(g_ref[:, sl])

    @pl.when(c == nchunks - 1)
    def _():
        for h in range(nheads):
            sout_ref[0, h] = st_sc[h].T


def hgrn_prompt(qfig, lb_param, norm_w, layer, batch, seq):
    t, w4 = qfig.shape
    width = w4 // 4
    nheads = width // HG_DIM
    cl = HG_CHUNK
    nchunks = seq // cl

    def col(which):
        return lambda b, c: (b * nchunks + c, which)

    kern = functools.partial(_hgrn_prompt_kernel, layer)
    return pl.pallas_call(
        kern,
        grid=(batch, nchunks),
        in_specs=[pl.BlockSpec((cl, width), col(j)) for j in range(4)]
        + [_resident((cl, cl)), _resident(lb_param.shape), _resident((1, HG_DIM))],
        out_specs=[pl.BlockSpec((cl, width), col(0)),
                   pl.BlockSpec((1, nheads, HG_DIM, HG_DIM), lambda b, c: (b, 0, 0, 0))],
        out_shape=[jax.ShapeDtypeStruct((t, width), F32),
                   jax.ShapeDtypeStruct((batch, nheads, HG_DIM, HG_DIM), F32)],
        scratch_shapes=[pltpu.VMEM((nheads, HG_DIM, HG_DIM), F32)],
        compiler_params=_params(("arbitrary", "arbitrary")),
        name="hgrn_prompt",
    )(qfig, qfig, qfig, qfig, jnp.asarray(_pair_levels(cl)), lb_param, norm_w)


HG_STEP_ROWS = 32


def _hgrn_step_kernel(layer, qfig_ref, lbp_ref, nw_ref, st_ref, o_ref, stout_ref,
                      qt_sc, ft_sc, kt_sc, vt_sc, ot_sc, acc_sc):
    h = pl.program_id(0)
    kc = pl.program_id(1)
    nheads = pl.num_programs(0)
    nkc = pl.num_programs(1)
    width = qt_sc.shape[0]

    @pl.when((h == 0) & (kc == 0))
    def _():
        raw = lbp_ref[...]
        e = jnp.exp(raw - jnp.max(raw, axis=0, keepdims=True))
        lb = jnp.sum(e[:layer + 1], axis=0, keepdims=True) / jnp.sum(e, axis=0, keepdims=True)
        q, f, k = _hgrn_gates(qfig_ref[:, 0:width], qfig_ref[:, width:2 * width], lb)
        qt_sc[...] = q.T
        ft_sc[...] = f.T
        kt_sc[...] = k.T
        vt_sc[...] = qfig_ref[:, 2 * width:3 * width].T

    @pl.when(kc == 0)
    def _():
        acc_sc[...] = jnp.zeros_like(acc_sc)

    base = pl.multiple_of(h * HG_DIM, HG_DIM)
    vt = vt_sc[pl.ds(base, HG_DIM), :]
    acc = acc_sc[...]
    for r in range(HG_STEP_ROWS):
        krow = base + kc * HG_STEP_ROWS + r
        new = (ft_sc[pl.ds(krow, 1), :] * st_ref[:, r * HG_DIM:(r + 1) * HG_DIM].T
               + kt_sc[pl.ds(krow, 1), :] * vt)
        acc = acc + new * qt_sc[pl.ds(krow, 1), :]
        stout_ref[:, r * HG_DIM:(r + 1) * HG_DIM] = new.T
    acc_sc[...] = acc

    @pl.when(kc == nkc - 1)
    def _():
        ot_sc[pl.ds(base, HG_DIM), :] = acc

    @pl.when((h == nheads - 1) & (kc == nkc - 1))
    def _():
        o = ot_sc[...].T
        for hh in range(width // HG_DIM):
            sl = slice(hh * HG_DIM, (hh + 1) * HG_DIM)
            o_ref[:, sl] = _rms(o[:, sl], nw_ref[...]) * _silu(qfig_ref[:, 3 * width + hh * HG_DIM:3 * width + (hh + 1) * HG_DIM])


def hgrn_step(qfig, lb_param, norm_w, layer, state):
    b, w4 = qfig.shape
    width = w4 // 4
    nheads = width // HG_DIM
    nkc = HG_DIM // HG_STEP_ROWS
    blk = HG_STEP_ROWS * HG_DIM
    kern = functools.partial(_hgrn_step_kernel, layer)
    full = lambda shape: pl.BlockSpec(shape, lambda h, kc: (0, 0))
    o, st = pl.pallas_call(
        kern,
        grid=(nheads, nkc),
        in_specs=[full((b, w4)), full(lb_param.shape), full((1, HG_DIM)),
                  pl.BlockSpec((b, blk), lambda h, kc: (0, h * nkc + kc))],
        out_specs=[full((b, width)), pl.BlockSpec((b, blk), lambda h, kc: (0, h * nkc + kc))],
        out_shape=[jax.ShapeDtypeStruct((b, width), F32),
                   jax.ShapeDtypeStruct((b, nheads * HG_DIM * HG_DIM), F32)],
        scratch_shapes=[pltpu.VMEM((width, b), F32)] * 5 + [pltpu.VMEM((HG_DIM, b), F32)],
        compiler_params=_params(("arbitrary", "arbitrary")),
        name="hgrn_step",
    )(qfig, lb_param, norm_w, state.reshape(b, nheads * HG_DIM * HG_DIM))
    return o, st.reshape(b, nheads, HG_DIM, HG_DIM)


def _s5_discretize_kernel(lre_ref, lim_ref, step_ref, bre_ref, bim_ref, are_ref, aim_ref, bbre_ref, bbim_ref):
    lre, lim = lre_ref[...], lim_ref[...]
    step = jnp.exp(step_ref[...])
    mag = jnp.exp(lre * step)
    are = mag * jnp.cos(lim * step)
    aim = mag * jnp.sin(lim * step)
    den = lre * lre + lim * lim
    nr = are - 1.0
    cre = (nr * lre + aim * lim) / den
    cim = (aim * lre - nr * lim) / den
    are_ref[...] = are
    aim_ref[...] = aim
    bre, bim = bre_ref[...], bim_ref[...]
    bbre_ref[...] = cre * bre - cim * bim
    bbim_ref[...] = cre * bim + cim * bre


def s5_discretize(lam_re, lam_im, log_step, b_re_t, b_im_t):
    g, n = lam_re.shape
    c = b_re_t.shape[1]
    return pl.pallas_call(
        _s5_discretize_kernel,
        out_shape=[jax.ShapeDtypeStruct((g, 1, n), F32)] * 2 + [jax.ShapeDtypeStruct((g, c, n), F32)] * 2,
        name="s5_discretize",
    )(lam_re.reshape(g, 1, n), lam_im.reshape(g, 1, n), log_step.reshape(g, 1, 1), b_re_t, b_im_t)


def _complex_power(re, im, n):
    out = None
    while n:
        if n & 1:
            out = (re, im) if out is None else (out[0] * re - out[1] * im, out[0] * im + out[1] * re)
        n >>= 1
        if n:
            re, im = re * re - im * im, 2.0 * re * im
    return out


def _s5_output(xre, xim, u, cre_ref, cim_ref, d_ref):
    y = _dot(xre.astype(BF16), cre_ref[0]) - _dot(xim.astype(BF16), cim_ref[0]) + d_ref[...] * u
    return jax.nn.gelu(y)


S5_ROWS = 256


S5_UNROLL = 4


def _s5_prompt_kernel(u_ref, b_ref, cre_ref, cim_ref, d_ref, are_ref, aim_ref,
                      y_ref, sre_ref, sim_ref, x_sc, ui_sc):
    nrows = u_ref.shape[0]
    ns = S5_BLOCK_STATE
    nt = ns // LANES
    nseg = S5_SEGMENTS
    seg = nrows // nseg

    steps = S5_ROWS // nseg

    def same_step_rows(i):
        return pl.ds(i, nseg, stride=seg)

    def project(s, carry):
        r = pl.ds(pl.multiple_of(s * S5_ROWS, S5_ROWS), S5_ROWS)
        ui = jnp.concatenate([u_ref[same_step_rows(s * steps + ii), :] for ii in range(steps)], axis=0)
        ui_sc[r, :] = ui
        bu = _dot(ui.astype(BF16), b_ref[0])
        for k in range(2 * nt):
            x_sc[k, r, :] = bu[:, k * LANES:(k + 1) * LANES]
        return carry
    lax.fori_loop(0, nrows // S5_ROWS, project, 0)

    def lane_tile(ref, k):
        return ref[:, k * LANES:(k + 1) * LANES]

    are = [jnp.broadcast_to(lane_tile(are_ref, k), (nseg, LANES)) for k in range(nt)]
    aim = [jnp.broadcast_to(lane_tile(aim_ref, k), (nseg, LANES)) for k in range(nt)]

    def scan(keep, i, carry):
        rows = pl.ds(pl.multiple_of(i * nseg, nseg), nseg)
        new = []
        for k in range(nt):
            sre, sim = carry[k]
            nre = are[k] * sre - aim[k] * sim + x_sc[k, rows, :]
            nim = are[k] * sim + aim[k] * sre + x_sc[nt + k, rows, :]
            if keep:
                x_sc[k, rows, :] = nre
                x_sc[nt + k, rows, :] = nim
            new.append((nre, nim))
        return tuple(new)

    zero = jnp.zeros((nseg, LANES), F32)
    ends = lax.fori_loop(0, seg, functools.partial(scan, False),
                         tuple((zero, zero) for _ in range(nt)), unroll=S5_UNROLL)

    seg_row = lax.broadcasted_iota(jnp.int32, (nseg, LANES), 0)
    starts = []
    for k in range(nt):
        ere, eim = ends[k]
        full_re, full_im = _complex_power(lane_tile(are_ref, k), lane_tile(aim_ref, k), seg)
        tre = tim = jnp.zeros((1, LANES), F32)
        cre = cim = zero
        for j in range(nseg):
            cre = jnp.where(seg_row == j, tre, cre)
            cim = jnp.where(seg_row == j, tim, cim)
            tre, tim = (ere[j:j + 1, :] + full_re * tre - full_im * tim,
                        eim[j:j + 1, :] + full_re * tim + full_im * tre)
        starts.append((cre, cim))
        sre_ref[0, :, k * LANES:(k + 1) * LANES] = tre
        sim_ref[0, :, k * LANES:(k + 1) * LANES] = tim
    lax.fori_loop(0, seg, functools.partial(scan, True), tuple(starts), unroll=S5_UNROLL)

    def output(s, carry):
        r = pl.ds(pl.multiple_of(s * S5_ROWS, S5_ROWS), S5_ROWS)
        xre = jnp.concatenate([x_sc[k, r, :] for k in range(nt)], axis=1)
        xim = jnp.concatenate([x_sc[nt + k, r, :] for k in range(nt)], axis=1)
        y = _s5_output(xre, xim, ui_sc[r, :], cre_ref, cim_ref, d_ref)
        for ii in range(steps):
            y_ref[same_step_rows(s * steps + ii), :] = y[ii * nseg:(ii + 1) * nseg, :]
        return carry
    lax.fori_loop(0, nrows // S5_ROWS, output, 0)


def s5_prompt(u, b_exp, c_re_exp, c_im_exp, d_skip, a_re, a_im, batch, seq):
    t, d = u.shape
    nblk = d // LANES
    ns = S5_BLOCK_STATE
    return pl.pallas_call(
        _s5_prompt_kernel,
        grid=(nblk, batch),
        in_specs=[pl.BlockSpec((seq, LANES), lambda j, b: (b, j)),
                  pl.BlockSpec((1, LANES, 2 * ns), lambda j, b: (j, 0, 0)),
                  pl.BlockSpec((1, ns, LANES), lambda j, b: (j, 0, 0)),
                  pl.BlockSpec((1, ns, LANES), lambda j, b: (j, 0, 0)),
                  pl.BlockSpec((1, LANES), lambda j, b: (0, j)),
                  pl.BlockSpec((1, ns), lambda j, b: (0, j)),
                  pl.BlockSpec((1, ns), lambda j, b: (0, j))],
        out_specs=[pl.BlockSpec((seq, LANES), lambda j, b: (b, j)),
                   pl.BlockSpec((1, 1, ns), lambda j, b: (b, 0, j)),
                   pl.BlockSpec((1, 1, ns), lambda j, b: (b, 0, j))],
        out_shape=[jax.ShapeDtypeStruct((t, d), F32),
                   jax.ShapeDtypeStruct((batch, 1, nblk * ns), F32),
                   jax.ShapeDtypeStruct((batch, 1, nblk * ns), F32)],
        scratch_shapes=[pltpu.VMEM((2 * ns // LANES, seq, LANES), F32), pltpu.VMEM((seq, LANES), F32)],
        compiler_params=_params(("arbitrary", "arbitrary")),
        name="s5_prompt",
    )(u, b_exp, c_re_exp, c_im_exp, d_skip, a_re, a_im)


def _s5_step_kernel(u_ref, b_ref, cre_ref, cim_ref, d_ref, are_ref, aim_ref, s0re_ref, s0im_ref,
                    y_ref, sre_ref, sim_ref):
    ns = S5_BLOCK_STATE
    u = u_ref[...]
    bu = _dot(u.astype(BF16), b_ref[0])
    are, aim = are_ref[...], aim_ref[...]
    s0re, s0im = s0re_ref[...], s0im_ref[...]
    xre = bu[:, 0:ns] + (are * s0re - aim * s0im)
    xim = bu[:, ns:2 * ns] + (are * s0im + aim * s0re)
    sre_ref[...] = xre
    sim_ref[...] = xim
    y_ref[...] = _s5_output(xre, xim, u, cre_ref, cim_ref, d_ref)


def s5_step(u, b_exp, c_re_exp, c_im_exp, d_skip, a_re, a_im, s0_re, s0_im):
    b, d = u.shape
    nblk = d // LANES
    ns = S5_BLOCK_STATE
    return pl.pallas_call(
        _s5_step_kernel,
        grid=(nblk,),
        in_specs=[pl.BlockSpec((b, LANES), lambda j: (0, j)),
                  pl.BlockSpec((1, LANES, 2 * ns), lambda j: (j, 0, 0)),
                  pl.BlockSpec((1, ns, LANES), lambda j: (j, 0, 0)),
                  pl.BlockSpec((1, ns, LANES), lambda j: (j, 0, 0)),
                  pl.BlockSpec((1, LANES), lambda j: (0, j)),
                  pl.BlockSpec((1, ns), lambda j: (0, j)),
                  pl.BlockSpec((1, ns), lambda j: (0, j)),
                  pl.BlockSpec((b, ns), lambda j: (0, j)),
                  pl.BlockSpec((b, ns), lambda j: (0, j))],
        out_specs=[pl.BlockSpec((b, LANES), lambda j: (0, j)),
                   pl.BlockSpec((b, ns), lambda j: (0, j)),
                   pl.BlockSpec((b, ns), lambda j: (0, j))],
        out_shape=[jax.ShapeDtypeStruct((b, d), F32),
                   jax.ShapeDtypeStruct((b, nblk * ns), F32),
                   jax.ShapeDtypeStruct((b, nblk * ns), F32)],
        compiler_params=_params(("parallel",)),
        name="s5_step",
    )(u, b_exp, c_re_exp, c_im_exp, d_skip, a_re, a_im, s0_re, s0_im)


def _block_diag(per_group):
    ngroups, r, c = per_group.shape
    p = S5_GROUPS_PER_BLOCK
    eye = jnp.eye(p, dtype=per_group.dtype)
    tiles = per_group.reshape(ngroups // p, p, r, c)
    return jnp.einsum('bjrc,jk->bjrkc', tiles, eye).reshape(ngroups // p, p * r, p * c)


def kernel(x_prompt, x_sample, state_ssm, state_ssm_conv, state_hgrn, state_s5_re, state_s5_im, state_ffn_conv, norm_mix_pre, norm_mix_post, norm_ffn_pre, norm_ffn_post, ab_in_w, ssm_conv_w, ssm_conv_b, ssm_dt_bias, ssm_a_log, ssm_d, ssm_norm_w, hgrn_lb, hgrn_norm_w, ab_out_w, s5_lam_re, s5_lam_im, s5_log_step, s5_b_re, s5_b_im, s5_c_re, s5_c_im, s5_d, s5_glu_w, ffn_up_w, ffn_conv_w, ffn_conv_b, ffn_down_w):
    batch, seq, d = x_prompt.shape
    dec_batch = x_sample.shape[0]
    depth = norm_mix_pre.shape[0]
    conv_dim = ssm_conv_w.shape[2]
    nheads = ssm_a_log.shape[1]
    d_inner = nheads * SSM_HEAD_DIM
    hg_width = hgrn_lb.shape[1]
    ffn_dim = ffn_down_w.shape[1]
    tm = 256

    def row(v):
        return v.reshape(1, -1)

    def pad_lanes(v):
        return jnp.pad(v, ((0, 0), (0, LANES - v.shape[1])))

    groups = [(x_prompt.reshape(batch * seq, d), True), (x_sample.reshape(dec_batch, d), False)]
    xs = [g[0] for g in groups]
    out = {'ssm': [[], []], 'sconv': [[], []], 'hg': [[], []], 's5r': [[], []], 's5i': [[], []], 'fconv': [[], []]}

    for l in range(depth):
        e = l // 2
        if l % 2 == 0:
            w_in = ab_in_w[e]
            o_xbc = d_inner
            o_dt = o_xbc + conv_dim
            o_q = o_dt + nheads
            w_z = w_in[:, :o_xbc].astype(BF16)
            w_xbc = w_in[:, o_xbc:o_dt].astype(BF16)
            w_dt = pad_lanes(w_in[:, o_dt:o_q]).astype(BF16)
            w_qfig = w_in[:, o_q:].astype(BF16)
            w_out = ab_out_w[e].astype(BF16)
            dt_bias = pad_lanes(row(ssm_dt_bias[e]))
            a_log = pad_lanes(row(ssm_a_log[e]))
            d_skip = row(jnp.repeat(ssm_d[e], SSM_HEAD_DIM))
            for gi, (_, is_prompt) in enumerate(groups):
                x = xs[gi]
                t = x.shape[0]
                z, xbc, dt, qfig = norm_matmul(x, row(norm_mix_pre[l]), [w_z, w_xbc, w_dt, w_qfig], min(tm, t))
                if is_prompt:
                    y, ssm_new = ssd_prompt(z, xbc, dt, ssm_conv_w[e], row(ssm_conv_b[e]), dt_bias, a_log,
                                            d_skip, row(ssm_norm_w[e]), batch, seq)
                    o, hg_new = hgrn_prompt(qfig, hgrn_lb, row(hgrn_norm_w[e]), e, batch, seq)
                    sconv_new = xbc.reshape(batch, seq, conv_dim)[:, seq - (SSM_CONV - 1):]
                else:
                    hist = [state_ssm_conv[e][:, k] for k in range(SSM_CONV - 1)]
                    y, ssm_new = ssd_step(z, xbc, hist, dt, ssm_conv_w[e], row(ssm_conv_b[e]), dt_bias, a_log,
                                          d_skip, row(ssm_norm_w[e]), state_ssm[e])
                    o, hg_new = hgrn_step(qfig, hgrn_lb, row(hgrn_norm_w[e]), e, state_hgrn[e])
                    sconv_new = jnp.stack(hist[1:] + [xbc], axis=1)
                out['ssm'][gi].append(ssm_new)
                out['hg'][gi].append(hg_new)
                out['sconv'][gi].append(sconv_new)
                xs[gi] = matmul_norm_residual([y, o], [w_out[:d_inner], w_out[d_inner:]],
                                              row(norm_mix_post[l]), x, min(tm, t))
        else:
            ngroups = s5_lam_re.shape[1]
            a_re, a_im, bb_re, bb_im = s5_discretize(
                s5_lam_re[e], s5_lam_im[e], s5_log_step[e],
                jnp.swapaxes(s5_b_re[e], 1, 2), jnp.swapaxes(s5_b_im[e], 1, 2))
            a_re = a_re.reshape(1, ngroups * S5_STATE)
            a_im = a_im.reshape(1, ngroups * S5_STATE)
            b_exp = jnp.concatenate([_block_diag(bb_re), _block_diag(bb_im)], axis=2).astype(BF16)
            c_re_exp = _block_diag(jnp.swapaxes(s5_c_re[e], 1, 2)).astype(BF16)
            c_im_exp = _block_diag(jnp.swapaxes(s5_c_im[e], 1, 2)).astype(BF16)
            glu_w = s5_glu_w[e].astype(BF16)
            for gi, (_, is_prompt) in enumerate(groups):
                x = xs[gi]
                t = x.shape[0]
                u = rmsnorm(x, row(norm_mix_pre[l]), min(tm, t))
                if is_prompt:
                    yg, s_re, s_im = s5_prompt(u, b_exp, c_re_exp, c_im_exp, row(s5_d[e]), a_re, a_im, batch, seq)
                else:
                    yg, s_re, s_im = s5_step(u, b_exp, c_re_exp, c_im_exp, row(s5_d[e]), a_re, a_im,
                                             state_s5_re[e].reshape(t, -1), state_s5_im[e].reshape(t, -1))
                out['s5r'][gi].append(s_re.reshape(-1, ngroups, S5_STATE))
                out['s5i'][gi].append(s_im.reshape(-1, ngroups, S5_STATE))
                xs[gi] = matmul_norm_residual([yg], [glu_w], row(norm_mix_post[l]), x, min(tm, t), glu=True)

        up_w = ffn_up_w[l].astype(BF16)
        down_w = ffn_down_w[l].astype(BF16)
        for gi, (_, is_prompt) in enumerate(groups):
            x = xs[gi]
            t = x.shape[0]
            if is_prompt:
                xs[gi], tail = ffn_seq(x, row(norm_ffn_pre[l]), up_w, ffn_conv_w[l], row(ffn_conv_b[l]), down_w,
                                       row(norm_ffn_post[l]), tm, seq)
                fconv_new = tail[:, SUBLANES - (FFN_CONV - 1):]
            else:
                (up,) = norm_matmul(x, row(norm_ffn_pre[l]), [up_w], t)
                old = state_ffn_conv[l]
                xs[gi] = ffn_tail_step(up, old[:, 0], old[:, 1], ffn_conv_w[l], row(ffn_conv_b[l]), down_w,
                                       row(norm_ffn_post[l]), x)
                fconv_new = jnp.stack([old[:, 1], up], axis=1)
            out['fconv'][gi].append(fconv_new)

    y_prompt = xs[0].reshape(batch, seq, d)
    y_sample = xs[1].reshape(dec_batch, 1, d)
    states = []
    for gi in range(2):
        states += [jnp.stack(out[k][gi]) for k in ('ssm', 'sconv', 'hg', 's5r', 's5i', 'fconv')]
    return (y_prompt, y_sample, *states)
```

```python
import functools
import math

import jax
import jax.numpy as jnp
import numpy as np
from jax import lax
from jax.experimental import pallas as pl
from jax.experimental.pallas import tpu as pltpu

F32 = jnp.float32
BF16 = jnp.bfloat16
EPS = 1e-6

LANES = 128
SUBLANES = 8
VMEM_LIMIT_BYTES = 56 * 1024 * 1024

SSM_HEAD_DIM = 64
SSM_GROUPS = 2
SSM_STATE = 64
SSM_CONV = 4
SSM_CHUNK = 128
HG_DIM = 128
HG_CHUNK = 128
S5_GROUP = 16
S5_STATE = 64
S5_GROUPS_PER_BLOCK = LANES // S5_GROUP
S5_BLOCK_STATE = S5_GROUPS_PER_BLOCK * S5_STATE
S5_SEGMENTS = SUBLANES
FFN_CONV = 3


def _params(sem):
    return pltpu.CompilerParams(dimension_semantics=sem, vmem_limit_bytes=VMEM_LIMIT_BYTES)


def _resident(shape):
    nd = len(shape)
    return pl.BlockSpec(shape, lambda *_: (0,) * nd, pipeline_mode=pl.Buffered(1))


def _rms(x, w):
    return x * lax.rsqrt(jnp.mean(x * x, axis=-1, keepdims=True) + EPS) * w


def _silu(x):
    return x * jax.nn.sigmoid(x)


def _dot(a, b):
    return jnp.dot(a, b, preferred_element_type=F32)


def _dot_nt(a, b):
    return lax.dot_general(a, b, (((1,), (1,)), ((), ())), preferred_element_type=F32)


def _dot_tn(a, b):
    return lax.dot_general(a, b, (((0,), (0,)), ((), ())), preferred_element_type=F32)


def _norm_matmul_kernel(n_out, x_ref, g_ref, *refs):
    w_refs, o_refs = refs[:n_out], refs[n_out:]
    xn = _rms(x_ref[...], g_ref[...]).astype(BF16)
    for w_ref, o_ref in zip(w_refs, o_refs):
        n = w_ref.shape[1]
        for c0 in range(0, n, 512):
            c1 = min(n, c0 + 512)
            o_ref[:, c0:c1] = _dot(xn, w_ref[:, c0:c1])


def norm_matmul(x, g, ws, tm):
    t, d = x.shape
    kern = functools.partial(_norm_matmul_kernel, len(ws))
    return pl.pallas_call(
        kern,
        grid=(t // tm,),
        in_specs=[pl.BlockSpec((tm, d), lambda i: (i, 0)), _resident((1, d))]
        + [_resident(w.shape) for w in ws],
        out_specs=[pl.BlockSpec((tm, w.shape[1]), lambda i: (i, 0)) for w in ws],
        out_shape=[jax.ShapeDtypeStruct((t, w.shape[1]), F32) for w in ws],
        compiler_params=_params(("parallel",)),
        name="norm_matmul",
    )(x, g, *ws)


def _rmsnorm_kernel(x_ref, g_ref, o_ref):
    o_ref[...] = _rms(x_ref[...], g_ref[...])


def rmsnorm(x, g, tm):
    t, d = x.shape
    return pl.pallas_call(
        _rmsnorm_kernel,
        grid=(t // tm,),
        in_specs=[pl.BlockSpec((tm, d), lambda i: (i, 0)), _resident((1, d))],
        out_specs=pl.BlockSpec((tm, d), lambda i: (i, 0)),
        out_shape=jax.ShapeDtypeStruct((t, d), F32),
        compiler_params=_params(("parallel",)),
        name="rmsnorm",
    )(x, g)


def _matmul_norm_residual_kernel(n_in, glu, *refs):
    a_refs, w_refs = refs[:n_in], refs[n_in:2 * n_in]
    g_ref, x_ref, o_ref = refs[2 * n_in:]
    acc = None
    for a_ref, w_ref in zip(a_refs, w_refs):
        part = _dot(a_ref[...].astype(BF16), w_ref[...])
        acc = part if acc is None else acc + part
    if glu:
        d = acc.shape[1] // 2
        acc = acc[:, :d] * jax.nn.sigmoid(acc[:, d:])
    o_ref[...] = x_ref[...] + _rms(acc, g_ref[...])


def matmul_norm_residual(a_list, w_list, g, x, tm, glu=False):
    t, d = x.shape
    kern = functools.partial(_matmul_norm_residual_kernel, len(a_list), glu)
    return pl.pallas_call(
        kern,
        grid=(t // tm,),
        in_specs=[pl.BlockSpec((tm, a.shape[1]), lambda i: (i, 0)) for a in a_list]
        + [_resident(w.shape) for w in w_list]
        + [_resident((1, d)), pl.BlockSpec((tm, d), lambda i: (i, 0))],
        out_specs=pl.BlockSpec((tm, d), lambda i: (i, 0)),
        out_shape=jax.ShapeDtypeStruct((t, d), F32),
        compiler_params=_params(("parallel",)),
        name="matmul_norm_residual",
    )(*a_list, *w_list, g, x)


FFN_CHUNK = 256


def _ffn_tail(up_ref, xm2_of, xm1_of, cw_ref, cb_ref, dw_ref, g_ref, x_ref, o_ref):
    f = dw_ref.shape[0]
    acc = jnp.zeros(o_ref.shape, F32)
    for c0 in range(0, f, FFN_CHUNK):
        c1 = c0 + FFN_CHUNK
        halves = []
        for off in (0, f):
            a, b = c0 + off, c1 + off
            halves.append(cb_ref[:, a:b] + xm2_of(a, b) * cw_ref[0:1, a:b]
                          + xm1_of(a, b) * cw_ref[1:2, a:b] + up_ref[:, a:b] * cw_ref[2:3, a:b])
        act = (halves[0] * _silu(halves[1])).astype(BF16)
        acc = acc + _dot(act, dw_ref[c0:c1, :])
    o_ref[...] = x_ref[...] + _rms(acc, g_ref[...])


def _ffn_seq_kernel(tiles_per_seq, emit_next, x_ref, g1_ref, uw_ref, cw_ref, cb_ref, dw_ref, g2_ref, *rest):
    if emit_next:
        g3_ref, o_ref, tail_ref, n_ref, carry_sc = rest
    else:
        o_ref, tail_ref, carry_sc = rest
    tm = x_ref.shape[0]
    f = dw_ref.shape[0]

    @pl.when(pl.program_id(0) % tiles_per_seq == 0)
    def _():
        carry_sc[...] = jnp.zeros_like(carry_sc)

    x = x_ref[...]
    xn = _rms(x, g1_ref[...]).astype(BF16)
    top_row = lax.broadcasted_iota(jnp.int32, (SUBLANES, FFN_CHUNK), 0)

    def earlier(up, prev, k):
        rolled = pltpu.roll(up, k, axis=0)
        top = jnp.where(top_row < k, pltpu.roll(prev, k, axis=0), rolled[:SUBLANES])
        return jnp.concatenate([top, rolled[SUBLANES:]], axis=0)

    def project(c0):
        return [_dot(xn, uw_ref[:, c0 + off:c0 + off + FFN_CHUNK]) for off in (0, f)]

    acc = jnp.zeros(o_ref.shape, F32)
    ups_next = project(0)
    act_prev = None
    for c0 in range(0, f, FFN_CHUNK):
        ups = ups_next
        if c0 + FFN_CHUNK < f:
            ups_next = project(c0 + FFN_CHUNK)
        if act_prev is not None:
            acc = acc + _dot(act_prev, dw_ref[c0 - FFN_CHUNK:c0, :])
        halves = []
        for up, off in zip(ups, (0, f)):
            a, b = c0 + off, c0 + off + FFN_CHUNK
            prev = carry_sc[:, a:b]
            carry_sc[:, a:b] = up[tm - SUBLANES:, :]
            tail_ref[0, :, a:b] = up[tm - SUBLANES:, :]
            halves.append(cb_ref[:, a:b] + earlier(up, prev, 2) * cw_ref[0:1, a:b]
                          + earlier(up, prev, 1) * cw_ref[1:2, a:b] + up * cw_ref[2:3, a:b])
        act_prev = (halves[0] * _silu(halves[1])).astype(BF16)
    acc = acc + _dot(act_prev, dw_ref[f - FFN_CHUNK:f, :])
    out = x + _rms(acc, g2_ref[...])
    o_ref[...] = out
    if emit_next:
        n_ref[...] = _rms(out, g3_ref[...])


def ffn_seq(x, g_pre, up_w, conv_w, conv_b, down_w, g_post, g_next, tm, seq):
    t, d = x.shape
    f2 = up_w.shape[1]
    tiles_per_seq = seq // tm
    emit_next = g_next is not None
    kern = functools.partial(_ffn_seq_kernel, tiles_per_seq, emit_next)
    tile = pl.BlockSpec((tm, d), lambda i: (i, 0))
    return pl.pallas_call(
        kern,
        grid=(t // tm,),
        in_specs=[tile, _resident((1, d)), _resident(up_w.shape), _resident(conv_w.shape), _resident((1, f2)),
                  _resident(down_w.shape), _resident((1, d))] + [_resident((1, d))] * emit_next,
        out_specs=[tile, pl.BlockSpec((1, SUBLANES, f2), lambda i: (i // tiles_per_seq, 0, 0))]
        + [tile] * emit_next,
        out_shape=[jax.ShapeDtypeStruct((t, d), F32), jax.ShapeDtypeStruct((t // seq, SUBLANES, f2), F32)]
        + [jax.ShapeDtypeStruct((t, d), F32)] * emit_next,
        scratch_shapes=[pltpu.VMEM((SUBLANES, f2), F32)],
        compiler_params=_params(("arbitrary",)),
        name="ffn_seq",
    )(x, g_pre, up_w, conv_w, conv_b, down_w, g_post, *([g_next] * emit_next))


def _ffn_step_kernel(up_ref, xm2_ref, xm1_ref, cw_ref, cb_ref, dw_ref, g_ref, x_ref, o_ref):
    _ffn_tail(up_ref, lambda a, b: xm2_ref[:, a:b], lambda a, b: xm1_ref[:, a:b],
              cw_ref, cb_ref, dw_ref, g_ref, x_ref, o_ref)


def ffn_tail_step(up, xm2, xm1, conv_w, conv_b, down_w, g, x):
    t, d = x.shape
    f2 = up.shape[1]
    return pl.pallas_call(
        _ffn_step_kernel,
        grid=(1,),
        in_specs=[_resident((t, f2))] * 3
        + [_resident(conv_w.shape), _resident((1, f2)), _resident(down_w.shape),
           _resident((1, d)), _resident((t, d))],
        out_specs=pl.BlockSpec((t, d), lambda i: (0, 0)),
        out_shape=jax.ShapeDtypeStruct((t, d), F32),
        compiler_params=_params(("arbitrary",)),
        name="ffn_tail_step",
    )(up, xm2, xm1, conv_w, conv_b, down_w, g, x)


def _ssd_gate_norm(y, xs, z, dsk_ref, nw_ref):
    y = y + dsk_ref[...] * xs
    return _rms(y * _silu(z), nw_ref[...])


def _head_expansion(nheads, head_dim):
    e = np.zeros((LANES, nheads * head_dim), np.float32)
    for h in range(nheads):
        e[h, h * head_dim:(h + 1) * head_dim] = 1.0
    return e


def _expand_heads(v, exp_ref):
    hi = v.astype(BF16)
    lo = (v - hi.astype(F32)).astype(BF16)
    return _dot(hi, exp_ref[...]) + _dot(lo, exp_ref[...])


def _ssd_prompt_kernel(z_ref, xbc_ref, dt_ref, cw_ref, cb_ref, dtb_ref, alog_ref, dsk_ref, nw_ref, exp_ref,
                       y_ref, hout_ref, hist_sc, h_sc, xp_sc, y_sc):
    c = pl.program_id(1)
    nchunks = pl.num_programs(1)
    cl = xbc_ref.shape[0]
    hn, d_inner = h_sc.shape
    hp = SSM_HEAD_DIM
    nheads = d_inner // hp
    heads_per_group = nheads // SSM_GROUPS
    group_width = heads_per_group * hp

    @pl.when(c == 0)
    def _():
        hist_sc[...] = jnp.zeros_like(hist_sc)
        h_sc[...] = jnp.zeros_like(h_sc)

    raw = xbc_ref[...]
    xp_sc[0:SUBLANES, :] = hist_sc[...]
    xp_sc[SUBLANES:, :] = raw
    hist_sc[...] = raw[cl - SUBLANES:, :]
    conv = cb_ref[...]
    for k in range(SSM_CONV):
        conv = conv + xp_sc[pl.ds(SUBLANES - (SSM_CONV - 1) + k, cl), :] * cw_ref[k:k + 1, :]
    act = _silu(conv)
    xs = act[:, :d_inner]
    bm = act[:, d_inner:d_inner + SSM_GROUPS * SSM_STATE]
    cm = act[:, d_inner + SSM_GROUPS * SSM_STATE:]

    dt = jax.nn.softplus(dt_ref[...] + dtb_ref[...])
    da = dt * (-jnp.exp(alog_ref[...]))
    ti = lax.broadcasted_iota(jnp.int32, (cl, cl), 0)
    si = lax.broadcasted_iota(jnp.int32, (cl, cl), 1)
    tril = ti >= si
    cum_col = jnp.dot(tril.astype(F32), da, preferred_element_type=F32,
                      precision=lax.Precision.HIGHEST)
    cum_row = cum_col.T
    dt_row = dt.T
    last = cum_col[cl - 1:cl, :]

    per_head = jnp.concatenate([jnp.exp(cum_col), jnp.exp(last - cum_col) * dt,
                                jnp.broadcast_to(jnp.exp(last), (SUBLANES, LANES))], axis=0)
    wide = _expand_heads(per_head, exp_ref)
    from_start, to_end, chunk_decay = wide[:cl], wide[cl:2 * cl], wide[2 * cl:2 * cl + 1]

    xs16 = xs.astype(BF16)
    xw16 = (xs * to_end).astype(BF16)
    bm_t = bm.T
    st = h_sc[...]
    st16 = st.astype(BF16)
    cbs, state_terms, updates = [], [], []
    for g in range(SSM_GROUPS):
        sl = slice(g * SSM_STATE, (g + 1) * SSM_STATE)
        gl = slice(g * group_width, (g + 1) * group_width)
        cm_g = cm[:, sl].astype(BF16)
        cbs.append(_dot_nt(cm_g, bm[:, sl].astype(BF16)))
        state_terms.append(_dot(cm_g, st16[:, gl]))
        updates.append(_dot(bm_t[sl, :].astype(BF16), xw16[:, gl]))
    h_sc[...] = st * chunk_decay + jnp.concatenate(updates, axis=1)
    y_sc[...] = jnp.concatenate(state_terms, axis=1) * from_start

    for h in range(nheads):
        hl = slice(h * hp, (h + 1) * hp)
        seg = jnp.exp(jnp.where(tril, cum_col[:, h:h + 1] - cum_row[h:h + 1, :], -jnp.inf))
        wts = cbs[h // heads_per_group] * seg * dt_row[h:h + 1, :]
        y_sc[:, hl] = y_sc[:, hl] + _dot(wts.astype(BF16), xs16[:, hl])

    y_ref[...] = _ssd_gate_norm(y_sc[...], xs, z_ref[...], dsk_ref, nw_ref).astype(y_ref.dtype)

    @pl.when(c == nchunks - 1)
    def _():
        final = h_sc[...]
        pad = jnp.zeros((LANES - hn, LANES), F32)
        for j in range(d_inner // LANES):
            tile_t = jnp.concatenate([final[:, j * LANES:(j + 1) * LANES], pad], axis=0).T
            for r in range(LANES // hp):
                hout_ref[0, j * (LANES // hp) + r] = tile_t[r * hp:(r + 1) * hp, :hn]


def ssd_prompt(z, xbc, dt, conv_w, conv_b, dt_bias, a_log, d_skip, norm_w, batch, seq):
    t, d_inner = z.shape
    conv_dim = xbc.shape[1]
    nheads = d_inner // SSM_HEAD_DIM
    cl = SSM_CHUNK
    nchunks = seq // cl
    row = lambda b, c: (b * nchunks + c, 0)
    return pl.pallas_call(
        _ssd_prompt_kernel,
        grid=(batch, nchunks),
        in_specs=[pl.BlockSpec((cl, d_inner), row), pl.BlockSpec((cl, conv_dim), row),
                  pl.BlockSpec((cl, LANES), row),
                  _resident(conv_w.shape), _resident((1, conv_dim)), _resident((1, LANES)),
                  _resident((1, LANES)), _resident((1, d_inner)), _resident((1, d_inner)),
                  _resident((LANES, d_inner))],
        out_specs=[pl.BlockSpec((cl, d_inner), row),
                   pl.BlockSpec((1, nheads, SSM_HEAD_DIM, SSM_STATE), lambda b, c: (b, 0, 0, 0))],
        out_shape=[jax.ShapeDtypeStruct((t, d_inner), BF16),
                   jax.ShapeDtypeStruct((batch, nheads, SSM_HEAD_DIM, SSM_STATE), F32)],
        scratch_shapes=[pltpu.VMEM((SUBLANES, conv_dim), F32),
                        pltpu.VMEM((SSM_STATE, d_inner), F32),
                        pltpu.VMEM((cl + SUBLANES, conv_dim), F32),
                        pltpu.VMEM((cl, d_inner), F32)],
        compiler_params=_params(("arbitrary", "arbitrary")),
        name="ssd_prompt",
    )(z, xbc, dt, conv_w, conv_b, dt_bias, a_log, d_skip, norm_w,
      jnp.asarray(_head_expansion(nheads, SSM_HEAD_DIM), BF16))


def _ssd_step_kernel(z_ref, xnew_ref, h0_ref, h1_ref, h2_ref, dt_ref, cw_ref, cb_ref, dtb_ref, alog_ref,
                     dsk_ref, nw_ref, st_ref, y_ref, stout_ref,
                     xs_sc, xst_sc, bmt_sc, cmt_sc, dtt_sc, dat_sc, yt_sc):
    h = pl.program_id(0)
    nheads = pl.num_programs(0)
    d_inner = xs_sc.shape[1]
    hp, hn = SSM_HEAD_DIM, SSM_STATE
    heads_per_group = d_inner // hp // SSM_GROUPS

    @pl.when(h == 0)
    def _():
        conv = (cb_ref[...] + h0_ref[...] * cw_ref[0:1, :] + h1_ref[...] * cw_ref[1:2, :]
                + h2_ref[...] * cw_ref[2:3, :] + xnew_ref[...] * cw_ref[3:4, :])
        act = _silu(conv)
        xs = act[:, :d_inner]
        dt = jax.nn.softplus(dt_ref[...] + dtb_ref[...])
        xs_sc[...] = xs
        xst_sc[...] = xs.T
        bmt_sc[...] = act[:, d_inner:d_inner + SSM_GROUPS * hn].T
        cmt_sc[...] = act[:, d_inner + SSM_GROUPS * hn:].T
        dtt_sc[...] = dt.T
        dat_sc[...] = jnp.exp(dt * (-jnp.exp(alog_ref[...]))).T

    g = h // heads_per_group
    bmt = bmt_sc[pl.ds(pl.multiple_of(g * hn, hn), hn), :]
    cmt = cmt_sc[pl.ds(pl.multiple_of(g * hn, hn), hn), :]
    da = dat_sc[pl.ds(h, 1), :]
    dtx = xst_sc[pl.ds(pl.multiple_of(h * hp, hp), hp), :] * dtt_sc[pl.ds(h, 1), :]
    rows_per_tile = LANES // hn
    for j in range(hp // rows_per_tile):
        tile = st_ref[:, j * LANES:(j + 1) * LANES].T
        news = []
        for r in range(rows_per_tile):
            p = j * rows_per_tile + r
            new = da * tile[r * hn:(r + 1) * hn, :] + dtx[p:p + 1, :] * bmt
            yt_sc[pl.ds(h * hp + p, 1), :] = jnp.sum(new * cmt, axis=0, keepdims=True)
            news.append(new)
        stout_ref[:, j * LANES:(j + 1) * LANES] = jnp.concatenate(news, axis=0).T

    @pl.when(h == nheads - 1)
    def _():
        y_ref[...] = _ssd_gate_norm(yt_sc[...].T, xs_sc[...], z_ref[...], dsk_ref, nw_ref).astype(y_ref.dtype)


def ssd_step(z, xnew, hist, dt, conv_w, conv_b, dt_bias, a_log, d_skip, norm_w, state):
    b, d_inner = z.shape
    conv_dim = xnew.shape[1]
    nheads = d_inner // SSM_HEAD_DIM
    per_head = SSM_HEAD_DIM * SSM_STATE
    full = lambda shape: pl.BlockSpec(shape, lambda h: (0, 0))
    y, st = pl.pallas_call(
        _ssd_step_kernel,
        grid=(nheads,),
        in_specs=[full((b, d_inner))] + [full((b, conv_dim))] * 4 + [full((b, LANES))]
        + [full(conv_w.shape), full((1, conv_dim)), full((1, LANES)), full((1, LANES)),
           full((1, d_inner)), full((1, d_inner)),
           pl.BlockSpec((b, per_head), lambda h: (0, h))],
        out_specs=[full((b, d_inner)), pl.BlockSpec((b, per_head), lambda h: (0, h))],
        out_shape=[jax.ShapeDtypeStruct((b, d_inner), BF16),
                   jax.ShapeDtypeStruct((b, nheads * per_head), F32)],
        scratch_shapes=[pltpu.VMEM((b, d_inner), F32), pltpu.VMEM((d_inner, b), F32),
                        pltpu.VMEM((SSM_GROUPS * SSM_STATE, b), F32),
                        pltpu.VMEM((SSM_GROUPS * SSM_STATE, b), F32),
                        pltpu.VMEM((LANES, b), F32), pltpu.VMEM((LANES, b), F32),
                        pltpu.VMEM((d_inner, b), F32)],
        compiler_params=_params(("arbitrary",)),
        name="ssd_step",
    )(z, xnew, hist[0], hist[1], hist[2], dt, conv_w, conv_b, dt_bias, a_log, d_skip, norm_w,
      state.reshape(b, nheads * per_head))
    return y, st.reshape(b, nheads, SSM_HEAD_DIM, SSM_STATE)


def _hgrn_lower_bound(lbp_ref, layer):
    raw = lbp_ref[...]
    e = jnp.exp(raw - jnp.max(raw, axis=0, keepdims=True))
    return jnp.sum(e[:layer + 1], axis=0, keepdims=True) / jnp.sum(e, axis=0, keepdims=True)


def _hgrn_gates(q_raw, f_raw, lb):
    q = _silu(q_raw)
    f = lb + (1.0 - lb) * jax.nn.sigmoid(f_raw)
    k = (1.0 - lb) * jax.nn.sigmoid(-f_raw)
    return q, f, k


def _pair_levels(cl):
    t = np.arange(cl)[:, None]
    s = np.arange(cl)[None, :]
    lvl = np.floor(np.log2(np.maximum(t ^ s, 1))).astype(np.int32)
    return np.where(t > s, lvl, np.where(t == s, -1, -2)).astype(np.int32)


def _hgrn_prompt_kernel(layer, q_ref, f_ref, i_ref, g_ref, lvl_ref, lbp_ref, nw_ref, o_ref, sout_ref, st_sc):
    c = pl.program_id(1)
    nchunks = pl.num_programs(1)
    cl = q_ref.shape[0]
    nheads = st_sc.shape[0]

    @pl.when(c == 0)
    def _():
        st_sc[...] = jnp.zeros_like(st_sc)

    lb_all = _hgrn_lower_bound(lbp_ref, layer)
    row = lax.broadcasted_iota(jnp.int32, (cl, LANES), 0)
    lvl = lvl_ref[...]

    for h in range(nheads):
        sl = slice(h * HG_DIM, (h + 1) * HG_DIM)
        q, f, k = _hgrn_gates(q_ref[:, sl], f_ref[:, sl], lb_all[:, sl])
        v = i_ref[:, sl]
        cum = jnp.log(f)
        shift = 1
        while shift < cl:
            cum = cum + jnp.where(row >= shift, pltpu.roll(cum, shift, axis=0), 0.0)
            shift *= 2

        att = jnp.where(lvl == -1, _dot_nt(q.astype(BF16), k.astype(BF16)), 0.0)
        last_of_block = cum
        blk = 1
        level = 0
        while blk < cl:
            odd = (row & blk) != 0
            expo = jnp.where(odd, cum - pltpu.roll(last_of_block, blk, axis=0), last_of_block - cum)
            scaled = (jnp.where(odd, q, k) * jnp.exp(expo)).astype(BF16)
            att = jnp.where(lvl == level, _dot_nt(scaled, scaled), att)
            last_of_block = jnp.where(odd, last_of_block, pltpu.roll(last_of_block, cl - blk, axis=0))
            blk *= 2
            level += 1
        cum_last = last_of_block

        st = st_sc[h]
        o = (_dot(att.astype(BF16), v.astype(BF16))
             + _dot_nt((q * jnp.exp(cum)).astype(BF16), st.astype(BF16)))
        k_end = (k * jnp.exp(cum_last - cum)).astype(BF16)
        st_sc[h] = st * jnp.exp(cum_last) + _dot(v.T.astype(BF16), k_end)
        o_ref[:, sl] = (_rms(o, nw_ref[...]) * _silu(g_ref[:, sl])).astype(o_ref.dtype)

    @pl.when(c == nchunks - 1)
    def _():
        for h in range(nheads):
            sout_ref[0, h] = st_sc[h].T


def hgrn_prompt(qfig, lb_param, norm_w, layer, batch, seq):
    t, w4 = qfig.shape
    width = w4 // 4
    nheads = width // HG_DIM
    cl = HG_CHUNK
    nchunks = seq // cl

    def col(which):
        return lambda b, c: (b * nchunks + c, which)

    kern = functools.partial(_hgrn_prompt_kernel, layer)
    return pl.pallas_call(
        kern,
        grid=(batch, nchunks),
        in_specs=[pl.BlockSpec((cl, width), col(j)) for j in range(4)]
        + [_resident((cl, cl)), _resident(lb_param.shape), _resident((1, HG_DIM))],
        out_specs=[pl.BlockSpec((cl, width), col(0)),
                   pl.BlockSpec((1, nheads, HG_DIM, HG_DIM), lambda b, c: (b, 0, 0, 0))],
        out_shape=[jax.ShapeDtypeStruct((t, width), BF16),
                   jax.ShapeDtypeStruct((batch, nheads, HG_DIM, HG_DIM), F32)],
        scratch_shapes=[pltpu.VMEM((nheads, HG_DIM, HG_DIM), F32)],
        compiler_params=_params(("arbitrary", "arbitrary")),
        name="hgrn_prompt",
    )(qfig, qfig, qfig, qfig, jnp.asarray(_pair_levels(cl)), lb_param, norm_w)


HG_STEP_ROWS = 32


def _hgrn_step_kernel(layer, qfig_ref, lbp_ref, nw_ref, st_ref, o_ref, stout_ref,
                      qt_sc, ft_sc, kt_sc, vt_sc, ot_sc, acc_sc):
    h = pl.program_id(0)
    kc = pl.program_id(1)
    nheads = pl.num_programs(0)
    nkc = pl.num_programs(1)
    width = qt_sc.shape[0]

    @pl.when((h == 0) & (kc == 0))
    def _():
        raw = lbp_ref[...]
        e = jnp.exp(raw - jnp.max(raw, axis=0, keepdims=True))
        lb = jnp.sum(e[:layer + 1], axis=0, keepdims=True) / jnp.sum(e, axis=0, keepdims=True)
        q, f, k = _hgrn_gates(qfig_ref[:, 0:width], qfig_ref[:, width:2 * width], lb)
        qt_sc[...] = q.T
        ft_sc[...] = f.T
        kt_sc[...] = k.T
        vt_sc[...] = qfig_ref[:, 2 * width:3 * width].T

    @pl.when(kc == 0)
    def _():
        acc_sc[...] = jnp.zeros_like(acc_sc)

    base = pl.multiple_of(h * HG_DIM, HG_DIM)
    vt = vt_sc[pl.ds(base, HG_DIM), :]
    acc = acc_sc[...]
    for r in range(HG_STEP_ROWS):
        krow = base + kc * HG_STEP_ROWS + r
        new = (ft_sc[pl.ds(krow, 1), :] * st_ref[:, r * HG_DIM:(r + 1) * HG_DIM].T
               + kt_sc[pl.ds(krow, 1), :] * vt)
        acc = acc + new * qt_sc[pl.ds(krow, 1), :]
        stout_ref[:, r * HG_DIM:(r + 1) * HG_DIM] = new.T
    acc_sc[...] = acc

    @pl.when(kc == nkc - 1)
    def _():
        ot_sc[pl.ds(base, HG_DIM), :] = acc

    @pl.when((h == nheads - 1) & (kc == nkc - 1))
    def _():
        o = ot_sc[...].T
        for hh in range(width // HG_DIM):
            sl = slice(hh * HG_DIM, (hh + 1) * HG_DIM)
            gate = qfig_ref[:, 3 * width + hh * HG_DIM:3 * width + (hh + 1) * HG_DIM]
            o_ref[:, sl] = (_rms(o[:, sl], nw_ref[...]) * _silu(gate)).astype(o_ref.dtype)


def hgrn_step(qfig, lb_param, norm_w, layer, state):
    b, w4 = qfig.shape
    width = w4 // 4
    nheads = width // HG_DIM
    nkc = HG_DIM // HG_STEP_ROWS
    blk = HG_STEP_ROWS * HG_DIM
    kern = functools.partial(_hgrn_step_kernel, layer)
    full = lambda shape: pl.BlockSpec(shape, lambda h, kc: (0, 0))
    o, st = pl.pallas_call(
        kern,
        grid=(nheads, nkc),
        in_specs=[full((b, w4)), full(lb_param.shape), full((1, HG_DIM)),
                  pl.BlockSpec((b, blk), lambda h, kc: (0, h * nkc + kc))],
        out_specs=[full((b, width)), pl.BlockSpec((b, blk), lambda h, kc: (0, h * nkc + kc))],
        out_shape=[jax.ShapeDtypeStruct((b, width), BF16),
                   jax.ShapeDtypeStruct((b, nheads * HG_DIM * HG_DIM), F32)],
        scratch_shapes=[pltpu.VMEM((width, b), F32)] * 5 + [pltpu.VMEM((HG_DIM, b), F32)],
        compiler_params=_params(("arbitrary", "arbitrary")),
        name="hgrn_step",
    )(qfig, lb_param, norm_w, state.reshape(b, nheads * HG_DIM * HG_DIM))
    return o, st.reshape(b, nheads, HG_DIM, HG_DIM)


def _s5_discretize_kernel(lre_ref, lim_ref, step_ref, bre_ref, bim_ref, are_ref, aim_ref, bbre_ref, bbim_ref):
    lre, lim = lre_ref[...], lim_ref[...]
    step = jnp.exp(step_ref[...])
    mag = jnp.exp(lre * step)
    are = mag * jnp.cos(lim * step)
    aim = mag * jnp.sin(lim * step)
    den = lre * lre + lim * lim
    nr = are - 1.0
    cre = (nr * lre + aim * lim) / den
    cim = (aim * lre - nr * lim) / den
    are_ref[...] = are
    aim_ref[...] = aim
    bre, bim = bre_ref[...], bim_ref[...]
    bbre_ref[...] = cre * bre - cim * bim
    bbim_ref[...] = cre * bim + cim * bre


def s5_discretize(lam_re, lam_im, log_step, b_re_t, b_im_t):
    g, n = lam_re.shape
    c = b_re_t.shape[1]
    return pl.pallas_call(
        _s5_discretize_kernel,
        out_shape=[jax.ShapeDtypeStruct((g, 1, n), F32)] * 2 + [jax.ShapeDtypeStruct((g, c, n), F32)] * 2,
        name="s5_discretize",
    )(lam_re.reshape(g, 1, n), lam_im.reshape(g, 1, n), log_step.reshape(g, 1, 1), b_re_t, b_im_t)


def _complex_power(re, im, n):
    out = None
    while n:
        if n & 1:
            out = (re, im) if out is None else (out[0] * re - out[1] * im, out[0] * im + out[1] * re)
        n >>= 1
        if n:
            re, im = re * re - im * im, 2.0 * re * im
    return out


def _s5_output(xre, xim, u, cre_ref, cim_ref, d_ref):
    y = _dot(xre.astype(BF16), cre_ref[0]) - _dot(xim.astype(BF16), cim_ref[0]) + d_ref[...] * u
    return jax.nn.gelu(y)


S5_ROWS = 256


def _s5_prompt_kernel(u_ref, b_ref, cre_ref, cim_ref, d_ref, are_ref, aim_ref,
                      y_ref, sre_ref, sim_ref, x_sc, ui_sc):
    nrows = u_ref.shape[0]
    ns = S5_BLOCK_STATE
    nt = ns // LANES
    nseg = S5_SEGMENTS
    seg = nrows // nseg

    steps = S5_ROWS // nseg

    def same_step_rows(i):
        return pl.ds(i, nseg, stride=seg)

    nslabs = nrows // S5_ROWS

    def slab_rows(s):
        return slice(s * S5_ROWS, (s + 1) * S5_ROWS)

    def project(s):
        r = slab_rows(s)
        ui = jnp.concatenate([u_ref[same_step_rows(s * steps + ii), :] for ii in range(steps)], axis=0)
        ui_sc[r, :] = ui
        bu = _dot(ui.astype(BF16), b_ref[0])
        for k in range(2 * nt):
            x_sc[k, r, :] = bu[:, k * LANES:(k + 1) * LANES]

    def lane_tile(ref, k):
        return ref[:, k * LANES:(k + 1) * LANES]

    are = [jnp.broadcast_to(lane_tile(are_ref, k), (nseg, LANES)) for k in range(nt)]
    aim = [jnp.broadcast_to(lane_tile(aim_ref, k), (nseg, LANES)) for k in range(nt)]

    def scan_slab(keep, s, carry):
        for i in range(s * steps, (s + 1) * steps):
            rows = slice(i * nseg, (i + 1) * nseg)
            new = []
            for k in range(nt):
                sre, sim = carry[k]
                nre = are[k] * sre - aim[k] * sim + x_sc[k, rows, :]
                nim = are[k] * sim + aim[k] * sre + x_sc[nt + k, rows, :]
                if keep:
                    x_sc[k, rows, :] = nre
                    x_sc[nt + k, rows, :] = nim
                new.append((nre, nim))
            carry = tuple(new)
        return carry

    zero = jnp.zeros((nseg, LANES), F32)
    ends = tuple((zero, zero) for _ in range(nt))
    project(0)
    for s in range(nslabs):
        if s + 1 < nslabs:
            project(s + 1)
        ends = scan_slab(False, s, ends)

    seg_row = lax.broadcasted_iota(jnp.int32, (nseg, LANES), 0)
    starts = []
    for k in range(nt):
        ere, eim = ends[k]
        full_re, full_im = _complex_power(lane_tile(are_ref, k), lane_tile(aim_ref, k), seg)
        tre = tim = jnp.zeros((1, LANES), F32)
        cre = cim = zero
        for j in range(nseg):
            cre = jnp.where(seg_row == j, tre, cre)
            cim = jnp.where(seg_row == j, tim, cim)
            tre, tim = (ere[j:j + 1, :] + full_re * tre - full_im * tim,
                        eim[j:j + 1, :] + full_re * tim + full_im * tre)
        starts.append((cre, cim))
        sre_ref[0, :, k * LANES:(k + 1) * LANES] = tre
        sim_ref[0, :, k * LANES:(k + 1) * LANES] = tim

    def output(s):
        r = slab_rows(s)
        xre = jnp.concatenate([x_sc[k, r, :] for k in range(nt)], axis=1)
        xim = jnp.concatenate([x_sc[nt + k, r, :] for k in range(nt)], axis=1)
        y = _s5_output(xre, xim, ui_sc[r, :], cre_ref, cim_ref, d_ref)
        for ii in range(steps):
            y_ref[same_step_rows(s * steps + ii), :] = y[ii * nseg:(ii + 1) * nseg, :]

    state = tuple(starts)
    for s in range(nslabs):
        if s > 0:
            output(s - 1)
        state = scan_slab(True, s, state)
    output(nslabs - 1)


def s5_prompt(u, b_exp, c_re_exp, c_im_exp, d_skip, a_re, a_im, batch, seq):
    t, d = u.shape
    nblk = d // LANES
    ns = S5_BLOCK_STATE
    return pl.pallas_call(
        _s5_prompt_kernel,
        grid=(nblk, batch),
        in_specs=[pl.BlockSpec((seq, LANES), lambda j, b: (b, j)),
                  pl.BlockSpec((1, LANES, 2 * ns), lambda j, b: (j, 0, 0)),
                  pl.BlockSpec((1, ns, LANES), lambda j, b: (j, 0, 0)),
                  pl.BlockSpec((1, ns, LANES), lambda j, b: (j, 0, 0)),
                  pl.BlockSpec((1, LANES), lambda j, b: (0, j)),
                  pl.BlockSpec((1, ns), lambda j, b: (0, j)),
                  pl.BlockSpec((1, ns), lambda j, b: (0, j))],
        out_specs=[pl.BlockSpec((seq, LANES), lambda j, b: (b, j)),
                   pl.BlockSpec((1, 1, ns), lambda j, b: (b, 0, j)),
                   pl.BlockSpec((1, 1, ns), lambda j, b: (b, 0, j))],
        out_shape=[jax.ShapeDtypeStruct((t, d), F32),
                   jax.ShapeDtypeStruct((batch, 1, nblk * ns), F32),
                   jax.ShapeDtypeStruct((batch, 1, nblk * ns), F32)],
        scratch_shapes=[pltpu.VMEM((2 * ns // LANES, seq, LANES), F32), pltpu.VMEM((seq, LANES), F32)],
        compiler_params=_params(("arbitrary", "arbitrary")),
        name="s5_prompt",
    )(u, b_exp, c_re_exp, c_im_exp, d_skip, a_re, a_im)


def _s5_step_kernel(u_ref, b_ref, cre_ref, cim_ref, d_ref, are_ref, aim_ref, s0re_ref, s0im_ref,
                    y_ref, sre_ref, sim_ref):
    ns = S5_BLOCK_STATE
    u = u_ref[...]
    bu = _dot(u.astype(BF16), b_ref[0])
    are, aim = are_ref[...], aim_ref[...]
    s0re, s0im = s0re_ref[...], s0im_ref[...]
    xre = bu[:, 0:ns] + (are * s0re - aim * s0im)
    xim = bu[:, ns:2 * ns] + (are * s0im + aim * s0re)
    sre_ref[...] = xre
    sim_ref[...] = xim
    y_ref[...] = _s5_output(xre, xim, u, cre_ref, cim_ref, d_ref)


def s5_step(u, b_exp, c_re_exp, c_im_exp, d_skip, a_re, a_im, s0_re, s0_im):
    b, d = u.shape
    nblk = d // LANES
    ns = S5_BLOCK_STATE
    return pl.pallas_call(
        _s5_step_kernel,
        grid=(nblk,),
        in_specs=[pl.BlockSpec((b, LANES), lambda j: (0, j)),
                  pl.BlockSpec((1, LANES, 2 * ns), lambda j: (j, 0, 0)),
                  pl.BlockSpec((1, ns, LANES), lambda j: (j, 0, 0)),
                  pl.BlockSpec((1, ns, LANES), lambda j: (j, 0, 0)),
                  pl.BlockSpec((1, LANES), lambda j: (0, j)),
                  pl.BlockSpec((1, ns), lambda j: (0, j)),
                  pl.BlockSpec((1, ns), lambda j: (0, j)),
                  pl.BlockSpec((b, ns), lambda j: (0, j)),
                  pl.BlockSpec((b, ns), lambda j: (0, j))],
        out_specs=[pl.BlockSpec((b, LANES), lambda j: (0, j)),
                   pl.BlockSpec((b, ns), lambda j: (0, j)),
                   pl.BlockSpec((b, ns), lambda j: (0, j))],
        out_shape=[jax.ShapeDtypeStruct((b, d), F32),
                   jax.ShapeDtypeStruct((b, nblk * ns), F32),
                   jax.ShapeDtypeStruct((b, nblk * ns), F32)],
        compiler_params=_params(("parallel",)),
        name="s5_step",
    )(u, b_exp, c_re_exp, c_im_exp, d_skip, a_re, a_im, s0_re, s0_im)


def _block_diag(per_group):
    ngroups, r, c = per_group.shape
    p = S5_GROUPS_PER_BLOCK
    eye = jnp.eye(p, dtype=per_group.dtype)
    tiles = per_group.reshape(ngroups // p, p, r, c)
    return jnp.einsum('bjrc,jk->bjrkc', tiles, eye).reshape(ngroups // p, p * r, p * c)


def kernel(x_prompt, x_sample, state_ssm, state_ssm_conv, state_hgrn, state_s5_re, state_s5_im, state_ffn_conv, norm_mix_pre, norm_mix_post, norm_ffn_pre, norm_ffn_post, ab_in_w, ssm_conv_w, ssm_conv_b, ssm_dt_bias, ssm_a_log, ssm_d, ssm_norm_w, hgrn_lb, hgrn_norm_w, ab_out_w, s5_lam_re, s5_lam_im, s5_log_step, s5_b_re, s5_b_im, s5_c_re, s5_c_im, s5_d, s5_glu_w, ffn_up_w, ffn_conv_w, ffn_conv_b, ffn_down_w):
    batch, seq, d = x_prompt.shape
    dec_batch = x_sample.shape[0]
    depth = norm_mix_pre.shape[0]
    conv_dim = ssm_conv_w.shape[2]
    nheads = ssm_a_log.shape[1]
    d_inner = nheads * SSM_HEAD_DIM
    hg_width = hgrn_lb.shape[1]
    ffn_dim = ffn_down_w.shape[1]
    tm = 256

    def row(v):
        return v.reshape(1, -1)

    def pad_lanes(v):
        return jnp.pad(v, ((0, 0), (0, LANES - v.shape[1])))

    groups = [(x_prompt.reshape(batch * seq, d), True), (x_sample.reshape(dec_batch, d), False)]
    xs = [g[0] for g in groups]
    normed = [None, None]
    out = {'ssm': [[], []], 'sconv': [[], []], 'hg': [[], []], 's5r': [[], []], 's5i': [[], []], 'fconv': [[], []]}

    for l in range(depth):
        e = l // 2
        if l % 2 == 0:
            w_in = ab_in_w[e]
            o_xbc = d_inner
            o_dt = o_xbc + conv_dim
            o_q = o_dt + nheads
            w_z = w_in[:, :o_xbc].astype(BF16)
            w_xbc = w_in[:, o_xbc:o_dt].astype(BF16)
            w_dt = pad_lanes(w_in[:, o_dt:o_q]).astype(BF16)
            w_qfig = w_in[:, o_q:].astype(BF16)
            w_out = ab_out_w[e].astype(BF16)
            dt_bias = pad_lanes(row(ssm_dt_bias[e]))
            a_log = pad_lanes(row(ssm_a_log[e]))
            d_skip = row(jnp.repeat(ssm_d[e], SSM_HEAD_DIM))
            for gi, (_, is_prompt) in enumerate(groups):
                x = xs[gi]
                t = x.shape[0]
                z, xbc, dt, qfig = norm_matmul(x, row(norm_mix_pre[l]), [w_z, w_xbc, w_dt, w_qfig], min(tm, t))
                if is_prompt:
                    y, ssm_new = ssd_prompt(z, xbc, dt, ssm_conv_w[e], row(ssm_conv_b[e]), dt_bias, a_log,
                                            d_skip, row(ssm_norm_w[e]), batch, seq)
                    o, hg_new = hgrn_prompt(qfig, hgrn_lb, row(hgrn_norm_w[e]), e, batch, seq)
                    sconv_new = xbc.reshape(batch, seq, conv_dim)[:, seq - (SSM_CONV - 1):]
                else:
                    hist = [state_ssm_conv[e][:, k] for k in range(SSM_CONV - 1)]
                    y, ssm_new = ssd_step(z, xbc, hist, dt, ssm_conv_w[e], row(ssm_conv_b[e]), dt_bias, a_log,
                                          d_skip, row(ssm_norm_w[e]), state_ssm[e])
                    o, hg_new = hgrn_step(qfig, hgrn_lb, row(hgrn_norm_w[e]), e, state_hgrn[e])
                    sconv_new = jnp.stack(hist[1:] + [xbc], axis=1)
                out['ssm'][gi].append(ssm_new)
                out['hg'][gi].append(hg_new)
                out['sconv'][gi].append(sconv_new)
                xs[gi] = matmul_norm_residual([y, o], [w_out[:d_inner], w_out[d_inner:]],
                                              row(norm_mix_post[l]), x, min(tm, t))
        else:
            ngroups = s5_lam_re.shape[1]
            a_re, a_im, bb_re, bb_im = s5_discretize(
                s5_lam_re[e], s5_lam_im[e], s5_log_step[e],
                jnp.swapaxes(s5_b_re[e], 1, 2), jnp.swapaxes(s5_b_im[e], 1, 2))
            a_re = a_re.reshape(1, ngroups * S5_STATE)
            a_im = a_im.reshape(1, ngroups * S5_STATE)
            b_exp = jnp.concatenate([_block_diag(bb_re), _block_diag(bb_im)], axis=2).astype(BF16)
            c_re_exp = _block_diag(jnp.swapaxes(s5_c_re[e], 1, 2)).astype(BF16)
            c_im_exp = _block_diag(jnp.swapaxes(s5_c_im[e], 1, 2)).astype(BF16)
            glu_w = s5_glu_w[e].astype(BF16)
            for gi, (_, is_prompt) in enumerate(groups):
                x = xs[gi]
                t = x.shape[0]
                u = normed[gi] if normed[gi] is not None else rmsnorm(x, row(norm_mix_pre[l]), min(tm, t))
                if is_prompt:
                    yg, s_re, s_im = s5_prompt(u, b_exp, c_re_exp, c_im_exp, row(s5_d[e]), a_re, a_im, batch, seq)
                else:
                    yg, s_re, s_im = s5_step(u, b_exp, c_re_exp, c_im_exp, row(s5_d[e]), a_re, a_im,
                                             state_s5_re[e].reshape(t, -1), state_s5_im[e].reshape(t, -1))
                out['s5r'][gi].append(s_re.reshape(-1, ngroups, S5_STATE))
                out['s5i'][gi].append(s_im.reshape(-1, ngroups, S5_STATE))
                xs[gi] = matmul_norm_residual([yg], [glu_w], row(norm_mix_post[l]), x, min(tm, t), glu=True)

        up_w = ffn_up_w[l].astype(BF16)
        down_w = ffn_down_w[l].astype(BF16)
        for gi, (_, is_prompt) in enumerate(groups):
            x = xs[gi]
            t = x.shape[0]
            normed[gi] = None
            if is_prompt:
                g_next = row(norm_mix_pre[l + 1]) if l + 1 < depth and (l + 1) % 2 == 1 else None
                res = ffn_seq(x, row(norm_ffn_pre[l]), up_w, ffn_conv_w[l], row(ffn_conv_b[l]), down_w,
                              row(norm_ffn_post[l]), g_next, tm, seq)
                xs[gi], tail = res[0], res[1]
                if g_next is not None:
                    normed[gi] = res[2]
                fconv_new = tail[:, SUBLANES - (FFN_CONV - 1):]
            else:
                (up,) = norm_matmul(x, row(norm_ffn_pre[l]), [up_w], t)
                old = state_ffn_conv[l]
                xs[gi] = ffn_tail_step(up, old[:, 0], old[:, 1], ffn_conv_w[l], row(ffn_conv_b[l]), down_w,
                                       row(norm_ffn_post[l]), x)
                fconv_new = jnp.stack([old[:, 1], up], axis=1)
            out['fconv'][gi].append(fconv_new)

    y_prompt = xs[0].reshape(batch, seq, d)
    y_sample = xs[1].reshape(dec_batch, 1, d)
    states = []
    for gi in range(2):
        states += [jnp.stack(out[k][gi]) for k in ('ssm', 'sconv', 'hg', 's5r', 's5i', 'fconv')]
    return (y_prompt, y_sample, *states)
```

```python
import functools
import math

import jax
import jax.numpy as jnp
import numpy as np
from jax import lax
from jax.experimental import pallas as pl
from jax.experimental.pallas import tpu as pltpu

F32 = jnp.float32
BF16 = jnp.bfloat16
EPS = 1e-6

LANES = 128
SUBLANES = 8
VMEM_LIMIT_BYTES = 56 * 1024 * 1024

SSM_HEAD_DIM = 64
SSM_GROUPS = 2
SSM_STATE = 64
SSM_CONV = 4
SSM_CHUNK = 128
HG_DIM = 128
HG_CHUNK = 128
S5_GROUP = 16
S5_STATE = 64
S5_GROUPS_PER_BLOCK = LANES // S5_GROUP
S5_BLOCK_STATE = S5_GROUPS_PER_BLOCK * S5_STATE
S5_SEGMENTS = SUBLANES
FFN_CONV = 3


def _params(sem):
    return pltpu.CompilerParams(dimension_semantics=sem, vmem_limit_bytes=VMEM_LIMIT_BYTES)


def _resident(shape):
    nd = len(shape)
    return pl.BlockSpec(shape, lambda *_: (0,) * nd, pipeline_mode=pl.Buffered(1))


def _rms(x, w):
    return x * lax.rsqrt(jnp.mean(x * x, axis=-1, keepdims=True) + EPS) * w


def _silu(x):
    return x * jax.nn.sigmoid(x)


def _dot(a, b):
    return jnp.dot(a, b, preferred_element_type=F32)


def _dot_nt(a, b):
    return lax.dot_general(a, b, (((1,), (1,)), ((), ())), preferred_element_type=F32)


def _dot_tn(a, b):
    return lax.dot_general(a, b, (((0,), (0,)), ((), ())), preferred_element_type=F32)


def _norm_matmul_kernel(n_out, x_ref, g_ref, *refs):
    w_refs, o_refs = refs[:n_out], refs[n_out:]
    xn = _rms(x_ref[...], g_ref[...]).astype(BF16)
    for w_ref, o_ref in zip(w_refs, o_refs):
        n = w_ref.shape[1]
        for c0 in range(0, n, 512):
            c1 = min(n, c0 + 512)
            o_ref[:, c0:c1] = _dot(xn, w_ref[:, c0:c1])


def norm_matmul(x, g, ws, tm):
    t, d = x.shape
    kern = functools.partial(_norm_matmul_kernel, len(ws))
    return pl.pallas_call(
        kern,
        grid=(t // tm,),
        in_specs=[pl.BlockSpec((tm, d), lambda i: (i, 0)), _resident((1, d))]
        + [_resident(w.shape) for w in ws],
        out_specs=[pl.BlockSpec((tm, w.shape[1]), lambda i: (i, 0)) for w in ws],
        out_shape=[jax.ShapeDtypeStruct((t, w.shape[1]), F32) for w in ws],
        compiler_params=_params(("parallel",)),
        name="norm_matmul",
    )(x, g, *ws)


def _rmsnorm_kernel(x_ref, g_ref, o_ref):
    o_ref[...] = _rms(x_ref[...], g_ref[...])


def rmsnorm(x, g, tm):
    t, d = x.shape
    return pl.pallas_call(
        _rmsnorm_kernel,
        grid=(t // tm,),
        in_specs=[pl.BlockSpec((tm, d), lambda i: (i, 0)), _resident((1, d))],
        out_specs=pl.BlockSpec((tm, d), lambda i: (i, 0)),
        out_shape=jax.ShapeDtypeStruct((t, d), F32),
        compiler_params=_params(("parallel",)),
        name="rmsnorm",
    )(x, g)


def _matmul_norm_residual_kernel(n_in, glu, *refs):
    a_refs, w_refs = refs[:n_in], refs[n_in:2 * n_in]
    g_ref, x_ref, o_ref = refs[2 * n_in:]
    acc = None
    for a_ref, w_ref in zip(a_refs, w_refs):
        part = _dot(a_ref[...].astype(BF16), w_ref[...])
        acc = part if acc is None else acc + part
    if glu:
        d = acc.shape[1] // 2
        acc = acc[:, :d] * jax.nn.sigmoid(acc[:, d:])
    o_ref[...] = x_ref[...] + _rms(acc, g_ref[...])


def matmul_norm_residual(a_list, w_list, g, x, tm, glu=False):
    t, d = x.shape
    kern = functools.partial(_matmul_norm_residual_kernel, len(a_list), glu)
    return pl.pallas_call(
        kern,
        grid=(t // tm,),
        in_specs=[pl.BlockSpec((tm, a.shape[1]), lambda i: (i, 0)) for a in a_list]
        + [_resident(w.shape) for w in w_list]
        + [_resident((1, d)), pl.BlockSpec((tm, d), lambda i: (i, 0))],
        out_specs=pl.BlockSpec((tm, d), lambda i: (i, 0)),
        out_shape=jax.ShapeDtypeStruct((t, d), F32),
        compiler_params=_params(("parallel",)),
        name="matmul_norm_residual",
    )(*a_list, *w_list, g, x)


FFN_CHUNK = 256


def _ffn_tail(up_ref, xm2_of, xm1_of, cw_ref, cb_ref, dw_ref, g_ref, x_ref, o_ref):
    f = dw_ref.shape[0]
    acc = jnp.zeros(o_ref.shape, F32)
    for c0 in range(0, f, FFN_CHUNK):
        c1 = c0 + FFN_CHUNK
        halves = []
        for off in (0, f):
            a, b = c0 + off, c1 + off
            halves.append(cb_ref[:, a:b] + xm2_of(a, b) * cw_ref[0:1, a:b]
                          + xm1_of(a, b) * cw_ref[1:2, a:b] + up_ref[:, a:b] * cw_ref[2:3, a:b])
        act = (halves[0] * _silu(halves[1])).astype(BF16)
        acc = acc + _dot(act, dw_ref[c0:c1, :])
    o_ref[...] = x_ref[...] + _rms(acc, g_ref[...])


def _ffn_seq_kernel(tiles_per_seq, emit_next, x_ref, g1_ref, uw_ref, cw_ref, cb_ref, dw_ref, g2_ref, *rest):
    if emit_next:
        g3_ref, o_ref, tail_ref, n_ref, carry_sc = rest
    else:
        o_ref, tail_ref, carry_sc = rest
    tm = x_ref.shape[0]
    f = dw_ref.shape[0]

    @pl.when(pl.program_id(0) % tiles_per_seq == 0)
    def _():
        carry_sc[...] = jnp.zeros_like(carry_sc)

    x = x_ref[...]
    xn = _rms(x, g1_ref[...]).astype(BF16)
    top_row = lax.broadcasted_iota(jnp.int32, (SUBLANES, FFN_CHUNK), 0)

    def earlier(up, prev, k):
        rolled = pltpu.roll(up, k, axis=0)
        top = jnp.where(top_row < k, pltpu.roll(prev, k, axis=0), rolled[:SUBLANES])
        return jnp.concatenate([top, rolled[SUBLANES:]], axis=0)

    def project(c0):
        return [_dot(xn, uw_ref[:, c0 + off:c0 + off + FFN_CHUNK]) for off in (0, f)]

    acc = jnp.zeros(o_ref.shape, F32)
    ups_next = project(0)
    act_prev = None
    for c0 in range(0, f, FFN_CHUNK):
        ups = ups_next
        if c0 + FFN_CHUNK < f:
            ups_next = project(c0 + FFN_CHUNK)
        if act_prev is not None:
            acc = acc + _dot(act_prev, dw_ref[c0 - FFN_CHUNK:c0, :])
        halves = []
        for up, off in zip(ups, (0, f)):
            a, b = c0 + off, c0 + off + FFN_CHUNK
            prev = carry_sc[:, a:b]
            carry_sc[:, a:b] = up[tm - SUBLANES:, :]
            tail_ref[0, :, a:b] = up[tm - SUBLANES:, :]
            halves.append(cb_ref[:, a:b] + earlier(up, prev, 2) * cw_ref[0:1, a:b]
                          + earlier(up, prev, 1) * cw_ref[1:2, a:b] + up * cw_ref[2:3, a:b])
        act_prev = (halves[0] * _silu(halves[1])).astype(BF16)
    acc = acc + _dot(act_prev, dw_ref[f - FFN_CHUNK:f, :])
    out = x + _rms(acc, g2_ref[...])
    o_ref[...] = out
    if emit_next:
        n_ref[...] = _rms(out, g3_ref[...])


def ffn_seq(x, g_pre, up_w, conv_w, conv_b, down_w, g_post, g_next, tm, seq):
    t, d = x.shape
    f2 = up_w.shape[1]
    tiles_per_seq = seq // tm
    emit_next = g_next is not None
    kern = functools.partial(_ffn_seq_kernel, tiles_per_seq, emit_next)
    tile = pl.BlockSpec((tm, d), lambda i: (i, 0))
    return pl.pallas_call(
        kern,
        grid=(t // tm,),
        in_specs=[tile, _resident((1, d)), _resident(up_w.shape), _resident(conv_w.shape), _resident((1, f2)),
                  _resident(down_w.shape), _resident((1, d))] + [_resident((1, d))] * emit_next,
        out_specs=[tile, pl.BlockSpec((1, SUBLANES, f2), lambda i: (i // tiles_per_seq, 0, 0))]
        + [tile] * emit_next,
        out_shape=[jax.ShapeDtypeStruct((t, d), F32), jax.ShapeDtypeStruct((t // seq, SUBLANES, f2), F32)]
        + [jax.ShapeDtypeStruct((t, d), F32)] * emit_next,
        scratch_shapes=[pltpu.VMEM((SUBLANES, f2), F32)],
        compiler_params=_params(("arbitrary",)),
        name="ffn_seq",
    )(x, g_pre, up_w, conv_w, conv_b, down_w, g_post, *([g_next] * emit_next))


def _ffn_step_kernel(up_ref, xm2_ref, xm1_ref, cw_ref, cb_ref, dw_ref, g_ref, x_ref, o_ref):
    _ffn_tail(up_ref, lambda a, b: xm2_ref[:, a:b], lambda a, b: xm1_ref[:, a:b],
              cw_ref, cb_ref, dw_ref, g_ref, x_ref, o_ref)


def ffn_tail_step(up, xm2, xm1, conv_w, conv_b, down_w, g, x):
    t, d = x.shape
    f2 = up.shape[1]
    return pl.pallas_call(
        _ffn_step_kernel,
        grid=(1,),
        in_specs=[_resident((t, f2))] * 3
        + [_resident(conv_w.shape), _resident((1, f2)), _resident(down_w.shape),
           _resident((1, d)), _resident((t, d))],
        out_specs=pl.BlockSpec((t, d), lambda i: (0, 0)),
        out_shape=jax.ShapeDtypeStruct((t, d), F32),
        compiler_params=_params(("arbitrary",)),
        name="ffn_tail_step",
    )(up, xm2, xm1, conv_w, conv_b, down_w, g, x)


def _ssd_gate_norm(y, xs, z, dsk_ref, nw_ref):
    y = y + dsk_ref[...] * xs
    return _rms(y * _silu(z), nw_ref[...])


def _head_expansion(nheads, head_dim):
    e = np.zeros((LANES, nheads * head_dim), np.float32)
    for h in range(nheads):
        e[h, h * head_dim:(h + 1) * head_dim] = 1.0
    return e


def _expand_heads(v, exp_ref):
    hi = v.astype(BF16)
    lo = (v - hi.astype(F32)).astype(BF16)
    return _dot(hi, exp_ref[...]) + _dot(lo, exp_ref[...])


def _ssd_prompt_kernel(x_ref, gpre_ref, wz_ref, wxbc_ref, wdt_ref, other_ref, wout_ref, gpost_ref,
                       cw_ref, cb_ref, dtb_ref, alog_ref, dsk_ref, nw_ref, exp_ref,
                       xo_ref, hout_ref, histout_ref, hist_sc, h_sc, xp_sc, y_sc):
    c = pl.program_id(1)
    nchunks = pl.num_programs(1)
    cl = x_ref.shape[0]
    hn, d_inner = h_sc.shape
    hp = SSM_HEAD_DIM
    nheads = d_inner // hp
    heads_per_group = nheads // SSM_GROUPS
    group_width = heads_per_group * hp

    @pl.when(c == 0)
    def _():
        hist_sc[...] = jnp.zeros_like(hist_sc)
        h_sc[...] = jnp.zeros_like(h_sc)

    x = x_ref[...]
    xn = _rms(x, gpre_ref[...]).astype(BF16)
    raw = _dot(xn, wxbc_ref[...])
    xp_sc[0:SUBLANES, :] = hist_sc[...]
    xp_sc[SUBLANES:, :] = raw
    hist_sc[...] = raw[cl - SUBLANES:, :]
    conv = cb_ref[...]
    for k in range(SSM_CONV):
        conv = conv + xp_sc[pl.ds(SUBLANES - (SSM_CONV - 1) + k, cl), :] * cw_ref[k:k + 1, :]
    act = _silu(conv)
    xs = act[:, :d_inner]
    bm = act[:, d_inner:d_inner + SSM_GROUPS * SSM_STATE]
    cm = act[:, d_inner + SSM_GROUPS * SSM_STATE:]

    dt = jax.nn.softplus(_dot(xn, wdt_ref[...]) + dtb_ref[...])
    da = dt * (-jnp.exp(alog_ref[...]))
    ti = lax.broadcasted_iota(jnp.int32, (cl, cl), 0)
    si = lax.broadcasted_iota(jnp.int32, (cl, cl), 1)
    tril = ti >= si
    cum_col = jnp.dot(tril.astype(F32), da, preferred_element_type=F32,
                      precision=lax.Precision.HIGHEST)
    cum_row = cum_col.T
    dt_row = dt.T
    last = cum_col[cl - 1:cl, :]

    per_head = jnp.concatenate([jnp.exp(cum_col), jnp.exp(last - cum_col) * dt,
                                jnp.broadcast_to(jnp.exp(last), (SUBLANES, LANES))], axis=0)
    wide = _expand_heads(per_head, exp_ref)
    from_start, to_end, chunk_decay = wide[:cl], wide[cl:2 * cl], wide[2 * cl:2 * cl + 1]

    z = _dot(xn, wz_ref[...])
    xs16 = xs.astype(BF16)
    xw16 = (xs * to_end).astype(BF16)
    bm_t = bm.T
    st = h_sc[...]
    st16 = st.astype(BF16)
    cbs, state_terms, updates = [], [], []
    for g in range(SSM_GROUPS):
        sl = slice(g * SSM_STATE, (g + 1) * SSM_STATE)
        gl = slice(g * group_width, (g + 1) * group_width)
        cm_g = cm[:, sl].astype(BF16)
        cbs.append(_dot_nt(cm_g, bm[:, sl].astype(BF16)))
        state_terms.append(_dot(cm_g, st16[:, gl]))
        updates.append(_dot(bm_t[sl, :].astype(BF16), xw16[:, gl]))
    h_sc[...] = st * chunk_decay + jnp.concatenate(updates, axis=1)
    y_sc[...] = jnp.concatenate(state_terms, axis=1) * from_start

    for h in range(nheads):
        hl = slice(h * hp, (h + 1) * hp)
        seg = jnp.exp(jnp.where(tril, cum_col[:, h:h + 1] - cum_row[h:h + 1, :], -jnp.inf))
        wts = cbs[h // heads_per_group] * seg * dt_row[h:h + 1, :]
        y_sc[:, hl] = y_sc[:, hl] + _dot(wts.astype(BF16), xs16[:, hl])

    y = _ssd_gate_norm(y_sc[...], xs, z, dsk_ref, nw_ref).astype(BF16)
    xo_ref[...] = x + _rms(_dot(y, wout_ref[...]) + other_ref[...], gpost_ref[...])

    @pl.when(c == nchunks - 1)
    def _():
        histout_ref[0] = hist_sc[...]
        final = h_sc[...]
        pad = jnp.zeros((LANES - hn, LANES), F32)
        for j in range(d_inner // LANES):
            tile_t = jnp.concatenate([final[:, j * LANES:(j + 1) * LANES], pad], axis=0).T
            for r in range(LANES // hp):
                hout_ref[0, j * (LANES // hp) + r] = tile_t[r * hp:(r + 1) * hp, :hn]


def ssd_prompt(x, g_pre, w_z, w_xbc, w_dt, other, w_out, g_post, conv_w, conv_b, dt_bias, a_log, d_skip, norm_w,
               batch, seq):
    t, d = x.shape
    d_inner = w_z.shape[1]
    conv_dim = w_xbc.shape[1]
    nheads = d_inner // SSM_HEAD_DIM
    cl = SSM_CHUNK
    nchunks = seq // cl
    row = lambda b, c: (b * nchunks + c, 0)
    return pl.pallas_call(
        _ssd_prompt_kernel,
        grid=(batch, nchunks),
        in_specs=[pl.BlockSpec((cl, d), row), _resident((1, d)), _resident(w_z.shape), _resident(w_xbc.shape),
                  _resident(w_dt.shape), pl.BlockSpec((cl, d), row), _resident(w_out.shape), _resident((1, d)),
                  _resident(conv_w.shape), _resident((1, conv_dim)), _resident((1, LANES)),
                  _resident((1, LANES)), _resident((1, d_inner)), _resident((1, d_inner)),
                  _resident((LANES, d_inner))],
        out_specs=[pl.BlockSpec((cl, d), row),
                   pl.BlockSpec((1, nheads, SSM_HEAD_DIM, SSM_STATE), lambda b, c: (b, 0, 0, 0)),
                   pl.BlockSpec((1, SUBLANES, conv_dim), lambda b, c: (b, 0, 0))],
        out_shape=[jax.ShapeDtypeStruct((t, d), F32),
                   jax.ShapeDtypeStruct((batch, nheads, SSM_HEAD_DIM, SSM_STATE), F32),
                   jax.ShapeDtypeStruct((batch, SUBLANES, conv_dim), F32)],
        scratch_shapes=[pltpu.VMEM((SUBLANES, conv_dim), F32),
                        pltpu.VMEM((SSM_STATE, d_inner), F32),
                        pltpu.VMEM((cl + SUBLANES, conv_dim), F32),
                        pltpu.VMEM((cl, d_inner), F32)],
        compiler_params=_params(("arbitrary", "arbitrary")),
        name="ssd_prompt",
    )(x, g_pre, w_z, w_xbc, w_dt, other, w_out, g_post, conv_w, conv_b, dt_bias, a_log, d_skip, norm_w,
      jnp.asarray(_head_expansion(nheads, SSM_HEAD_DIM), BF16))


def _ssd_step_kernel(z_ref, xnew_ref, h0_ref, h1_ref, h2_ref, dt_ref, cw_ref, cb_ref, dtb_ref, alog_ref,
                     dsk_ref, nw_ref, st_ref, y_ref, stout_ref,
                     xs_sc, xst_sc, bmt_sc, cmt_sc, dtt_sc, dat_sc, yt_sc):
    h = pl.program_id(0)
    nheads = pl.num_programs(0)
    d_inner = xs_sc.shape[1]
    hp, hn = SSM_HEAD_DIM, SSM_STATE
    heads_per_group = d_inner // hp // SSM_GROUPS

    @pl.when(h == 0)
    def _():
        conv = (cb_ref[...] + h0_ref[...] * cw_ref[0:1, :] + h1_ref[...] * cw_ref[1:2, :]
                + h2_ref[...] * cw_ref[2:3, :] + xnew_ref[...] * cw_ref[3:4, :])
        act = _silu(conv)
        xs = act[:, :d_inner]
        dt = jax.nn.softplus(dt_ref[...] + dtb_ref[...])
        xs_sc[...] = xs
        xst_sc[...] = xs.T
        bmt_sc[...] = act[:, d_inner:d_inner + SSM_GROUPS * hn].T
        cmt_sc[...] = act[:, d_inner + SSM_GROUPS * hn:].T
        dtt_sc[...] = dt.T
        dat_sc[...] = jnp.exp(dt * (-jnp.exp(alog_ref[...]))).T

    g = h // heads_per_group
    bmt = bmt_sc[pl.ds(pl.multiple_of(g * hn, hn), hn), :]
    cmt = cmt_sc[pl.ds(pl.multiple_of(g * hn, hn), hn), :]
    da = dat_sc[pl.ds(h, 1), :]
    dtx = xst_sc[pl.ds(pl.multiple_of(h * hp, hp), hp), :] * dtt_sc[pl.ds(h, 1), :]
    rows_per_tile = LANES // hn
    for j in range(hp // rows_per_tile):
        tile = st_ref[:, j * LANES:(j + 1) * LANES].T
        news = []
        for r in range(rows_per_tile):
            p = j * rows_per_tile + r
            new = da * tile[r * hn:(r + 1) * hn, :] + dtx[p:p + 1, :] * bmt
            yt_sc[pl.ds(h * hp + p, 1), :] = jnp.sum(new * cmt, axis=0, keepdims=True)
            news.append(new)
        stout_ref[:, j * LANES:(j + 1) * LANES] = jnp.concatenate(news, axis=0).T

    @pl.when(h == nheads - 1)
    def _():
        y_ref[...] = _ssd_gate_norm(yt_sc[...].T, xs_sc[...], z_ref[...], dsk_ref, nw_ref).astype(y_ref.dtype)


def ssd_step(z, xnew, hist, dt, conv_w, conv_b, dt_bias, a_log, d_skip, norm_w, state):
    b, d_inner = z.shape
    conv_dim = xnew.shape[1]
    nheads = d_inner // SSM_HEAD_DIM
    per_head = SSM_HEAD_DIM * SSM_STATE
    full = lambda shape: pl.BlockSpec(shape, lambda h: (0, 0))
    y, st = pl.pallas_call(
        _ssd_step_kernel,
        grid=(nheads,),
        in_specs=[full((b, d_inner))] + [full((b, conv_dim))] * 4 + [full((b, LANES))]
        + [full(conv_w.shape), full((1, conv_dim)), full((1, LANES)), full((1, LANES)),
           full((1, d_inner)), full((1, d_inner)),
           pl.BlockSpec((b, per_head), lambda h: (0, h))],
        out_specs=[full((b, d_inner)), pl.BlockSpec((b, per_head), lambda h: (0, h))],
        out_shape=[jax.ShapeDtypeStruct((b, d_inner), BF16),
                   jax.ShapeDtypeStruct((b, nheads * per_head), F32)],
        scratch_shapes=[pltpu.VMEM((b, d_inner), F32), pltpu.VMEM((d_inner, b), F32),
                        pltpu.VMEM((SSM_GROUPS * SSM_STATE, b), F32),
                        pltpu.VMEM((SSM_GROUPS * SSM_STATE, b), F32),
                        pltpu.VMEM((LANES, b), F32), pltpu.VMEM((LANES, b), F32),
                        pltpu.VMEM((d_inner, b), F32)],
        compiler_params=_params(("arbitrary",)),
        name="ssd_step",
    )(z, xnew, hist[0], hist[1], hist[2], dt, conv_w, conv_b, dt_bias, a_log, d_skip, norm_w,
      state.reshape(b, nheads * per_head))
    return y, st.reshape(b, nheads, SSM_HEAD_DIM, SSM_STATE)


def _hgrn_lower_bound(lbp_ref, layer):
    raw = lbp_ref[...]
    e = jnp.exp(raw - jnp.max(raw, axis=0, keepdims=True))
    return jnp.sum(e[:layer + 1], axis=0, keepdims=True) / jnp.sum(e, axis=0, keepdims=True)


def _hgrn_gates(q_raw, f_raw, lb):
    q = _silu(q_raw)
    f = lb + (1.0 - lb) * jax.nn.sigmoid(f_raw)
    k = (1.0 - lb) * jax.nn.sigmoid(-f_raw)
    return q, f, k


def _pair_levels(cl):
    t = np.arange(cl)[:, None]
    s = np.arange(cl)[None, :]
    lvl = np.floor(np.log2(np.maximum(t ^ s, 1))).astype(np.int32)
    return np.where(t > s, lvl, np.where(t == s, -1, -2)).astype(np.int32)


HG_HEADS_PER_DOT = 2


def _hgrn_prompt_kernel(layer, x_ref, gpre_ref, win_ref, wout_ref, lvl_ref, lbp_ref, nw_ref,
                        m_ref, sout_ref, st_sc):
    c = pl.program_id(1)
    nchunks = pl.num_programs(1)
    cl = x_ref.shape[0]
    nheads = st_sc.shape[0]
    width = nheads * HG_DIM
    span = HG_HEADS_PER_DOT * HG_DIM

    @pl.when(c == 0)
    def _():
        st_sc[...] = jnp.zeros_like(st_sc)

    lb_all = _hgrn_lower_bound(lbp_ref, layer)
    row = lax.broadcasted_iota(jnp.int32, (cl, LANES), 0)
    lvl = lvl_ref[...]
    tril16 = (lvl >= -1).astype(BF16)
    xn = _rms(x_ref[...], gpre_ref[...]).astype(BF16)

    def project(j):
        return [_dot(xn, win_ref[:, which * width + j * span:which * width + (j + 1) * span])
                for which in range(4)]

    heads = range(nheads)
    raw = [[], [], [], []]
    for j in range(nheads // HG_HEADS_PER_DOT):
        for which, both in enumerate(project(j)):
            raw[which] += [both[:, hh * HG_DIM:(hh + 1) * HG_DIM] for hh in range(HG_HEADS_PER_DOT)]
    q, k, cum = [], [], []
    for h in heads:
        qh, fh, kh = _hgrn_gates(raw[0][h], raw[1][h], lb_all[:, h * HG_DIM:(h + 1) * HG_DIM])
        q.append(qh)
        k.append(kh)
        rest = jnp.log(fh)
        total = None
        for _ in range(3):
            term = rest.astype(BF16)
            rest = rest - term.astype(F32)
            part = _dot(tril16, term)
            total = part if total is None else total + part
        cum.append(total)

    att = [jnp.where(lvl == -1, _dot_nt(q[h].astype(BF16), k[h].astype(BF16)), 0.0) for h in heads]
    last_of_block = list(cum)
    blk = 1
    level = 0
    while blk < cl:
        odd = (row & blk) != 0
        for h in heads:
            expo = jnp.where(odd, cum[h] - pltpu.roll(last_of_block[h], blk, axis=0), last_of_block[h] - cum[h])
            scaled = (jnp.where(odd, q[h], k[h]) * jnp.exp(expo)).astype(BF16)
            att[h] = jnp.where(lvl == level, _dot_nt(scaled, scaled), att[h])
            last_of_block[h] = jnp.where(odd, last_of_block[h], pltpu.roll(last_of_block[h], cl - blk, axis=0))
        blk *= 2
        level += 1
    cum_last = last_of_block

    outs = []
    for h in heads:
        st = st_sc[h]
        v = raw[2][h]
        o = (_dot(att[h].astype(BF16), v.astype(BF16))
             + _dot_nt((q[h] * jnp.exp(cum[h])).astype(BF16), st.astype(BF16)))
        k_end = (k[h] * jnp.exp(cum_last[h] - cum[h])).astype(BF16)
        st_sc[h] = st * jnp.exp(cum_last[h]) + _dot(v.T.astype(BF16), k_end)
        outs.append((_rms(o, nw_ref[...]) * _silu(raw[3][h])).astype(BF16))
    m_ref[...] = _dot(jnp.concatenate(outs, axis=1), wout_ref[...])

    @pl.when(c == nchunks - 1)
    def _():
        for h in range(nheads):
            sout_ref[0, h] = st_sc[h].T


def hgrn_prompt(x, g_pre, w_qfig, w_out, lb_param, norm_w, layer, batch, seq):
    t, d = x.shape
    width = w_out.shape[0]
    nheads = width // HG_DIM
    cl = HG_CHUNK
    nchunks = seq // cl
    row = lambda b, c: (b * nchunks + c, 0)
    kern = functools.partial(_hgrn_prompt_kernel, layer)
    return pl.pallas_call(
        kern,
        grid=(batch, nchunks),
        in_specs=[pl.BlockSpec((cl, d), row), _resident((1, d)), _resident(w_qfig.shape), _resident(w_out.shape),
                  _resident((cl, cl)), _resident(lb_param.shape), _resident((1, HG_DIM))],
        out_specs=[pl.BlockSpec((cl, d), row),
                   pl.BlockSpec((1, nheads, HG_DIM, HG_DIM), lambda b, c: (b, 0, 0, 0))],
        out_shape=[jax.ShapeDtypeStruct((t, d), F32),
                   jax.ShapeDtypeStruct((batch, nheads, HG_DIM, HG_DIM), F32)],
        scratch_shapes=[pltpu.VMEM((nheads, HG_DIM, HG_DIM), F32)],
        compiler_params=_params(("arbitrary", "arbitrary")),
        name="hgrn_prompt",
    )(x, g_pre, w_qfig, w_out, jnp.asarray(_pair_levels(cl)), lb_param, norm_w)


HG_STEP_ROWS = 32


def _hgrn_step_kernel(layer, qfig_ref, lbp_ref, nw_ref, st_ref, o_ref, stout_ref,
                      qt_sc, ft_sc, kt_sc, vt_sc, ot_sc, acc_sc):
    h = pl.program_id(0)
    kc = pl.program_id(1)
    nheads = pl.num_programs(0)
    nkc = pl.num_programs(1)
    width = qt_sc.shape[0]

    @pl.when((h == 0) & (kc == 0))
    def _():
        raw = lbp_ref[...]
        e = jnp.exp(raw - jnp.max(raw, axis=0, keepdims=True))
        lb = jnp.sum(e[:layer + 1], axis=0, keepdims=True) / jnp.sum(e, axis=0, keepdims=True)
        q, f, k = _hgrn_gates(qfig_ref[:, 0:width], qfig_ref[:, width:2 * width], lb)
        qt_sc[...] = q.T
        ft_sc[...] = f.T
        kt_sc[...] = k.T
        vt_sc[...] = qfig_ref[:, 2 * width:3 * width].T

    @pl.when(kc == 0)
    def _():
        acc_sc[...] = jnp.zeros_like(acc_sc)

    base = pl.multiple_of(h * HG_DIM, HG_DIM)
    vt = vt_sc[pl.ds(base, HG_DIM), :]
    acc = acc_sc[...]
    for r in range(HG_STEP_ROWS):
        krow = base + kc * HG_STEP_ROWS + r
        new = (ft_sc[pl.ds(krow, 1), :] * st_ref[:, r * HG_DIM:(r + 1) * HG_DIM].T
               + kt_sc[pl.ds(krow, 1), :] * vt)
        acc = acc + new * qt_sc[pl.ds(krow, 1), :]
        stout_ref[:, r * HG_DIM:(r + 1) * HG_DIM] = new.T
    acc_sc[...] = acc

    @pl.when(kc == nkc - 1)
    def _():
        ot_sc[pl.ds(base, HG_DIM), :] = acc

    @pl.when((h == nheads - 1) & (kc == nkc - 1))
    def _():
        o = ot_sc[...].T
        for hh in range(width // HG_DIM):
            sl = slice(hh * HG_DIM, (hh + 1) * HG_DIM)
            gate = qfig_ref[:, 3 * width + hh * HG_DIM:3 * width + (hh + 1) * HG_DIM]
            o_ref[:, sl] = (_rms(o[:, sl], nw_ref[...]) * _silu(gate)).astype(o_ref.dtype)


def hgrn_step(qfig, lb_param, norm_w, layer, state):
    b, w4 = qfig.shape
    width = w4 // 4
    nheads = width // HG_DIM
    nkc = HG_DIM // HG_STEP_ROWS
    blk = HG_STEP_ROWS * HG_DIM
    kern = functools.partial(_hgrn_step_kernel, layer)
    full = lambda shape: pl.BlockSpec(shape, lambda h, kc: (0, 0))
    o, st = pl.pallas_call(
        kern,
        grid=(nheads, nkc),
        in_specs=[full((b, w4)), full(lb_param.shape), full((1, HG_DIM)),
                  pl.BlockSpec((b, blk), lambda h, kc: (0, h * nkc + kc))],
        out_specs=[full((b, width)), pl.BlockSpec((b, blk), lambda h, kc: (0, h * nkc + kc))],
        out_shape=[jax.ShapeDtypeStruct((b, width), BF16),
                   jax.ShapeDtypeStruct((b, nheads * HG_DIM * HG_DIM), F32)],
        scratch_shapes=[pltpu.VMEM((width, b), F32)] * 5 + [pltpu.VMEM((HG_DIM, b), F32)],
        compiler_params=_params(("arbitrary", "arbitrary")),
        name="hgrn_step",
    )(qfig, lb_param, norm_w, state.reshape(b, nheads * HG_DIM * HG_DIM))
    return o, st.reshape(b, nheads, HG_DIM, HG_DIM)


def _s5_discretize_kernel(lre_ref, lim_ref, step_ref, bre_ref, bim_ref, are_ref, aim_ref, bbre_ref, bbim_ref):
    lre, lim = lre_ref[...], lim_ref[...]
    step = jnp.exp(step_ref[...])
    mag = jnp.exp(lre * step)
    are = mag * jnp.cos(lim * step)
    aim = mag * jnp.sin(lim * step)
    den = lre * lre + lim * lim
    nr = are - 1.0
    cre = (nr * lre + aim * lim) / den
    cim = (aim * lre - nr * lim) / den
    are_ref[...] = are
    aim_ref[...] = aim
    bre, bim = bre_ref[...], bim_ref[...]
    bbre_ref[...] = cre * bre - cim * bim
    bbim_ref[...] = cre * bim + cim * bre


def s5_discretize(lam_re, lam_im, log_step, b_re_t, b_im_t):
    g, n = lam_re.shape
    c = b_re_t.shape[1]
    return pl.pallas_call(
        _s5_discretize_kernel,
        out_shape=[jax.ShapeDtypeStruct((g, 1, n), F32)] * 2 + [jax.ShapeDtypeStruct((g, c, n), F32)] * 2,
        name="s5_discretize",
    )(lam_re.reshape(g, 1, n), lam_im.reshape(g, 1, n), log_step.reshape(g, 1, 1), b_re_t, b_im_t)


def _complex_power(re, im, n):
    out = None
    while n:
        if n & 1:
            out = (re, im) if out is None else (out[0] * re - out[1] * im, out[0] * im + out[1] * re)
        n >>= 1
        if n:
            re, im = re * re - im * im, 2.0 * re * im
    return out


def _s5_output(xre, xim, u, cre_ref, cim_ref, d_ref):
    y = _dot(xre.astype(BF16), cre_ref[0]) - _dot(xim.astype(BF16), cim_ref[0]) + d_ref[...] * u
    return jax.nn.gelu(y)


S5_ROWS = 256


def _s5_prompt_kernel(u_ref, b_ref, cre_ref, cim_ref, d_ref, are_ref, aim_ref,
                      y_ref, sre_ref, sim_ref, x_sc, ui_sc):
    nrows = u_ref.shape[0]
    ns = S5_BLOCK_STATE
    nt = ns // LANES
    nseg = S5_SEGMENTS
    seg = nrows // nseg

    steps = S5_ROWS // nseg

    def same_step_rows(i):
        return pl.ds(i, nseg, stride=seg)

    nslabs = nrows // S5_ROWS

    def slab_rows(s):
        return slice(s * S5_ROWS, (s + 1) * S5_ROWS)

    def project(s):
        r = slab_rows(s)
        ui = jnp.concatenate([u_ref[same_step_rows(s * steps + ii), :] for ii in range(steps)], axis=0)
        ui_sc[r, :] = ui
        bu = _dot(ui.astype(BF16), b_ref[0])
        for k in range(2 * nt):
            x_sc[k, r, :] = bu[:, k * LANES:(k + 1) * LANES]

    def lane_tile(ref, k):
        return ref[:, k * LANES:(k + 1) * LANES]

    are = [jnp.broadcast_to(lane_tile(are_ref, k), (nseg, LANES)) for k in range(nt)]
    aim = [jnp.broadcast_to(lane_tile(aim_ref, k), (nseg, LANES)) for k in range(nt)]

    def scan_slab(keep, s, carry):
        for i in range(s * steps, (s + 1) * steps):
            rows = slice(i * nseg, (i + 1) * nseg)
            new = []
            for k in range(nt):
                sre, sim = carry[k]
                nre = are[k] * sre - aim[k] * sim + x_sc[k, rows, :]
                nim = are[k] * sim + aim[k] * sre + x_sc[nt + k, rows, :]
                if keep:
                    x_sc[k, rows, :] = nre
                    x_sc[nt + k, rows, :] = nim
                new.append((nre, nim))
            carry = tuple(new)
        return carry

    zero = jnp.zeros((nseg, LANES), F32)
    ends = tuple((zero, zero) for _ in range(nt))
    project(0)
    for s in range(nslabs):
        if s + 1 < nslabs:
            project(s + 1)
        ends = scan_slab(False, s, ends)

    seg_row = lax.broadcasted_iota(jnp.int32, (nseg, LANES), 0)
    starts = []
    for k in range(nt):
        ere, eim = ends[k]
        full_re, full_im = _complex_power(lane_tile(are_ref, k), lane_tile(aim_ref, k), seg)
        tre = tim = jnp.zeros((1, LANES), F32)
        cre = cim = zero
        for j in range(nseg):
            cre = jnp.where(seg_row == j, tre, cre)
            cim = jnp.where(seg_row == j, tim, cim)
            tre, tim = (ere[j:j + 1, :] + full_re * tre - full_im * tim,
                        eim[j:j + 1, :] + full_re * tim + full_im * tre)
        starts.append((cre, cim))
        sre_ref[0, :, k * LANES:(k + 1) * LANES] = tre
        sim_ref[0, :, k * LANES:(k + 1) * LANES] = tim

    def output(s):
        r = slab_rows(s)
        xre = jnp.concatenate([x_sc[k, r, :] for k in range(nt)], axis=1)
        xim = jnp.concatenate([x_sc[nt + k, r, :] for k in range(nt)], axis=1)
        y = _s5_output(xre, xim, ui_sc[r, :], cre_ref, cim_ref, d_ref)
        for ii in range(steps):
            y_ref[same_step_rows(s * steps + ii), :] = y[ii * nseg:(ii + 1) * nseg, :]

    state = tuple(starts)
    for s in range(nslabs):
        if s > 0:
            output(s - 1)
        state = scan_slab(True, s, state)
    output(nslabs - 1)


def s5_prompt(u, b_exp, c_re_exp, c_im_exp, d_skip, a_re, a_im, batch, seq):
    t, d = u.shape
    nblk = d // LANES
    ns = S5_BLOCK_STATE
    return pl.pallas_call(
        _s5_prompt_kernel,
        grid=(nblk, batch),
        in_specs=[pl.BlockSpec((seq, LANES), lambda j, b: (b, j)),
                  pl.BlockSpec((1, LANES, 2 * ns), lambda j, b: (j, 0, 0)),
                  pl.BlockSpec((1, ns, LANES), lambda j, b: (j, 0, 0)),
                  pl.BlockSpec((1, ns, LANES), lambda j, b: (j, 0, 0)),
                  pl.BlockSpec((1, LANES), lambda j, b: (0, j)),
                  pl.BlockSpec((1, ns), lambda j, b: (0, j)),
                  pl.BlockSpec((1, ns), lambda j, b: (0, j))],
        out_specs=[pl.BlockSpec((seq, LANES), lambda j, b: (b, j)),
                   pl.BlockSpec((1, 1, ns), lambda j, b: (b, 0, j)),
                   pl.BlockSpec((1, 1, ns), lambda j, b: (b, 0, j))],
        out_shape=[jax.ShapeDtypeStruct((t, d), F32),
                   jax.ShapeDtypeStruct((batch, 1, nblk * ns), F32),
                   jax.ShapeDtypeStruct((batch, 1, nblk * ns), F32)],
        scratch_shapes=[pltpu.VMEM((2 * ns // LANES, seq, LANES), F32), pltpu.VMEM((seq, LANES), F32)],
        compiler_params=_params(("arbitrary", "arbitrary")),
        name="s5_prompt",
    )(u, b_exp, c_re_exp, c_im_exp, d_skip, a_re, a_im)


def _s5_step_kernel(u_ref, b_ref, cre_ref, cim_ref, d_ref, are_ref, aim_ref, s0re_ref, s0im_ref,
                    y_ref, sre_ref, sim_ref):
    ns = S5_BLOCK_STATE
    u = u_ref[...]
    bu = _dot(u.astype(BF16), b_ref[0])
    are, aim = are_ref[...], aim_ref[...]
    s0re, s0im = s0re_ref[...], s0im_ref[...]
    xre = bu[:, 0:ns] + (are * s0re - aim * s0im)
    xim = bu[:, ns:2 * ns] + (are * s0im + aim * s0re)
    sre_ref[...] = xre
    sim_ref[...] = xim
    y_ref[...] = _s5_output(xre, xim, u, cre_ref, cim_ref, d_ref)


def s5_step(u, b_exp, c_re_exp, c_im_exp, d_skip, a_re, a_im, s0_re, s0_im):
    b, d = u.shape
    nblk = d // LANES
    ns = S5_BLOCK_STATE
    return pl.pallas_call(
        _s5_step_kernel,
        grid=(nblk,),
        in_specs=[pl.BlockSpec((b, LANES), lambda j: (0, j)),
                  pl.BlockSpec((1, LANES, 2 * ns), lambda j: (j, 0, 0)),
                  pl.BlockSpec((1, ns, LANES), lambda j: (j, 0, 0)),
                  pl.BlockSpec((1, ns, LANES), lambda j: (j, 0, 0)),
                  pl.BlockSpec((1, LANES), lambda j: (0, j)),
                  pl.BlockSpec((1, ns), lambda j: (0, j)),
                  pl.BlockSpec((1, ns), lambda j: (0, j)),
                  pl.BlockSpec((b, ns), lambda j: (0, j)),
                  pl.BlockSpec((b, ns), lambda j: (0, j))],
        out_specs=[pl.BlockSpec((b, LANES), lambda j: (0, j)),
                   pl.BlockSpec((b, ns), lambda j: (0, j)),
                   pl.BlockSpec((b, ns), lambda j: (0, j))],
        out_shape=[jax.ShapeDtypeStruct((b, d), F32),
                   jax.ShapeDtypeStruct((b, nblk * ns), F32),
                   jax.ShapeDtypeStruct((b, nblk * ns), F32)],
        compiler_params=_params(("parallel",)),
        name="s5_step",
    )(u, b_exp, c_re_exp, c_im_exp, d_skip, a_re, a_im, s0_re, s0_im)


def _block_diag(per_group):
    ngroups, r, c = per_group.shape
    p = S5_GROUPS_PER_BLOCK
    eye = jnp.eye(p, dtype=per_group.dtype)
    tiles = per_group.reshape(ngroups // p, p, r, c)
    return jnp.einsum('bjrc,jk->bjrkc', tiles, eye).reshape(ngroups // p, p * r, p * c)


def kernel(x_prompt, x_sample, state_ssm, state_ssm_conv, state_hgrn, state_s5_re, state_s5_im, state_ffn_conv, norm_mix_pre, norm_mix_post, norm_ffn_pre, norm_ffn_post, ab_in_w, ssm_conv_w, ssm_conv_b, ssm_dt_bias, ssm_a_log, ssm_d, ssm_norm_w, hgrn_lb, hgrn_norm_w, ab_out_w, s5_lam_re, s5_lam_im, s5_log_step, s5_b_re, s5_b_im, s5_c_re, s5_c_im, s5_d, s5_glu_w, ffn_up_w, ffn_conv_w, ffn_conv_b, ffn_down_w):
    batch, seq, d = x_prompt.shape
    dec_batch = x_sample.shape[0]
    depth = norm_mix_pre.shape[0]
    conv_dim = ssm_conv_w.shape[2]
    nheads = ssm_a_log.shape[1]
    d_inner = nheads * SSM_HEAD_DIM
    hg_width = hgrn_lb.shape[1]
    ffn_dim = ffn_down_w.shape[1]
    tm = 256

    def row(v):
        return v.reshape(1, -1)

    def pad_lanes(v):
        return jnp.pad(v, ((0, 0), (0, LANES - v.shape[1])))

    groups = [(x_prompt.reshape(batch * seq, d), True), (x_sample.reshape(dec_batch, d), False)]
    xs = [g[0] for g in groups]
    normed = [None, None]
    out = {'ssm': [[], []], 'sconv': [[], []], 'hg': [[], []], 's5r': [[], []], 's5i': [[], []], 'fconv': [[], []]}

    for l in range(depth):
        e = l // 2
        if l % 2 == 0:
            w_in = ab_in_w[e]
            o_xbc = d_inner
            o_dt = o_xbc + conv_dim
            o_q = o_dt + nheads
            w_z = w_in[:, :o_xbc].astype(BF16)
            w_xbc = w_in[:, o_xbc:o_dt].astype(BF16)
            w_dt = pad_lanes(w_in[:, o_dt:o_q]).astype(BF16)
            w_qfig = w_in[:, o_q:].astype(BF16)
            w_out = ab_out_w[e].astype(BF16)
            dt_bias = pad_lanes(row(ssm_dt_bias[e]))
            a_log = pad_lanes(row(ssm_a_log[e]))
            d_skip = row(jnp.repeat(ssm_d[e], SSM_HEAD_DIM))
            for gi, (_, is_prompt) in enumerate(groups):
                x = xs[gi]
                t = x.shape[0]
                if is_prompt:
                    m_hg, hg_new = hgrn_prompt(x, row(norm_mix_pre[l]), w_qfig, w_out[d_inner:], hgrn_lb,
                                               row(hgrn_norm_w[e]), e, batch, seq)
                    xs[gi], ssm_new, xbc_tail = ssd_prompt(
                        x, row(norm_mix_pre[l]), w_z, w_xbc, w_dt, m_hg, w_out[:d_inner], row(norm_mix_post[l]),
                        ssm_conv_w[e], row(ssm_conv_b[e]), dt_bias, a_log, d_skip, row(ssm_norm_w[e]), batch, seq)
                    sconv_new = xbc_tail[:, SUBLANES - (SSM_CONV - 1):]
                else:
                    z, xbc, dt, qfig = norm_matmul(x, row(norm_mix_pre[l]), [w_z, w_xbc, w_dt, w_qfig], t)
                    hist = [state_ssm_conv[e][:, k] for k in range(SSM_CONV - 1)]
                    y, ssm_new = ssd_step(z, xbc, hist, dt, ssm_conv_w[e], row(ssm_conv_b[e]), dt_bias, a_log,
                                          d_skip, row(ssm_norm_w[e]), state_ssm[e])
                    o, hg_new = hgrn_step(qfig, hgrn_lb, row(hgrn_norm_w[e]), e, state_hgrn[e])
                    sconv_new = jnp.stack(hist[1:] + [xbc], axis=1)
                    xs[gi] = matmul_norm_residual([y, o], [w_out[:d_inner], w_out[d_inner:]],
                                                  row(norm_mix_post[l]), x, t)
                out['ssm'][gi].append(ssm_new)
                out['hg'][gi].append(hg_new)
                out['sconv'][gi].append(sconv_new)
        else:
            ngroups = s5_lam_re.shape[1]
            a_re, a_im, bb_re, bb_im = s5_discretize(
                s5_lam_re[e], s5_lam_im[e], s5_log_step[e],
                jnp.swapaxes(s5_b_re[e], 1, 2), jnp.swapaxes(s5_b_im[e], 1, 2))
            a_re = a_re.reshape(1, ngroups * S5_STATE)
            a_im = a_im.reshape(1, ngroups * S5_STATE)
            b_exp = jnp.concatenate([_block_diag(bb_re), _block_diag(bb_im)], axis=2).astype(BF16)
            c_re_exp = _block_diag(jnp.swapaxes(s5_c_re[e], 1, 2)).astype(BF16)
            c_im_exp = _block_diag(jnp.swapaxes(s5_c_im[e], 1, 2)).astype(BF16)
            glu_w = s5_glu_w[e].astype(BF16)
            for gi, (_, is_prompt) in enumerate(groups):
                x = xs[gi]
                t = x.shape[0]
                u = normed[gi] if normed[gi] is not None else rmsnorm(x, row(norm_mix_pre[l]), min(tm, t))
                if is_prompt:
                    yg, s_re, s_im = s5_prompt(u, b_exp, c_re_exp, c_im_exp, row(s5_d[e]), a_re, a_im, batch, seq)
                else:
                    yg, s_re, s_im = s5_step(u, b_exp, c_re_exp, c_im_exp, row(s5_d[e]), a_re, a_im,
                                             state_s5_re[e].reshape(t, -1), state_s5_im[e].reshape(t, -1))
                out['s5r'][gi].append(s_re.reshape(-1, ngroups, S5_STATE))
                out['s5i'][gi].append(s_im.reshape(-1, ngroups, S5_STATE))
                xs[gi] = matmul_norm_residual([yg], [glu_w], row(norm_mix_post[l]), x, min(tm, t), glu=True)

        up_w = ffn_up_w[l].astype(BF16)
        down_w = ffn_down_w[l].astype(BF16)
        for gi, (_, is_prompt) in enumerate(groups):
            x = xs[gi]
            t = x.shape[0]
            normed[gi] = None
            if is_prompt:
                g_next = row(norm_mix_pre[l + 1]) if l + 1 < depth and (l + 1) % 2 == 1 else None
                res = ffn_seq(x, row(norm_ffn_pre[l]), up_w, ffn_conv_w[l], row(ffn_conv_b[l]), down_w,
                              row(norm_ffn_post[l]), g_next, tm, seq)
                xs[gi], tail = res[0], res[1]
                if g_next is not None:
                    normed[gi] = res[2]
                fconv_new = tail[:, SUBLANES - (FFN_CONV - 1):]
            else:
                (up,) = norm_matmul(x, row(norm_ffn_pre[l]), [up_w], t)
                old = state_ffn_conv[l]
                xs[gi] = ffn_tail_step(up, old[:, 0], old[:, 1], ffn_conv_w[l], row(ffn_conv_b[l]), down_w,
                                       row(norm_ffn_post[l]), x)
                fconv_new = jnp.stack([old[:, 1], up], axis=1)
            out['fconv'][gi].append(fconv_new)

    y_prompt = xs[0].reshape(batch, seq, d)
    y_sample = xs[1].reshape(dec_batch, 1, d)
    states = []
    for gi in range(2):
        states += [jnp.stack(out[k][gi]) for k in ('ssm', 'sconv', 'hg', 's5r', 's5i', 'fconv')]
    return (y_prompt, y_sample, *states)
```

```python
import functools
import math

import jax
import jax.numpy as jnp
import numpy as np
from jax import lax
from jax.experimental import pallas as pl
from jax.experimental.pallas import tpu as pltpu

F32 = jnp.float32
BF16 = jnp.bfloat16
EPS = 1e-6

LANES = 128
SUBLANES = 8
VMEM_LIMIT_BYTES = 56 * 1024 * 1024

SSM_HEAD_DIM = 64
SSM_GROUPS = 2
SSM_STATE = 64
SSM_CONV = 4
SSM_CHUNK = 128
HG_DIM = 128
HG_CHUNK = 128
S5_GROUP = 16
S5_STATE = 64
S5_GROUPS_PER_BLOCK = LANES // S5_GROUP
S5_BLOCK_STATE = S5_GROUPS_PER_BLOCK * S5_STATE
S5_SEGMENTS = SUBLANES
FFN_CONV = 3


def _params(sem):
    return pltpu.CompilerParams(dimension_semantics=sem, vmem_limit_bytes=VMEM_LIMIT_BYTES)


def _resident(shape):
    nd = len(shape)
    return pl.BlockSpec(shape, lambda *_: (0,) * nd, pipeline_mode=pl.Buffered(1))


def _rms(x, w):
    return x * lax.rsqrt(jnp.mean(x * x, axis=-1, keepdims=True) + EPS) * w


def _silu(x):
    return x * jax.nn.sigmoid(x)


def _dot(a, b):
    return jnp.dot(a, b, preferred_element_type=F32)


def _dot_nt(a, b):
    return lax.dot_general(a, b, (((1,), (1,)), ((), ())), preferred_element_type=F32)


def _dot_tn(a, b):
    return lax.dot_general(a, b, (((0,), (0,)), ((), ())), preferred_element_type=F32)


def _norm_matmul_kernel(n_out, x_ref, g_ref, *refs):
    w_refs, o_refs = refs[:n_out], refs[n_out:]
    xn = _rms(x_ref[...], g_ref[...]).astype(BF16)
    for w_ref, o_ref in zip(w_refs, o_refs):
        n = w_ref.shape[1]
        for c0 in range(0, n, 512):
            c1 = min(n, c0 + 512)
            o_ref[:, c0:c1] = _dot(xn, w_ref[:, c0:c1])


def norm_matmul(x, g, ws, tm):
    t, d = x.shape
    kern = functools.partial(_norm_matmul_kernel, len(ws))
    return pl.pallas_call(
        kern,
        grid=(t // tm,),
        in_specs=[pl.BlockSpec((tm, d), lambda i: (i, 0)), _resident((1, d))]
        + [_resident(w.shape) for w in ws],
        out_specs=[pl.BlockSpec((tm, w.shape[1]), lambda i: (i, 0)) for w in ws],
        out_shape=[jax.ShapeDtypeStruct((t, w.shape[1]), F32) for w in ws],
        compiler_params=_params(("parallel",)),
        name="norm_matmul",
    )(x, g, *ws)


def _rmsnorm_kernel(x_ref, g_ref, o_ref):
    o_ref[...] = _rms(x_ref[...], g_ref[...])


def rmsnorm(x, g, tm):
    t, d = x.shape
    return pl.pallas_call(
        _rmsnorm_kernel,
        grid=(t // tm,),
        in_specs=[pl.BlockSpec((tm, d), lambda i: (i, 0)), _resident((1, d))],
        out_specs=pl.BlockSpec((tm, d), lambda i: (i, 0)),
        out_shape=jax.ShapeDtypeStruct((t, d), F32),
        compiler_params=_params(("parallel",)),
        name="rmsnorm",
    )(x, g)


def _matmul_norm_residual_kernel(n_in, glu, *refs):
    a_refs, w_refs = refs[:n_in], refs[n_in:2 * n_in]
    g_ref, x_ref, o_ref = refs[2 * n_in:]
    acc = None
    for a_ref, w_ref in zip(a_refs, w_refs):
        part = _dot(a_ref[...].astype(BF16), w_ref[...])
        acc = part if acc is None else acc + part
    if glu:
        d = acc.shape[1] // 2
        acc = acc[:, :d] * jax.nn.sigmoid(acc[:, d:])
    o_ref[...] = x_ref[...] + _rms(acc, g_ref[...])


def matmul_norm_residual(a_list, w_list, g, x, tm, glu=False):
    t, d = x.shape
    kern = functools.partial(_matmul_norm_residual_kernel, len(a_list), glu)
    return pl.pallas_call(
        kern,
        grid=(t // tm,),
        in_specs=[pl.BlockSpec((tm, a.shape[1]), lambda i: (i, 0)) for a in a_list]
        + [_resident(w.shape) for w in w_list]
        + [_resident((1, d)), pl.BlockSpec((tm, d), lambda i: (i, 0))],
        out_specs=pl.BlockSpec((tm, d), lambda i: (i, 0)),
        out_shape=jax.ShapeDtypeStruct((t, d), F32),
        compiler_params=_params(("parallel",)),
        name="matmul_norm_residual",
    )(*a_list, *w_list, g, x)


FFN_CHUNK = 256


def _ffn_tail(up_ref, xm2_of, xm1_of, cw_ref, cb_ref, dw_ref, g_ref, x_ref, o_ref):
    f = dw_ref.shape[0]
    acc = jnp.zeros(o_ref.shape, F32)
    for c0 in range(0, f, FFN_CHUNK):
        c1 = c0 + FFN_CHUNK
        halves = []
        for off in (0, f):
            a, b = c0 + off, c1 + off
            halves.append(cb_ref[:, a:b] + xm2_of(a, b) * cw_ref[0:1, a:b]
                          + xm1_of(a, b) * cw_ref[1:2, a:b] + up_ref[:, a:b] * cw_ref[2:3, a:b])
        act = (halves[0] * _silu(halves[1])).astype(BF16)
        acc = acc + _dot(act, dw_ref[c0:c1, :])
    o_ref[...] = x_ref[...] + _rms(acc, g_ref[...])


def _ffn_seq_kernel(tiles_per_seq, emit_next, x_ref, g1_ref, uw_ref, cw_ref, cb_ref, dw_ref, g2_ref, *rest):
    if emit_next:
        g3_ref, o_ref, tail_ref, n_ref, carry_sc = rest
    else:
        o_ref, tail_ref, carry_sc = rest
    tm = x_ref.shape[0]
    f = dw_ref.shape[0]

    @pl.when(pl.program_id(0) % tiles_per_seq == 0)
    def _():
        carry_sc[...] = jnp.zeros_like(carry_sc)

    x = x_ref[...]
    xn = _rms(x, g1_ref[...]).astype(BF16)
    top_row = lax.broadcasted_iota(jnp.int32, (SUBLANES, FFN_CHUNK), 0)

    def earlier(up, prev, k):
        rolled = pltpu.roll(up, k, axis=0)
        top = jnp.where(top_row < k, pltpu.roll(prev, k, axis=0), rolled[:SUBLANES])
        return jnp.concatenate([top, rolled[SUBLANES:]], axis=0)

    def project(c0):
        return [_dot(xn, uw_ref[:, c0 + off:c0 + off + FFN_CHUNK]) for off in (0, f)]

    acc = jnp.zeros(o_ref.shape, F32)
    ups_next = project(0)
    act_prev = None
    for c0 in range(0, f, FFN_CHUNK):
        ups = ups_next
        if c0 + FFN_CHUNK < f:
            ups_next = project(c0 + FFN_CHUNK)
        if act_prev is not None:
            acc = acc + _dot(act_prev, dw_ref[c0 - FFN_CHUNK:c0, :])
        halves = []
        for up, off in zip(ups, (0, f)):
            a, b = c0 + off, c0 + off + FFN_CHUNK
            prev = carry_sc[:, a:b]
            carry_sc[:, a:b] = up[tm - SUBLANES:, :]
            tail_ref[0, :, a:b] = up[tm - SUBLANES:, :]
            halves.append(cb_ref[:, a:b] + earlier(up, prev, 2) * cw_ref[0:1, a:b]
                          + earlier(up, prev, 1) * cw_ref[1:2, a:b] + up * cw_ref[2:3, a:b])
        act_prev = (halves[0] * _silu(halves[1])).astype(BF16)
    acc = acc + _dot(act_prev, dw_ref[f - FFN_CHUNK:f, :])
    out = x + _rms(acc, g2_ref[...])
    o_ref[...] = out
    if emit_next:
        n_ref[...] = _rms(out, g3_ref[...])


def ffn_seq(x, g_pre, up_w, conv_w, conv_b, down_w, g_post, g_next, tm, seq):
    t, d = x.shape
    f2 = up_w.shape[1]
    tiles_per_seq = seq // tm
    emit_next = g_next is not None
    kern = functools.partial(_ffn_seq_kernel, tiles_per_seq, emit_next)
    tile = pl.BlockSpec((tm, d), lambda i: (i, 0))
    return pl.pallas_call(
        kern,
        grid=(t // tm,),
        in_specs=[tile, _resident((1, d)), _resident(up_w.shape), _resident(conv_w.shape), _resident((1, f2)),
                  _resident(down_w.shape), _resident((1, d))] + [_resident((1, d))] * emit_next,
        out_specs=[tile, pl.BlockSpec((1, SUBLANES, f2), lambda i: (i // tiles_per_seq, 0, 0))]
        + [tile] * emit_next,
        out_shape=[jax.ShapeDtypeStruct((t, d), F32), jax.ShapeDtypeStruct((t // seq, SUBLANES, f2), F32)]
        + [jax.ShapeDtypeStruct((t, d), F32)] * emit_next,
        scratch_shapes=[pltpu.VMEM((SUBLANES, f2), F32)],
        compiler_params=_params(("arbitrary",)),
        name="ffn_seq",
    )(x, g_pre, up_w, conv_w, conv_b, down_w, g_post, *([g_next] * emit_next))


def _ffn_step_kernel(up_ref, xm2_ref, xm1_ref, cw_ref, cb_ref, dw_ref, g_ref, x_ref, o_ref):
    _ffn_tail(up_ref, lambda a, b: xm2_ref[:, a:b], lambda a, b: xm1_ref[:, a:b],
              cw_ref, cb_ref, dw_ref, g_ref, x_ref, o_ref)


def ffn_tail_step(up, xm2, xm1, conv_w, conv_b, down_w, g, x):
    t, d = x.shape
    f2 = up.shape[1]
    return pl.pallas_call(
        _ffn_step_kernel,
        grid=(1,),
        in_specs=[_resident((t, f2))] * 3
        + [_resident(conv_w.shape), _resident((1, f2)), _resident(down_w.shape),
           _resident((1, d)), _resident((t, d))],
        out_specs=pl.BlockSpec((t, d), lambda i: (0, 0)),
        out_shape=jax.ShapeDtypeStruct((t, d), F32),
        compiler_params=_params(("arbitrary",)),
        name="ffn_tail_step",
    )(up, xm2, xm1, conv_w, conv_b, down_w, g, x)


def _ssd_gate_norm(y, xs, z, dsk_ref, nw_ref):
    y = y + dsk_ref[...] * xs
    return _rms(y * _silu(z), nw_ref[...])


def _head_expansion(nheads, head_dim):
    e = np.zeros((LANES, nheads * head_dim), np.float32)
    for h in range(nheads):
        e[h, h * head_dim:(h + 1) * head_dim] = 1.0
    return e


def _expand_heads(v, exp_ref):
    hi = v.astype(BF16)
    lo = (v - hi.astype(F32)).astype(BF16)
    return _dot(hi, exp_ref[...]) + _dot(lo, exp_ref[...])


def _ssd_prompt_kernel(x_ref, gpre_ref, wz_ref, wxbc_ref, wdt_ref, other_ref, wout_ref, gpost_ref,
                       cw_ref, cb_ref, dtb_ref, alog_ref, dsk_ref, nw_ref, exp_ref,
                       xo_ref, hout_ref, histout_ref, hist_sc, h_sc, xp_sc, y_sc):
    c = pl.program_id(1)
    nchunks = pl.num_programs(1)
    cl = x_ref.shape[0]
    hn, d_inner = h_sc.shape
    hp = SSM_HEAD_DIM
    nheads = d_inner // hp
    heads_per_group = nheads // SSM_GROUPS
    group_width = heads_per_group * hp

    @pl.when(c == 0)
    def _():
        hist_sc[...] = jnp.zeros_like(hist_sc)
        h_sc[...] = jnp.zeros_like(h_sc)

    x = x_ref[...]
    xn = _rms(x, gpre_ref[...]).astype(BF16)
    raw = _dot(xn, wxbc_ref[...])
    xp_sc[0:SUBLANES, :] = hist_sc[...]
    xp_sc[SUBLANES:, :] = raw
    hist_sc[...] = raw[cl - SUBLANES:, :]
    conv = cb_ref[...]
    for k in range(SSM_CONV):
        conv = conv + xp_sc[pl.ds(SUBLANES - (SSM_CONV - 1) + k, cl), :] * cw_ref[k:k + 1, :]
    act = _silu(conv)
    xs = act[:, :d_inner]
    bm = act[:, d_inner:d_inner + SSM_GROUPS * SSM_STATE]
    cm = act[:, d_inner + SSM_GROUPS * SSM_STATE:]

    dt = jax.nn.softplus(_dot(xn, wdt_ref[...]) + dtb_ref[...])
    da = dt * (-jnp.exp(alog_ref[...]))
    ti = lax.broadcasted_iota(jnp.int32, (cl, cl), 0)
    si = lax.broadcasted_iota(jnp.int32, (cl, cl), 1)
    tril = ti >= si
    cum_col = jnp.dot(tril.astype(F32), da, preferred_element_type=F32,
                      precision=lax.Precision.HIGHEST)
    cum_row = cum_col.T
    dt_row = dt.T
    last = cum_col[cl - 1:cl, :]

    per_head = jnp.concatenate([jnp.exp(cum_col), jnp.exp(last - cum_col) * dt,
                                jnp.broadcast_to(jnp.exp(last), (SUBLANES, LANES))], axis=0)
    wide = _expand_heads(per_head, exp_ref)
    from_start, to_end, chunk_decay = wide[:cl], wide[cl:2 * cl], wide[2 * cl:2 * cl + 1]

    z = _dot(xn, wz_ref[...])
    xs16 = xs.astype(BF16)
    xw16 = (xs * to_end).astype(BF16)
    bm_t = bm.T
    st = h_sc[...]
    st16 = st.astype(BF16)
    cbs, state_terms, updates = [], [], []
    for g in range(SSM_GROUPS):
        sl = slice(g * SSM_STATE, (g + 1) * SSM_STATE)
        gl = slice(g * group_width, (g + 1) * group_width)
        cm_g = cm[:, sl].astype(BF16)
        cbs.append(_dot_nt(cm_g, bm[:, sl].astype(BF16)))
        state_terms.append(_dot(cm_g, st16[:, gl]))
        updates.append(_dot(bm_t[sl, :].astype(BF16), xw16[:, gl]))
    h_sc[...] = st * chunk_decay + jnp.concatenate(updates, axis=1)
    y_sc[...] = jnp.concatenate(state_terms, axis=1) * from_start

    for h in range(nheads):
        hl = slice(h * hp, (h + 1) * hp)
        seg = jnp.exp(jnp.where(tril, cum_col[:, h:h + 1] - cum_row[h:h + 1, :], -jnp.inf))
        wts = cbs[h // heads_per_group] * seg * dt_row[h:h + 1, :]
        y_sc[:, hl] = y_sc[:, hl] + _dot(wts.astype(BF16), xs16[:, hl])

    y = _ssd_gate_norm(y_sc[...], xs, z, dsk_ref, nw_ref).astype(BF16)
    xo_ref[...] = x + _rms(_dot(y, wout_ref[...]) + other_ref[...], gpost_ref[...])

    @pl.when(c == nchunks - 1)
    def _():
        histout_ref[0] = hist_sc[...]
        final = h_sc[...]
        pad = jnp.zeros((LANES - hn, LANES), F32)
        for j in range(d_inner // LANES):
            tile_t = jnp.concatenate([final[:, j * LANES:(j + 1) * LANES], pad], axis=0).T
            for r in range(LANES // hp):
                hout_ref[0, j * (LANES // hp) + r] = tile_t[r * hp:(r + 1) * hp, :hn]


def ssd_prompt(x, g_pre, w_z, w_xbc, w_dt, other, w_out, g_post, conv_w, conv_b, dt_bias, a_log, d_skip, norm_w,
               batch, seq):
    t, d = x.shape
    d_inner = w_z.shape[1]
    conv_dim = w_xbc.shape[1]
    nheads = d_inner // SSM_HEAD_DIM
    cl = SSM_CHUNK
    nchunks = seq // cl
    row = lambda b, c: (b * nchunks + c, 0)
    return pl.pallas_call(
        _ssd_prompt_kernel,
        grid=(batch, nchunks),
        in_specs=[pl.BlockSpec((cl, d), row), _resident((1, d)), _resident(w_z.shape), _resident(w_xbc.shape),
                  _resident(w_dt.shape), pl.BlockSpec((cl, d), row), _resident(w_out.shape), _resident((1, d)),
                  _resident(conv_w.shape), _resident((1, conv_dim)), _resident((1, LANES)),
                  _resident((1, LANES)), _resident((1, d_inner)), _resident((1, d_inner)),
                  _resident((LANES, d_inner))],
        out_specs=[pl.BlockSpec((cl, d), row),
                   pl.BlockSpec((1, nheads, SSM_HEAD_DIM, SSM_STATE), lambda b, c: (b, 0, 0, 0)),
                   pl.BlockSpec((1, SUBLANES, conv_dim), lambda b, c: (b, 0, 0))],
        out_shape=[jax.ShapeDtypeStruct((t, d), F32),
                   jax.ShapeDtypeStruct((batch, nheads, SSM_HEAD_DIM, SSM_STATE), F32),
                   jax.ShapeDtypeStruct((batch, SUBLANES, conv_dim), F32)],
        scratch_shapes=[pltpu.VMEM((SUBLANES, conv_dim), F32),
                        pltpu.VMEM((SSM_STATE, d_inner), F32),
                        pltpu.VMEM((cl + SUBLANES, conv_dim), F32),
                        pltpu.VMEM((cl, d_inner), F32)],
        compiler_params=_params(("arbitrary", "arbitrary")),
        name="ssd_prompt",
    )(x, g_pre, w_z, w_xbc, w_dt, other, w_out, g_post, conv_w, conv_b, dt_bias, a_log, d_skip, norm_w,
      jnp.asarray(_head_expansion(nheads, SSM_HEAD_DIM), BF16))


def _ssd_step_kernel(z_ref, xnew_ref, h0_ref, h1_ref, h2_ref, dt_ref, cw_ref, cb_ref, dtb_ref, alog_ref,
                     dsk_ref, nw_ref, st_ref, y_ref, stout_ref,
                     xs_sc, xst_sc, bmt_sc, cmt_sc, dtt_sc, dat_sc, yt_sc):
    h = pl.program_id(0)
    nheads = pl.num_programs(0)
    d_inner = xs_sc.shape[1]
    hp, hn = SSM_HEAD_DIM, SSM_STATE
    heads_per_group = d_inner // hp // SSM_GROUPS

    @pl.when(h == 0)
    def _():
        conv = (cb_ref[...] + h0_ref[...] * cw_ref[0:1, :] + h1_ref[...] * cw_ref[1:2, :]
                + h2_ref[...] * cw_ref[2:3, :] + xnew_ref[...] * cw_ref[3:4, :])
        act = _silu(conv)
        xs = act[:, :d_inner]
        dt = jax.nn.softplus(dt_ref[...] + dtb_ref[...])
        xs_sc[...] = xs
        xst_sc[...] = xs.T
        bmt_sc[...] = act[:, d_inner:d_inner + SSM_GROUPS * hn].T
        cmt_sc[...] = act[:, d_inner + SSM_GROUPS * hn:].T
        dtt_sc[...] = dt.T
        dat_sc[...] = jnp.exp(dt * (-jnp.exp(alog_ref[...]))).T

    g = h // heads_per_group
    bmt = bmt_sc[pl.ds(pl.multiple_of(g * hn, hn), hn), :]
    cmt = cmt_sc[pl.ds(pl.multiple_of(g * hn, hn), hn), :]
    da = dat_sc[pl.ds(h, 1), :]
    dtx = xst_sc[pl.ds(pl.multiple_of(h * hp, hp), hp), :] * dtt_sc[pl.ds(h, 1), :]
    rows_per_tile = LANES // hn
    for j in range(hp // rows_per_tile):
        tile = st_ref[:, j * LANES:(j + 1) * LANES].T
        news = []
        for r in range(rows_per_tile):
            p = j * rows_per_tile + r
            new = da * tile[r * hn:(r + 1) * hn, :] + dtx[p:p + 1, :] * bmt
            yt_sc[pl.ds(h * hp + p, 1), :] = jnp.sum(new * cmt, axis=0, keepdims=True)
            news.append(new)
        stout_ref[:, j * LANES:(j + 1) * LANES] = jnp.concatenate(news, axis=0).T

    @pl.when(h == nheads - 1)
    def _():
        y_ref[...] = _ssd_gate_norm(yt_sc[...].T, xs_sc[...], z_ref[...], dsk_ref, nw_ref).astype(y_ref.dtype)


def ssd_step(z, xnew, hist, dt, conv_w, conv_b, dt_bias, a_log, d_skip, norm_w, state):
    b, d_inner = z.shape
    conv_dim = xnew.shape[1]
    nheads = d_inner // SSM_HEAD_DIM
    per_head = SSM_HEAD_DIM * SSM_STATE
    full = lambda shape: pl.BlockSpec(shape, lambda h: (0, 0))
    y, st = pl.pallas_call(
        _ssd_step_kernel,
        grid=(nheads,),
        in_specs=[full((b, d_inner))] + [full((b, conv_dim))] * 4 + [full((b, LANES))]
        + [full(conv_w.shape), full((1, conv_dim)), full((1, LANES)), full((1, LANES)),
           full((1, d_inner)), full((1, d_inner)),
           pl.BlockSpec((b, per_head), lambda h: (0, h))],
        out_specs=[full((b, d_inner)), pl.BlockSpec((b, per_head), lambda h: (0, h))],
        out_shape=[jax.ShapeDtypeStruct((b, d_inner), BF16),
                   jax.ShapeDtypeStruct((b, nheads * per_head), F32)],
        scratch_shapes=[pltpu.VMEM((b, d_inner), F32), pltpu.VMEM((d_inner, b), F32),
                        pltpu.VMEM((SSM_GROUPS * SSM_STATE, b), F32),
                        pltpu.VMEM((SSM_GROUPS * SSM_STATE, b), F32),
                        pltpu.VMEM((LANES, b), F32), pltpu.VMEM((LANES, b), F32),
                        pltpu.VMEM((d_inner, b), F32)],
        compiler_params=_params(("arbitrary",)),
        name="ssd_step",
    )(z, xnew, hist[0], hist[1], hist[2], dt, conv_w, conv_b, dt_bias, a_log, d_skip, norm_w,
      state.reshape(b, nheads * per_head))
    return y, st.reshape(b, nheads, SSM_HEAD_DIM, SSM_STATE)


def _hgrn_lower_bound(lbp_ref, layer):
    raw = lbp_ref[...]
    e = jnp.exp(raw - jnp.max(raw, axis=0, keepdims=True))
    return jnp.sum(e[:layer + 1], axis=0, keepdims=True) / jnp.sum(e, axis=0, keepdims=True)


def _hgrn_gates(q_raw, f_raw, lb):
    q = _silu(q_raw)
    f = lb + (1.0 - lb) * jax.nn.sigmoid(f_raw)
    k = (1.0 - lb) * jax.nn.sigmoid(-f_raw)
    return q, f, k


def _pair_levels(cl):
    t = np.arange(cl)[:, None]
    s = np.arange(cl)[None, :]
    lvl = np.floor(np.log2(np.maximum(t ^ s, 1))).astype(np.int32)
    return np.where(t > s, lvl, np.where(t == s, -1, -2)).astype(np.int32)


HG_HEADS_PER_DOT = 2
HG_GROUP = 8


def _hgrn_prompt_kernel(layer, x_ref, gpre_ref, win_ref, wout_ref, lvl_ref, lbp_ref, nw_ref,
                        m_ref, sout_ref, st_sc):
    c = pl.program_id(1)
    nchunks = pl.num_programs(1)
    cl = x_ref.shape[0]
    nheads = st_sc.shape[0]
    width = nheads * HG_DIM
    span = HG_HEADS_PER_DOT * HG_DIM

    @pl.when(c == 0)
    def _():
        st_sc[...] = jnp.zeros_like(st_sc)

    lb_all = _hgrn_lower_bound(lbp_ref, layer)
    row = lax.broadcasted_iota(jnp.int32, (cl, LANES), 0)
    lvl = lvl_ref[...]
    tril16 = (lvl >= -1).astype(BF16)
    xn = _rms(x_ref[...], gpre_ref[...]).astype(BF16)

    def project(j):
        return [_dot(xn, win_ref[:, which * width + j * span:which * width + (j + 1) * span])
                for which in range(4)]

    acc = None
    for g0 in range(0, nheads, HG_GROUP):
        heads = range(g0, g0 + HG_GROUP)
        raw = [{}, {}, {}, {}]
        for j in range(g0 // HG_HEADS_PER_DOT, (g0 + HG_GROUP) // HG_HEADS_PER_DOT):
            for which, both in enumerate(project(j)):
                for hh in range(HG_HEADS_PER_DOT):
                    raw[which][j * HG_HEADS_PER_DOT + hh] = both[:, hh * HG_DIM:(hh + 1) * HG_DIM]
        q, k, cum = {}, {}, {}
        for h in heads:
            q[h], fh, k[h] = _hgrn_gates(raw[0][h], raw[1][h], lb_all[:, h * HG_DIM:(h + 1) * HG_DIM])
            total = jnp.log(fh)
            shift = 1
            while shift < cl:
                total = total + jnp.where(row >= shift, pltpu.roll(total, shift, axis=0), 0.0)
                shift *= 2
            cum[h] = total

        att = {h: jnp.where(lvl == -1, _dot_nt(q[h].astype(BF16), k[h].astype(BF16)), 0.0) for h in heads}
        last_of_block = dict(cum)
        blk = 1
        level = 0
        while blk < cl:
            odd = (row & blk) != 0
            for h in heads:
                expo = jnp.where(odd, cum[h] - pltpu.roll(last_of_block[h], blk, axis=0),
                                 last_of_block[h] - cum[h])
                scaled = (jnp.where(odd, q[h], k[h]) * jnp.exp(expo)).astype(BF16)
                att[h] = jnp.where(lvl == level, _dot_nt(scaled, scaled), att[h])
                last_of_block[h] = jnp.where(odd, last_of_block[h],
                                             pltpu.roll(last_of_block[h], cl - blk, axis=0))
            blk *= 2
            level += 1
        cum_last = last_of_block

        outs = []
        for h in heads:
            st = st_sc[h]
            v = raw[2][h]
            o = (_dot(att[h].astype(BF16), v.astype(BF16))
                 + _dot_nt((q[h] * jnp.exp(cum[h])).astype(BF16), st.astype(BF16)))
            k_end = (k[h] * jnp.exp(cum_last[h] - cum[h])).astype(BF16)
            st_sc[h] = st * jnp.exp(cum_last[h]) + _dot(v.T.astype(BF16), k_end)
            outs.append((_rms(o, nw_ref[...]) * _silu(raw[3][h])).astype(BF16))
        part = _dot(jnp.concatenate(outs, axis=1), wout_ref[g0 * HG_DIM:(g0 + HG_GROUP) * HG_DIM, :])
        acc = part if acc is None else acc + part
    m_ref[...] = acc

    @pl.when(c == nchunks - 1)
    def _():
        for h in range(nheads):
            sout_ref[0, h] = st_sc[h].T


def hgrn_prompt(x, g_pre, w_qfig, w_out, lb_param, norm_w, layer, batch, seq):
    t, d = x.shape
    width = w_out.shape[0]
    nheads = width // HG_DIM
    cl = HG_CHUNK
    nchunks = seq // cl
    row = lambda b, c: (b * nchunks + c, 0)
    kern = functools.partial(_hgrn_prompt_kernel, layer)
    return pl.pallas_call(
        kern,
        grid=(batch, nchunks),
        in_specs=[pl.BlockSpec((cl, d), row), _resident((1, d)), _resident(w_qfig.shape), _resident(w_out.shape),
                  _resident((cl, cl)), _resident(lb_param.shape), _resident((1, HG_DIM))],
        out_specs=[pl.BlockSpec((cl, d), row),
                   pl.BlockSpec((1, nheads, HG_DIM, HG_DIM), lambda b, c: (b, 0, 0, 0))],
        out_shape=[jax.ShapeDtypeStruct((t, d), F32),
                   jax.ShapeDtypeStruct((batch, nheads, HG_DIM, HG_DIM), F32)],
        scratch_shapes=[pltpu.VMEM((nheads, HG_DIM, HG_DIM), F32)],
        compiler_params=_params(("arbitrary", "arbitrary")),
        name="hgrn_prompt",
    )(x, g_pre, w_qfig, w_out, jnp.asarray(_pair_levels(cl)), lb_param, norm_w)


HG_STEP_TOKENS = 8


def _hgrn_step_kernel(layer, qfig_ref, lbp_ref, nw_ref, st_ref, o_ref, stout_ref):
    nheads = st_ref.shape[2]
    raw = lbp_ref[...]
    e = jnp.exp(raw - jnp.max(raw, axis=0, keepdims=True))
    lb = jnp.sum(e[:layer + 1], axis=0) / jnp.sum(e, axis=0)
    pad = jnp.zeros((HG_DIM - 2 * nheads, HG_DIM), F32)
    for bl in range(st_ref.shape[1]):
        t = qfig_ref[bl]
        q, f, k = _hgrn_gates(t[0:nheads], t[nheads:2 * nheads], lb)
        v = t[2 * nheads:3 * nheads]
        cols = jnp.concatenate([f, k, pad], axis=0).T
        q16 = q.astype(BF16)
        rows = []
        for h in range(nheads):
            f_col = cols[:, h:h + 1]
            k_col = cols[:, nheads + h:nheads + h + 1]
            new = f_col * st_ref[0, bl, h] + k_col * v[h:h + 1, :]
            stout_ref[bl, h] = new
            rows.append(_dot(q16, new.astype(BF16))[h:h + 1, :])
        o = jnp.concatenate(rows, axis=0)
        o_ref[bl] = _rms(o, nw_ref[...]) * _silu(t[3 * nheads:])


def hgrn_step(qfig, lb_param, norm_w, layer, state_all):
    b, w4 = qfig.shape
    width = w4 // 4
    nheads = width // HG_DIM
    bb = HG_STEP_TOKENS
    kern = functools.partial(_hgrn_step_kernel, layer)
    o, st = pl.pallas_call(
        kern,
        grid=(b // bb,),
        in_specs=[pl.BlockSpec((bb, 4 * nheads, HG_DIM), lambda i: (i, 0, 0)),
                  _resident((lb_param.shape[0], nheads, HG_DIM)), _resident((1, HG_DIM)),
                  pl.BlockSpec((1, bb, nheads, HG_DIM, HG_DIM), lambda i: (layer, i, 0, 0, 0))],
        out_specs=[pl.BlockSpec((bb, nheads, HG_DIM), lambda i: (i, 0, 0)),
                   pl.BlockSpec((bb, nheads, HG_DIM, HG_DIM), lambda i: (i, 0, 0, 0))],
        out_shape=[jax.ShapeDtypeStruct((b, nheads, HG_DIM), F32),
                   jax.ShapeDtypeStruct((b, nheads, HG_DIM, HG_DIM), F32)],
        compiler_params=_params(("parallel",)),
        name="hgrn_step",
    )(qfig.reshape(b, 4 * nheads, HG_DIM), lb_param.reshape(-1, nheads, HG_DIM), norm_w, state_all)
    return o.reshape(b, width), st


def _s5_discretize_kernel(lre_ref, lim_ref, step_ref, bre_ref, bim_ref, are_ref, aim_ref, bbre_ref, bbim_ref):
    lre, lim = lre_ref[...], lim_ref[...]
    step = jnp.exp(step_ref[...])
    mag = jnp.exp(lre * step)
    are = mag * jnp.cos(lim * step)
    aim = mag * jnp.sin(lim * step)
    den = lre * lre + lim * lim
    nr = are - 1.0
    cre = (nr * lre + aim * lim) / den
    cim = (aim * lre - nr * lim) / den
    are_ref[...] = are
    aim_ref[...] = aim
    bre, bim = bre_ref[...], bim_ref[...]
    bbre_ref[...] = cre * bre - cim * bim
    bbim_ref[...] = cre * bim + cim * bre


def s5_discretize(lam_re, lam_im, log_step, b_re_t, b_im_t):
    g, n = lam_re.shape
    c = b_re_t.shape[1]
    return pl.pallas_call(
        _s5_discretize_kernel,
        out_shape=[jax.ShapeDtypeStruct((g, 1, n), F32)] * 2 + [jax.ShapeDtypeStruct((g, c, n), F32)] * 2,
        name="s5_discretize",
    )(lam_re.reshape(g, 1, n), lam_im.reshape(g, 1, n), log_step.reshape(g, 1, 1), b_re_t, b_im_t)


def _complex_power(re, im, n):
    out = None
    while n:
        if n & 1:
            out = (re, im) if out is None else (out[0] * re - out[1] * im, out[0] * im + out[1] * re)
        n >>= 1
        if n:
            re, im = re * re - im * im, 2.0 * re * im
    return out


def _s5_output(xre, xim, u, cre_ref, cim_ref, d_ref):
    y = _dot(xre.astype(BF16), cre_ref[0]) - _dot(xim.astype(BF16), cim_ref[0]) + d_ref[...] * u
    return jax.nn.gelu(y)


S5_ROWS = 256


def _s5_prompt_kernel(u_ref, b_ref, cre_ref, cim_ref, d_ref, are_ref, aim_ref,
                      y_ref, sre_ref, sim_ref, x_sc, ui_sc):
    nrows = u_ref.shape[0]
    ns = S5_BLOCK_STATE
    nt = ns // LANES
    nseg = S5_SEGMENTS
    seg = nrows // nseg

    steps = S5_ROWS // nseg

    def same_step_rows(i):
        return pl.ds(i, nseg, stride=seg)

    nslabs = nrows // S5_ROWS

    def slab_rows(s):
        return slice(s * S5_ROWS, (s + 1) * S5_ROWS)

    def project(s):
        r = slab_rows(s)
        ui = jnp.concatenate([u_ref[same_step_rows(s * steps + ii), :] for ii in range(steps)], axis=0)
        ui_sc[r, :] = ui
        bu = _dot(ui.astype(BF16), b_ref[0])
        for k in range(2 * nt):
            x_sc[k, r, :] = bu[:, k * LANES:(k + 1) * LANES]

    def lane_tile(ref, k):
        return ref[:, k * LANES:(k + 1) * LANES]

    are = [jnp.broadcast_to(lane_tile(are_ref, k), (nseg, LANES)) for k in range(nt)]
    aim = [jnp.broadcast_to(lane_tile(aim_ref, k), (nseg, LANES)) for k in range(nt)]

    def scan_slab(keep, s, carry):
        for i in range(s * steps, (s + 1) * steps):
            rows = slice(i * nseg, (i + 1) * nseg)
            new = []
            for k in range(nt):
                sre, sim = carry[k]
                nre = are[k] * sre - aim[k] * sim + x_sc[k, rows, :]
                nim = are[k] * sim + aim[k] * sre + x_sc[nt + k, rows, :]
                if keep:
                    x_sc[k, rows, :] = nre
                    x_sc[nt + k, rows, :] = nim
                new.append((nre, nim))
            carry = tuple(new)
        return carry

    zero = jnp.zeros((nseg, LANES), F32)
    ends = tuple((zero, zero) for _ in range(nt))
    project(0)
    for s in range(nslabs):
        if s + 1 < nslabs:
            project(s + 1)
        ends = scan_slab(False, s, ends)

    seg_row = lax.broadcasted_iota(jnp.int32, (nseg, LANES), 0)
    starts = []
    for k in range(nt):
        ere, eim = ends[k]
        full_re, full_im = _complex_power(lane_tile(are_ref, k), lane_tile(aim_ref, k), seg)
        tre = tim = jnp.zeros((1, LANES), F32)
        cre = cim = zero
        for j in range(nseg):
            cre = jnp.where(seg_row == j, tre, cre)
            cim = jnp.where(seg_row == j, tim, cim)
            tre, tim = (ere[j:j + 1, :] + full_re * tre - full_im * tim,
                        eim[j:j + 1, :] + full_re * tim + full_im * tre)
        starts.append((cre, cim))
        sre_ref[0, :, k * LANES:(k + 1) * LANES] = tre
        sim_ref[0, :, k * LANES:(k + 1) * LANES] = tim

    def output(s):
        r = slab_rows(s)
        xre = jnp.concatenate([x_sc[k, r, :] for k in range(nt)], axis=1)
        xim = jnp.concatenate([x_sc[nt + k, r, :] for k in range(nt)], axis=1)
        y = _s5_output(xre, xim, ui_sc[r, :], cre_ref, cim_ref, d_ref)
        for ii in range(steps):
            y_ref[same_step_rows(s * steps + ii), :] = y[ii * nseg:(ii + 1) * nseg, :]

    state = tuple(starts)
    for s in range(nslabs):
        if s > 0:
            output(s - 1)
        state = scan_slab(True, s, state)
    output(nslabs - 1)


def s5_prompt(u, b_exp, c_re_exp, c_im_exp, d_skip, a_re, a_im, batch, seq):
    t, d = u.shape
    nblk = d // LANES
    ns = S5_BLOCK_STATE
    return pl.pallas_call(
        _s5_prompt_kernel,
        grid=(nblk, batch),
        in_specs=[pl.BlockSpec((seq, LANES), lambda j, b: (b, j)),
                  pl.BlockSpec((1, LANES, 2 * ns), lambda j, b: (j, 0, 0)),
                  pl.BlockSpec((1, ns, LANES), lambda j, b: (j, 0, 0)),
                  pl.BlockSpec((1, ns, LANES), lambda j, b: (j, 0, 0)),
                  pl.BlockSpec((1, LANES), lambda j, b: (0, j)),
                  pl.BlockSpec((1, ns), lambda j, b: (0, j)),
                  pl.BlockSpec((1, ns), lambda j, b: (0, j))],
        out_specs=[pl.BlockSpec((seq, LANES), lambda j, b: (b, j)),
                   pl.BlockSpec((1, 1, ns), lambda j, b: (b, 0, j)),
                   pl.BlockSpec((1, 1, ns), lambda j, b: (b, 0, j))],
        out_shape=[jax.ShapeDtypeStruct((t, d), F32),
                   jax.ShapeDtypeStruct((batch, 1, nblk * ns), F32),
                   jax.ShapeDtypeStruct((batch, 1, nblk * ns), F32)],
        scratch_shapes=[pltpu.VMEM((2 * ns // LANES, seq, LANES), F32), pltpu.VMEM((seq, LANES), F32)],
        compiler_params=_params(("arbitrary", "arbitrary")),
        name="s5_prompt",
    )(u, b_exp, c_re_exp, c_im_exp, d_skip, a_re, a_im)


def _s5_step_kernel(u_ref, b_ref, cre_ref, cim_ref, d_ref, are_ref, aim_ref, s0re_ref, s0im_ref,
                    y_ref, sre_ref, sim_ref):
    ns = S5_BLOCK_STATE
    u = u_ref[...]
    bu = _dot(u.astype(BF16), b_ref[0])
    are, aim = are_ref[...], aim_ref[...]
    s0re, s0im = s0re_ref[...], s0im_ref[...]
    xre = bu[:, 0:ns] + (are * s0re - aim * s0im)
    xim = bu[:, ns:2 * ns] + (are * s0im + aim * s0re)
    sre_ref[...] = xre
    sim_ref[...] = xim
    y_ref[...] = _s5_output(xre, xim, u, cre_ref, cim_ref, d_ref)


def s5_step(u, b_exp, c_re_exp, c_im_exp, d_skip, a_re, a_im, s0_re, s0_im):
    b, d = u.shape
    nblk = d // LANES
    ns = S5_BLOCK_STATE
    return pl.pallas_call(
        _s5_step_kernel,
        grid=(nblk,),
        in_specs=[pl.BlockSpec((b, LANES), lambda j: (0, j)),
                  pl.BlockSpec((1, LANES, 2 * ns), lambda j: (j, 0, 0)),
                  pl.BlockSpec((1, ns, LANES), lambda j: (j, 0, 0)),
                  pl.BlockSpec((1, ns, LANES), lambda j: (j, 0, 0)),
                  pl.BlockSpec((1, LANES), lambda j: (0, j)),
                  pl.BlockSpec((1, ns), lambda j: (0, j)),
                  pl.BlockSpec((1, ns), lambda j: (0, j)),
                  pl.BlockSpec((b, ns), lambda j: (0, j)),
                  pl.BlockSpec((b, ns), lambda j: (0, j))],
        out_specs=[pl.BlockSpec((b, LANES), lambda j: (0, j)),
                   pl.BlockSpec((b, ns), lambda j: (0, j)),
                   pl.BlockSpec((b, ns), lambda j: (0, j))],
        out_shape=[jax.ShapeDtypeStruct((b, d), F32),
                   jax.ShapeDtypeStruct((b, nblk * ns), F32),
                   jax.ShapeDtypeStruct((b, nblk * ns), F32)],
        compiler_params=_params(("parallel",)),
        name="s5_step",
    )(u, b_exp, c_re_exp, c_im_exp, d_skip, a_re, a_im, s0_re, s0_im)


def _block_diag(per_group):
    ngroups, r, c = per_group.shape
    p = S5_GROUPS_PER_BLOCK
    eye = jnp.eye(p, dtype=per_group.dtype)
    tiles = per_group.reshape(ngroups // p, p, r, c)
    return jnp.einsum('bjrc,jk->bjrkc', tiles, eye).reshape(ngroups // p, p * r, p * c)


def kernel(x_prompt, x_sample, state_ssm, state_ssm_conv, state_hgrn, state_s5_re, state_s5_im, state_ffn_conv, norm_mix_pre, norm_mix_post, norm_ffn_pre, norm_ffn_post, ab_in_w, ssm_conv_w, ssm_conv_b, ssm_dt_bias, ssm_a_log, ssm_d, ssm_norm_w, hgrn_lb, hgrn_norm_w, ab_out_w, s5_lam_re, s5_lam_im, s5_log_step, s5_b_re, s5_b_im, s5_c_re, s5_c_im, s5_d, s5_glu_w, ffn_up_w, ffn_conv_w, ffn_conv_b, ffn_down_w):
    batch, seq, d = x_prompt.shape
    dec_batch = x_sample.shape[0]
    depth = norm_mix_pre.shape[0]
    conv_dim = ssm_conv_w.shape[2]
    nheads = ssm_a_log.shape[1]
    d_inner = nheads * SSM_HEAD_DIM
    hg_width = hgrn_lb.shape[1]
    ffn_dim = ffn_down_w.shape[1]
    tm = 256

    def row(v):
        return v.reshape(1, -1)

    def pad_lanes(v):
        return jnp.pad(v, ((0, 0), (0, LANES - v.shape[1])))

    groups = [(x_prompt.reshape(batch * seq, d), True), (x_sample.reshape(dec_batch, d), False)]
    xs = [g[0] for g in groups]
    normed = [None, None]
    out = {'ssm': [[], []], 'sconv': [[], []], 'hg': [[], []], 's5r': [[], []], 's5i': [[], []], 'fconv': [[], []]}

    for l in range(depth):
        e = l // 2
        if l % 2 == 0:
            w_in = ab_in_w[e]
            o_xbc = d_inner
            o_dt = o_xbc + conv_dim
            o_q = o_dt + nheads
            w_z = w_in[:, :o_xbc].astype(BF16)
            w_xbc = w_in[:, o_xbc:o_dt].astype(BF16)
            w_dt = pad_lanes(w_in[:, o_dt:o_q]).astype(BF16)
            w_qfig = w_in[:, o_q:].astype(BF16)
            w_out = ab_out_w[e].astype(BF16)
            dt_bias = pad_lanes(row(ssm_dt_bias[e]))
            a_log = pad_lanes(row(ssm_a_log[e]))
            d_skip = row(jnp.repeat(ssm_d[e], SSM_HEAD_DIM))
            for gi, (_, is_prompt) in enumerate(groups):
                x = xs[gi]
                t = x.shape[0]
                if is_prompt:
                    m_hg, hg_new = hgrn_prompt(x, row(norm_mix_pre[l]), w_qfig, w_out[d_inner:], hgrn_lb,
                                               row(hgrn_norm_w[e]), e, batch, seq)
                    xs[gi], ssm_new, xbc_tail = ssd_prompt(
                        x, row(norm_mix_pre[l]), w_z, w_xbc, w_dt, m_hg, w_out[:d_inner], row(norm_mix_post[l]),
                        ssm_conv_w[e], row(ssm_conv_b[e]), dt_bias, a_log, d_skip, row(ssm_norm_w[e]), batch, seq)
                    sconv_new = xbc_tail[:, SUBLANES - (SSM_CONV - 1):]
                else:
                    z, xbc, dt, qfig = norm_matmul(x, row(norm_mix_pre[l]), [w_z, w_xbc, w_dt, w_qfig], t)
                    hist = [state_ssm_conv[e][:, k] for k in range(SSM_CONV - 1)]
                    y, ssm_new = ssd_step(z, xbc, hist, dt, ssm_conv_w[e], row(ssm_conv_b[e]), dt_bias, a_log,
                                          d_skip, row(ssm_norm_w[e]), state_ssm[e])
                    o, hg_new = hgrn_step(qfig, hgrn_lb, row(hgrn_norm_w[e]), e, state_hgrn)
                    sconv_new = jnp.stack(hist[1:] + [xbc], axis=1)
                    xs[gi] = matmul_norm_residual([y, o], [w_out[:d_inner], w_out[d_inner:]],
                                                  row(norm_mix_post[l]), x, t)
                out['ssm'][gi].append(ssm_new)
                out['hg'][gi].append(hg_new)
                out['sconv'][gi].append(sconv_new)
        else:
            ngroups = s5_lam_re.shape[1]
            a_re, a_im, bb_re, bb_im = s5_discretize(
                s5_lam_re[e], s5_lam_im[e], s5_log_step[e],
                jnp.swapaxes(s5_b_re[e], 1, 2), jnp.swapaxes(s5_b_im[e], 1, 2))
            a_re = a_re.reshape(1, ngroups * S5_STATE)
            a_im = a_im.reshape(1, ngroups * S5_STATE)
            b_exp = jnp.concatenate([_block_diag(bb_re), _block_diag(bb_im)], axis=2).astype(BF16)
            c_re_exp = _block_diag(jnp.swapaxes(s5_c_re[e], 1, 2)).astype(BF16)
            c_im_exp = _block_diag(jnp.swapaxes(s5_c_im[e], 1, 2)).astype(BF16)
            glu_w = s5_glu_w[e].astype(BF16)
            for gi, (_, is_prompt) in enumerate(groups):
                x = xs[gi]
                t = x.shape[0]
                u = normed[gi] if normed[gi] is not None else rmsnorm(x, row(norm_mix_pre[l]), min(tm, t))
                if is_prompt:
                    yg, s_re, s_im = s5_prompt(u, b_exp, c_re_exp, c_im_exp, row(s5_d[e]), a_re, a_im, batch, seq)
                else:
                    yg, s_re, s_im = s5_step(u, b_exp, c_re_exp, c_im_exp, row(s5_d[e]), a_re, a_im,
                                             state_s5_re[e].reshape(t, -1), state_s5_im[e].reshape(t, -1))
                out['s5r'][gi].append(s_re.reshape(-1, ngroups, S5_STATE))
                out['s5i'][gi].append(s_im.reshape(-1, ngroups, S5_STATE))
                xs[gi] = matmul_norm_residual([yg], [glu_w], row(norm_mix_post[l]), x, min(tm, t), glu=True)

        up_w = ffn_up_w[l].astype(BF16)
        down_w = ffn_down_w[l].astype(BF16)
        for gi, (_, is_prompt) in enumerate(groups):
            x = xs[gi]
            t = x.shape[0]
            normed[gi] = None
            if is_prompt:
                g_next = row(norm_mix_pre[l + 1]) if l + 1 < depth and (l + 1) % 2 == 1 else None
                res = ffn_seq(x, row(norm_ffn_pre[l]), up_w, ffn_conv_w[l], row(ffn_conv_b[l]), down_w,
                              row(norm_ffn_post[l]), g_next, tm, seq)
                xs[gi], tail = res[0], res[1]
                if g_next is not None:
                    normed[gi] = res[2]
                fconv_new = tail[:, SUBLANES - (FFN_CONV - 1):]
            else:
                (up,) = norm_matmul(x, row(norm_ffn_pre[l]), [up_w], t)
                old = state_ffn_conv[l]
                xs[gi] = ffn_tail_step(up, old[:, 0], old[:, 1], ffn_conv_w[l], row(ffn_conv_b[l]), down_w,
                                       row(norm_ffn_post[l]), x)
                fconv_new = jnp.stack([old[:, 1], up], axis=1)
            out['fconv'][gi].append(fconv_new)

    y_prompt = xs[0].reshape(batch, seq, d)
    y_sample = xs[1].reshape(dec_batch, 1, d)
    states = []
    for gi in range(2):
        states += [jnp.stack(out[k][gi]) for k in ('ssm', 'sconv', 'hg', 's5r', 's5i', 'fconv')]
    return (y_prompt, y_sample, *states)
```

```python
import functools
import math

import jax
import jax.numpy as jnp
import numpy as np
from jax import lax
from jax.experimental import pallas as pl
from jax.experimental.pallas import tpu as pltpu

F32 = jnp.float32
BF16 = jnp.bfloat16
EPS = 1e-6

LANES = 128
SUBLANES = 8
VMEM_LIMIT_BYTES = 56 * 1024 * 1024

SSM_HEAD_DIM = 64
SSM_GROUPS = 2
SSM_STATE = 64
SSM_CONV = 4
SSM_CHUNK = 128
HG_DIM = 128
HG_CHUNK = 128
S5_GROUP = 16
S5_STATE = 64
S5_GROUPS_PER_BLOCK = LANES // S5_GROUP
S5_BLOCK_STATE = S5_GROUPS_PER_BLOCK * S5_STATE
S5_SEGMENTS = SUBLANES
FFN_CONV = 3


def _params(sem):
    return pltpu.CompilerParams(dimension_semantics=sem, vmem_limit_bytes=VMEM_LIMIT_BYTES)


def _resident(shape, layer=None):
    if layer is None:
        nd = len(shape)
        return pl.BlockSpec(shape, lambda *_: (0,) * nd, pipeline_mode=pl.Buffered(1))
    rest = tuple(shape[1:])
    return pl.BlockSpec((None,) + rest, lambda *_: (layer,) + (0,) * len(rest), pipeline_mode=pl.Buffered(1))


def _rms(x, w):
    return x * lax.rsqrt(jnp.mean(x * x, axis=-1, keepdims=True) + EPS) * w


def _silu(x):
    return x * jax.nn.sigmoid(x)


def _dot(a, b):
    return jnp.dot(a, b, preferred_element_type=F32)


def _dot_nt(a, b):
    return lax.dot_general(a, b, (((1,), (1,)), ((), ())), preferred_element_type=F32)


def _dot_tn(a, b):
    return lax.dot_general(a, b, (((0,), (0,)), ((), ())), preferred_element_type=F32)


def _norm_matmul_kernel(n_out, x_ref, g_ref, *refs):
    w_refs, o_refs = refs[:n_out], refs[n_out:]
    xn = _rms(x_ref[...], g_ref[...]).astype(BF16)
    for w_ref, o_ref in zip(w_refs, o_refs):
        n = w_ref.shape[1]
        for c0 in range(0, n, 512):
            c1 = min(n, c0 + 512)
            o_ref[:, c0:c1] = _dot(xn, w_ref[:, c0:c1])


def norm_matmul(x, g, ws, tm, layer=None):
    t, d = x.shape
    kern = functools.partial(_norm_matmul_kernel, len(ws))
    return pl.pallas_call(
        kern,
        grid=(t // tm,),
        in_specs=[pl.BlockSpec((tm, d), lambda i: (i, 0)), _resident((1, d))]
        + [_resident(w.shape, layer) for w in ws],
        out_specs=[pl.BlockSpec((tm, w.shape[-1]), lambda i: (i, 0)) for w in ws],
        out_shape=[jax.ShapeDtypeStruct((t, w.shape[-1]), F32) for w in ws],
        compiler_params=_params(("parallel",)),
        name="norm_matmul",
    )(x, g, *ws)


def _rmsnorm_kernel(x_ref, g_ref, o_ref):
    o_ref[...] = _rms(x_ref[...], g_ref[...])


def rmsnorm(x, g, tm):
    t, d = x.shape
    return pl.pallas_call(
        _rmsnorm_kernel,
        grid=(t // tm,),
        in_specs=[pl.BlockSpec((tm, d), lambda i: (i, 0)), _resident((1, d))],
        out_specs=pl.BlockSpec((tm, d), lambda i: (i, 0)),
        out_shape=jax.ShapeDtypeStruct((t, d), F32),
        compiler_params=_params(("parallel",)),
        name="rmsnorm",
    )(x, g)


def _matmul_norm_residual_kernel(n_in, glu, *refs):
    a_refs, w_refs = refs[:n_in], refs[n_in:2 * n_in]
    g_ref, x_ref, o_ref = refs[2 * n_in:]
    acc = None
    for a_ref, w_ref in zip(a_refs, w_refs):
        part = _dot(a_ref[...].astype(BF16), w_ref[...])
        acc = part if acc is None else acc + part
    if glu:
        d = acc.shape[1] // 2
        acc = acc[:, :d] * jax.nn.sigmoid(acc[:, d:])
    o_ref[...] = x_ref[...] + _rms(acc, g_ref[...])


def matmul_norm_residual(a_list, w_list, g, x, tm, glu=False):
    t, d = x.shape
    kern = functools.partial(_matmul_norm_residual_kernel, len(a_list), glu)
    return pl.pallas_call(
        kern,
        grid=(t // tm,),
        in_specs=[pl.BlockSpec((tm, a.shape[1]), lambda i: (i, 0)) for a in a_list]
        + [_resident(w.shape) for w in w_list]
        + [_resident((1, d)), pl.BlockSpec((tm, d), lambda i: (i, 0))],
        out_specs=pl.BlockSpec((tm, d), lambda i: (i, 0)),
        out_shape=jax.ShapeDtypeStruct((t, d), F32),
        compiler_params=_params(("parallel",)),
        name="matmul_norm_residual",
    )(*a_list, *w_list, g, x)


FFN_CHUNK = 256


def _ffn_tail(up_ref, xm2_of, xm1_of, cw_ref, cb_ref, dw_ref, g_ref, x_ref, o_ref):
    f = dw_ref.shape[0]
    acc = jnp.zeros(o_ref.shape, F32)
    for c0 in range(0, f, FFN_CHUNK):
        c1 = c0 + FFN_CHUNK
        halves = []
        for off in (0, f):
            a, b = c0 + off, c1 + off
            halves.append(cb_ref[:, a:b] + xm2_of(a, b) * cw_ref[0:1, a:b]
                          + xm1_of(a, b) * cw_ref[1:2, a:b] + up_ref[:, a:b] * cw_ref[2:3, a:b])
        act = (halves[0] * _silu(halves[1])).astype(BF16)
        acc = acc + _dot(act, dw_ref[c0:c1, :])
    o_ref[...] = x_ref[...] + _rms(acc, g_ref[...])


def _ffn_seq_kernel(tiles_per_seq, emit_next, x_ref, g1_ref, uw_ref, cw_ref, cb_ref, dw_ref, g2_ref, *rest):
    if emit_next:
        g3_ref, o_ref, tail_ref, n_ref, carry_sc = rest
    else:
        o_ref, tail_ref, carry_sc = rest
    tm = x_ref.shape[0]
    f = dw_ref.shape[0]

    @pl.when(pl.program_id(0) % tiles_per_seq == 0)
    def _():
        carry_sc[...] = jnp.zeros_like(carry_sc)

    x = x_ref[...]
    xn = _rms(x, g1_ref[...]).astype(BF16)
    top_row = lax.broadcasted_iota(jnp.int32, (SUBLANES, FFN_CHUNK), 0)

    def earlier(up, prev, k):
        rolled = pltpu.roll(up, k, axis=0)
        top = jnp.where(top_row < k, pltpu.roll(prev, k, axis=0), rolled[:SUBLANES])
        return jnp.concatenate([top, rolled[SUBLANES:]], axis=0)

    def project(c0):
        return [_dot(xn, uw_ref[:, c0 + off:c0 + off + FFN_CHUNK]) for off in (0, f)]

    acc = jnp.zeros(o_ref.shape, F32)
    ups_next = project(0)
    act_prev = None
    for c0 in range(0, f, FFN_CHUNK):
        ups = ups_next
        if c0 + FFN_CHUNK < f:
            ups_next = project(c0 + FFN_CHUNK)
        if act_prev is not None:
            acc = acc + _dot(act_prev, dw_ref[c0 - FFN_CHUNK:c0, :])
        halves = []
        for up, off in zip(ups, (0, f)):
            a, b = c0 + off, c0 + off + FFN_CHUNK
            prev = carry_sc[:, a:b]
            carry_sc[:, a:b] = up[tm - SUBLANES:, :]
            tail_ref[0, :, a:b] = up[tm - SUBLANES:, :]
            halves.append(cb_ref[:, a:b] + earlier(up, prev, 2) * cw_ref[0:1, a:b]
                          + earlier(up, prev, 1) * cw_ref[1:2, a:b] + up * cw_ref[2:3, a:b])
        act_prev = (halves[0] * _silu(halves[1])).astype(BF16)
    acc = acc + _dot(act_prev, dw_ref[f - FFN_CHUNK:f, :])
    out = x + _rms(acc, g2_ref[...])
    o_ref[...] = out
    if emit_next:
        n_ref[...] = _rms(out, g3_ref[...])


def ffn_seq(x, g_pre, up_w, conv_w, conv_b, down_w, g_post, g_next, tm, seq, layer):
    t, d = x.shape
    f2 = up_w.shape[-1]
    tiles_per_seq = seq // tm
    emit_next = g_next is not None
    kern = functools.partial(_ffn_seq_kernel, tiles_per_seq, emit_next)
    tile = pl.BlockSpec((tm, d), lambda i: (i, 0))
    return pl.pallas_call(
        kern,
        grid=(t // tm,),
        in_specs=[tile, _resident((1, d)), _resident(up_w.shape, layer), _resident(conv_w.shape),
                  _resident((1, f2)), _resident(down_w.shape, layer), _resident((1, d))]
        + [_resident((1, d))] * emit_next,
        out_specs=[tile, pl.BlockSpec((1, SUBLANES, f2), lambda i: (i // tiles_per_seq, 0, 0))]
        + [tile] * emit_next,
        out_shape=[jax.ShapeDtypeStruct((t, d), F32), jax.ShapeDtypeStruct((t // seq, SUBLANES, f2), F32)]
        + [jax.ShapeDtypeStruct((t, d), F32)] * emit_next,
        scratch_shapes=[pltpu.VMEM((SUBLANES, f2), F32)],
        compiler_params=_params(("arbitrary",)),
        name="ffn_seq",
    )(x, g_pre, up_w, conv_w, conv_b, down_w, g_post, *([g_next] * emit_next))


def _ffn_step_kernel(up_ref, xm2_ref, xm1_ref, cw_ref, cb_ref, dw_ref, g_ref, x_ref, o_ref):
    _ffn_tail(up_ref, lambda a, b: xm2_ref[:, a:b], lambda a, b: xm1_ref[:, a:b],
              cw_ref, cb_ref, dw_ref, g_ref, x_ref, o_ref)


def ffn_tail_step(up, xm2, xm1, conv_w, conv_b, down_w, g, x, layer):
    t, d = x.shape
    f2 = up.shape[1]
    return pl.pallas_call(
        _ffn_step_kernel,
        grid=(1,),
        in_specs=[_resident((t, f2))] * 3
        + [_resident(conv_w.shape), _resident((1, f2)), _resident(down_w.shape, layer),
           _resident((1, d)), _resident((t, d))],
        out_specs=pl.BlockSpec((t, d), lambda i: (0, 0)),
        out_shape=jax.ShapeDtypeStruct((t, d), F32),
        compiler_params=_params(("arbitrary",)),
        name="ffn_tail_step",
    )(up, xm2, xm1, conv_w, conv_b, down_w, g, x)


def _ssd_gate_norm(y, xs, z, dsk_ref, nw_ref):
    y = y + dsk_ref[...] * xs
    return _rms(y * _silu(z), nw_ref[...])


SSM_SEQS_PER_STEP = 4


def _head_expansion(nheads, head_dim):
    e = np.zeros((LANES, nheads * head_dim), np.float32)
    for h in range(nheads):
        e[h, h * head_dim:(h + 1) * head_dim] = 1.0
    return e


def _expand_heads(v, exp_ref):
    hi = v.astype(BF16)
    lo = (v - hi.astype(F32)).astype(BF16)
    return _dot(hi, exp_ref[...]) + _dot(lo, exp_ref[...])


def _ssd_prompt_kernel(x_ref, gpre_ref, wz_ref, wxbc_ref, wdt_ref, other_ref, wout_ref, gpost_ref,
                       cw_ref, cb_ref, dtb_ref, alog_ref, dsk_ref, nw_ref, exp_ref,
                       xo_ref, hout_ref, histout_ref, hist_sc, h_sc, xp_sc, y_sc):
    c = pl.program_id(1)
    nchunks = pl.num_programs(1)
    nseq, cl = x_ref.shape[0], x_ref.shape[1]
    hn, d_inner = h_sc.shape[1], h_sc.shape[2]
    hp = SSM_HEAD_DIM
    nheads = d_inner // hp
    heads_per_group = nheads // SSM_GROUPS
    group_width = heads_per_group * hp
    seqs = range(nseq)

    @pl.when(c == 0)
    def _():
        hist_sc[...] = jnp.zeros_like(hist_sc)
        h_sc[...] = jnp.zeros_like(h_sc)

    x = [x_ref[s] for s in seqs]
    xn = [_rms(x[s], gpre_ref[...]).astype(BF16) for s in seqs]
    xs, bm, cm = [], [], []
    for s in seqs:
        raw = _dot(xn[s], wxbc_ref[...])
        xp_sc[s, 0:SUBLANES, :] = hist_sc[s]
        xp_sc[s, SUBLANES:, :] = raw
        hist_sc[s] = raw[cl - SUBLANES:, :]
        conv = cb_ref[...]
        for k in range(SSM_CONV):
            conv = conv + xp_sc[s, pl.ds(SUBLANES - (SSM_CONV - 1) + k, cl), :] * cw_ref[k:k + 1, :]
        act = _silu(conv)
        xs.append(act[:, :d_inner])
        bm.append(act[:, d_inner:d_inner + SSM_GROUPS * SSM_STATE])
        cm.append(act[:, d_inner + SSM_GROUPS * SSM_STATE:])

    ti = lax.broadcasted_iota(jnp.int32, (cl, cl), 0)
    si = lax.broadcasted_iota(jnp.int32, (cl, cl), 1)
    tril = ti >= si
    tril_f = tril.astype(F32)
    dt, cum_col = [], []
    for s in seqs:
        dt.append(jax.nn.softplus(_dot(xn[s], wdt_ref[...]) + dtb_ref[...]))
        da = dt[s] * (-jnp.exp(alog_ref[...]))
        cum_col.append(jnp.dot(tril_f, da, preferred_element_type=F32,
                               precision=lax.Precision.HIGHEST))
    cum_row = [cum_col[s].T for s in seqs]
    dt_row = [dt[s].T for s in seqs]

    from_start, to_end, chunk_decay = [], [], []
    for s in seqs:
        last = cum_col[s][cl - 1:cl, :]
        per_head = jnp.concatenate([jnp.exp(cum_col[s]), jnp.exp(last - cum_col[s]) * dt[s],
                                    jnp.broadcast_to(jnp.exp(last), (SUBLANES, LANES))], axis=0)
        wide = _expand_heads(per_head, exp_ref)
        from_start.append(wide[:cl])
        to_end.append(wide[cl:2 * cl])
        chunk_decay.append(wide[2 * cl:2 * cl + 1])

    z = [_dot(xn[s], wz_ref[...]) for s in seqs]
    xs16 = [xs[s].astype(BF16) for s in seqs]
    cbs = []
    for s in seqs:
        xw16 = (xs[s] * to_end[s]).astype(BF16)
        bm_t = bm[s].T
        st = h_sc[s]
        st16 = st.astype(BF16)
        cb_s, state_terms, updates = [], [], []
        for g in range(SSM_GROUPS):
            sl = slice(g * SSM_STATE, (g + 1) * SSM_STATE)
            gl = slice(g * group_width, (g + 1) * group_width)
            cm_g = cm[s][:, sl].astype(BF16)
            cb_s.append(_dot_nt(cm_g, bm[s][:, sl].astype(BF16)))
            state_terms.append(_dot(cm_g, st16[:, gl]))
            updates.append(_dot(bm_t[sl, :].astype(BF16), xw16[:, gl]))
        cbs.append(cb_s)
        h_sc[s] = st * chunk_decay[s] + jnp.concatenate(updates, axis=1)
        y_sc[s] = jnp.concatenate(state_terms, axis=1) * from_start[s]

    for h in range(nheads):
        hl = slice(h * hp, (h + 1) * hp)
        for s in seqs:
            seg = jnp.exp(jnp.where(tril, cum_col[s][:, h:h + 1] - cum_row[s][h:h + 1, :], -jnp.inf))
            wts = cbs[s][h // heads_per_group] * seg * dt_row[s][h:h + 1, :]
            y_sc[s, :, hl] = y_sc[s, :, hl] + _dot(wts.astype(BF16), xs16[s][:, hl])

    for s in seqs:
        y = _ssd_gate_norm(y_sc[s], xs[s], z[s], dsk_ref, nw_ref).astype(BF16)
        xo_ref[s] = x[s] + _rms(_dot(y, wout_ref[...]) + other_ref[s], gpost_ref[...])

    @pl.when(c == nchunks - 1)
    def _():
        pad = jnp.zeros((LANES - hn, LANES), F32)
        for s in seqs:
            histout_ref[s] = hist_sc[s]
            final = h_sc[s]
            for j in range(d_inner // LANES):
                tile_t = jnp.concatenate([final[:, j * LANES:(j + 1) * LANES], pad], axis=0).T
                for r in range(LANES // hp):
                    hout_ref[s, j * (LANES // hp) + r] = tile_t[r * hp:(r + 1) * hp, :hn]


def ssd_prompt(x, g_pre, w_z, w_xbc, w_dt, other, w_out, g_post, conv_w, conv_b, dt_bias, a_log, d_skip, norm_w,
               batch, seq):
    t, d = x.shape
    d_inner = w_z.shape[1]
    conv_dim = w_xbc.shape[1]
    nheads = d_inner // SSM_HEAD_DIM
    cl = SSM_CHUNK
    nchunks = seq // cl
    ns = SSM_SEQS_PER_STEP
    rows = pl.BlockSpec((ns, cl, d), lambda b, c: (b, c, 0))
    xo, hout, hist = pl.pallas_call(
        _ssd_prompt_kernel,
        grid=(batch // ns, nchunks),
        in_specs=[rows, _resident((1, d)), _resident(w_z.shape), _resident(w_xbc.shape),
                  _resident(w_dt.shape), rows, _resident(w_out.shape), _resident((1, d)),
                  _resident(conv_w.shape), _resident((1, conv_dim)), _resident((1, LANES)),
                  _resident((1, LANES)), _resident((1, d_inner)), _resident((1, d_inner)),
                  _resident((LANES, d_inner))],
        out_specs=[rows,
                   pl.BlockSpec((ns, nheads, SSM_HEAD_DIM, SSM_STATE), lambda b, c: (b, 0, 0, 0)),
                   pl.BlockSpec((ns, SUBLANES, conv_dim), lambda b, c: (b, 0, 0))],
        out_shape=[jax.ShapeDtypeStruct((batch, seq, d), F32),
                   jax.ShapeDtypeStruct((batch, nheads, SSM_HEAD_DIM, SSM_STATE), F32),
                   jax.ShapeDtypeStruct((batch, SUBLANES, conv_dim), F32)],
        scratch_shapes=[pltpu.VMEM((ns, SUBLANES, conv_dim), F32),
                        pltpu.VMEM((ns, SSM_STATE, d_inner), F32),
                        pltpu.VMEM((ns, cl + SUBLANES, conv_dim), F32),
                        pltpu.VMEM((ns, cl, d_inner), F32)],
        compiler_params=_params(("arbitrary", "arbitrary")),
        name="ssd_prompt",
    )(x.reshape(batch, seq, d), g_pre, w_z, w_xbc, w_dt, other.reshape(batch, seq, d), w_out, g_post,
      conv_w, conv_b, dt_bias, a_log, d_skip, norm_w, jnp.asarray(_head_expansion(nheads, SSM_HEAD_DIM), BF16))
    return xo.reshape(t, d), hout, hist


def _ssd_step_kernel(z_ref, xnew_ref, h0_ref, h1_ref, h2_ref, dt_ref, cw_ref, cb_ref, dtb_ref, alog_ref,
                     dsk_ref, nw_ref, st_ref, y_ref, stout_ref,
                     xs_sc, xst_sc, bmt_sc, cmt_sc, dtt_sc, dat_sc, yt_sc):
    h = pl.program_id(0)
    nheads = pl.num_programs(0)
    d_inner = xs_sc.shape[1]
    hp, hn = SSM_HEAD_DIM, SSM_STATE
    heads_per_group = d_inner // hp // SSM_GROUPS

    @pl.when(h == 0)
    def _():
        conv = (cb_ref[...] + h0_ref[...] * cw_ref[0:1, :] + h1_ref[...] * cw_ref[1:2, :]
                + h2_ref[...] * cw_ref[2:3, :] + xnew_ref[...] * cw_ref[3:4, :])
        act = _silu(conv)
        xs = act[:, :d_inner]
        dt = jax.nn.softplus(dt_ref[...] + dtb_ref[...])
        xs_sc[...] = xs
        xst_sc[...] = xs.T
        bmt_sc[...] = act[:, d_inner:d_inner + SSM_GROUPS * hn].T
        cmt_sc[...] = act[:, d_inner + SSM_GROUPS * hn:].T
        dtt_sc[...] = dt.T
        dat_sc[...] = jnp.exp(dt * (-jnp.exp(alog_ref[...]))).T

    g = h // heads_per_group
    bmt = bmt_sc[pl.ds(pl.multiple_of(g * hn, hn), hn), :]
    cmt = cmt_sc[pl.ds(pl.multiple_of(g * hn, hn), hn), :]
    da = dat_sc[pl.ds(h, 1), :]
    dtx = xst_sc[pl.ds(pl.multiple_of(h * hp, hp), hp), :] * dtt_sc[pl.ds(h, 1), :]
    rows_per_tile = LANES // hn
    for j in range(hp // rows_per_tile):
        tile = st_ref[:, j * LANES:(j + 1) * LANES].T
        news = []
        for r in range(rows_per_tile):
            p = j * rows_per_tile + r
            new = da * tile[r * hn:(r + 1) * hn, :] + dtx[p:p + 1, :] * bmt
            yt_sc[pl.ds(h * hp + p, 1), :] = jnp.sum(new * cmt, axis=0, keepdims=True)
            news.append(new)
        stout_ref[:, j * LANES:(j + 1) * LANES] = jnp.concatenate(news, axis=0).T

    @pl.when(h == nheads - 1)
    def _():
        y_ref[...] = _ssd_gate_norm(yt_sc[...].T, xs_sc[...], z_ref[...], dsk_ref, nw_ref).astype(y_ref.dtype)


def ssd_step(z, xnew, hist, dt, conv_w, conv_b, dt_bias, a_log, d_skip, norm_w, state):
    b, d_inner = z.shape
    conv_dim = xnew.shape[1]
    nheads = d_inner // SSM_HEAD_DIM
    per_head = SSM_HEAD_DIM * SSM_STATE
    full = lambda shape: pl.BlockSpec(shape, lambda h: (0, 0))
    y, st = pl.pallas_call(
        _ssd_step_kernel,
        grid=(nheads,),
        in_specs=[full((b, d_inner))] + [full((b, conv_dim))] * 4 + [full((b, LANES))]
        + [full(conv_w.shape), full((1, conv_dim)), full((1, LANES)), full((1, LANES)),
           full((1, d_inner)), full((1, d_inner)),
           pl.BlockSpec((b, per_head), lambda h: (0, h))],
        out_specs=[full((b, d_inner)), pl.BlockSpec((b, per_head), lambda h: (0, h))],
        out_shape=[jax.ShapeDtypeStruct((b, d_inner), BF16),
                   jax.ShapeDtypeStruct((b, nheads * per_head), F32)],
        scratch_shapes=[pltpu.VMEM((b, d_inner), F32), pltpu.VMEM((d_inner, b), F32),
                        pltpu.VMEM((SSM_GROUPS * SSM_STATE, b), F32),
                        pltpu.VMEM((SSM_GROUPS * SSM_STATE, b), F32),
                        pltpu.VMEM((LANES, b), F32), pltpu.VMEM((LANES, b), F32),
                        pltpu.VMEM((d_inner, b), F32)],
        compiler_params=_params(("arbitrary",)),
        name="ssd_step",
    )(z, xnew, hist[0], hist[1], hist[2], dt, conv_w, conv_b, dt_bias, a_log, d_skip, norm_w,
      state.reshape(b, nheads * per_head))
    return y, st.reshape(b, nheads, SSM_HEAD_DIM, SSM_STATE)


def _hgrn_lower_bound(lbp_ref, layer):
    raw = lbp_ref[...]
    e = jnp.exp(raw - jnp.max(raw, axis=0, keepdims=True))
    return jnp.sum(e[:layer + 1], axis=0, keepdims=True) / jnp.sum(e, axis=0, keepdims=True)


def _hgrn_gates(q_raw, f_raw, lb):
    q = _silu(q_raw)
    f = lb + (1.0 - lb) * jax.nn.sigmoid(f_raw)
    k = (1.0 - lb) * jax.nn.sigmoid(-f_raw)
    return q, f, k


def _pair_levels(cl):
    t = np.arange(cl)[:, None]
    s = np.arange(cl)[None, :]
    lvl = np.floor(np.log2(np.maximum(t ^ s, 1))).astype(np.int32)
    return np.where(t > s, lvl, np.where(t == s, -1, -2)).astype(np.int32)


HG_HEADS_PER_DOT = 2
HG_GROUP = 8


def _hgrn_prompt_kernel(layer, x_ref, gpre_ref, win_ref, wout_ref, lvl_ref, lbp_ref, nw_ref,
                        m_ref, sout_ref, st_sc):
    c = pl.program_id(1)
    nchunks = pl.num_programs(1)
    cl = x_ref.shape[0]
    nheads = st_sc.shape[0]
    width = nheads * HG_DIM
    span = HG_HEADS_PER_DOT * HG_DIM

    @pl.when(c == 0)
    def _():
        st_sc[...] = jnp.zeros_like(st_sc)

    lb_all = _hgrn_lower_bound(lbp_ref, layer)
    row = lax.broadcasted_iota(jnp.int32, (cl, LANES), 0)
    lvl = lvl_ref[...]
    tril16 = (lvl >= -1).astype(BF16)
    xn = _rms(x_ref[...], gpre_ref[...]).astype(BF16)

    def project(j):
        return [_dot(xn, win_ref[:, which * width + j * span:which * width + (j + 1) * span])
                for which in range(4)]

    acc = None
    for g0 in range(0, nheads, HG_GROUP):
        heads = range(g0, g0 + HG_GROUP)
        raw = [{}, {}, {}, {}]
        for j in range(g0 // HG_HEADS_PER_DOT, (g0 + HG_GROUP) // HG_HEADS_PER_DOT):
            for which, both in enumerate(project(j)):
                for hh in range(HG_HEADS_PER_DOT):
                    raw[which][j * HG_HEADS_PER_DOT + hh] = both[:, hh * HG_DIM:(hh + 1) * HG_DIM]
        q, k, cum = {}, {}, {}
        for h in heads:
            q[h], fh, k[h] = _hgrn_gates(raw[0][h], raw[1][h], lb_all[:, h * HG_DIM:(h + 1) * HG_DIM])
            total = jnp.log(fh)
            shift = 1
            while shift < cl:
                total = total + jnp.where(row >= shift, pltpu.roll(total, shift, axis=0), 0.0)
                shift *= 2
            cum[h] = total

        att = {h: jnp.where(lvl == -1, _dot_nt(q[h].astype(BF16), k[h].astype(BF16)), 0.0) for h in heads}
        last_of_block = dict(cum)
        blk = 1
        level = 0
        while blk < cl:
            odd = (row & blk) != 0
            for h in heads:
                expo = jnp.where(odd, cum[h] - pltpu.roll(last_of_block[h], blk, axis=0),
                                 last_of_block[h] - cum[h])
                scaled = (jnp.where(odd, q[h], k[h]) * jnp.exp(expo)).astype(BF16)
                att[h] = jnp.where(lvl == level, _dot_nt(scaled, scaled), att[h])
                last_of_block[h] = jnp.where(odd, last_of_block[h],
                                             pltpu.roll(last_of_block[h], cl - blk, axis=0))
            blk *= 2
            level += 1
        cum_last = last_of_block

        outs = []
        for h in heads:
            st = st_sc[h]
            v = raw[2][h]
            o = (_dot(att[h].astype(BF16), v.astype(BF16))
                 + _dot_nt((q[h] * jnp.exp(cum[h])).astype(BF16), st.astype(BF16)))
            k_end = (k[h] * jnp.exp(cum_last[h] - cum[h])).astype(BF16)
            st_sc[h] = st * jnp.exp(cum_last[h]) + _dot(v.T.astype(BF16), k_end)
            outs.append((_rms(o, nw_ref[...]) * _silu(raw[3][h])).astype(BF16))
        part = _dot(jnp.concatenate(outs, axis=1), wout_ref[g0 * HG_DIM:(g0 + HG_GROUP) * HG_DIM, :])
        acc = part if acc is None else acc + part
    m_ref[...] = acc

    @pl.when(c == nchunks - 1)
    def _():
        for h in range(nheads):
            sout_ref[0, h] = st_sc[h].T


def hgrn_prompt(x, g_pre, w_qfig, w_out, lb_param, norm_w, layer, batch, seq):
    t, d = x.shape
    width = w_out.shape[0]
    nheads = width // HG_DIM
    cl = HG_CHUNK
    nchunks = seq // cl
    row = lambda b, c: (b * nchunks + c, 0)
    kern = functools.partial(_hgrn_prompt_kernel, layer)
    return pl.pallas_call(
        kern,
        grid=(batch, nchunks),
        in_specs=[pl.BlockSpec((cl, d), row), _resident((1, d)), _resident(w_qfig.shape), _resident(w_out.shape),
                  _resident((cl, cl)), _resident(lb_param.shape), _resident((1, HG_DIM))],
        out_specs=[pl.BlockSpec((cl, d), row),
                   pl.BlockSpec((1, nheads, HG_DIM, HG_DIM), lambda b, c: (b, 0, 0, 0))],
        out_shape=[jax.ShapeDtypeStruct((t, d), F32),
                   jax.ShapeDtypeStruct((batch, nheads, HG_DIM, HG_DIM), F32)],
        scratch_shapes=[pltpu.VMEM((nheads, HG_DIM, HG_DIM), F32)],
        compiler_params=_params(("arbitrary", "arbitrary")),
        name="hgrn_prompt",
    )(x, g_pre, w_qfig, w_out, jnp.asarray(_pair_levels(cl)), lb_param, norm_w)


HG_STEP_TOKENS = 8


def _hgrn_step_kernel(layer, qfig_ref, lbp_ref, nw_ref, st_ref, o_ref, stout_ref):
    nheads = st_ref.shape[2]
    raw = lbp_ref[...]
    e = jnp.exp(raw - jnp.max(raw, axis=0, keepdims=True))
    lb = jnp.sum(e[:layer + 1], axis=0) / jnp.sum(e, axis=0)
    pad = jnp.zeros((HG_DIM - 2 * nheads, HG_DIM), F32)
    for bl in range(st_ref.shape[1]):
        t = qfig_ref[bl]
        q, f, k = _hgrn_gates(t[0:nheads], t[nheads:2 * nheads], lb)
        v = t[2 * nheads:3 * nheads]
        cols = jnp.concatenate([f, k, pad], axis=0).T
        q16 = q.astype(BF16)
        rows = []
        for h in range(nheads):
            f_col = cols[:, h:h + 1]
            k_col = cols[:, nheads + h:nheads + h + 1]
            new = f_col * st_ref[0, bl, h] + k_col * v[h:h + 1, :]
            stout_ref[bl, h] = new
            rows.append(_dot(q16, new.astype(BF16))[h:h + 1, :])
        o = jnp.concatenate(rows, axis=0)
        o_ref[bl] = _rms(o, nw_ref[...]) * _silu(t[3 * nheads:])


def hgrn_step(qfig, lb_param, norm_w, layer, state_all):
    b, w4 = qfig.shape
    width = w4 // 4
    nheads = width // HG_DIM
    bb = HG_STEP_TOKENS
    kern = functools.partial(_hgrn_step_kernel, layer)
    o, st = pl.pallas_call(
        kern,
        grid=(b // bb,),
        in_specs=[pl.BlockSpec((bb, 4 * nheads, HG_DIM), lambda i: (i, 0, 0)),
                  _resident((lb_param.shape[0], nheads, HG_DIM)), _resident((1, HG_DIM)),
                  pl.BlockSpec((1, bb, nheads, HG_DIM, HG_DIM), lambda i: (layer, i, 0, 0, 0))],
        out_specs=[pl.BlockSpec((bb, nheads, HG_DIM), lambda i: (i, 0, 0)),
                   pl.BlockSpec((bb, nheads, HG_DIM, HG_DIM), lambda i: (i, 0, 0, 0))],
        out_shape=[jax.ShapeDtypeStruct((b, nheads, HG_DIM), F32),
                   jax.ShapeDtypeStruct((b, nheads, HG_DIM, HG_DIM), F32)],
        compiler_params=_params(("parallel",)),
        name="hgrn_step",
    )(qfig.reshape(b, 4 * nheads, HG_DIM), lb_param.reshape(-1, nheads, HG_DIM), norm_w, state_all)
    return o.reshape(b, width), st


def _s5_discretize_kernel(lre_ref, lim_ref, step_ref, bre_ref, bim_ref, are_ref, aim_ref, bbre_ref, bbim_ref):
    lre, lim = lre_ref[...], lim_ref[...]
    step = jnp.exp(step_ref[...])
    mag = jnp.exp(lre * step)
    are = mag * jnp.cos(lim * step)
    aim = mag * jnp.sin(lim * step)
    den = lre * lre + lim * lim
    nr = are - 1.0
    cre = (nr * lre + aim * lim) / den
    cim = (aim * lre - nr * lim) / den
    are_ref[...] = are
    aim_ref[...] = aim
    bre, bim = bre_ref[...], bim_ref[...]
    bbre_ref[...] = cre * bre - cim * bim
    bbim_ref[...] = cre * bim + cim * bre


def s5_discretize(lam_re, lam_im, log_step, b_re_t, b_im_t):
    g, n = lam_re.shape
    c = b_re_t.shape[1]
    return pl.pallas_call(
        _s5_discretize_kernel,
        out_shape=[jax.ShapeDtypeStruct((g, 1, n), F32)] * 2 + [jax.ShapeDtypeStruct((g, c, n), F32)] * 2,
        name="s5_discretize",
    )(lam_re.reshape(g, 1, n), lam_im.reshape(g, 1, n), log_step.reshape(g, 1, 1), b_re_t, b_im_t)


def _complex_power(re, im, n):
    out = None
    while n:
        if n & 1:
            out = (re, im) if out is None else (out[0] * re - out[1] * im, out[0] * im + out[1] * re)
        n >>= 1
        if n:
            re, im = re * re - im * im, 2.0 * re * im
    return out


def _s5_output(xre, xim, u, cre_ref, cim_ref, d_ref):
    y = _dot(xre.astype(BF16), cre_ref[0]) - _dot(xim.astype(BF16), cim_ref[0]) + d_ref[...] * u
    return jax.nn.gelu(y)


S5_ROWS = 256


def _s5_prompt_kernel(u_ref, b_ref, cre_ref, cim_ref, d_ref, are_ref, aim_ref,
                      y_ref, sre_ref, sim_ref, x_sc, ui_sc):
    nrows = u_ref.shape[0]
    ns = S5_BLOCK_STATE
    nt = ns // LANES
    nseg = S5_SEGMENTS
    seg = nrows // nseg

    steps = S5_ROWS // nseg

    def same_step_rows(i):
        return pl.ds(i, nseg, stride=seg)

    nslabs = nrows // S5_ROWS

    def slab_rows(s):
        return slice(s * S5_ROWS, (s + 1) * S5_ROWS)

    def project(s):
        r = slab_rows(s)
        ui = jnp.concatenate([u_ref[same_step_rows(s * steps + ii), :] for ii in range(steps)], axis=0)
        ui_sc[r, :] = ui
        bu = _dot(ui.astype(BF16), b_ref[0])
        for k in range(2 * nt):
            x_sc[k, r, :] = bu[:, k * LANES:(k + 1) * LANES]

    def lane_tile(ref, k):
        return ref[:, k * LANES:(k + 1) * LANES]

    are = [jnp.broadcast_to(lane_tile(are_ref, k), (nseg, LANES)) for k in range(nt)]
    aim = [jnp.broadcast_to(lane_tile(aim_ref, k), (nseg, LANES)) for k in range(nt)]

    def scan_slab(keep, s, carry):
        for i in range(s * steps, (s + 1) * steps):
            rows = slice(i * nseg, (i + 1) * nseg)
            new = []
            for k in range(nt):
                sre, sim = carry[k]
                nre = are[k] * sre - aim[k] * sim + x_sc[k, rows, :]
                nim = are[k] * sim + aim[k] * sre + x_sc[nt + k, rows, :]
                if keep:
                    x_sc[k, rows, :] = nre
                    x_sc[nt + k, rows, :] = nim
                new.append((nre, nim))
            carry = tuple(new)
        return carry

    zero = jnp.zeros((nseg, LANES), F32)
    ends = tuple((zero, zero) for _ in range(nt))
    project(0)
    for s in range(nslabs):
        if s + 1 < nslabs:
            project(s + 1)
        ends = scan_slab(False, s, ends)

    seg_row = lax.broadcasted_iota(jnp.int32, (nseg, LANES), 0)
    starts = []
    for k in range(nt):
        ere, eim = ends[k]
        full_re, full_im = _complex_power(lane_tile(are_ref, k), lane_tile(aim_ref, k), seg)
        tre = tim = jnp.zeros((1, LANES), F32)
        cre = cim = zero
        for j in range(nseg):
            cre = jnp.where(seg_row == j, tre, cre)
            cim = jnp.where(seg_row == j, tim, cim)
            tre, tim = (ere[j:j + 1, :] + full_re * tre - full_im * tim,
                        eim[j:j + 1, :] + full_re * tim + full_im * tre)
        starts.append((cre, cim))
        sre_ref[0, :, k * LANES:(k + 1) * LANES] = tre
        sim_ref[0, :, k * LANES:(k + 1) * LANES] = tim

    def output(s):
        r = slab_rows(s)
        xre = jnp.concatenate([x_sc[k, r, :] for k in range(nt)], axis=1)
        xim = jnp.concatenate([x_sc[nt + k, r, :] for k in range(nt)], axis=1)
        y = _s5_output(xre, xim, ui_sc[r, :], cre_ref, cim_ref, d_ref)
        for ii in range(steps):
            y_ref[same_step_rows(s * steps + ii), :] = y[ii * nseg:(ii + 1) * nseg, :]

    state = tuple(starts)
    for s in range(nslabs):
        if s > 0:
            output(s - 1)
        state = scan_slab(True, s, state)
    output(nslabs - 1)


def s5_prompt(u, b_exp, c_re_exp, c_im_exp, d_skip, a_re, a_im, batch, seq):
    t, d = u.shape
    nblk = d // LANES
    ns = S5_BLOCK_STATE
    return pl.pallas_call(
        _s5_prompt_kernel,
        grid=(nblk, batch),
        in_specs=[pl.BlockSpec((seq, LANES), lambda j, b: (b, j)),
                  pl.BlockSpec((1, LANES, 2 * ns), lambda j, b: (j, 0, 0)),
                  pl.BlockSpec((1, ns, LANES), lambda j, b: (j, 0, 0)),
                  pl.BlockSpec((1, ns, LANES), lambda j, b: (j, 0, 0)),
                  pl.BlockSpec((1, LANES), lambda j, b: (0, j)),
                  pl.BlockSpec((1, ns), lambda j, b: (0, j)),
                  pl.BlockSpec((1, ns), lambda j, b: (0, j))],
        out_specs=[pl.BlockSpec((seq, LANES), lambda j, b: (b, j)),
                   pl.BlockSpec((1, 1, ns), lambda j, b: (b, 0, j)),
                   pl.BlockSpec((1, 1, ns), lambda j, b: (b, 0, j))],
        out_shape=[jax.ShapeDtypeStruct((t, d), F32),
                   jax.ShapeDtypeStruct((batch, 1, nblk * ns), F32),
                   jax.ShapeDtypeStruct((batch, 1, nblk * ns), F32)],
        scratch_shapes=[pltpu.VMEM((2 * ns // LANES, seq, LANES), F32), pltpu.VMEM((seq, LANES), F32)],
        compiler_params=_params(("arbitrary", "arbitrary")),
        name="s5_prompt",
    )(u, b_exp, c_re_exp, c_im_exp, d_skip, a_re, a_im)


def _s5_step_kernel(u_ref, b_ref, cre_ref, cim_ref, d_ref, are_ref, aim_ref, s0re_ref, s0im_ref,
                    y_ref, sre_ref, sim_ref):
    ns = S5_BLOCK_STATE
    u = u_ref[...]
    bu = _dot(u.astype(BF16), b_ref[0])
    are, aim = are_ref[...], aim_ref[...]
    s0re, s0im = s0re_ref[...], s0im_ref[...]
    xre = bu[:, 0:ns] + (are * s0re - aim * s0im)
    xim = bu[:, ns:2 * ns] + (are * s0im + aim * s0re)
    sre_ref[...] = xre
    sim_ref[...] = xim
    y_ref[...] = _s5_output(xre, xim, u, cre_ref, cim_ref, d_ref)


def s5_step(u, b_exp, c_re_exp, c_im_exp, d_skip, a_re, a_im, s0_re, s0_im):
    b, d = u.shape
    nblk = d // LANES
    ns = S5_BLOCK_STATE
    return pl.pallas_call(
        _s5_step_kernel,
        grid=(nblk,),
        in_specs=[pl.BlockSpec((b, LANES), lambda j: (0, j)),
                  pl.BlockSpec((1, LANES, 2 * ns), lambda j: (j, 0, 0)),
                  pl.BlockSpec((1, ns, LANES), lambda j: (j, 0, 0)),
                  pl.BlockSpec((1, ns, LANES), lambda j: (j, 0, 0)),
                  pl.BlockSpec((1, LANES), lambda j: (0, j)),
                  pl.BlockSpec((1, ns), lambda j: (0, j)),
                  pl.BlockSpec((1, ns), lambda j: (0, j)),
                  pl.BlockSpec((b, ns), lambda j: (0, j)),
                  pl.BlockSpec((b, ns), lambda j: (0, j))],
        out_specs=[pl.BlockSpec((b, LANES), lambda j: (0, j)),
                   pl.BlockSpec((b, ns), lambda j: (0, j)),
                   pl.BlockSpec((b, ns), lambda j: (0, j))],
        out_shape=[jax.ShapeDtypeStruct((b, d), F32),
                   jax.ShapeDtypeStruct((b, nblk * ns), F32),
                   jax.ShapeDtypeStruct((b, nblk * ns), F32)],
        compiler_params=_params(("parallel",)),
        name="s5_step",
    )(u, b_exp, c_re_exp, c_im_exp, d_skip, a_re, a_im, s0_re, s0_im)


def _block_diag(per_group):
    ngroups, r, c = per_group.shape
    p = S5_GROUPS_PER_BLOCK
    eye = jnp.eye(p, dtype=per_group.dtype)
    tiles = per_group.reshape(ngroups // p, p, r, c)
    return jnp.einsum('bjrc,jk->bjrkc', tiles, eye).reshape(ngroups // p, p * r, p * c)


def kernel(x_prompt, x_sample, state_ssm, state_ssm_conv, state_hgrn, state_s5_re, state_s5_im, state_ffn_conv, norm_mix_pre, norm_mix_post, norm_ffn_pre, norm_ffn_post, ab_in_w, ssm_conv_w, ssm_conv_b, ssm_dt_bias, ssm_a_log, ssm_d, ssm_norm_w, hgrn_lb, hgrn_norm_w, ab_out_w, s5_lam_re, s5_lam_im, s5_log_step, s5_b_re, s5_b_im, s5_c_re, s5_c_im, s5_d, s5_glu_w, ffn_up_w, ffn_conv_w, ffn_conv_b, ffn_down_w):
    batch, seq, d = x_prompt.shape
    dec_batch = x_sample.shape[0]
    depth = norm_mix_pre.shape[0]
    conv_dim = ssm_conv_w.shape[2]
    nheads = ssm_a_log.shape[1]
    d_inner = nheads * SSM_HEAD_DIM
    hg_width = hgrn_lb.shape[1]
    ffn_dim = ffn_down_w.shape[1]
    tm = 256

    def row(v):
        return v.reshape(1, -1)

    def pad_lanes(v):
        return jnp.pad(v, ((0, 0), (0, LANES - v.shape[1])))

    groups = [(x_prompt.reshape(batch * seq, d), True), (x_sample.reshape(dec_batch, d), False)]
    xs = [g[0] for g in groups]
    normed = [None, None]
    up_w_all = ffn_up_w.astype(BF16)
    down_w_all = ffn_down_w.astype(BF16)
    out = {'ssm': [[], []], 'sconv': [[], []], 'hg': [[], []], 's5r': [[], []], 's5i': [[], []], 'fconv': [[], []]}

    for l in range(depth):
        e = l // 2
        if l % 2 == 0:
            w_in = ab_in_w[e]
            o_xbc = d_inner
            o_dt = o_xbc + conv_dim
            o_q = o_dt + nheads
            w_z = w_in[:, :o_xbc].astype(BF16)
            w_xbc = w_in[:, o_xbc:o_dt].astype(BF16)
            w_dt = pad_lanes(w_in[:, o_dt:o_q]).astype(BF16)
            w_qfig = w_in[:, o_q:].astype(BF16)
            w_out = ab_out_w[e].astype(BF16)
            dt_bias = pad_lanes(row(ssm_dt_bias[e]))
            a_log = pad_lanes(row(ssm_a_log[e]))
            d_skip = row(jnp.repeat(ssm_d[e], SSM_HEAD_DIM))
            for gi, (_, is_prompt) in enumerate(groups):
                x = xs[gi]
                t = x.shape[0]
                if is_prompt:
                    m_hg, hg_new = hgrn_prompt(x, row(norm_mix_pre[l]), w_qfig, w_out[d_inner:], hgrn_lb,
                                               row(hgrn_norm_w[e]), e, batch, seq)
                    xs[gi], ssm_new, xbc_tail = ssd_prompt(
                        x, row(norm_mix_pre[l]), w_z, w_xbc, w_dt, m_hg, w_out[:d_inner], row(norm_mix_post[l]),
                        ssm_conv_w[e], row(ssm_conv_b[e]), dt_bias, a_log, d_skip, row(ssm_norm_w[e]), batch, seq)
                    sconv_new = xbc_tail[:, SUBLANES - (SSM_CONV - 1):]
                else:
                    z, xbc, dt, qfig = norm_matmul(x, row(norm_mix_pre[l]), [w_z, w_xbc, w_dt, w_qfig], t)
                    hist = [state_ssm_conv[e][:, k] for k in range(SSM_CONV - 1)]
                    y, ssm_new = ssd_step(z, xbc, hist, dt, ssm_conv_w[e], row(ssm_conv_b[e]), dt_bias, a_log,
                                          d_skip, row(ssm_norm_w[e]), state_ssm[e])
                    o, hg_new = hgrn_step(qfig, hgrn_lb, row(hgrn_norm_w[e]), e, state_hgrn)
                    sconv_new = jnp.stack(hist[1:] + [xbc], axis=1)
                    xs[gi] = matmul_norm_residual([y, o], [w_out[:d_inner], w_out[d_inner:]],
                                                  row(norm_mix_post[l]), x, t)
                out['ssm'][gi].append(ssm_new)
                out['hg'][gi].append(hg_new)
                out['sconv'][gi].append(sconv_new)
        else:
            ngroups = s5_lam_re.shape[1]
            a_re, a_im, bb_re, bb_im = s5_discretize(
                s5_lam_re[e], s5_lam_im[e], s5_log_step[e],
                jnp.swapaxes(s5_b_re[e], 1, 2), jnp.swapaxes(s5_b_im[e], 1, 2))
            a_re = a_re.reshape(1, ngroups * S5_STATE)
            a_im = a_im.reshape(1, ngroups * S5_STATE)
            b_exp = jnp.concatenate([_block_diag(bb_re), _block_diag(bb_im)], axis=2).astype(BF16)
            c_re_exp = _block_diag(jnp.swapaxes(s5_c_re[e], 1, 2)).astype(BF16)
            c_im_exp = _block_diag(jnp.swapaxes(s5_c_im[e], 1, 2)).astype(BF16)
            glu_w = s5_glu_w[e].astype(BF16)
            for gi, (_, is_prompt) in enumerate(groups):
                x = xs[gi]
                t = x.shape[0]
                u = normed[gi] if normed[gi] is not None else rmsnorm(x, row(norm_mix_pre[l]), min(tm, t))
                if is_prompt:
                    yg, s_re, s_im = s5_prompt(u, b_exp, c_re_exp, c_im_exp, row(s5_d[e]), a_re, a_im, batch, seq)
                else:
                    yg, s_re, s_im = s5_step(u, b_exp, c_re_exp, c_im_exp, row(s5_d[e]), a_re, a_im,
                                             state_s5_re[e].reshape(t, -1), state_s5_im[e].reshape(t, -1))
                out['s5r'][gi].append(s_re.reshape(-1, ngroups, S5_STATE))
                out['s5i'][gi].append(s_im.reshape(-1, ngroups, S5_STATE))
                xs[gi] = matmul_norm_residual([yg], [glu_w], row(norm_mix_post[l]), x, min(tm, t), glu=True)

        for gi, (_, is_prompt) in enumerate(groups):
            x = xs[gi]
            t = x.shape[0]
            normed[gi] = None
            if is_prompt:
                g_next = row(norm_mix_pre[l + 1]) if l + 1 < depth and (l + 1) % 2 == 1 else None
                res = ffn_seq(x, row(norm_ffn_pre[l]), up_w_all, ffn_conv_w[l], row(ffn_conv_b[l]), down_w_all,
                              row(norm_ffn_post[l]), g_next, tm, seq, l)
                xs[gi], tail = res[0], res[1]
                if g_next is not None:
                    normed[gi] = res[2]
                fconv_new = tail[:, SUBLANES - (FFN_CONV - 1):]
            else:
                (up,) = norm_matmul(x, row(norm_ffn_pre[l]), [up_w_all], t, layer=l)
                old = state_ffn_conv[l]
                xs[gi] = ffn_tail_step(up, old[:, 0], old[:, 1], ffn_conv_w[l], row(ffn_conv_b[l]), down_w_all,
                                       row(norm_ffn_post[l]), x, l)
                fconv_new = jnp.stack([old[:, 1], up], axis=1)
            out['fconv'][gi].append(fconv_new)

    y_prompt = xs[0].reshape(batch, seq, d)
    y_sample = xs[1].reshape(dec_batch, 1, d)
    states = []
    for gi in range(2):
        states += [jnp.stack(out[k][gi]) for k in ('ssm', 'sconv', 'hg', 's5r', 's5i', 'fconv')]
    return (y_prompt, y_sample, *states)
```

```python
import functools
import math

import jax
import jax.numpy as jnp
import numpy as np
from jax import lax
from jax.experimental import pallas as pl
from jax.experimental.pallas import tpu as pltpu

F32 = jnp.float32
BF16 = jnp.bfloat16
EPS = 1e-6

LANES = 128
SUBLANES = 8
VMEM_LIMIT_BYTES = 56 * 1024 * 1024

SSM_HEAD_DIM = 64
SSM_GROUPS = 2
SSM_STATE = 64
SSM_CONV = 4
SSM_CHUNK = 128
HG_DIM = 128
HG_CHUNK = 128
S5_GROUP = 16
S5_STATE = 64
S5_GROUPS_PER_BLOCK = LANES // S5_GROUP
S5_BLOCK_STATE = S5_GROUPS_PER_BLOCK * S5_STATE
FFN_CONV = 3


def _params(sem):
    return pltpu.CompilerParams(dimension_semantics=sem, vmem_limit_bytes=VMEM_LIMIT_BYTES)


def _resident(shape, layer=None):
    if layer is None:
        nd = len(shape)
        return pl.BlockSpec(shape, lambda *_: (0,) * nd, pipeline_mode=pl.Buffered(1))
    rest = tuple(shape[1:])
    return pl.BlockSpec((None,) + rest, lambda *_: (layer,) + (0,) * len(rest), pipeline_mode=pl.Buffered(1))


def _rms(x, w):
    return x * lax.rsqrt(jnp.mean(x * x, axis=-1, keepdims=True) + EPS) * w


def _silu(x):
    return x * jax.nn.sigmoid(x)


def _dot(a, b):
    return jnp.dot(a, b, preferred_element_type=F32)


def _dot_nt(a, b):
    return lax.dot_general(a, b, (((1,), (1,)), ((), ())), preferred_element_type=F32)


def _dot_tn(a, b):
    return lax.dot_general(a, b, (((0,), (0,)), ((), ())), preferred_element_type=F32)


def _norm_matmul_kernel(n_out, x_ref, g_ref, *refs):
    w_refs, o_refs = refs[:n_out], refs[n_out:]
    xn = _rms(x_ref[...], g_ref[...]).astype(BF16)
    for w_ref, o_ref in zip(w_refs, o_refs):
        n = w_ref.shape[1]
        for c0 in range(0, n, 512):
            c1 = min(n, c0 + 512)
            o_ref[:, c0:c1] = _dot(xn, w_ref[:, c0:c1])


def norm_matmul(x, g, ws, tm, layer=None):
    t, d = x.shape
    kern = functools.partial(_norm_matmul_kernel, len(ws))
    return pl.pallas_call(
        kern,
        grid=(t // tm,),
        in_specs=[pl.BlockSpec((tm, d), lambda i: (i, 0)), _resident((1, d))]
        + [_resident(w.shape, layer) for w in ws],
        out_specs=[pl.BlockSpec((tm, w.shape[-1]), lambda i: (i, 0)) for w in ws],
        out_shape=[jax.ShapeDtypeStruct((t, w.shape[-1]), F32) for w in ws],
        compiler_params=_params(("parallel",)),
        name="norm_matmul",
    )(x, g, *ws)


def _rmsnorm_kernel(x_ref, g_ref, o_ref):
    o_ref[...] = _rms(x_ref[...], g_ref[...])


def rmsnorm(x, g, tm):
    t, d = x.shape
    return pl.pallas_call(
        _rmsnorm_kernel,
        grid=(t // tm,),
        in_specs=[pl.BlockSpec((tm, d), lambda i: (i, 0)), _resident((1, d))],
        out_specs=pl.BlockSpec((tm, d), lambda i: (i, 0)),
        out_shape=jax.ShapeDtypeStruct((t, d), F32),
        compiler_params=_params(("parallel",)),
        name="rmsnorm",
    )(x, g)


def _matmul_norm_residual_kernel(n_in, glu, *refs):
    a_refs, w_refs = refs[:n_in], refs[n_in:2 * n_in]
    g_ref, x_ref, o_ref = refs[2 * n_in:]
    acc = None
    for a_ref, w_ref in zip(a_refs, w_refs):
        part = _dot(a_ref[...].astype(BF16), w_ref[...])
        acc = part if acc is None else acc + part
    if glu:
        d = acc.shape[1] // 2
        acc = acc[:, :d] * jax.nn.sigmoid(acc[:, d:])
    o_ref[...] = x_ref[...] + _rms(acc, g_ref[...])


def matmul_norm_residual(a_list, w_list, g, x, tm, glu=False):
    t, d = x.shape
    kern = functools.partial(_matmul_norm_residual_kernel, len(a_list), glu)
    return pl.pallas_call(
        kern,
        grid=(t // tm,),
        in_specs=[pl.BlockSpec((tm, a.shape[1]), lambda i: (i, 0)) for a in a_list]
        + [_resident(w.shape) for w in w_list]
        + [_resident((1, d)), pl.BlockSpec((tm, d), lambda i: (i, 0))],
        out_specs=pl.BlockSpec((tm, d), lambda i: (i, 0)),
        out_shape=jax.ShapeDtypeStruct((t, d), F32),
        compiler_params=_params(("parallel",)),
        name="matmul_norm_residual",
    )(*a_list, *w_list, g, x)


FFN_CHUNK = 256


def _ffn_tail(up_ref, xm2_of, xm1_of, cw_ref, cb_ref, dw_ref, g_ref, x_ref, o_ref):
    f = dw_ref.shape[0]
    acc = jnp.zeros(o_ref.shape, F32)
    for c0 in range(0, f, FFN_CHUNK):
        c1 = c0 + FFN_CHUNK
        halves = []
        for off in (0, f):
            a, b = c0 + off, c1 + off
            halves.append(cb_ref[:, a:b] + xm2_of(a, b) * cw_ref[0:1, a:b]
                          + xm1_of(a, b) * cw_ref[1:2, a:b] + up_ref[:, a:b] * cw_ref[2:3, a:b])
        act = (halves[0] * _silu(halves[1])).astype(BF16)
        acc = acc + _dot(act, dw_ref[c0:c1, :])
    o_ref[...] = x_ref[...] + _rms(acc, g_ref[...])


def _ffn_seq_kernel(tiles_per_seq, emit_next, x_ref, g1_ref, uw_ref, cw_ref, cb_ref, dw_ref, g2_ref, *rest):
    if emit_next:
        g3_ref, o_ref, tail_ref, n_ref, carry_sc = rest
    else:
        o_ref, tail_ref, carry_sc = rest
    tm = x_ref.shape[0]
    f = dw_ref.shape[0]

    @pl.when(pl.program_id(0) % tiles_per_seq == 0)
    def _():
        carry_sc[...] = jnp.zeros_like(carry_sc)

    x = x_ref[...]
    xn = _rms(x, g1_ref[...]).astype(BF16)
    top_row = lax.broadcasted_iota(jnp.int32, (SUBLANES, FFN_CHUNK), 0)

    def earlier(up, prev, k):
        rolled = pltpu.roll(up, k, axis=0)
        top = jnp.where(top_row < k, pltpu.roll(prev, k, axis=0), rolled[:SUBLANES])
        return jnp.concatenate([top, rolled[SUBLANES:]], axis=0)

    def project(c0):
        return [_dot(xn, uw_ref[:, c0 + off:c0 + off + FFN_CHUNK]) for off in (0, f)]

    acc = jnp.zeros(o_ref.shape, F32)
    ups_next = project(0)
    act_prev = None
    for c0 in range(0, f, FFN_CHUNK):
        ups = ups_next
        if c0 + FFN_CHUNK < f:
            ups_next = project(c0 + FFN_CHUNK)
        if act_prev is not None:
            acc = acc + _dot(act_prev, dw_ref[c0 - FFN_CHUNK:c0, :])
        halves = []
        for up, off in zip(ups, (0, f)):
            a, b = c0 + off, c0 + off + FFN_CHUNK
            prev = carry_sc[:, a:b]
            carry_sc[:, a:b] = up[tm - SUBLANES:, :]
            tail_ref[0, :, a:b] = up[tm - SUBLANES:, :]
            halves.append(cb_ref[:, a:b] + earlier(up, prev, 2) * cw_ref[0:1, a:b]
                          + earlier(up, prev, 1) * cw_ref[1:2, a:b] + up * cw_ref[2:3, a:b])
        act_prev = (halves[0] * _silu(halves[1])).astype(BF16)
    acc = acc + _dot(act_prev, dw_ref[f - FFN_CHUNK:f, :])
    out = x + _rms(acc, g2_ref[...])
    o_ref[...] = out
    if emit_next:
        n_ref[...] = _rms(out, g3_ref[...])


def ffn_seq(x, g_pre, up_w, conv_w, conv_b, down_w, g_post, g_next, tm, seq, layer):
    t, d = x.shape
    f2 = up_w.shape[-1]
    tiles_per_seq = seq // tm
    emit_next = g_next is not None
    kern = functools.partial(_ffn_seq_kernel, tiles_per_seq, emit_next)
    tile = pl.BlockSpec((tm, d), lambda i: (i, 0))
    return pl.pallas_call(
        kern,
        grid=(t // tm,),
        in_specs=[tile, _resident((1, d)), _resident(up_w.shape, layer), _resident(conv_w.shape),
                  _resident((1, f2)), _resident(down_w.shape, layer), _resident((1, d))]
        + [_resident((1, d))] * emit_next,
        out_specs=[tile, pl.BlockSpec((1, SUBLANES, f2), lambda i: (i // tiles_per_seq, 0, 0))]
        + [tile] * emit_next,
        out_shape=[jax.ShapeDtypeStruct((t, d), F32), jax.ShapeDtypeStruct((t // seq, SUBLANES, f2), F32)]
        + [jax.ShapeDtypeStruct((t, d), F32)] * emit_next,
        scratch_shapes=[pltpu.VMEM((SUBLANES, f2), F32)],
        compiler_params=_params(("arbitrary",)),
        name="ffn_seq",
    )(x, g_pre, up_w, conv_w, conv_b, down_w, g_post, *([g_next] * emit_next))


def _ffn_step_kernel(up_ref, xm2_ref, xm1_ref, cw_ref, cb_ref, dw_ref, g_ref, x_ref, o_ref):
    _ffn_tail(up_ref, lambda a, b: xm2_ref[:, a:b], lambda a, b: xm1_ref[:, a:b],
              cw_ref, cb_ref, dw_ref, g_ref, x_ref, o_ref)


def ffn_tail_step(up, xm2, xm1, conv_w, conv_b, down_w, g, x, layer):
    t, d = x.shape
    f2 = up.shape[1]
    return pl.pallas_call(
        _ffn_step_kernel,
        grid=(1,),
        in_specs=[_resident((t, f2))] * 3
        + [_resident(conv_w.shape), _resident((1, f2)), _resident(down_w.shape, layer),
           _resident((1, d)), _resident((t, d))],
        out_specs=pl.BlockSpec((t, d), lambda i: (0, 0)),
        out_shape=jax.ShapeDtypeStruct((t, d), F32),
        compiler_params=_params(("arbitrary",)),
        name="ffn_tail_step",
    )(up, xm2, xm1, conv_w, conv_b, down_w, g, x)


def _ssd_gate_norm(y, xs, z, dsk_ref, nw_ref):
    y = y + dsk_ref[...] * xs
    return _rms(y * _silu(z), nw_ref[...])


SSM_SEQS_PER_STEP = 4


def _head_expansion(nheads, head_dim):
    e = np.zeros((LANES, nheads * head_dim), np.float32)
    for h in range(nheads):
        e[h, h * head_dim:(h + 1) * head_dim] = 1.0
    return e


def _expand_heads(v, exp_ref):
    hi = v.astype(BF16)
    lo = (v - hi.astype(F32)).astype(BF16)
    return _dot(hi, exp_ref[...]) + _dot(lo, exp_ref[...])


def _ssd_prompt_kernel(x_ref, gpre_ref, wz_ref, wxbc_ref, wdt_ref, other_ref, wout_ref, gpost_ref,
                       cw_ref, cb_ref, dtb_ref, alog_ref, dsk_ref, nw_ref, exp_ref,
                       xo_ref, hout_ref, histout_ref, hist_sc, h_sc, xp_sc, y_sc):
    c = pl.program_id(1)
    nchunks = pl.num_programs(1)
    nseq, cl = x_ref.shape[0], x_ref.shape[1]
    hn, d_inner = h_sc.shape[1], h_sc.shape[2]
    hp = SSM_HEAD_DIM
    nheads = d_inner // hp
    heads_per_group = nheads // SSM_GROUPS
    group_width = heads_per_group * hp
    seqs = range(nseq)

    @pl.when(c == 0)
    def _():
        hist_sc[...] = jnp.zeros_like(hist_sc)
        h_sc[...] = jnp.zeros_like(h_sc)

    x = [x_ref[s] for s in seqs]
    xn = [_rms(x[s], gpre_ref[...]).astype(BF16) for s in seqs]
    xs, bm, cm = [], [], []
    for s in seqs:
        raw = _dot(xn[s], wxbc_ref[...])
        xp_sc[s, 0:SUBLANES, :] = hist_sc[s]
        xp_sc[s, SUBLANES:, :] = raw
        hist_sc[s] = raw[cl - SUBLANES:, :]
        conv = cb_ref[...]
        for k in range(SSM_CONV):
            conv = conv + xp_sc[s, pl.ds(SUBLANES - (SSM_CONV - 1) + k, cl), :] * cw_ref[k:k + 1, :]
        act = _silu(conv)
        xs.append(act[:, :d_inner])
        bm.append(act[:, d_inner:d_inner + SSM_GROUPS * SSM_STATE])
        cm.append(act[:, d_inner + SSM_GROUPS * SSM_STATE:])

    ti = lax.broadcasted_iota(jnp.int32, (cl, cl), 0)
    si = lax.broadcasted_iota(jnp.int32, (cl, cl), 1)
    tril = ti >= si
    tril_f = tril.astype(F32)
    dt, cum_col = [], []
    for s in seqs:
        dt.append(jax.nn.softplus(_dot(xn[s], wdt_ref[...]) + dtb_ref[...]))
        da = dt[s] * (-jnp.exp(alog_ref[...]))
        cum_col.append(jnp.dot(tril_f, da, preferred_element_type=F32,
                               precision=lax.Precision.HIGHEST))
    cum_row = [cum_col[s].T for s in seqs]
    dt_row = [dt[s].T for s in seqs]

    from_start, to_end, chunk_decay = [], [], []
    for s in seqs:
        last = cum_col[s][cl - 1:cl, :]
        per_head = jnp.concatenate([jnp.exp(cum_col[s]), jnp.exp(last - cum_col[s]) * dt[s],
                                    jnp.broadcast_to(jnp.exp(last), (SUBLANES, LANES))], axis=0)
        wide = _expand_heads(per_head, exp_ref)
        from_start.append(wide[:cl])
        to_end.append(wide[cl:2 * cl])
        chunk_decay.append(wide[2 * cl:2 * cl + 1])

    z = [_dot(xn[s], wz_ref[...]) for s in seqs]
    xs16 = [xs[s].astype(BF16) for s in seqs]
    cbs = []
    for s in seqs:
        xw16 = (xs[s] * to_end[s]).astype(BF16)
        bm_t = bm[s].T
        st = h_sc[s]
        st16 = st.astype(BF16)
        cb_s, state_terms, updates = [], [], []
        for g in range(SSM_GROUPS):
            sl = slice(g * SSM_STATE, (g + 1) * SSM_STATE)
            gl = slice(g * group_width, (g + 1) * group_width)
            cm_g = cm[s][:, sl].astype(BF16)
            cb_s.append(_dot_nt(cm_g, bm[s][:, sl].astype(BF16)))
            state_terms.append(_dot(cm_g, st16[:, gl]))
            updates.append(_dot(bm_t[sl, :].astype(BF16), xw16[:, gl]))
        cbs.append(cb_s)
        h_sc[s] = st * chunk_decay[s] + jnp.concatenate(updates, axis=1)
        y_sc[s] = jnp.concatenate(state_terms, axis=1) * from_start[s]

    for h in range(nheads):
        hl = slice(h * hp, (h + 1) * hp)
        for s in seqs:
            seg = jnp.exp(jnp.where(tril, cum_col[s][:, h:h + 1] - cum_row[s][h:h + 1, :], -jnp.inf))
            wts = cbs[s][h // heads_per_group] * seg * dt_row[s][h:h + 1, :]
            y_sc[s, :, hl] = y_sc[s, :, hl] + _dot(wts.astype(BF16), xs16[s][:, hl])

    for s in seqs:
        y = _ssd_gate_norm(y_sc[s], xs[s], z[s], dsk_ref, nw_ref).astype(BF16)
        xo_ref[s] = x[s] + _rms(_dot(y, wout_ref[...]) + other_ref[s], gpost_ref[...])

    @pl.when(c == nchunks - 1)
    def _():
        pad = jnp.zeros((LANES - hn, LANES), F32)
        for s in seqs:
            histout_ref[s] = hist_sc[s]
            final = h_sc[s]
            for j in range(d_inner // LANES):
                tile_t = jnp.concatenate([final[:, j * LANES:(j + 1) * LANES], pad], axis=0).T
                for r in range(LANES // hp):
                    hout_ref[s, j * (LANES // hp) + r] = tile_t[r * hp:(r + 1) * hp, :hn]


def ssd_prompt(x, g_pre, w_z, w_xbc, w_dt, other, w_out, g_post, conv_w, conv_b, dt_bias, a_log, d_skip, norm_w,
               batch, seq):
    t, d = x.shape
    d_inner = w_z.shape[1]
    conv_dim = w_xbc.shape[1]
    nheads = d_inner // SSM_HEAD_DIM
    cl = SSM_CHUNK
    nchunks = seq // cl
    ns = SSM_SEQS_PER_STEP
    rows = pl.BlockSpec((ns, cl, d), lambda b, c: (b, c, 0))
    xo, hout, hist = pl.pallas_call(
        _ssd_prompt_kernel,
        grid=(batch // ns, nchunks),
        in_specs=[rows, _resident((1, d)), _resident(w_z.shape), _resident(w_xbc.shape),
                  _resident(w_dt.shape), rows, _resident(w_out.shape), _resident((1, d)),
                  _resident(conv_w.shape), _resident((1, conv_dim)), _resident((1, LANES)),
                  _resident((1, LANES)), _resident((1, d_inner)), _resident((1, d_inner)),
                  _resident((LANES, d_inner))],
        out_specs=[rows,
                   pl.BlockSpec((ns, nheads, SSM_HEAD_DIM, SSM_STATE), lambda b, c: (b, 0, 0, 0)),
                   pl.BlockSpec((ns, SUBLANES, conv_dim), lambda b, c: (b, 0, 0))],
        out_shape=[jax.ShapeDtypeStruct((batch, seq, d), F32),
                   jax.ShapeDtypeStruct((batch, nheads, SSM_HEAD_DIM, SSM_STATE), F32),
                   jax.ShapeDtypeStruct((batch, SUBLANES, conv_dim), F32)],
        scratch_shapes=[pltpu.VMEM((ns, SUBLANES, conv_dim), F32),
                        pltpu.VMEM((ns, SSM_STATE, d_inner), F32),
                        pltpu.VMEM((ns, cl + SUBLANES, conv_dim), F32),
                        pltpu.VMEM((ns, cl, d_inner), F32)],
        compiler_params=_params(("arbitrary", "arbitrary")),
        name="ssd_prompt",
    )(x.reshape(batch, seq, d), g_pre, w_z, w_xbc, w_dt, other.reshape(batch, seq, d), w_out, g_post,
      conv_w, conv_b, dt_bias, a_log, d_skip, norm_w, jnp.asarray(_head_expansion(nheads, SSM_HEAD_DIM), BF16))
    return xo.reshape(t, d), hout, hist


def _ssd_step_kernel(z_ref, xnew_ref, h0_ref, h1_ref, h2_ref, dt_ref, cw_ref, cb_ref, dtb_ref, alog_ref,
                     dsk_ref, nw_ref, st_ref, y_ref, stout_ref,
                     xs_sc, xst_sc, bmt_sc, cmt_sc, dtt_sc, dat_sc, yt_sc):
    h = pl.program_id(0)
    nheads = pl.num_programs(0)
    d_inner = xs_sc.shape[1]
    hp, hn = SSM_HEAD_DIM, SSM_STATE
    heads_per_group = d_inner // hp // SSM_GROUPS

    @pl.when(h == 0)
    def _():
        conv = (cb_ref[...] + h0_ref[...] * cw_ref[0:1, :] + h1_ref[...] * cw_ref[1:2, :]
                + h2_ref[...] * cw_ref[2:3, :] + xnew_ref[...] * cw_ref[3:4, :])
        act = _silu(conv)
        xs = act[:, :d_inner]
        dt = jax.nn.softplus(dt_ref[...] + dtb_ref[...])
        xs_sc[...] = xs
        xst_sc[...] = xs.T
        bmt_sc[...] = act[:, d_inner:d_inner + SSM_GROUPS * hn].T
        cmt_sc[...] = act[:, d_inner + SSM_GROUPS * hn:].T
        dtt_sc[...] = dt.T
        dat_sc[...] = jnp.exp(dt * (-jnp.exp(alog_ref[...]))).T

    g = h // heads_per_group
    bmt = bmt_sc[pl.ds(pl.multiple_of(g * hn, hn), hn), :]
    cmt = cmt_sc[pl.ds(pl.multiple_of(g * hn, hn), hn), :]
    da = dat_sc[pl.ds(h, 1), :]
    dtx = xst_sc[pl.ds(pl.multiple_of(h * hp, hp), hp), :] * dtt_sc[pl.ds(h, 1), :]
    rows_per_tile = LANES // hn
    for j in range(hp // rows_per_tile):
        tile = st_ref[:, j * LANES:(j + 1) * LANES].T
        news = []
        for r in range(rows_per_tile):
            p = j * rows_per_tile + r
            new = da * tile[r * hn:(r + 1) * hn, :] + dtx[p:p + 1, :] * bmt
            yt_sc[pl.ds(h * hp + p, 1), :] = jnp.sum(new * cmt, axis=0, keepdims=True)
            news.append(new)
        stout_ref[:, j * LANES:(j + 1) * LANES] = jnp.concatenate(news, axis=0).T

    @pl.when(h == nheads - 1)
    def _():
        y_ref[...] = _ssd_gate_norm(yt_sc[...].T, xs_sc[...], z_ref[...], dsk_ref, nw_ref).astype(y_ref.dtype)


def ssd_step(z, xnew, hist, dt, conv_w, conv_b, dt_bias, a_log, d_skip, norm_w, state):
    b, d_inner = z.shape
    conv_dim = xnew.shape[1]
    nheads = d_inner // SSM_HEAD_DIM
    per_head = SSM_HEAD_DIM * SSM_STATE
    full = lambda shape: pl.BlockSpec(shape, lambda h: (0, 0))
    y, st = pl.pallas_call(
        _ssd_step_kernel,
        grid=(nheads,),
        in_specs=[full((b, d_inner))] + [full((b, conv_dim))] * 4 + [full((b, LANES))]
        + [full(conv_w.shape), full((1, conv_dim)), full((1, LANES)), full((1, LANES)),
           full((1, d_inner)), full((1, d_inner)),
           pl.BlockSpec((b, per_head), lambda h: (0, h))],
        out_specs=[full((b, d_inner)), pl.BlockSpec((b, per_head), lambda h: (0, h))],
        out_shape=[jax.ShapeDtypeStruct((b, d_inner), BF16),
                   jax.ShapeDtypeStruct((b, nheads * per_head), F32)],
        scratch_shapes=[pltpu.VMEM((b, d_inner), F32), pltpu.VMEM((d_inner, b), F32),
                        pltpu.VMEM((SSM_GROUPS * SSM_STATE, b), F32),
                        pltpu.VMEM((SSM_GROUPS * SSM_STATE, b), F32),
                        pltpu.VMEM((LANES, b), F32), pltpu.VMEM((LANES, b), F32),
                        pltpu.VMEM((d_inner, b), F32)],
        compiler_params=_params(("arbitrary",)),
        name="ssd_step",
    )(z, xnew, hist[0], hist[1], hist[2], dt, conv_w, conv_b, dt_bias, a_log, d_skip, norm_w,
      state.reshape(b, nheads * per_head))
    return y, st.reshape(b, nheads, SSM_HEAD_DIM, SSM_STATE)


def _hgrn_lower_bound(lbp_ref, layer):
    raw = lbp_ref[...]
    e = jnp.exp(raw - jnp.max(raw, axis=0, keepdims=True))
    return jnp.sum(e[:layer + 1], axis=0, keepdims=True) / jnp.sum(e, axis=0, keepdims=True)


def _hgrn_gates(q_raw, f_raw, lb):
    q = _silu(q_raw)
    f = lb + (1.0 - lb) * jax.nn.sigmoid(f_raw)
    k = (1.0 - lb) * jax.nn.sigmoid(-f_raw)
    return q, f, k


def _pair_levels(cl):
    t = np.arange(cl)[:, None]
    s = np.arange(cl)[None, :]
    lvl = np.floor(np.log2(np.maximum(t ^ s, 1))).astype(np.int32)
    return np.where(t > s, lvl, np.where(t == s, -1, -2)).astype(np.int32)


HG_HEADS_PER_DOT = 2
HG_GROUP = 8


def _hgrn_prompt_kernel(layer, x_ref, gpre_ref, win_ref, wout_ref, lvl_ref, lbp_ref, nw_ref,
                        m_ref, sout_ref, st_sc):
    c = pl.program_id(1)
    nchunks = pl.num_programs(1)
    cl = x_ref.shape[0]
    nheads = st_sc.shape[0]
    width = nheads * HG_DIM
    span = HG_HEADS_PER_DOT * HG_DIM

    @pl.when(c == 0)
    def _():
        st_sc[...] = jnp.zeros_like(st_sc)

    lb_all = _hgrn_lower_bound(lbp_ref, layer)
    row = lax.broadcasted_iota(jnp.int32, (cl, LANES), 0)
    lvl = lvl_ref[...]
    tril16 = (lvl >= -1).astype(BF16)
    xn = _rms(x_ref[...], gpre_ref[...]).astype(BF16)

    def project(j):
        return [_dot(xn, win_ref[:, which * width + j * span:which * width + (j + 1) * span])
                for which in range(4)]

    acc = None
    for g0 in range(0, nheads, HG_GROUP):
        heads = range(g0, g0 + HG_GROUP)
        raw = [{}, {}, {}, {}]
        for j in range(g0 // HG_HEADS_PER_DOT, (g0 + HG_GROUP) // HG_HEADS_PER_DOT):
            for which, both in enumerate(project(j)):
                for hh in range(HG_HEADS_PER_DOT):
                    raw[which][j * HG_HEADS_PER_DOT + hh] = both[:, hh * HG_DIM:(hh + 1) * HG_DIM]
        q, k, cum = {}, {}, {}
        for h in heads:
            q[h], fh, k[h] = _hgrn_gates(raw[0][h], raw[1][h], lb_all[:, h * HG_DIM:(h + 1) * HG_DIM])
            total = jnp.log(fh)
            shift = 1
            while shift < cl:
                total = total + jnp.where(row >= shift, pltpu.roll(total, shift, axis=0), 0.0)
                shift *= 2
            cum[h] = total

        att = {h: jnp.where(lvl == -1, _dot_nt(q[h].astype(BF16), k[h].astype(BF16)), 0.0) for h in heads}
        last_of_block = dict(cum)
        blk = 1
        level = 0
        while blk < cl:
            odd = (row & blk) != 0
            for h in heads:
                expo = jnp.where(odd, cum[h] - pltpu.roll(last_of_block[h], blk, axis=0),
                                 last_of_block[h] - cum[h])
                scaled = (jnp.where(odd, q[h], k[h]) * jnp.exp(expo)).astype(BF16)
                att[h] = jnp.where(lvl == level, _dot_nt(scaled, scaled), att[h])
                last_of_block[h] = jnp.where(odd, last_of_block[h],
                                             pltpu.roll(last_of_block[h], cl - blk, axis=0))
            blk *= 2
            level += 1
        cum_last = last_of_block

        outs = []
        for h in heads:
            st = st_sc[h]
            v = raw[2][h]
            o = (_dot(att[h].astype(BF16), v.astype(BF16))
                 + _dot_nt((q[h] * jnp.exp(cum[h])).astype(BF16), st.astype(BF16)))
            k_end = (k[h] * jnp.exp(cum_last[h] - cum[h])).astype(BF16)
            st_sc[h] = st * jnp.exp(cum_last[h]) + _dot(v.T.astype(BF16), k_end)
            outs.append((_rms(o, nw_ref[...]) * _silu(raw[3][h])).astype(BF16))
        part = _dot(jnp.concatenate(outs, axis=1), wout_ref[g0 * HG_DIM:(g0 + HG_GROUP) * HG_DIM, :])
        acc = part if acc is None else acc + part
    m_ref[...] = acc

    @pl.when(c == nchunks - 1)
    def _():
        for h in range(nheads):
            sout_ref[0, h] = st_sc[h].T


def hgrn_prompt(x, g_pre, w_qfig, w_out, lb_param, norm_w, layer, batch, seq):
    t, d = x.shape
    width = w_out.shape[0]
    nheads = width // HG_DIM
    cl = HG_CHUNK
    nchunks = seq // cl
    row = lambda b, c: (b * nchunks + c, 0)
    kern = functools.partial(_hgrn_prompt_kernel, layer)
    return pl.pallas_call(
        kern,
        grid=(batch, nchunks),
        in_specs=[pl.BlockSpec((cl, d), row), _resident((1, d)), _resident(w_qfig.shape), _resident(w_out.shape),
                  _resident((cl, cl)), _resident(lb_param.shape), _resident((1, HG_DIM))],
        out_specs=[pl.BlockSpec((cl, d), row),
                   pl.BlockSpec((1, nheads, HG_DIM, HG_DIM), lambda b, c: (b, 0, 0, 0))],
        out_shape=[jax.ShapeDtypeStruct((t, d), F32),
                   jax.ShapeDtypeStruct((batch, nheads, HG_DIM, HG_DIM), F32)],
        scratch_shapes=[pltpu.VMEM((nheads, HG_DIM, HG_DIM), F32)],
        compiler_params=_params(("arbitrary", "arbitrary")),
        name="hgrn_prompt",
    )(x, g_pre, w_qfig, w_out, jnp.asarray(_pair_levels(cl)), lb_param, norm_w)


HG_STEP_TOKENS = 8


def _hgrn_step_kernel(layer, qfig_ref, lbp_ref, nw_ref, st_ref, o_ref, stout_ref):
    nheads = st_ref.shape[2]
    raw = lbp_ref[...]
    e = jnp.exp(raw - jnp.max(raw, axis=0, keepdims=True))
    lb = jnp.sum(e[:layer + 1], axis=0) / jnp.sum(e, axis=0)
    pad = jnp.zeros((HG_DIM - 2 * nheads, HG_DIM), F32)
    for bl in range(st_ref.shape[1]):
        t = qfig_ref[bl]
        q, f, k = _hgrn_gates(t[0:nheads], t[nheads:2 * nheads], lb)
        v = t[2 * nheads:3 * nheads]
        cols = jnp.concatenate([f, k, pad], axis=0).T
        q16 = q.astype(BF16)
        rows = []
        for h in range(nheads):
            f_col = cols[:, h:h + 1]
            k_col = cols[:, nheads + h:nheads + h + 1]
            new = f_col * st_ref[0, bl, h] + k_col * v[h:h + 1, :]
            stout_ref[bl, h] = new
            rows.append(_dot(q16, new.astype(BF16))[h:h + 1, :])
        o = jnp.concatenate(rows, axis=0)
        o_ref[bl] = _rms(o, nw_ref[...]) * _silu(t[3 * nheads:])


def hgrn_step(qfig, lb_param, norm_w, layer, state_all):
    b, w4 = qfig.shape
    width = w4 // 4
    nheads = width // HG_DIM
    bb = HG_STEP_TOKENS
    kern = functools.partial(_hgrn_step_kernel, layer)
    o, st = pl.pallas_call(
        kern,
        grid=(b // bb,),
        in_specs=[pl.BlockSpec((bb, 4 * nheads, HG_DIM), lambda i: (i, 0, 0)),
                  _resident((lb_param.shape[0], nheads, HG_DIM)), _resident((1, HG_DIM)),
                  pl.BlockSpec((1, bb, nheads, HG_DIM, HG_DIM), lambda i: (layer, i, 0, 0, 0))],
        out_specs=[pl.BlockSpec((bb, nheads, HG_DIM), lambda i: (i, 0, 0)),
                   pl.BlockSpec((bb, nheads, HG_DIM, HG_DIM), lambda i: (i, 0, 0, 0))],
        out_shape=[jax.ShapeDtypeStruct((b, nheads, HG_DIM), F32),
                   jax.ShapeDtypeStruct((b, nheads, HG_DIM, HG_DIM), F32)],
        compiler_params=_params(("parallel",)),
        name="hgrn_step",
    )(qfig.reshape(b, 4 * nheads, HG_DIM), lb_param.reshape(-1, nheads, HG_DIM), norm_w, state_all)
    return o.reshape(b, width), st


def _s5_discretize_kernel(lre_ref, lim_ref, step_ref, bre_ref, bim_ref, are_ref, aim_ref, bbre_ref, bbim_ref):
    lre, lim = lre_ref[...], lim_ref[...]
    step = jnp.exp(step_ref[...])
    mag = jnp.exp(lre * step)
    are = mag * jnp.cos(lim * step)
    aim = mag * jnp.sin(lim * step)
    den = lre * lre + lim * lim
    nr = are - 1.0
    cre = (nr * lre + aim * lim) / den
    cim = (aim * lre - nr * lim) / den
    are_ref[...] = are
    aim_ref[...] = aim
    bre, bim = bre_ref[...], bim_ref[...]
    bbre_ref[...] = cre * bre - cim * bim
    bbim_ref[...] = cre * bim + cim * bre


def s5_discretize(lam_re, lam_im, log_step, b_re_t, b_im_t):
    g, n = lam_re.shape
    c = b_re_t.shape[1]
    return pl.pallas_call(
        _s5_discretize_kernel,
        out_shape=[jax.ShapeDtypeStruct((g, 1, n), F32)] * 2 + [jax.ShapeDtypeStruct((g, c, n), F32)] * 2,
        name="s5_discretize",
    )(lam_re.reshape(g, 1, n), lam_im.reshape(g, 1, n), log_step.reshape(g, 1, 1), b_re_t, b_im_t)


def _s5_output(xre, xim, u, cre_ref, cim_ref, d_ref):
    y = _dot(xre.astype(BF16), cre_ref[0]) - _dot(xim.astype(BF16), cim_ref[0]) + d_ref[...] * u
    return jax.nn.gelu(y)


S5_ROWS = 256
S5_CHUNK = 256


def _s5_prompt_kernel(u_ref, b_ref, cre_ref, cim_ref, d_ref, are_ref, aim_ref,
                      y_ref, sre_ref, sim_ref, x_sc, ui_sc, st_sc):
    c = pl.program_id(2)
    nseq, tc = u_ref.shape[0], u_ref.shape[1]
    ns = S5_BLOCK_STATE
    nt = ns // LANES
    steps = S5_ROWS // nseq
    nslabs = tc // steps

    @pl.when(c == 0)
    def _():
        st_sc[...] = jnp.zeros_like(st_sc)

    def slab_rows(s):
        return slice(s * S5_ROWS, (s + 1) * S5_ROWS)

    def project(s):
        r = slab_rows(s)
        ui = jnp.concatenate([u_ref[:, i, :] for i in range(s * steps, (s + 1) * steps)], axis=0)
        ui_sc[r, :] = ui
        bu = _dot(ui.astype(BF16), b_ref[0])
        for k in range(2 * nt):
            x_sc[k, r, :] = bu[:, k * LANES:(k + 1) * LANES]

    def lane_tile(ref, k):
        return ref[:, k * LANES:(k + 1) * LANES]

    are = [jnp.broadcast_to(lane_tile(are_ref, k), (nseq, LANES)) for k in range(nt)]
    aim = [jnp.broadcast_to(lane_tile(aim_ref, k), (nseq, LANES)) for k in range(nt)]

    def scan_slab(s, carry):
        for i in range(s * steps, (s + 1) * steps):
            rows = slice(i * nseq, (i + 1) * nseq)
            new = []
            for k in range(nt):
                sre, sim = carry[k]
                nre = are[k] * sre - aim[k] * sim + x_sc[k, rows, :]
                nim = are[k] * sim + aim[k] * sre + x_sc[nt + k, rows, :]
                x_sc[k, rows, :] = nre
                x_sc[nt + k, rows, :] = nim
                new.append((nre, nim))
            carry = tuple(new)
        return carry

    def output(s):
        r = slab_rows(s)
        xre = jnp.concatenate([x_sc[k, r, :] for k in range(nt)], axis=1)
        xim = jnp.concatenate([x_sc[nt + k, r, :] for k in range(nt)], axis=1)
        y = _s5_output(xre, xim, ui_sc[r, :], cre_ref, cim_ref, d_ref)
        for ii in range(steps):
            y_ref[:, s * steps + ii, :] = y[ii * nseq:(ii + 1) * nseq, :]

    state = tuple((st_sc[k], st_sc[nt + k]) for k in range(nt))
    project(0)
    for s in range(nslabs):
        if s + 1 < nslabs:
            project(s + 1)
        if s > 0:
            output(s - 1)
        state = scan_slab(s, state)
    output(nslabs - 1)
    for k in range(nt):
        st_sc[k], st_sc[nt + k] = state[k]
        sre_ref[0, :, k * LANES:(k + 1) * LANES] = state[k][0]
        sim_ref[0, :, k * LANES:(k + 1) * LANES] = state[k][1]


def s5_prompt(u, b_exp, c_re_exp, c_im_exp, d_skip, a_re, a_im, batch, seq):
    t, d = u.shape
    nblk = d // LANES
    ns = S5_BLOCK_STATE
    tc = S5_CHUNK
    groups = batch // SUBLANES
    rows = pl.BlockSpec((SUBLANES, tc, LANES), lambda j, g, c: (g, c, j))
    state = pl.BlockSpec((1, SUBLANES, ns), lambda j, g, c: (g, 0, j))
    y, s_re, s_im = pl.pallas_call(
        _s5_prompt_kernel,
        grid=(nblk, groups, seq // tc),
        in_specs=[rows,
                  pl.BlockSpec((1, LANES, 2 * ns), lambda j, g, c: (j, 0, 0)),
                  pl.BlockSpec((1, ns, LANES), lambda j, g, c: (j, 0, 0)),
                  pl.BlockSpec((1, ns, LANES), lambda j, g, c: (j, 0, 0)),
                  pl.BlockSpec((1, LANES), lambda j, g, c: (0, j)),
                  pl.BlockSpec((1, ns), lambda j, g, c: (0, j)),
                  pl.BlockSpec((1, ns), lambda j, g, c: (0, j))],
        out_specs=[rows, state, state],
        out_shape=[jax.ShapeDtypeStruct((batch, seq, d), F32),
                   jax.ShapeDtypeStruct((groups, SUBLANES, nblk * ns), F32),
                   jax.ShapeDtypeStruct((groups, SUBLANES, nblk * ns), F32)],
        scratch_shapes=[pltpu.VMEM((2 * ns // LANES, SUBLANES * tc, LANES), F32),
                        pltpu.VMEM((SUBLANES * tc, LANES), F32),
                        pltpu.VMEM((2 * ns // LANES, SUBLANES, LANES), F32)],
        compiler_params=_params(("arbitrary", "arbitrary", "arbitrary")),
        name="s5_prompt",
    )(u.reshape(batch, seq, d), b_exp, c_re_exp, c_im_exp, d_skip, a_re, a_im)
    return y.reshape(t, d), s_re, s_im


def _s5_step_kernel(u_ref, b_ref, cre_ref, cim_ref, d_ref, are_ref, aim_ref, s0re_ref, s0im_ref,
                    y_ref, sre_ref, sim_ref):
    ns = S5_BLOCK_STATE
    u = u_ref[...]
    bu = _dot(u.astype(BF16), b_ref[0])
    are, aim = are_ref[...], aim_ref[...]
    s0re, s0im = s0re_ref[...], s0im_ref[...]
    xre = bu[:, 0:ns] + (are * s0re - aim * s0im)
    xim = bu[:, ns:2 * ns] + (are * s0im + aim * s0re)
    sre_ref[...] = xre
    sim_ref[...] = xim
    y_ref[...] = _s5_output(xre, xim, u, cre_ref, cim_ref, d_ref)


def s5_step(u, b_exp, c_re_exp, c_im_exp, d_skip, a_re, a_im, s0_re, s0_im):
    b, d = u.shape
    nblk = d // LANES
    ns = S5_BLOCK_STATE
    return pl.pallas_call(
        _s5_step_kernel,
        grid=(nblk,),
        in_specs=[pl.BlockSpec((b, LANES), lambda j: (0, j)),
                  pl.BlockSpec((1, LANES, 2 * ns), lambda j: (j, 0, 0)),
                  pl.BlockSpec((1, ns, LANES), lambda j: (j, 0, 0)),
                  pl.BlockSpec((1, ns, LANES), lambda j: (j, 0, 0)),
                  pl.BlockSpec((1, LANES), lambda j: (0, j)),
                  pl.BlockSpec((1, ns), lambda j: (0, j)),
                  pl.BlockSpec((1, ns), lambda j: (0, j)),
                  pl.BlockSpec((b, ns), lambda j: (0, j)),
                  pl.BlockSpec((b, ns), lambda j: (0, j))],
        out_specs=[pl.BlockSpec((b, LANES), lambda j: (0, j)),
                   pl.BlockSpec((b, ns), lambda j: (0, j)),
                   pl.BlockSpec((b, ns), lambda j: (0, j))],
        out_shape=[jax.ShapeDtypeStruct((b, d), F32),
                   jax.ShapeDtypeStruct((b, nblk * ns), F32),
                   jax.ShapeDtypeStruct((b, nblk * ns), F32)],
        compiler_params=_params(("parallel",)),
        name="s5_step",
    )(u, b_exp, c_re_exp, c_im_exp, d_skip, a_re, a_im, s0_re, s0_im)


def _block_diag(per_group):
    ngroups, r, c = per_group.shape
    p = S5_GROUPS_PER_BLOCK
    eye = jnp.eye(p, dtype=per_group.dtype)
    tiles = per_group.reshape(ngroups // p, p, r, c)
    return jnp.einsum('bjrc,jk->bjrkc', tiles, eye).reshape(ngroups // p, p * r, p * c)


def kernel(x_prompt, x_sample, state_ssm, state_ssm_conv, state_hgrn, state_s5_re, state_s5_im, state_ffn_conv, norm_mix_pre, norm_mix_post, norm_ffn_pre, norm_ffn_post, ab_in_w, ssm_conv_w, ssm_conv_b, ssm_dt_bias, ssm_a_log, ssm_d, ssm_norm_w, hgrn_lb, hgrn_norm_w, ab_out_w, s5_lam_re, s5_lam_im, s5_log_step, s5_b_re, s5_b_im, s5_c_re, s5_c_im, s5_d, s5_glu_w, ffn_up_w, ffn_conv_w, ffn_conv_b, ffn_down_w):
    batch, seq, d = x_prompt.shape
    dec_batch = x_sample.shape[0]
    depth = norm_mix_pre.shape[0]
    conv_dim = ssm_conv_w.shape[2]
    nheads = ssm_a_log.shape[1]
    d_inner = nheads * SSM_HEAD_DIM
    hg_width = hgrn_lb.shape[1]
    ffn_dim = ffn_down_w.shape[1]
    tm = 256

    def row(v):
        return v.reshape(1, -1)

    def pad_lanes(v):
        return jnp.pad(v, ((0, 0), (0, LANES - v.shape[1])))

    groups = [(x_prompt.reshape(batch * seq, d), True), (x_sample.reshape(dec_batch, d), False)]
    xs = [g[0] for g in groups]
    normed = [None, None]
    up_w_all = ffn_up_w.astype(BF16)
    down_w_all = ffn_down_w.astype(BF16)
    out = {'ssm': [[], []], 'sconv': [[], []], 'hg': [[], []], 's5r': [[], []], 's5i': [[], []], 'fconv': [[], []]}

    for l in range(depth):
        e = l // 2
        if l % 2 == 0:
            w_in = ab_in_w[e]
            o_xbc = d_inner
            o_dt = o_xbc + conv_dim
            o_q = o_dt + nheads
            w_z = w_in[:, :o_xbc].astype(BF16)
            w_xbc = w_in[:, o_xbc:o_dt].astype(BF16)
            w_dt = pad_lanes(w_in[:, o_dt:o_q]).astype(BF16)
            w_qfig = w_in[:, o_q:].astype(BF16)
            w_out = ab_out_w[e].astype(BF16)
            dt_bias = pad_lanes(row(ssm_dt_bias[e]))
            a_log = pad_lanes(row(ssm_a_log[e]))
            d_skip = row(jnp.repeat(ssm_d[e], SSM_HEAD_DIM))
            for gi, (_, is_prompt) in enumerate(groups):
                x = xs[gi]
                t = x.shape[0]
                if is_prompt:
                    m_hg, hg_new = hgrn_prompt(x, row(norm_mix_pre[l]), w_qfig, w_out[d_inner:], hgrn_lb,
                                               row(hgrn_norm_w[e]), e, batch, seq)
                    xs[gi], ssm_new, xbc_tail = ssd_prompt(
                        x, row(norm_mix_pre[l]), w_z, w_xbc, w_dt, m_hg, w_out[:d_inner], row(norm_mix_post[l]),
                        ssm_conv_w[e], row(ssm_conv_b[e]), dt_bias, a_log, d_skip, row(ssm_norm_w[e]), batch, seq)
                    sconv_new = xbc_tail[:, SUBLANES - (SSM_CONV - 1):]
                else:
                    z, xbc, dt, qfig = norm_matmul(x, row(norm_mix_pre[l]), [w_z, w_xbc, w_dt, w_qfig], t)
                    hist = [state_ssm_conv[e][:, k] for k in range(SSM_CONV - 1)]
                    y, ssm_new = ssd_step(z, xbc, hist, dt, ssm_conv_w[e], row(ssm_conv_b[e]), dt_bias, a_log,
                                          d_skip, row(ssm_norm_w[e]), state_ssm[e])
                    o, hg_new = hgrn_step(qfig, hgrn_lb, row(hgrn_norm_w[e]), e, state_hgrn)
                    sconv_new = jnp.stack(hist[1:] + [xbc], axis=1)
                    xs[gi] = matmul_norm_residual([y, o], [w_out[:d_inner], w_out[d_inner:]],
                                                  row(norm_mix_post[l]), x, t)
                out['ssm'][gi].append(ssm_new)
                out['hg'][gi].append(hg_new)
                out['sconv'][gi].append(sconv_new)
        else:
            ngroups = s5_lam_re.shape[1]
            a_re, a_im, bb_re, bb_im = s5_discretize(
                s5_lam_re[e], s5_lam_im[e], s5_log_step[e],
                jnp.swapaxes(s5_b_re[e], 1, 2), jnp.swapaxes(s5_b_im[e], 1, 2))
            a_re = a_re.reshape(1, ngroups * S5_STATE)
            a_im = a_im.reshape(1, ngroups * S5_STATE)
            b_exp = jnp.concatenate([_block_diag(bb_re), _block_diag(bb_im)], axis=2).astype(BF16)
            c_re_exp = _block_diag(jnp.swapaxes(s5_c_re[e], 1, 2)).astype(BF16)
            c_im_exp = _block_diag(jnp.swapaxes(s5_c_im[e], 1, 2)).astype(BF16)
            glu_w = s5_glu_w[e].astype(BF16)
            for gi, (_, is_prompt) in enumerate(groups):
                x = xs[gi]
                t = x.shape[0]
                u = normed[gi] if normed[gi] is not None else rmsnorm(x, row(norm_mix_pre[l]), min(tm, t))
                if is_prompt:
                    yg, s_re, s_im = s5_prompt(u, b_exp, c_re_exp, c_im_exp, row(s5_d[e]), a_re, a_im, batch, seq)
                else:
                    yg, s_re, s_im = s5_step(u, b_exp, c_re_exp, c_im_exp, row(s5_d[e]), a_re, a_im,
                                             state_s5_re[e].reshape(t, -1), state_s5_im[e].reshape(t, -1))
                out['s5r'][gi].append(s_re.reshape(-1, ngroups, S5_STATE))
                out['s5i'][gi].append(s_im.reshape(-1, ngroups, S5_STATE))
                xs[gi] = matmul_norm_residual([yg], [glu_w], row(norm_mix_post[l]), x, min(tm, t), glu=True)

        for gi, (_, is_prompt) in enumerate(groups):
            x = xs[gi]
            t = x.shape[0]
            normed[gi] = None
            if is_prompt:
                g_next = row(norm_mix_pre[l + 1]) if l + 1 < depth and (l + 1) % 2 == 1 else None
                res = ffn_seq(x, row(norm_ffn_pre[l]), up_w_all, ffn_conv_w[l], row(ffn_conv_b[l]), down_w_all,
                              row(norm_ffn_post[l]), g_next, tm, seq, l)
                xs[gi], tail = res[0], res[1]
                if g_next is not None:
                    normed[gi] = res[2]
                fconv_new = tail[:, SUBLANES - (FFN_CONV - 1):]
            else:
                (up,) = norm_matmul(x, row(norm_ffn_pre[l]), [up_w_all], t, layer=l)
                old = state_ffn_conv[l]
                xs[gi] = ffn_tail_step(up, old[:, 0], old[:, 1], ffn_conv_w[l], row(ffn_conv_b[l]), down_w_all,
                                       row(norm_ffn_post[l]), x, l)
                fconv_new = jnp.stack([old[:, 1], up], axis=1)
            out['fconv'][gi].append(fconv_new)

    y_prompt = xs[0].reshape(batch, seq, d)
    y_sample = xs[1].reshape(dec_batch, 1, d)
    states = []
    for gi in range(2):
        states += [jnp.stack(out[k][gi]) for k in ('ssm', 'sconv', 'hg', 's5r', 's5i', 'fconv')]
    return (y_prompt, y_sample, *states)
```

```python
import functools
import math

import jax
import jax.numpy as jnp
import numpy as np
from jax import lax
from jax.experimental import pallas as pl
from jax.experimental.pallas import tpu as pltpu

F32 = jnp.float32
BF16 = jnp.bfloat16
EPS = 1e-6

LANES = 128
SUBLANES = 8
VMEM_LIMIT_BYTES = 56 * 1024 * 1024

SSM_HEAD_DIM = 64
SSM_GROUPS = 2
SSM_STATE = 64
SSM_CONV = 4
SSM_CHUNK = 128
HG_DIM = 128
HG_CHUNK = 128
S5_GROUP = 16
S5_STATE = 64
S5_GROUPS_PER_BLOCK = LANES // S5_GROUP
S5_BLOCK_STATE = S5_GROUPS_PER_BLOCK * S5_STATE
FFN_CONV = 3


def _params(sem):
    return pltpu.CompilerParams(dimension_semantics=sem, vmem_limit_bytes=VMEM_LIMIT_BYTES)


def _resident(shape, layer=None):
    if layer is None:
        nd = len(shape)
        return pl.BlockSpec(shape, lambda *_: (0,) * nd, pipeline_mode=pl.Buffered(1))
    rest = tuple(shape[1:])
    return pl.BlockSpec((None,) + rest, lambda *_: (layer,) + (0,) * len(rest), pipeline_mode=pl.Buffered(1))


def _resident_rows(w, first, nrows):
    assert first % nrows == 0
    return pl.BlockSpec((nrows, w.shape[1]), lambda *_: (first // nrows, 0), pipeline_mode=pl.Buffered(1))


def _rms(x, w):
    return x * lax.rsqrt(jnp.mean(x * x, axis=-1, keepdims=True) + EPS) * w


def _silu(x):
    return x * jax.nn.sigmoid(x)


def _dot(a, b):
    return jnp.dot(a, b, preferred_element_type=F32)


def _dot_nt(a, b):
    return lax.dot_general(a, b, (((1,), (1,)), ((), ())), preferred_element_type=F32)


def _dot_tn(a, b):
    return lax.dot_general(a, b, (((0,), (0,)), ((), ())), preferred_element_type=F32)


def _norm_matmul_kernel(n_out, x_ref, g_ref, *refs):
    w_refs, o_refs = refs[:n_out], refs[n_out:]
    xn = _rms(x_ref[...], g_ref[...]).astype(BF16)
    for w_ref, o_ref in zip(w_refs, o_refs):
        n = w_ref.shape[1]
        for c0 in range(0, n, 512):
            c1 = min(n, c0 + 512)
            o_ref[:, c0:c1] = _dot(xn, w_ref[:, c0:c1])


def norm_matmul(x, g, ws, tm, layer=None):
    t, d = x.shape
    kern = functools.partial(_norm_matmul_kernel, len(ws))
    return pl.pallas_call(
        kern,
        grid=(t // tm,),
        in_specs=[pl.BlockSpec((tm, d), lambda i: (i, 0)), _resident((1, d))]
        + [_resident(w.shape, layer) for w in ws],
        out_specs=[pl.BlockSpec((tm, w.shape[-1]), lambda i: (i, 0)) for w in ws],
        out_shape=[jax.ShapeDtypeStruct((t, w.shape[-1]), F32) for w in ws],
        compiler_params=_params(("parallel",)),
        name="norm_matmul",
    )(x, g, *ws)


def _rmsnorm_kernel(x_ref, g_ref, o_ref):
    o_ref[...] = _rms(x_ref[...], g_ref[...])


def rmsnorm(x, g, tm):
    t, d = x.shape
    return pl.pallas_call(
        _rmsnorm_kernel,
        grid=(t // tm,),
        in_specs=[pl.BlockSpec((tm, d), lambda i: (i, 0)), _resident((1, d))],
        out_specs=pl.BlockSpec((tm, d), lambda i: (i, 0)),
        out_shape=jax.ShapeDtypeStruct((t, d), F32),
        compiler_params=_params(("parallel",)),
        name="rmsnorm",
    )(x, g)


def _matmul_norm_residual_kernel(n_in, glu, *refs):
    a_refs, w_refs = refs[:n_in], refs[n_in:2 * n_in]
    g_ref, x_ref, o_ref = refs[2 * n_in:]
    acc = None
    for a_ref, w_ref in zip(a_refs, w_refs):
        part = _dot(a_ref[...].astype(BF16), w_ref[...])
        acc = part if acc is None else acc + part
    if glu:
        d = acc.shape[1] // 2
        acc = acc[:, :d] * jax.nn.sigmoid(acc[:, d:])
    o_ref[...] = x_ref[...] + _rms(acc, g_ref[...])


def matmul_norm_residual(a_list, w, g, x, tm, glu=False):
    t, d = x.shape
    kern = functools.partial(_matmul_norm_residual_kernel, len(a_list), glu)
    firsts = [sum(a.shape[1] for a in a_list[:i]) for i in range(len(a_list))]
    return pl.pallas_call(
        kern,
        grid=(t // tm,),
        in_specs=[pl.BlockSpec((tm, a.shape[1]), lambda i: (i, 0)) for a in a_list]
        + [_resident_rows(w, first, a.shape[1]) for a, first in zip(a_list, firsts)]
        + [_resident((1, d)), pl.BlockSpec((tm, d), lambda i: (i, 0))],
        out_specs=pl.BlockSpec((tm, d), lambda i: (i, 0)),
        out_shape=jax.ShapeDtypeStruct((t, d), F32),
        compiler_params=_params(("parallel",)),
        name="matmul_norm_residual",
    )(*a_list, *([w] * len(a_list)), g, x)


FFN_CHUNK = 256


def _ffn_tail(up_ref, xm2_of, xm1_of, cw_ref, cb_ref, dw_ref, g_ref, x_ref, o_ref):
    f = dw_ref.shape[0]
    acc = jnp.zeros(o_ref.shape, F32)
    for c0 in range(0, f, FFN_CHUNK):
        c1 = c0 + FFN_CHUNK
        halves = []
        for off in (0, f):
            a, b = c0 + off, c1 + off
            halves.append(cb_ref[:, a:b] + xm2_of(a, b) * cw_ref[0:1, a:b]
                          + xm1_of(a, b) * cw_ref[1:2, a:b] + up_ref[:, a:b] * cw_ref[2:3, a:b])
        act = (halves[0] * _silu(halves[1])).astype(BF16)
        acc = acc + _dot(act, dw_ref[c0:c1, :])
    o_ref[...] = x_ref[...] + _rms(acc, g_ref[...])


def _ffn_seq_kernel(tiles_per_seq, emit_next, x_ref, g1_ref, uw_ref, cw_ref, cb_ref, dw_ref, g2_ref, *rest):
    if emit_next:
        g3_ref, o_ref, tail_ref, n_ref, carry_sc = rest
    else:
        o_ref, tail_ref, carry_sc = rest
    tm = x_ref.shape[0]
    f = dw_ref.shape[0]

    @pl.when(pl.program_id(0) % tiles_per_seq == 0)
    def _():
        carry_sc[...] = jnp.zeros_like(carry_sc)

    x = x_ref[...]
    xn = _rms(x, g1_ref[...]).astype(BF16)
    top_row = lax.broadcasted_iota(jnp.int32, (SUBLANES, FFN_CHUNK), 0)

    def earlier(up, prev, k):
        rolled = pltpu.roll(up, k, axis=0)
        top = jnp.where(top_row < k, pltpu.roll(prev, k, axis=0), rolled[:SUBLANES])
        return jnp.concatenate([top, rolled[SUBLANES:]], axis=0)

    def project(c0):
        return [_dot(xn, uw_ref[:, c0 + off:c0 + off + FFN_CHUNK]) for off in (0, f)]

    acc = jnp.zeros(o_ref.shape, F32)
    ups_next = project(0)
    act_prev = None
    for c0 in range(0, f, FFN_CHUNK):
        ups = ups_next
        if c0 + FFN_CHUNK < f:
            ups_next = project(c0 + FFN_CHUNK)
        if act_prev is not None:
            acc = acc + _dot(act_prev, dw_ref[c0 - FFN_CHUNK:c0, :])
        halves = []
        for up, off in zip(ups, (0, f)):
            a, b = c0 + off, c0 + off + FFN_CHUNK
            prev = carry_sc[:, a:b]
            carry_sc[:, a:b] = up[tm - SUBLANES:, :]
            tail_ref[0, :, a:b] = up[tm - SUBLANES:, :]
            halves.append(cb_ref[:, a:b] + earlier(up, prev, 2) * cw_ref[0:1, a:b]
                          + earlier(up, prev, 1) * cw_ref[1:2, a:b] + up * cw_ref[2:3, a:b])
        act_prev = (halves[0] * _silu(halves[1])).astype(BF16)
    acc = acc + _dot(act_prev, dw_ref[f - FFN_CHUNK:f, :])
    out = x + _rms(acc, g2_ref[...])
    o_ref[...] = out
    if emit_next:
        n_ref[...] = _rms(out, g3_ref[...])


def ffn_seq(x, g_pre, up_w, conv_w, conv_b, down_w, g_post, g_next, tm, seq, layer):
    t, d = x.shape
    f2 = up_w.shape[-1]
    tiles_per_seq = seq // tm
    emit_next = g_next is not None
    kern = functools.partial(_ffn_seq_kernel, tiles_per_seq, emit_next)
    tile = pl.BlockSpec((tm, d), lambda i: (i, 0))
    return pl.pallas_call(
        kern,
        grid=(t // tm,),
        in_specs=[tile, _resident((1, d)), _resident(up_w.shape, layer), _resident(conv_w.shape),
                  _resident((1, f2)), _resident(down_w.shape, layer), _resident((1, d))]
        + [_resident((1, d))] * emit_next,
        out_specs=[tile, pl.BlockSpec((1, SUBLANES, f2), lambda i: (i // tiles_per_seq, 0, 0))]
        + [tile] * emit_next,
        out_shape=[jax.ShapeDtypeStruct((t, d), F32), jax.ShapeDtypeStruct((t // seq, SUBLANES, f2), F32)]
        + [jax.ShapeDtypeStruct((t, d), F32)] * emit_next,
        scratch_shapes=[pltpu.VMEM((SUBLANES, f2), F32)],
        compiler_params=_params(("arbitrary",)),
        name="ffn_seq",
    )(x, g_pre, up_w, conv_w, conv_b, down_w, g_post, *([g_next] * emit_next))


def _ffn_step_kernel(up_ref, xm2_ref, xm1_ref, cw_ref, cb_ref, dw_ref, g_ref, x_ref, o_ref):
    _ffn_tail(up_ref, lambda a, b: xm2_ref[:, a:b], lambda a, b: xm1_ref[:, a:b],
              cw_ref, cb_ref, dw_ref, g_ref, x_ref, o_ref)


def ffn_tail_step(up, xm2, xm1, conv_w, conv_b, down_w, g, x, layer):
    t, d = x.shape
    f2 = up.shape[1]
    return pl.pallas_call(
        _ffn_step_kernel,
        grid=(1,),
        in_specs=[_resident((t, f2))] * 3
        + [_resident(conv_w.shape), _resident((1, f2)), _resident(down_w.shape, layer),
           _resident((1, d)), _resident((t, d))],
        out_specs=pl.BlockSpec((t, d), lambda i: (0, 0)),
        out_shape=jax.ShapeDtypeStruct((t, d), F32),
        compiler_params=_params(("arbitrary",)),
        name="ffn_tail_step",
    )(up, xm2, xm1, conv_w, conv_b, down_w, g, x)


def _ssd_gate_norm(y, xs, z, dsk_ref, nw_ref):
    y = y + dsk_ref[...] * xs
    return _rms(y * _silu(z), nw_ref[...])


SSM_SEQS_PER_STEP = 4


def _head_expansion(nheads, head_dim):
    e = np.zeros((LANES, nheads * head_dim), np.float32)
    for h in range(nheads):
        e[h, h * head_dim:(h + 1) * head_dim] = 1.0
    return e


def _expand_heads(v, exp_ref):
    hi = v.astype(BF16)
    lo = (v - hi.astype(F32)).astype(BF16)
    return _dot(hi, exp_ref[...]) + _dot(lo, exp_ref[...])


def _ssd_prompt_kernel(x_ref, gpre_ref, wz_ref, wxbc_ref, wdt_ref, other_ref, wout_ref, gpost_ref,
                       cw_ref, cb_ref, dtb_ref, alog_ref, dsk_ref, nw_ref, exp_ref,
                       xo_ref, hout_ref, histout_ref, hist_sc, h_sc, xp_sc, y_sc):
    c = pl.program_id(1)
    nchunks = pl.num_programs(1)
    nseq, cl = x_ref.shape[0], x_ref.shape[1]
    hn, d_inner = h_sc.shape[1], h_sc.shape[2]
    hp = SSM_HEAD_DIM
    nheads = d_inner // hp
    heads_per_group = nheads // SSM_GROUPS
    group_width = heads_per_group * hp
    seqs = range(nseq)

    @pl.when(c == 0)
    def _():
        hist_sc[...] = jnp.zeros_like(hist_sc)
        h_sc[...] = jnp.zeros_like(h_sc)

    x = [x_ref[s] for s in seqs]
    xn = [_rms(x[s], gpre_ref[...]).astype(BF16) for s in seqs]
    xs, bm, cm = [], [], []
    for s in seqs:
        raw = _dot(xn[s], wxbc_ref[...])
        xp_sc[s, 0:SUBLANES, :] = hist_sc[s]
        xp_sc[s, SUBLANES:, :] = raw
        hist_sc[s] = raw[cl - SUBLANES:, :]
        conv = cb_ref[...]
        for k in range(SSM_CONV):
            conv = conv + xp_sc[s, pl.ds(SUBLANES - (SSM_CONV - 1) + k, cl), :] * cw_ref[k:k + 1, :]
        act = _silu(conv)
        xs.append(act[:, :d_inner])
        bm.append(act[:, d_inner:d_inner + SSM_GROUPS * SSM_STATE])
        cm.append(act[:, d_inner + SSM_GROUPS * SSM_STATE:])

    ti = lax.broadcasted_iota(jnp.int32, (cl, cl), 0)
    si = lax.broadcasted_iota(jnp.int32, (cl, cl), 1)
    tril = ti >= si
    tril_f = tril.astype(F32)
    dt, cum_col = [], []
    for s in seqs:
        dt.append(jax.nn.softplus(_dot(xn[s], wdt_ref[...]) + dtb_ref[...]))
        da = dt[s] * (-jnp.exp(alog_ref[...]))
        cum_col.append(jnp.dot(tril_f, da, preferred_element_type=F32,
                               precision=lax.Precision.HIGHEST))
    cum_row = [cum_col[s].T for s in seqs]
    dt_row = [dt[s].T for s in seqs]

    from_start, to_end, chunk_decay = [], [], []
    for s in seqs:
        last = cum_col[s][cl - 1:cl, :]
        per_head = jnp.concatenate([jnp.exp(cum_col[s]), jnp.exp(last - cum_col[s]) * dt[s],
                                    jnp.broadcast_to(jnp.exp(last), (SUBLANES, LANES))], axis=0)
        wide = _expand_heads(per_head, exp_ref)
        from_start.append(wide[:cl])
        to_end.append(wide[cl:2 * cl])
        chunk_decay.append(wide[2 * cl:2 * cl + 1])

    z = [_dot(xn[s], wz_ref[...]) for s in seqs]
    xs16 = [xs[s].astype(BF16) for s in seqs]
    cbs = []
    for s in seqs:
        xw16 = (xs[s] * to_end[s]).astype(BF16)
        bm_t = bm[s].T
        st = h_sc[s]
        st16 = st.astype(BF16)
        cb_s, state_terms, updates = [], [], []
        for g in range(SSM_GROUPS):
            sl = slice(g * SSM_STATE, (g + 1) * SSM_STATE)
            gl = slice(g * group_width, (g + 1) * group_width)
            cm_g = cm[s][:, sl].astype(BF16)
            cb_s.append(_dot_nt(cm_g, bm[s][:, sl].astype(BF16)))
            state_terms.append(_dot(cm_g, st16[:, gl]))
            updates.append(_dot(bm_t[sl, :].astype(BF16), xw16[:, gl]))
        cbs.append(cb_s)
        h_sc[s] = st * chunk_decay[s] + jnp.concatenate(updates, axis=1)
        y_sc[s] = jnp.concatenate(state_terms, axis=1) * from_start[s]

    for h in range(nheads):
        hl = slice(h * hp, (h + 1) * hp)
        for s in seqs:
            seg = jnp.exp(jnp.where(tril, cum_col[s][:, h:h + 1] - cum_row[s][h:h + 1, :], -jnp.inf))
            wts = cbs[s][h // heads_per_group] * seg * dt_row[s][h:h + 1, :]
            y_sc[s, :, hl] = y_sc[s, :, hl] + _dot(wts.astype(BF16), xs16[s][:, hl])

    for s in seqs:
        y = _ssd_gate_norm(y_sc[s], xs[s], z[s], dsk_ref, nw_ref).astype(BF16)
        xo_ref[s] = x[s] + _rms(_dot(y, wout_ref[...]) + other_ref[s], gpost_ref[...])

    @pl.when(c == nchunks - 1)
    def _():
        pad = jnp.zeros((LANES - hn, LANES), F32)
        for s in seqs:
            histout_ref[s] = hist_sc[s]
            final = h_sc[s]
            for j in range(d_inner // LANES):
                tile_t = jnp.concatenate([final[:, j * LANES:(j + 1) * LANES], pad], axis=0).T
                for r in range(LANES // hp):
                    hout_ref[s, j * (LANES // hp) + r] = tile_t[r * hp:(r + 1) * hp, :hn]


def ssd_prompt(x, g_pre, w_z, w_xbc, w_dt, other, w_out, g_post, conv_w, conv_b, dt_bias, a_log, d_skip, norm_w,
               batch, seq):
    t, d = x.shape
    d_inner = w_z.shape[1]
    conv_dim = w_xbc.shape[1]
    nheads = d_inner // SSM_HEAD_DIM
    cl = SSM_CHUNK
    nchunks = seq // cl
    ns = SSM_SEQS_PER_STEP
    rows = pl.BlockSpec((ns, cl, d), lambda b, c: (b, c, 0))
    xo, hout, hist = pl.pallas_call(
        _ssd_prompt_kernel,
        grid=(batch // ns, nchunks),
        in_specs=[rows, _resident((1, d)), _resident(w_z.shape), _resident(w_xbc.shape),
                  _resident(w_dt.shape), rows, _resident_rows(w_out, 0, d_inner), _resident((1, d)),
                  _resident(conv_w.shape), _resident((1, conv_dim)), _resident((1, LANES)),
                  _resident((1, LANES)), _resident((1, d_inner)), _resident((1, d_inner)),
                  _resident((LANES, d_inner))],
        out_specs=[rows,
                   pl.BlockSpec((ns, nheads, SSM_HEAD_DIM, SSM_STATE), lambda b, c: (b, 0, 0, 0)),
                   pl.BlockSpec((ns, SUBLANES, conv_dim), lambda b, c: (b, 0, 0))],
        out_shape=[jax.ShapeDtypeStruct((batch, seq, d), F32),
                   jax.ShapeDtypeStruct((batch, nheads, SSM_HEAD_DIM, SSM_STATE), F32),
                   jax.ShapeDtypeStruct((batch, SUBLANES, conv_dim), F32)],
        scratch_shapes=[pltpu.VMEM((ns, SUBLANES, conv_dim), F32),
                        pltpu.VMEM((ns, SSM_STATE, d_inner), F32),
                        pltpu.VMEM((ns, cl + SUBLANES, conv_dim), F32),
                        pltpu.VMEM((ns, cl, d_inner), F32)],
        compiler_params=_params(("arbitrary", "arbitrary")),
        name="ssd_prompt",
    )(x.reshape(batch, seq, d), g_pre, w_z, w_xbc, w_dt, other.reshape(batch, seq, d), w_out, g_post,
      conv_w, conv_b, dt_bias, a_log, d_skip, norm_w, jnp.asarray(_head_expansion(nheads, SSM_HEAD_DIM), BF16))
    return xo.reshape(t, d), hout, hist


def _ssd_step_kernel(z_ref, xnew_ref, h0_ref, h1_ref, h2_ref, dt_ref, cw_ref, cb_ref, dtb_ref, alog_ref,
                     dsk_ref, nw_ref, st_ref, y_ref, stout_ref,
                     xs_sc, xst_sc, bmt_sc, cmt_sc, dtt_sc, dat_sc, yt_sc):
    h = pl.program_id(0)
    nheads = pl.num_programs(0)
    d_inner = xs_sc.shape[1]
    hp, hn = SSM_HEAD_DIM, SSM_STATE
    heads_per_group = d_inner // hp // SSM_GROUPS

    @pl.when(h == 0)
    def _():
        conv = (cb_ref[...] + h0_ref[...] * cw_ref[0:1, :] + h1_ref[...] * cw_ref[1:2, :]
                + h2_ref[...] * cw_ref[2:3, :] + xnew_ref[...] * cw_ref[3:4, :])
        act = _silu(conv)
        xs = act[:, :d_inner]
        dt = jax.nn.softplus(dt_ref[...] + dtb_ref[...])
        xs_sc[...] = xs
        xst_sc[...] = xs.T
        bmt_sc[...] = act[:, d_inner:d_inner + SSM_GROUPS * hn].T
        cmt_sc[...] = act[:, d_inner + SSM_GROUPS * hn:].T
        dtt_sc[...] = dt.T
        dat_sc[...] = jnp.exp(dt * (-jnp.exp(alog_ref[...]))).T

    g = h // heads_per_group
    bmt = bmt_sc[pl.ds(pl.multiple_of(g * hn, hn), hn), :]
    cmt = cmt_sc[pl.ds(pl.multiple_of(g * hn, hn), hn), :]
    da = dat_sc[pl.ds(h, 1), :]
    dtx = xst_sc[pl.ds(pl.multiple_of(h * hp, hp), hp), :] * dtt_sc[pl.ds(h, 1), :]
    rows_per_tile = LANES // hn
    for j in range(hp // rows_per_tile):
        tile = st_ref[:, j * LANES:(j + 1) * LANES].T
        news = []
        for r in range(rows_per_tile):
            p = j * rows_per_tile + r
            new = da * tile[r * hn:(r + 1) * hn, :] + dtx[p:p + 1, :] * bmt
            yt_sc[pl.ds(h * hp + p, 1), :] = jnp.sum(new * cmt, axis=0, keepdims=True)
            news.append(new)
        stout_ref[:, j * LANES:(j + 1) * LANES] = jnp.concatenate(news, axis=0).T

    @pl.when(h == nheads - 1)
    def _():
        y_ref[...] = _ssd_gate_norm(yt_sc[...].T, xs_sc[...], z_ref[...], dsk_ref, nw_ref).astype(y_ref.dtype)


def ssd_step(z, xnew, hist, dt, conv_w, conv_b, dt_bias, a_log, d_skip, norm_w, state):
    b, d_inner = z.shape
    conv_dim = xnew.shape[1]
    nheads = d_inner // SSM_HEAD_DIM
    per_head = SSM_HEAD_DIM * SSM_STATE
    full = lambda shape: pl.BlockSpec(shape, lambda h: (0, 0))
    y, st = pl.pallas_call(
        _ssd_step_kernel,
        grid=(nheads,),
        in_specs=[full((b, d_inner))] + [full((b, conv_dim))] * 4 + [full((b, LANES))]
        + [full(conv_w.shape), full((1, conv_dim)), full((1, LANES)), full((1, LANES)),
           full((1, d_inner)), full((1, d_inner)),
           pl.BlockSpec((b, per_head), lambda h: (0, h))],
        out_specs=[full((b, d_inner)), pl.BlockSpec((b, per_head), lambda h: (0, h))],
        out_shape=[jax.ShapeDtypeStruct((b, d_inner), BF16),
                   jax.ShapeDtypeStruct((b, nheads * per_head), F32)],
        scratch_shapes=[pltpu.VMEM((b, d_inner), F32), pltpu.VMEM((d_inner, b), F32),
                        pltpu.VMEM((SSM_GROUPS * SSM_STATE, b), F32),
                        pltpu.VMEM((SSM_GROUPS * SSM_STATE, b), F32),
                        pltpu.VMEM((LANES, b), F32), pltpu.VMEM((LANES, b), F32),
                        pltpu.VMEM((d_inner, b), F32)],
        compiler_params=_params(("arbitrary",)),
        name="ssd_step",
    )(z, xnew, hist[0], hist[1], hist[2], dt, conv_w, conv_b, dt_bias, a_log, d_skip, norm_w,
      state.reshape(b, nheads * per_head))
    return y, st.reshape(b, nheads, SSM_HEAD_DIM, SSM_STATE)


def _hgrn_lower_bound(lbp_ref, layer):
    raw = lbp_ref[...]
    e = jnp.exp(raw - jnp.max(raw, axis=0, keepdims=True))
    return jnp.sum(e[:layer + 1], axis=0, keepdims=True) / jnp.sum(e, axis=0, keepdims=True)


def _hgrn_gates(q_raw, f_raw, lb):
    q = _silu(q_raw)
    f = lb + (1.0 - lb) * jax.nn.sigmoid(f_raw)
    k = 1.0 - f
    return q, f, k


def _pair_levels(cl):
    t = np.arange(cl)[:, None]
    s = np.arange(cl)[None, :]
    lvl = np.floor(np.log2(np.maximum(t ^ s, 1))).astype(np.int32)
    return np.where(t > s, lvl, np.where(t == s, -1, -2)).astype(np.int32)


HG_HEADS_PER_DOT = 2
HG_GROUP = 8


def _hgrn_prompt_kernel(layer, x_ref, gpre_ref, win_ref, wout_ref, lvl_ref, lbp_ref, nw_ref,
                        m_ref, sout_ref, st_sc):
    c = pl.program_id(1)
    nchunks = pl.num_programs(1)
    cl = x_ref.shape[0]
    nheads = st_sc.shape[0]
    width = nheads * HG_DIM
    span = HG_HEADS_PER_DOT * HG_DIM

    @pl.when(c == 0)
    def _():
        st_sc[...] = jnp.zeros_like(st_sc)

    lb_all = _hgrn_lower_bound(lbp_ref, layer)
    row = lax.broadcasted_iota(jnp.int32, (cl, LANES), 0)
    lvl = lvl_ref[...]
    tril16 = (lvl >= -1).astype(BF16)
    xn = _rms(x_ref[...], gpre_ref[...]).astype(BF16)

    def project(j):
        return [_dot(xn, win_ref[:, which * width + j * span:which * width + (j + 1) * span])
                for which in range(4)]

    acc = None
    for g0 in range(0, nheads, HG_GROUP):
        heads = range(g0, g0 + HG_GROUP)
        raw = [{}, {}, {}, {}]
        for j in range(g0 // HG_HEADS_PER_DOT, (g0 + HG_GROUP) // HG_HEADS_PER_DOT):
            for which, both in enumerate(project(j)):
                for hh in range(HG_HEADS_PER_DOT):
                    raw[which][j * HG_HEADS_PER_DOT + hh] = both[:, hh * HG_DIM:(hh + 1) * HG_DIM]
        q, k, cum = {}, {}, {}
        for h in heads:
            q[h], fh, k[h] = _hgrn_gates(raw[0][h], raw[1][h], lb_all[:, h * HG_DIM:(h + 1) * HG_DIM])
            total = jnp.log(fh)
            shift = 1
            while shift < cl:
                total = total + jnp.where(row >= shift, pltpu.roll(total, shift, axis=0), 0.0)
                shift *= 2
            cum[h] = total

        att = {h: jnp.where(lvl == -1, _dot_nt(q[h].astype(BF16), k[h].astype(BF16)), 0.0) for h in heads}
        last_of_block = dict(cum)
        blk = 1
        level = 0
        while blk < cl:
            odd = (row & blk) != 0
            for h in heads:
                expo = jnp.where(odd, cum[h] - pltpu.roll(last_of_block[h], blk, axis=0),
                                 last_of_block[h] - cum[h])
                scaled = (jnp.where(odd, q[h], k[h]) * jnp.exp(expo)).astype(BF16)
                att[h] = jnp.where(lvl == level, _dot_nt(scaled, scaled), att[h])
                last_of_block[h] = jnp.where(odd, last_of_block[h],
                                             pltpu.roll(last_of_block[h], cl - blk, axis=0))
            blk *= 2
            level += 1
        cum_last = last_of_block

        outs = []
        for h in heads:
            st = st_sc[h]
            v = raw[2][h]
            o = (_dot(att[h].astype(BF16), v.astype(BF16))
                 + _dot_nt((q[h] * jnp.exp(cum[h])).astype(BF16), st.astype(BF16)))
            k_end = (k[h] * jnp.exp(cum_last[h] - cum[h])).astype(BF16)
            st_sc[h] = st * jnp.exp(cum_last[h]) + _dot(v.T.astype(BF16), k_end)
            outs.append((_rms(o, nw_ref[...]) * _silu(raw[3][h])).astype(BF16))
        part = _dot(jnp.concatenate(outs, axis=1), wout_ref[g0 * HG_DIM:(g0 + HG_GROUP) * HG_DIM, :])
        acc = part if acc is None else acc + part
    m_ref[...] = acc

    @pl.when(c == nchunks - 1)
    def _():
        for h in range(nheads):
            sout_ref[0, h] = st_sc[h].T


def hgrn_prompt(x, g_pre, w_qfig, w_out, lb_param, norm_w, layer, batch, seq):
    t, d = x.shape
    width = w_qfig.shape[1] // 4
    nheads = width // HG_DIM
    cl = HG_CHUNK
    nchunks = seq // cl
    row = lambda b, c: (b * nchunks + c, 0)
    kern = functools.partial(_hgrn_prompt_kernel, layer)
    return pl.pallas_call(
        kern,
        grid=(batch, nchunks),
        in_specs=[pl.BlockSpec((cl, d), row), _resident((1, d)), _resident(w_qfig.shape),
                  _resident_rows(w_out, w_out.shape[0] - width, width),
                  _resident((cl, cl)), _resident(lb_param.shape), _resident((1, HG_DIM))],
        out_specs=[pl.BlockSpec((cl, d), row),
                   pl.BlockSpec((1, nheads, HG_DIM, HG_DIM), lambda b, c: (b, 0, 0, 0))],
        out_shape=[jax.ShapeDtypeStruct((t, d), F32),
                   jax.ShapeDtypeStruct((batch, nheads, HG_DIM, HG_DIM), F32)],
        scratch_shapes=[pltpu.VMEM((nheads, HG_DIM, HG_DIM), F32)],
        compiler_params=_params(("arbitrary", "arbitrary")),
        name="hgrn_prompt",
    )(x, g_pre, w_qfig, w_out, jnp.asarray(_pair_levels(cl)), lb_param, norm_w)


HG_STEP_TOKENS = 8


def _hgrn_step_kernel(layer, qfig_ref, lbp_ref, nw_ref, st_ref, o_ref, stout_ref):
    nheads = st_ref.shape[2]
    raw = lbp_ref[...]
    e = jnp.exp(raw - jnp.max(raw, axis=0, keepdims=True))
    lb = jnp.sum(e[:layer + 1], axis=0) / jnp.sum(e, axis=0)
    pad = jnp.zeros((HG_DIM - 2 * nheads, HG_DIM), F32)
    for bl in range(st_ref.shape[1]):
        t = qfig_ref[bl]
        q, f, k = _hgrn_gates(t[0:nheads], t[nheads:2 * nheads], lb)
        v = t[2 * nheads:3 * nheads]
        cols = jnp.concatenate([f, k, pad], axis=0).T
        q16 = q.astype(BF16)
        rows = []
        for h in range(nheads):
            f_col = cols[:, h:h + 1]
            k_col = cols[:, nheads + h:nheads + h + 1]
            new = f_col * st_ref[0, bl, h] + k_col * v[h:h + 1, :]
            stout_ref[bl, h] = new
            rows.append(_dot(q16, new.astype(BF16))[h:h + 1, :])
        o = jnp.concatenate(rows, axis=0)
        o_ref[bl] = _rms(o, nw_ref[...]) * _silu(t[3 * nheads:])


def hgrn_step(qfig, lb_param, norm_w, layer, state_all):
    b, w4 = qfig.shape
    width = w4 // 4
    nheads = width // HG_DIM
    bb = HG_STEP_TOKENS
    kern = functools.partial(_hgrn_step_kernel, layer)
    o, st = pl.pallas_call(
        kern,
        grid=(b // bb,),
        in_specs=[pl.BlockSpec((bb, 4 * nheads, HG_DIM), lambda i: (i, 0, 0)),
                  _resident((lb_param.shape[0], nheads, HG_DIM)), _resident((1, HG_DIM)),
                  pl.BlockSpec((1, bb, nheads, HG_DIM, HG_DIM), lambda i: (layer, i, 0, 0, 0))],
        out_specs=[pl.BlockSpec((bb, nheads, HG_DIM), lambda i: (i, 0, 0)),
                   pl.BlockSpec((bb, nheads, HG_DIM, HG_DIM), lambda i: (i, 0, 0, 0))],
        out_shape=[jax.ShapeDtypeStruct((b, nheads, HG_DIM), F32),
                   jax.ShapeDtypeStruct((b, nheads, HG_DIM, HG_DIM), F32)],
        compiler_params=_params(("parallel",)),
        name="hgrn_step",
    )(qfig.reshape(b, 4 * nheads, HG_DIM), lb_param.reshape(-1, nheads, HG_DIM), norm_w, state_all)
    return o.reshape(b, width), st


def _s5_discretize_kernel(lre_ref, lim_ref, step_ref, bre_ref, bim_ref, are_ref, aim_ref, bbre_ref, bbim_ref):
    lre, lim = lre_ref[...], lim_ref[...]
    step = jnp.exp(step_ref[...])
    mag = jnp.exp(lre * step)
    are = mag * jnp.cos(lim * step)
    aim = mag * jnp.sin(lim * step)
    den = lre * lre + lim * lim
    nr = are - 1.0
    cre = (nr * lre + aim * lim) / den
    cim = (aim * lre - nr * lim) / den
    are_ref[...] = are
    aim_ref[...] = aim
    bre, bim = bre_ref[...], bim_ref[...]
    bbre_ref[...] = cre * bre - cim * bim
    bbim_ref[...] = cre * bim + cim * bre


def s5_discretize(lam_re, lam_im, log_step, b_re_t, b_im_t):
    g, n = lam_re.shape
    c = b_re_t.shape[1]
    return pl.pallas_call(
        _s5_discretize_kernel,
        out_shape=[jax.ShapeDtypeStruct((g, 1, n), F32)] * 2 + [jax.ShapeDtypeStruct((g, c, n), F32)] * 2,
        name="s5_discretize",
    )(lam_re.reshape(g, 1, n), lam_im.reshape(g, 1, n), log_step.reshape(g, 1, 1), b_re_t, b_im_t)


def _s5_output(xre, xim, u, cre_ref, cim_ref, d_ref):
    y = _dot(xre.astype(BF16), cre_ref[0]) - _dot(xim.astype(BF16), cim_ref[0]) + d_ref[...] * u
    return jax.nn.gelu(y)


S5_ROWS = 256
S5_CHUNK = 256


def _s5_prompt_kernel(u_ref, b_ref, cre_ref, cim_ref, d_ref, are_ref, aim_ref,
                      y_ref, sre_ref, sim_ref, x_sc, ui_sc, st_sc):
    c = pl.program_id(2)
    nseq, tc = u_ref.shape[0], u_ref.shape[1]
    ns = S5_BLOCK_STATE
    nt = ns // LANES
    steps = S5_ROWS // nseq
    nslabs = tc // steps

    @pl.when(c == 0)
    def _():
        st_sc[...] = jnp.zeros_like(st_sc)

    def slab_rows(s):
        return slice(s * S5_ROWS, (s + 1) * S5_ROWS)

    def project(s):
        r = slab_rows(s)
        ui = jnp.concatenate([u_ref[:, i, :] for i in range(s * steps, (s + 1) * steps)], axis=0)
        ui_sc[r, :] = ui
        bu = _dot(ui.astype(BF16), b_ref[0])
        for k in range(2 * nt):
            x_sc[k, r, :] = bu[:, k * LANES:(k + 1) * LANES]

    def lane_tile(ref, k):
        return ref[:, k * LANES:(k + 1) * LANES]

    are = [jnp.broadcast_to(lane_tile(are_ref, k), (nseq, LANES)) for k in range(nt)]
    aim = [jnp.broadcast_to(lane_tile(aim_ref, k), (nseq, LANES)) for k in range(nt)]

    def scan_slab(s, carry):
        for i in range(s * steps, (s + 1) * steps):
            rows = slice(i * nseq, (i + 1) * nseq)
            new = []
            for k in range(nt):
                sre, sim = carry[k]
                nre = are[k] * sre - aim[k] * sim + x_sc[k, rows, :]
                nim = are[k] * sim + aim[k] * sre + x_sc[nt + k, rows, :]
                x_sc[k, rows, :] = nre
                x_sc[nt + k, rows, :] = nim
                new.append((nre, nim))
            carry = tuple(new)
        return carry

    def output(s):
        r = slab_rows(s)
        xre = jnp.concatenate([x_sc[k, r, :] for k in range(nt)], axis=1)
        xim = jnp.concatenate([x_sc[nt + k, r, :] for k in range(nt)], axis=1)
        y = _s5_output(xre, xim, ui_sc[r, :], cre_ref, cim_ref, d_ref)
        for ii in range(steps):
            y_ref[:, s * steps + ii, :] = y[ii * nseq:(ii + 1) * nseq, :]

    state = tuple((st_sc[k], st_sc[nt + k]) for k in range(nt))
    project(0)
    for s in range(nslabs):
        if s + 1 < nslabs:
            project(s + 1)
        if s > 0:
            output(s - 1)
        state = scan_slab(s, state)
    output(nslabs - 1)
    for k in range(nt):
        st_sc[k], st_sc[nt + k] = state[k]
        sre_ref[0, :, k * LANES:(k + 1) * LANES] = state[k][0]
        sim_ref[0, :, k * LANES:(k + 1) * LANES] = state[k][1]


def s5_prompt(u, b_exp, c_re_exp, c_im_exp, d_skip, a_re, a_im, batch, seq):
    t, d = u.shape
    nblk = d // LANES
    ns = S5_BLOCK_STATE
    tc = S5_CHUNK
    groups = batch // SUBLANES
    rows = pl.BlockSpec((SUBLANES, tc, LANES), lambda j, g, c: (g, c, j))
    state = pl.BlockSpec((1, SUBLANES, ns), lambda j, g, c: (g, 0, j))
    y, s_re, s_im = pl.pallas_call(
        _s5_prompt_kernel,
        grid=(nblk, groups, seq // tc),
        in_specs=[rows,
                  pl.BlockSpec((1, LANES, 2 * ns), lambda j, g, c: (j, 0, 0)),
                  pl.BlockSpec((1, ns, LANES), lambda j, g, c: (j, 0, 0)),
                  pl.BlockSpec((1, ns, LANES), lambda j, g, c: (j, 0, 0)),
                  pl.BlockSpec((1, LANES), lambda j, g, c: (0, j)),
                  pl.BlockSpec((1, ns), lambda j, g, c: (0, j)),
                  pl.BlockSpec((1, ns), lambda j, g, c: (0, j))],
        out_specs=[rows, state, state],
        out_shape=[jax.ShapeDtypeStruct((batch, seq, d), F32),
                   jax.ShapeDtypeStruct((groups, SUBLANES, nblk * ns), F32),
                   jax.ShapeDtypeStruct((groups, SUBLANES, nblk * ns), F32)],
        scratch_shapes=[pltpu.VMEM((2 * ns // LANES, SUBLANES * tc, LANES), F32),
                        pltpu.VMEM((SUBLANES * tc, LANES), F32),
                        pltpu.VMEM((2 * ns // LANES, SUBLANES, LANES), F32)],
        compiler_params=_params(("arbitrary", "arbitrary", "arbitrary")),
        name="s5_prompt",
    )(u.reshape(batch, seq, d), b_exp, c_re_exp, c_im_exp, d_skip, a_re, a_im)
    return y.reshape(t, d), s_re, s_im


def _s5_step_kernel(u_ref, b_ref, cre_ref, cim_ref, d_ref, are_ref, aim_ref, s0re_ref, s0im_ref,
                    y_ref, sre_ref, sim_ref):
    ns = S5_BLOCK_STATE
    u = u_ref[...]
    bu = _dot(u.astype(BF16), b_ref[0])
    are, aim = are_ref[...], aim_ref[...]
    s0re, s0im = s0re_ref[...], s0im_ref[...]
    xre = bu[:, 0:ns] + (are * s0re - aim * s0im)
    xim = bu[:, ns:2 * ns] + (are * s0im + aim * s0re)
    sre_ref[...] = xre
    sim_ref[...] = xim
    y_ref[...] = _s5_output(xre, xim, u, cre_ref, cim_ref, d_ref)


def s5_step(u, b_exp, c_re_exp, c_im_exp, d_skip, a_re, a_im, s0_re, s0_im):
    b, d = u.shape
    nblk = d // LANES
    ns = S5_BLOCK_STATE
    return pl.pallas_call(
        _s5_step_kernel,
        grid=(nblk,),
        in_specs=[pl.BlockSpec((b, LANES), lambda j: (0, j)),
                  pl.BlockSpec((1, LANES, 2 * ns), lambda j: (j, 0, 0)),
                  pl.BlockSpec((1, ns, LANES), lambda j: (j, 0, 0)),
                  pl.BlockSpec((1, ns, LANES), lambda j: (j, 0, 0)),
                  pl.BlockSpec((1, LANES), lambda j: (0, j)),
                  pl.BlockSpec((1, ns), lambda j: (0, j)),
                  pl.BlockSpec((1, ns), lambda j: (0, j)),
                  pl.BlockSpec((b, ns), lambda j: (0, j)),
                  pl.BlockSpec((b, ns), lambda j: (0, j))],
        out_specs=[pl.BlockSpec((b, LANES), lambda j: (0, j)),
                   pl.BlockSpec((b, ns), lambda j: (0, j)),
                   pl.BlockSpec((b, ns), lambda j: (0, j))],
        out_shape=[jax.ShapeDtypeStruct((b, d), F32),
                   jax.ShapeDtypeStruct((b, nblk * ns), F32),
                   jax.ShapeDtypeStruct((b, nblk * ns), F32)],
        compiler_params=_params(("parallel",)),
        name="s5_step",
    )(u, b_exp, c_re_exp, c_im_exp, d_skip, a_re, a_im, s0_re, s0_im)


def _block_diag(per_group):
    ngroups, r, c = per_group.shape
    p = S5_GROUPS_PER_BLOCK
    eye = jnp.eye(p, dtype=per_group.dtype)
    tiles = per_group.reshape(ngroups // p, p, r, c)
    return jnp.einsum('bjrc,jk->bjrkc', tiles, eye).reshape(ngroups // p, p * r, p * c)


def kernel(x_prompt, x_sample, state_ssm, state_ssm_conv, state_hgrn, state_s5_re, state_s5_im, state_ffn_conv, norm_mix_pre, norm_mix_post, norm_ffn_pre, norm_ffn_post, ab_in_w, ssm_conv_w, ssm_conv_b, ssm_dt_bias, ssm_a_log, ssm_d, ssm_norm_w, hgrn_lb, hgrn_norm_w, ab_out_w, s5_lam_re, s5_lam_im, s5_log_step, s5_b_re, s5_b_im, s5_c_re, s5_c_im, s5_d, s5_glu_w, ffn_up_w, ffn_conv_w, ffn_conv_b, ffn_down_w):
    batch, seq, d = x_prompt.shape
    dec_batch = x_sample.shape[0]
    depth = norm_mix_pre.shape[0]
    conv_dim = ssm_conv_w.shape[2]
    nheads = ssm_a_log.shape[1]
    d_inner = nheads * SSM_HEAD_DIM
    hg_width = hgrn_lb.shape[1]
    ffn_dim = ffn_down_w.shape[1]
    tm = 256

    def row(v):
        return v.reshape(1, -1)

    def pad_lanes(v):
        return jnp.pad(v, ((0, 0), (0, LANES - v.shape[1])))

    groups = [(x_prompt.reshape(batch * seq, d), True), (x_sample.reshape(dec_batch, d), False)]
    xs = [g[0] for g in groups]
    normed = [None, None]
    up_w_all = ffn_up_w.astype(BF16)
    down_w_all = ffn_down_w.astype(BF16)
    out = {'ssm': [[], []], 'sconv': [[], []], 'hg': [[], []], 's5r': [[], []], 's5i': [[], []], 'fconv': [[], []]}

    for l in range(depth):
        e = l // 2
        if l % 2 == 0:
            w_in = ab_in_w[e]
            o_xbc = d_inner
            o_dt = o_xbc + conv_dim
            o_q = o_dt + nheads
            w_z = w_in[:, :o_xbc].astype(BF16)
            w_xbc = w_in[:, o_xbc:o_dt].astype(BF16)
            w_dt = pad_lanes(w_in[:, o_dt:o_q]).astype(BF16)
            w_qfig = w_in[:, o_q:].astype(BF16)
            w_out = ab_out_w[e].astype(BF16)
            dt_bias = pad_lanes(row(ssm_dt_bias[e]))
            a_log = pad_lanes(row(ssm_a_log[e]))
            d_skip = row(jnp.repeat(ssm_d[e], SSM_HEAD_DIM))
            for gi, (_, is_prompt) in enumerate(groups):
                x = xs[gi]
                t = x.shape[0]
                if is_prompt:
                    m_hg, hg_new = hgrn_prompt(x, row(norm_mix_pre[l]), w_qfig, w_out, hgrn_lb,
                                               row(hgrn_norm_w[e]), e, batch, seq)
                    xs[gi], ssm_new, xbc_tail = ssd_prompt(
                        x, row(norm_mix_pre[l]), w_z, w_xbc, w_dt, m_hg, w_out, row(norm_mix_post[l]),
                        ssm_conv_w[e], row(ssm_conv_b[e]), dt_bias, a_log, d_skip, row(ssm_norm_w[e]), batch, seq)
                    sconv_new = xbc_tail[:, SUBLANES - (SSM_CONV - 1):]
                else:
                    z, xbc, dt, qfig = norm_matmul(x, row(norm_mix_pre[l]), [w_z, w_xbc, w_dt, w_qfig], t)
                    hist = [state_ssm_conv[e][:, k] for k in range(SSM_CONV - 1)]
                    y, ssm_new = ssd_step(z, xbc, hist, dt, ssm_conv_w[e], row(ssm_conv_b[e]), dt_bias, a_log,
                                          d_skip, row(ssm_norm_w[e]), state_ssm[e])
                    o, hg_new = hgrn_step(qfig, hgrn_lb, row(hgrn_norm_w[e]), e, state_hgrn)
                    sconv_new = jnp.stack(hist[1:] + [xbc], axis=1)
                    xs[gi] = matmul_norm_residual([y, o], w_out, row(norm_mix_post[l]), x, t)
                out['ssm'][gi].append(ssm_new)
                out['hg'][gi].append(hg_new)
                out['sconv'][gi].append(sconv_new)
        else:
            ngroups = s5_lam_re.shape[1]
            a_re, a_im, bb_re, bb_im = s5_discretize(
                s5_lam_re[e], s5_lam_im[e], s5_log_step[e],
                jnp.swapaxes(s5_b_re[e], 1, 2), jnp.swapaxes(s5_b_im[e], 1, 2))
            a_re = a_re.reshape(1, ngroups * S5_STATE)
            a_im = a_im.reshape(1, ngroups * S5_STATE)
            b_exp = jnp.concatenate([_block_diag(bb_re), _block_diag(bb_im)], axis=2).astype(BF16)
            c_re_exp = _block_diag(jnp.swapaxes(s5_c_re[e], 1, 2)).astype(BF16)
            c_im_exp = _block_diag(jnp.swapaxes(s5_c_im[e], 1, 2)).astype(BF16)
            glu_w = s5_glu_w[e].astype(BF16)
            for gi, (_, is_prompt) in enumerate(groups):
                x = xs[gi]
                t = x.shape[0]
                u = normed[gi] if normed[gi] is not None else rmsnorm(x, row(norm_mix_pre[l]), min(tm, t))
                if is_prompt:
                    yg, s_re, s_im = s5_prompt(u, b_exp, c_re_exp, c_im_exp, row(s5_d[e]), a_re, a_im, batch, seq)
                else:
                    yg, s_re, s_im = s5_step(u, b_exp, c_re_exp, c_im_exp, row(s5_d[e]), a_re, a_im,
                                             state_s5_re[e].reshape(t, -1), state_s5_im[e].reshape(t, -1))
                out['s5r'][gi].append(s_re.reshape(-1, ngroups, S5_STATE))
                out['s5i'][gi].append(s_im.reshape(-1, ngroups, S5_STATE))
                xs[gi] = matmul_norm_residual([yg], glu_w, row(norm_mix_post[l]), x, min(tm, t), glu=True)

        for gi, (_, is_prompt) in enumerate(groups):
            x = xs[gi]
            t = x.shape[0]
            normed[gi] = None
            if is_prompt:
                g_next = row(norm_mix_pre[l + 1]) if l + 1 < depth and (l + 1) % 2 == 1 else None
                res = ffn_seq(x, row(norm_ffn_pre[l]), up_w_all, ffn_conv_w[l], row(ffn_conv_b[l]), down_w_all,
                              row(norm_ffn_post[l]), g_next, tm, seq, l)
                xs[gi], tail = res[0], res[1]
                if g_next is not None:
                    normed[gi] = res[2]
                fconv_new = tail[:, SUBLANES - (FFN_CONV - 1):]
            else:
                (up,) = norm_matmul(x, row(norm_ffn_pre[l]), [up_w_all], t, layer=l)
                old = state_ffn_conv[l]
                xs[gi] = ffn_tail_step(up, old[:, 0], old[:, 1], ffn_conv_w[l], row(ffn_conv_b[l]), down_w_all,
                                       row(norm_ffn_post[l]), x, l)
                fconv_new = jnp.stack([old[:, 1], up], axis=1)
            out['fconv'][gi].append(fconv_new)

    y_prompt = xs[0].reshape(batch, seq, d)
    y_sample = xs[1].reshape(dec_batch, 1, d)
    states = []
    for gi in range(2):
        states += [jnp.stack(out[k][gi]) for k in ('ssm', 'sconv', 'hg', 's5r', 's5i', 'fconv')]
    return (y_prompt, y_sample, *states)
```

```python
import functools
import math

import jax
import jax.numpy as jnp
import numpy as np
from jax import lax
from jax.experimental import pallas as pl
from jax.experimental.pallas import tpu as pltpu

F32 = jnp.float32
BF16 = jnp.bfloat16
EPS = 1e-6

LANES = 128
SUBLANES = 8
VMEM_LIMIT_BYTES = 56 * 1024 * 1024

SSM_HEAD_DIM = 64
SSM_GROUPS = 2
SSM_STATE = 64
SSM_CONV = 4
SSM_CHUNK = 128
HG_DIM = 128
HG_CHUNK = 128
S5_GROUP = 16
S5_STATE = 64
S5_GROUPS_PER_BLOCK = LANES // S5_GROUP
S5_BLOCK_STATE = S5_GROUPS_PER_BLOCK * S5_STATE
FFN_CONV = 3


def _params(sem):
    return pltpu.CompilerParams(dimension_semantics=sem, vmem_limit_bytes=VMEM_LIMIT_BYTES)


def _resident(shape, layer=None):
    if layer is None:
        nd = len(shape)
        return pl.BlockSpec(shape, lambda *_: (0,) * nd, pipeline_mode=pl.Buffered(1))
    rest = tuple(shape[1:])
    return pl.BlockSpec((None,) + rest, lambda *_: (layer,) + (0,) * len(rest), pipeline_mode=pl.Buffered(1))


def _resident_rows(w, first, nrows):
    assert first % nrows == 0
    return pl.BlockSpec((nrows, w.shape[1]), lambda *_: (first // nrows, 0), pipeline_mode=pl.Buffered(1))


def _rms(x, w):
    return x * lax.rsqrt(jnp.mean(x * x, axis=-1, keepdims=True) + EPS) * w


def _silu(x):
    return x * jax.nn.sigmoid(x)


def _dot(a, b):
    return jnp.dot(a, b, preferred_element_type=F32)


def _dot_nt(a, b):
    return lax.dot_general(a, b, (((1,), (1,)), ((), ())), preferred_element_type=F32)


def _dot_tn(a, b):
    return lax.dot_general(a, b, (((0,), (0,)), ((), ())), preferred_element_type=F32)


def _norm_matmul_kernel(n_out, x_ref, g_ref, *refs):
    w_refs, o_refs = refs[:n_out], refs[n_out:]
    xn = _rms(x_ref[...], g_ref[...]).astype(BF16)
    for w_ref, o_ref in zip(w_refs, o_refs):
        n = w_ref.shape[1]
        for c0 in range(0, n, 512):
            c1 = min(n, c0 + 512)
            o_ref[:, c0:c1] = _dot(xn, w_ref[:, c0:c1])


def norm_matmul(x, g, ws, tm, layer=None):
    t, d = x.shape
    kern = functools.partial(_norm_matmul_kernel, len(ws))
    return pl.pallas_call(
        kern,
        grid=(t // tm,),
        in_specs=[pl.BlockSpec((tm, d), lambda i: (i, 0)), _resident((1, d))]
        + [_resident(w.shape, layer) for w in ws],
        out_specs=[pl.BlockSpec((tm, w.shape[-1]), lambda i: (i, 0)) for w in ws],
        out_shape=[jax.ShapeDtypeStruct((t, w.shape[-1]), F32) for w in ws],
        compiler_params=_params(("parallel",)),
        name="norm_matmul",
    )(x, g, *ws)


def _rmsnorm_kernel(x_ref, g_ref, o_ref):
    o_ref[...] = _rms(x_ref[...], g_ref[...])


def rmsnorm(x, g, tm):
    t, d = x.shape
    return pl.pallas_call(
        _rmsnorm_kernel,
        grid=(t // tm,),
        in_specs=[pl.BlockSpec((tm, d), lambda i: (i, 0)), _resident((1, d))],
        out_specs=pl.BlockSpec((tm, d), lambda i: (i, 0)),
        out_shape=jax.ShapeDtypeStruct((t, d), F32),
        compiler_params=_params(("parallel",)),
        name="rmsnorm",
    )(x, g)


def _matmul_norm_residual_kernel(n_in, glu, *refs):
    a_refs, w_refs = refs[:n_in], refs[n_in:2 * n_in]
    g_ref, x_ref, o_ref = refs[2 * n_in:]
    acc = None
    for a_ref, w_ref in zip(a_refs, w_refs):
        part = _dot(a_ref[...].astype(BF16), w_ref[...])
        acc = part if acc is None else acc + part
    if glu:
        d = acc.shape[1] // 2
        acc = acc[:, :d] * jax.nn.sigmoid(acc[:, d:])
    o_ref[...] = x_ref[...] + _rms(acc, g_ref[...])


def matmul_norm_residual(a_list, w, g, x, tm, glu=False):
    t, d = x.shape
    kern = functools.partial(_matmul_norm_residual_kernel, len(a_list), glu)
    firsts = [sum(a.shape[1] for a in a_list[:i]) for i in range(len(a_list))]
    return pl.pallas_call(
        kern,
        grid=(t // tm,),
        in_specs=[pl.BlockSpec((tm, a.shape[1]), lambda i: (i, 0)) for a in a_list]
        + [_resident_rows(w, first, a.shape[1]) for a, first in zip(a_list, firsts)]
        + [_resident((1, d)), pl.BlockSpec((tm, d), lambda i: (i, 0))],
        out_specs=pl.BlockSpec((tm, d), lambda i: (i, 0)),
        out_shape=jax.ShapeDtypeStruct((t, d), F32),
        compiler_params=_params(("parallel",)),
        name="matmul_norm_residual",
    )(*a_list, *([w] * len(a_list)), g, x)


FFN_CHUNK = 256
FFN_TILE = 256


def _ffn_tail(up_ref, xm2_of, xm1_of, cw_ref, cb_ref, dw_ref, g_ref, x_ref, o_ref):
    f = dw_ref.shape[0]
    acc = jnp.zeros(o_ref.shape, F32)
    for c0 in range(0, f, FFN_CHUNK):
        c1 = c0 + FFN_CHUNK
        halves = []
        for off in (0, f):
            a, b = c0 + off, c1 + off
            halves.append(cb_ref[:, a:b] + xm2_of(a, b) * cw_ref[0:1, a:b]
                          + xm1_of(a, b) * cw_ref[1:2, a:b] + up_ref[:, a:b] * cw_ref[2:3, a:b])
        act = (halves[0] * _silu(halves[1])).astype(BF16)
        acc = acc + _dot(act, dw_ref[c0:c1, :])
    o_ref[...] = x_ref[...] + _rms(acc, g_ref[...])


def _ffn_seq_kernel(tiles_per_seq, emit_next, x_ref, g1_ref, uw_ref, cw_ref, cb_ref, dw_ref, g2_ref, *rest):
    if emit_next:
        g3_ref, o_ref, tail_ref, n_ref, carry_sc = rest
    else:
        o_ref, tail_ref, carry_sc = rest
    tm = x_ref.shape[0]
    f = dw_ref.shape[0]

    @pl.when(pl.program_id(0) % tiles_per_seq == 0)
    def _():
        carry_sc[...] = jnp.zeros_like(carry_sc)

    x = x_ref[...]
    xn = _rms(x, g1_ref[...]).astype(BF16)
    top_row = lax.broadcasted_iota(jnp.int32, (SUBLANES, FFN_CHUNK), 0)

    def earlier(up, prev, k):
        rolled = pltpu.roll(up, k, axis=0)
        top = jnp.where(top_row < k, pltpu.roll(prev, k, axis=0), rolled[:SUBLANES])
        return jnp.concatenate([top, rolled[SUBLANES:]], axis=0)

    def project(c0):
        return [_dot(xn, uw_ref[:, c0 + off:c0 + off + FFN_CHUNK]) for off in (0, f)]

    acc = jnp.zeros(o_ref.shape, F32)
    ups_next = project(0)
    act_prev = None
    for c0 in range(0, f, FFN_CHUNK):
        ups = ups_next
        if c0 + FFN_CHUNK < f:
            ups_next = project(c0 + FFN_CHUNK)
        if act_prev is not None:
            acc = acc + _dot(act_prev, dw_ref[c0 - FFN_CHUNK:c0, :])
        halves = []
        for up, off in zip(ups, (0, f)):
            a, b = c0 + off, c0 + off + FFN_CHUNK
            prev = carry_sc[:, a:b]
            carry_sc[:, a:b] = up[tm - SUBLANES:, :]
            tail_ref[0, :, a:b] = up[tm - SUBLANES:, :]
            halves.append(cb_ref[:, a:b] + earlier(up, prev, 2) * cw_ref[0:1, a:b]
                          + earlier(up, prev, 1) * cw_ref[1:2, a:b] + up * cw_ref[2:3, a:b])
        act_prev = (halves[0] * _silu(halves[1])).astype(BF16)
    acc = acc + _dot(act_prev, dw_ref[f - FFN_CHUNK:f, :])
    out = x + _rms(acc, g2_ref[...])
    o_ref[...] = out
    if emit_next:
        n_ref[...] = _rms(out, g3_ref[...])


def ffn_seq(x, g_pre, up_w, conv_w, conv_b, down_w, g_post, g_next, tm, seq, layer):
    t, d = x.shape
    f2 = up_w.shape[-1]
    tiles_per_seq = seq // tm
    emit_next = g_next is not None
    kern = functools.partial(_ffn_seq_kernel, tiles_per_seq, emit_next)
    tile = pl.BlockSpec((tm, d), lambda i: (i, 0))
    return pl.pallas_call(
        kern,
        grid=(t // tm,),
        in_specs=[tile, _resident((1, d)), _resident(up_w.shape, layer), _resident(conv_w.shape),
                  _resident((1, f2)), _resident(down_w.shape, layer), _resident((1, d))]
        + [_resident((1, d))] * emit_next,
        out_specs=[tile, pl.BlockSpec((1, SUBLANES, f2), lambda i: (i // tiles_per_seq, 0, 0))]
        + [tile] * emit_next,
        out_shape=[jax.ShapeDtypeStruct((t, d), F32), jax.ShapeDtypeStruct((t // seq, SUBLANES, f2), F32)]
        + [jax.ShapeDtypeStruct((t, d), F32)] * emit_next,
        scratch_shapes=[pltpu.VMEM((SUBLANES, f2), F32)],
        compiler_params=_params(("arbitrary",)),
        name="ffn_seq",
    )(x, g_pre, up_w, conv_w, conv_b, down_w, g_post, *([g_next] * emit_next))


def _ffn_step_kernel(up_ref, xm2_ref, xm1_ref, cw_ref, cb_ref, dw_ref, g_ref, x_ref, o_ref):
    _ffn_tail(up_ref, lambda a, b: xm2_ref[:, a:b], lambda a, b: xm1_ref[:, a:b],
              cw_ref, cb_ref, dw_ref, g_ref, x_ref, o_ref)


def ffn_tail_step(up, xm2, xm1, conv_w, conv_b, down_w, g, x, layer):
    t, d = x.shape
    f2 = up.shape[1]
    return pl.pallas_call(
        _ffn_step_kernel,
        grid=(1,),
        in_specs=[_resident((t, f2))] * 3
        + [_resident(conv_w.shape), _resident((1, f2)), _resident(down_w.shape, layer),
           _resident((1, d)), _resident((t, d))],
        out_specs=pl.BlockSpec((t, d), lambda i: (0, 0)),
        out_shape=jax.ShapeDtypeStruct((t, d), F32),
        compiler_params=_params(("arbitrary",)),
        name="ffn_tail_step",
    )(up, xm2, xm1, conv_w, conv_b, down_w, g, x)


def _ssd_gate_norm(y, xs, z, dsk_ref, nw_ref):
    y = y + dsk_ref[...] * xs
    return _rms(y * _silu(z), nw_ref[...])


SSM_SEQS_PER_STEP = 4


def _head_expansion(nheads, head_dim):
    e = np.zeros((LANES, nheads * head_dim), np.float32)
    for h in range(nheads):
        e[h, h * head_dim:(h + 1) * head_dim] = 1.0
    return e


def _expand_heads(v, exp_ref):
    hi = v.astype(BF16)
    lo = (v - hi.astype(F32)).astype(BF16)
    return _dot(hi, exp_ref[...]) + _dot(lo, exp_ref[...])


def _ssd_prompt_kernel(x_ref, gpre_ref, wz_ref, wxbc_ref, wdt_ref, other_ref, wout_ref, gpost_ref,
                       cw_ref, cb_ref, dtb_ref, alog_ref, dsk_ref, nw_ref, exp_ref,
                       xo_ref, hout_ref, histout_ref, hist_sc, h_sc, xp_sc, y_sc):
    c = pl.program_id(1)
    nchunks = pl.num_programs(1)
    nseq, cl = x_ref.shape[0], x_ref.shape[1]
    hn, d_inner = h_sc.shape[1], h_sc.shape[2]
    hp = SSM_HEAD_DIM
    nheads = d_inner // hp
    heads_per_group = nheads // SSM_GROUPS
    group_width = heads_per_group * hp
    seqs = range(nseq)

    @pl.when(c == 0)
    def _():
        hist_sc[...] = jnp.zeros_like(hist_sc)
        h_sc[...] = jnp.zeros_like(h_sc)

    x = [x_ref[s] for s in seqs]
    xn = [_rms(x[s], gpre_ref[...]).astype(BF16) for s in seqs]
    xs, bm, cm = [], [], []
    for s in seqs:
        raw = _dot(xn[s], wxbc_ref[...])
        xp_sc[s, 0:SUBLANES, :] = hist_sc[s]
        xp_sc[s, SUBLANES:, :] = raw
        hist_sc[s] = raw[cl - SUBLANES:, :]
        conv = cb_ref[...]
        for k in range(SSM_CONV):
            conv = conv + xp_sc[s, pl.ds(SUBLANES - (SSM_CONV - 1) + k, cl), :] * cw_ref[k:k + 1, :]
        act = _silu(conv)
        xs.append(act[:, :d_inner])
        bm.append(act[:, d_inner:d_inner + SSM_GROUPS * SSM_STATE])
        cm.append(act[:, d_inner + SSM_GROUPS * SSM_STATE:])

    ti = lax.broadcasted_iota(jnp.int32, (cl, cl), 0)
    si = lax.broadcasted_iota(jnp.int32, (cl, cl), 1)
    tril = ti >= si
    tril_f = tril.astype(F32)
    dt, cum_col = [], []
    for s in seqs:
        dt.append(jax.nn.softplus(_dot(xn[s], wdt_ref[...]) + dtb_ref[...]))
        da = dt[s] * (-jnp.exp(alog_ref[...]))
        cum_col.append(jnp.dot(tril_f, da, preferred_element_type=F32,
                               precision=lax.Precision.HIGHEST))
    cum_row = [cum_col[s].T for s in seqs]
    dt_row = [dt[s].T for s in seqs]

    from_start, to_end, chunk_decay = [], [], []
    for s in seqs:
        last = cum_col[s][cl - 1:cl, :]
        per_head = jnp.concatenate([jnp.exp(cum_col[s]), jnp.exp(last - cum_col[s]) * dt[s],
                                    jnp.broadcast_to(jnp.exp(last), (SUBLANES, LANES))], axis=0)
        wide = _expand_heads(per_head, exp_ref)
        from_start.append(wide[:cl])
        to_end.append(wide[cl:2 * cl])
        chunk_decay.append(wide[2 * cl:2 * cl + 1])

    z = [_dot(xn[s], wz_ref[...]) for s in seqs]
    xs16 = [xs[s].astype(BF16) for s in seqs]
    cbs = []
    for s in seqs:
        xw16 = (xs[s] * to_end[s]).astype(BF16)
        bm_t = bm[s].T
        st = h_sc[s]
        st16 = st.astype(BF16)
        cb_s, state_terms, updates = [], [], []
        for g in range(SSM_GROUPS):
            sl = slice(g * SSM_STATE, (g + 1) * SSM_STATE)
            gl = slice(g * group_width, (g + 1) * group_width)
            cm_g = cm[s][:, sl].astype(BF16)
            cb_s.append(_dot_nt(cm_g, bm[s][:, sl].astype(BF16)))
            state_terms.append(_dot(cm_g, st16[:, gl]))
            updates.append(_dot(bm_t[sl, :].astype(BF16), xw16[:, gl]))
        cbs.append(cb_s)
        h_sc[s] = st * chunk_decay[s] + jnp.concatenate(updates, axis=1)
        y_sc[s] = jnp.concatenate(state_terms, axis=1) * from_start[s]

    for h in range(nheads):
        hl = slice(h * hp, (h + 1) * hp)
        for s in seqs:
            seg = jnp.exp(jnp.where(tril, cum_col[s][:, h:h + 1] - cum_row[s][h:h + 1, :], -jnp.inf))
            wts = cbs[s][h // heads_per_group] * seg * dt_row[s][h:h + 1, :]
            y_sc[s, :, hl] = y_sc[s, :, hl] + _dot(wts.astype(BF16), xs16[s][:, hl])

    for s in seqs:
        y = _ssd_gate_norm(y_sc[s], xs[s], z[s], dsk_ref, nw_ref).astype(BF16)
        xo_ref[s] = x[s] + _rms(_dot(y, wout_ref[...]) + other_ref[s], gpost_ref[...])

    @pl.when(c == nchunks - 1)
    def _():
        pad = jnp.zeros((LANES - hn, LANES), F32)
        for s in seqs:
            histout_ref[s] = hist_sc[s]
            final = h_sc[s]
            for j in range(d_inner // LANES):
                tile_t = jnp.concatenate([final[:, j * LANES:(j + 1) * LANES], pad], axis=0).T
                for r in range(LANES // hp):
                    hout_ref[s, j * (LANES // hp) + r] = tile_t[r * hp:(r + 1) * hp, :hn]


def ssd_prompt(x, g_pre, w_z, w_xbc, w_dt, other, w_out, g_post, conv_w, conv_b, dt_bias, a_log, d_skip, norm_w,
               batch, seq):
    t, d = x.shape
    d_inner = w_z.shape[1]
    conv_dim = w_xbc.shape[1]
    nheads = d_inner // SSM_HEAD_DIM
    cl = SSM_CHUNK
    nchunks = seq // cl
    ns = SSM_SEQS_PER_STEP
    rows = pl.BlockSpec((ns, cl, d), lambda b, c: (b, c, 0))
    xo, hout, hist = pl.pallas_call(
        _ssd_prompt_kernel,
        grid=(batch // ns, nchunks),
        in_specs=[rows, _resident((1, d)), _resident(w_z.shape), _resident(w_xbc.shape),
                  _resident(w_dt.shape), rows, _resident_rows(w_out, 0, d_inner), _resident((1, d)),
                  _resident(conv_w.shape), _resident((1, conv_dim)), _resident((1, LANES)),
                  _resident((1, LANES)), _resident((1, d_inner)), _resident((1, d_inner)),
                  _resident((LANES, d_inner))],
        out_specs=[rows,
                   pl.BlockSpec((ns, nheads, SSM_HEAD_DIM, SSM_STATE), lambda b, c: (b, 0, 0, 0)),
                   pl.BlockSpec((ns, SUBLANES, conv_dim), lambda b, c: (b, 0, 0))],
        out_shape=[jax.ShapeDtypeStruct((batch, seq, d), F32),
                   jax.ShapeDtypeStruct((batch, nheads, SSM_HEAD_DIM, SSM_STATE), F32),
                   jax.ShapeDtypeStruct((batch, SUBLANES, conv_dim), F32)],
        scratch_shapes=[pltpu.VMEM((ns, SUBLANES, conv_dim), F32),
                        pltpu.VMEM((ns, SSM_STATE, d_inner), F32),
                        pltpu.VMEM((ns, cl + SUBLANES, conv_dim), F32),
                        pltpu.VMEM((ns, cl, d_inner), F32)],
        compiler_params=_params(("arbitrary", "arbitrary")),
        name="ssd_prompt",
    )(x.reshape(batch, seq, d), g_pre, w_z, w_xbc, w_dt, other.reshape(batch, seq, d), w_out, g_post,
      conv_w, conv_b, dt_bias, a_log, d_skip, norm_w, jnp.asarray(_head_expansion(nheads, SSM_HEAD_DIM), BF16))
    return xo.reshape(t, d), hout, hist


def _ssd_step_kernel(z_ref, xnew_ref, h0_ref, h1_ref, h2_ref, dt_ref, cw_ref, cb_ref, dtb_ref, alog_ref,
                     dsk_ref, nw_ref, st_ref, y_ref, stout_ref,
                     xs_sc, xst_sc, bmt_sc, cmt_sc, dtt_sc, dat_sc, yt_sc):
    h = pl.program_id(0)
    nheads = pl.num_programs(0)
    d_inner = xs_sc.shape[1]
    hp, hn = SSM_HEAD_DIM, SSM_STATE
    heads_per_group = d_inner // hp // SSM_GROUPS

    @pl.when(h == 0)
    def _():
        conv = (cb_ref[...] + h0_ref[...] * cw_ref[0:1, :] + h1_ref[...] * cw_ref[1:2, :]
                + h2_ref[...] * cw_ref[2:3, :] + xnew_ref[...] * cw_ref[3:4, :])
        act = _silu(conv)
        xs = act[:, :d_inner]
        dt = jax.nn.softplus(dt_ref[...] + dtb_ref[...])
        xs_sc[...] = xs
        xst_sc[...] = xs.T
        bmt_sc[...] = act[:, d_inner:d_inner + SSM_GROUPS * hn].T
        cmt_sc[...] = act[:, d_inner + SSM_GROUPS * hn:].T
        dtt_sc[...] = dt.T
        dat_sc[...] = jnp.exp(dt * (-jnp.exp(alog_ref[...]))).T

    g = h // heads_per_group
    bmt = bmt_sc[pl.ds(pl.multiple_of(g * hn, hn), hn), :]
    cmt = cmt_sc[pl.ds(pl.multiple_of(g * hn, hn), hn), :]
    da = dat_sc[pl.ds(h, 1), :]
    dtx = xst_sc[pl.ds(pl.multiple_of(h * hp, hp), hp), :] * dtt_sc[pl.ds(h, 1), :]
    rows_per_tile = LANES // hn
    for j in range(hp // rows_per_tile):
        tile = st_ref[:, j * LANES:(j + 1) * LANES].T
        news = []
        for r in range(rows_per_tile):
            p = j * rows_per_tile + r
            new = da * tile[r * hn:(r + 1) * hn, :] + dtx[p:p + 1, :] * bmt
            yt_sc[pl.ds(h * hp + p, 1), :] = jnp.sum(new * cmt, axis=0, keepdims=True)
            news.append(new)
        stout_ref[:, j * LANES:(j + 1) * LANES] = jnp.concatenate(news, axis=0).T

    @pl.when(h == nheads - 1)
    def _():
        y_ref[...] = _ssd_gate_norm(yt_sc[...].T, xs_sc[...], z_ref[...], dsk_ref, nw_ref).astype(y_ref.dtype)


def ssd_step(z, xnew, hist, dt, conv_w, conv_b, dt_bias, a_log, d_skip, norm_w, state):
    b, d_inner = z.shape
    conv_dim = xnew.shape[1]
    nheads = d_inner // SSM_HEAD_DIM
    per_head = SSM_HEAD_DIM * SSM_STATE
    full = lambda shape: pl.BlockSpec(shape, lambda h: (0, 0))
    y, st = pl.pallas_call(
        _ssd_step_kernel,
        grid=(nheads,),
        in_specs=[full((b, d_inner))] + [full((b, conv_dim))] * 4 + [full((b, LANES))]
        + [full(conv_w.shape), full((1, conv_dim)), full((1, LANES)), full((1, LANES)),
           full((1, d_inner)), full((1, d_inner)),
           pl.BlockSpec((b, per_head), lambda h: (0, h))],
        out_specs=[full((b, d_inner)), pl.BlockSpec((b, per_head), lambda h: (0, h))],
        out_shape=[jax.ShapeDtypeStruct((b, d_inner), BF16),
                   jax.ShapeDtypeStruct((b, nheads * per_head), F32)],
        scratch_shapes=[pltpu.VMEM((b, d_inner), F32), pltpu.VMEM((d_inner, b), F32),
                        pltpu.VMEM((SSM_GROUPS * SSM_STATE, b), F32),
                        pltpu.VMEM((SSM_GROUPS * SSM_STATE, b), F32),
                        pltpu.VMEM((LANES, b), F32), pltpu.VMEM((LANES, b), F32),
                        pltpu.VMEM((d_inner, b), F32)],
        compiler_params=_params(("arbitrary",)),
        name="ssd_step",
    )(z, xnew, hist[0], hist[1], hist[2], dt, conv_w, conv_b, dt_bias, a_log, d_skip, norm_w,
      state.reshape(b, nheads * per_head))
    return y, st.reshape(b, nheads, SSM_HEAD_DIM, SSM_STATE)


def _hgrn_lower_bound(lbp_ref, layer):
    raw = lbp_ref[...]
    e = jnp.exp(raw - jnp.max(raw, axis=0, keepdims=True))
    return jnp.sum(e[:layer + 1], axis=0, keepdims=True) / jnp.sum(e, axis=0, keepdims=True)


def _hgrn_gates(q_raw, f_raw, lb):
    q = _silu(q_raw)
    f = lb + (1.0 - lb) * jax.nn.sigmoid(f_raw)
    k = 1.0 - f
    return q, f, k


def _pair_levels(cl):
    t = np.arange(cl)[:, None]
    s = np.arange(cl)[None, :]
    lvl = np.floor(np.log2(np.maximum(t ^ s, 1))).astype(np.int32)
    return np.where(t > s, lvl, np.where(t == s, -1, -2)).astype(np.int32)


HG_HEADS_PER_DOT = 2
HG_GROUP = 8


def _hgrn_prompt_kernel(layer, x_ref, gpre_ref, win_ref, wout_ref, lvl_ref, lbp_ref, nw_ref,
                        m_ref, sout_ref, st_sc):
    c = pl.program_id(1)
    nchunks = pl.num_programs(1)
    cl = x_ref.shape[0]
    nheads = st_sc.shape[0]
    width = nheads * HG_DIM
    span = HG_HEADS_PER_DOT * HG_DIM

    @pl.when(c == 0)
    def _():
        st_sc[...] = jnp.zeros_like(st_sc)

    lb_all = _hgrn_lower_bound(lbp_ref, layer)
    row = lax.broadcasted_iota(jnp.int32, (cl, LANES), 0)
    lvl = lvl_ref[...]
    sub_row = lax.broadcasted_iota(jnp.int32, (SUBLANES, LANES), 0)
    xn = _rms(x_ref[...], gpre_ref[...]).astype(BF16)

    def project(j):
        return [_dot(xn, win_ref[:, which * width + j * span:which * width + (j + 1) * span])
                for which in range(4)]

    acc = None
    for g0 in range(0, nheads, HG_GROUP):
        heads = range(g0, g0 + HG_GROUP)
        raw = [{}, {}, {}, {}]
        for j in range(g0 // HG_HEADS_PER_DOT, (g0 + HG_GROUP) // HG_HEADS_PER_DOT):
            for which, both in enumerate(project(j)):
                for hh in range(HG_HEADS_PER_DOT):
                    raw[which][j * HG_HEADS_PER_DOT + hh] = both[:, hh * HG_DIM:(hh + 1) * HG_DIM]
        q, k, cum = {}, {}, {}
        for h in heads:
            q[h], fh, k[h] = _hgrn_gates(raw[0][h], raw[1][h], lb_all[:, h * HG_DIM:(h + 1) * HG_DIM])
            logf = jnp.log(fh)
            tiles = []
            before = None
            for m in range(cl // SUBLANES):
                tile = logf[m * SUBLANES:(m + 1) * SUBLANES, :]
                shift = 1
                while shift < SUBLANES:
                    tile = tile + jnp.where(sub_row >= shift, pltpu.roll(tile, shift, axis=0), 0.0)
                    shift *= 2
                if before is not None:
                    tile = tile + before
                before = tile[SUBLANES - 1:SUBLANES, :]
                tiles.append(tile)
            cum[h] = jnp.concatenate(tiles, axis=0)

        att = {h: jnp.where(lvl == -1, _dot_nt(q[h].astype(BF16), k[h].astype(BF16)), 0.0) for h in heads}
        last_of_block = dict(cum)
        blk = 1
        level = 0
        while blk < cl:
            odd = (row & blk) != 0
            for h in heads:
                expo = jnp.where(odd, cum[h] - pltpu.roll(last_of_block[h], blk, axis=0),
                                 last_of_block[h] - cum[h])
                scaled = (jnp.where(odd, q[h], k[h]) * jnp.exp(expo)).astype(BF16)
                att[h] = jnp.where(lvl == level, _dot_nt(scaled, scaled), att[h])
                last_of_block[h] = jnp.where(odd, last_of_block[h],
                                             pltpu.roll(last_of_block[h], cl - blk, axis=0))
            blk *= 2
            level += 1
        cum_last = last_of_block

        outs = []
        for h in heads:
            st = st_sc[h]
            v = raw[2][h]
            o = (_dot(att[h].astype(BF16), v.astype(BF16))
                 + _dot_nt((q[h] * jnp.exp(cum[h])).astype(BF16), st.astype(BF16)))
            k_end = (k[h] * jnp.exp(cum_last[h] - cum[h])).astype(BF16)
            st_sc[h] = st * jnp.exp(cum_last[h]) + _dot(v.T.astype(BF16), k_end)
            outs.append((_rms(o, nw_ref[...]) * _silu(raw[3][h])).astype(BF16))
        part = _dot(jnp.concatenate(outs, axis=1), wout_ref[g0 * HG_DIM:(g0 + HG_GROUP) * HG_DIM, :])
        acc = part if acc is None else acc + part
    m_ref[...] = acc

    @pl.when(c == nchunks - 1)
    def _():
        for h in range(nheads):
            sout_ref[0, h] = st_sc[h].T


def hgrn_prompt(x, g_pre, w_qfig, w_out, lb_param, norm_w, layer, batch, seq):
    t, d = x.shape
    width = w_qfig.shape[1] // 4
    nheads = width // HG_DIM
    cl = HG_CHUNK
    nchunks = seq // cl
    row = lambda b, c: (b * nchunks + c, 0)
    kern = functools.partial(_hgrn_prompt_kernel, layer)
    return pl.pallas_call(
        kern,
        grid=(batch, nchunks),
        in_specs=[pl.BlockSpec((cl, d), row), _resident((1, d)), _resident(w_qfig.shape),
                  _resident_rows(w_out, w_out.shape[0] - width, width),
                  _resident((cl, cl)), _resident(lb_param.shape), _resident((1, HG_DIM))],
        out_specs=[pl.BlockSpec((cl, d), row),
                   pl.BlockSpec((1, nheads, HG_DIM, HG_DIM), lambda b, c: (b, 0, 0, 0))],
        out_shape=[jax.ShapeDtypeStruct((t, d), F32),
                   jax.ShapeDtypeStruct((batch, nheads, HG_DIM, HG_DIM), F32)],
        scratch_shapes=[pltpu.VMEM((nheads, HG_DIM, HG_DIM), F32)],
        compiler_params=_params(("arbitrary", "arbitrary")),
        name="hgrn_prompt",
    )(x, g_pre, w_qfig, w_out, jnp.asarray(_pair_levels(cl)), lb_param, norm_w)


HG_STEP_TOKENS = 8


def _hgrn_step_kernel(layer, qfig_ref, lbp_ref, nw_ref, st_ref, o_ref, stout_ref):
    nheads = st_ref.shape[2]
    raw = lbp_ref[...]
    e = jnp.exp(raw - jnp.max(raw, axis=0, keepdims=True))
    lb = jnp.sum(e[:layer + 1], axis=0) / jnp.sum(e, axis=0)
    pad = jnp.zeros((HG_DIM - 2 * nheads, HG_DIM), F32)
    for bl in range(st_ref.shape[1]):
        t = qfig_ref[bl]
        q, f, k = _hgrn_gates(t[0:nheads], t[nheads:2 * nheads], lb)
        v = t[2 * nheads:3 * nheads]
        cols = jnp.concatenate([f, k, pad], axis=0).T
        q16 = q.astype(BF16)
        rows = []
        for h in range(nheads):
            f_col = cols[:, h:h + 1]
            k_col = cols[:, nheads + h:nheads + h + 1]
            new = f_col * st_ref[0, bl, h] + k_col * v[h:h + 1, :]
            stout_ref[bl, h] = new
            rows.append(_dot(q16, new.astype(BF16))[h:h + 1, :])
        o = jnp.concatenate(rows, axis=0)
        o_ref[bl] = _rms(o, nw_ref[...]) * _silu(t[3 * nheads:])


def hgrn_step(qfig, lb_param, norm_w, layer, state_all):
    b, w4 = qfig.shape
    width = w4 // 4
    nheads = width // HG_DIM
    bb = HG_STEP_TOKENS
    kern = functools.partial(_hgrn_step_kernel, layer)
    o, st = pl.pallas_call(
        kern,
        grid=(b // bb,),
        in_specs=[pl.BlockSpec((bb, 4 * nheads, HG_DIM), lambda i: (i, 0, 0)),
                  _resident((lb_param.shape[0], nheads, HG_DIM)), _resident((1, HG_DIM)),
                  pl.BlockSpec((1, bb, nheads, HG_DIM, HG_DIM), lambda i: (layer, i, 0, 0, 0))],
        out_specs=[pl.BlockSpec((bb, nheads, HG_DIM), lambda i: (i, 0, 0)),
                   pl.BlockSpec((bb, nheads, HG_DIM, HG_DIM), lambda i: (i, 0, 0, 0))],
        out_shape=[jax.ShapeDtypeStruct((b, nheads, HG_DIM), F32),
                   jax.ShapeDtypeStruct((b, nheads, HG_DIM, HG_DIM), F32)],
        compiler_params=_params(("parallel",)),
        name="hgrn_step",
    )(qfig.reshape(b, 4 * nheads, HG_DIM), lb_param.reshape(-1, nheads, HG_DIM), norm_w, state_all)
    return o.reshape(b, width), st


def _s5_discretize_kernel(lre_ref, lim_ref, step_ref, bre_ref, bim_ref, are_ref, aim_ref, bbre_ref, bbim_ref):
    lre, lim = lre_ref[...], lim_ref[...]
    step = jnp.exp(step_ref[...])
    mag = jnp.exp(lre * step)
    are = mag * jnp.cos(lim * step)
    aim = mag * jnp.sin(lim * step)
    den = lre * lre + lim * lim
    nr = are - 1.0
    cre = (nr * lre + aim * lim) / den
    cim = (aim * lre - nr * lim) / den
    are_ref[...] = are
    aim_ref[...] = aim
    bre, bim = bre_ref[...], bim_ref[...]
    bbre_ref[...] = cre * bre - cim * bim
    bbim_ref[...] = cre * bim + cim * bre


def s5_discretize(lam_re, lam_im, log_step, b_re_t, b_im_t):
    g, n = lam_re.shape
    c = b_re_t.shape[1]
    return pl.pallas_call(
        _s5_discretize_kernel,
        out_shape=[jax.ShapeDtypeStruct((g, 1, n), F32)] * 2 + [jax.ShapeDtypeStruct((g, c, n), F32)] * 2,
        name="s5_discretize",
    )(lam_re.reshape(g, 1, n), lam_im.reshape(g, 1, n), log_step.reshape(g, 1, 1), b_re_t, b_im_t)


def _s5_output(xre, xim, u, cre_ref, cim_ref, d_ref):
    y = _dot(xre.astype(BF16), cre_ref[0]) - _dot(xim.astype(BF16), cim_ref[0]) + d_ref[...] * u
    return jax.nn.gelu(y)


S5_ROWS = 256
S5_CHUNK = 256


def _s5_prompt_kernel(u_ref, b_ref, cre_ref, cim_ref, d_ref, are_ref, aim_ref,
                      y_ref, sre_ref, sim_ref, x_sc, ui_sc, st_sc):
    c = pl.program_id(2)
    nseq, tc = u_ref.shape[0], u_ref.shape[1]
    ns = S5_BLOCK_STATE
    nt = ns // LANES
    steps = S5_ROWS // nseq
    nslabs = tc // steps

    @pl.when(c == 0)
    def _():
        st_sc[...] = jnp.zeros_like(st_sc)

    def slab_rows(s):
        return slice(s * S5_ROWS, (s + 1) * S5_ROWS)

    def project(s):
        r = slab_rows(s)
        ui = jnp.concatenate([u_ref[:, i, :] for i in range(s * steps, (s + 1) * steps)], axis=0)
        ui_sc[r, :] = ui
        bu = _dot(ui.astype(BF16), b_ref[0])
        for k in range(2 * nt):
            x_sc[k, r, :] = bu[:, k * LANES:(k + 1) * LANES]

    def lane_tile(ref, k):
        return ref[:, k * LANES:(k + 1) * LANES]

    are = [jnp.broadcast_to(lane_tile(are_ref, k), (nseq, LANES)) for k in range(nt)]
    aim = [jnp.broadcast_to(lane_tile(aim_ref, k), (nseq, LANES)) for k in range(nt)]

    def scan_slab(s, carry):
        for i in range(s * steps, (s + 1) * steps):
            rows = slice(i * nseq, (i + 1) * nseq)
            new = []
            for k in range(nt):
                sre, sim = carry[k]
                nre = are[k] * sre - aim[k] * sim + x_sc[k, rows, :]
                nim = are[k] * sim + aim[k] * sre + x_sc[nt + k, rows, :]
                x_sc[k, rows, :] = nre
                x_sc[nt + k, rows, :] = nim
                new.append((nre, nim))
            carry = tuple(new)
        return carry

    def output(s):
        r = slab_rows(s)
        xre = jnp.concatenate([x_sc[k, r, :] for k in range(nt)], axis=1)
        xim = jnp.concatenate([x_sc[nt + k, r, :] for k in range(nt)], axis=1)
        y = _s5_output(xre, xim, ui_sc[r, :], cre_ref, cim_ref, d_ref)
        for ii in range(steps):
            y_ref[:, s * steps + ii, :] = y[ii * nseq:(ii + 1) * nseq, :]

    state = tuple((st_sc[k], st_sc[nt + k]) for k in range(nt))
    project(0)
    for s in range(nslabs):
        if s + 1 < nslabs:
            project(s + 1)
        if s > 0:
            output(s - 1)
        state = scan_slab(s, state)
    output(nslabs - 1)
    for k in range(nt):
        st_sc[k], st_sc[nt + k] = state[k]
        sre_ref[0, :, k * LANES:(k + 1) * LANES] = state[k][0]
        sim_ref[0, :, k * LANES:(k + 1) * LANES] = state[k][1]


def s5_prompt(u, b_exp, c_re_exp, c_im_exp, d_skip, a_re, a_im, batch, seq):
    t, d = u.shape
    nblk = d // LANES
    ns = S5_BLOCK_STATE
    tc = S5_CHUNK
    groups = batch // SUBLANES
    rows = pl.BlockSpec((SUBLANES, tc, LANES), lambda j, g, c: (g, c, j))
    state = pl.BlockSpec((1, SUBLANES, ns), lambda j, g, c: (g, 0, j))
    y, s_re, s_im = pl.pallas_call(
        _s5_prompt_kernel,
        grid=(nblk, groups, seq // tc),
        in_specs=[rows,
                  pl.BlockSpec((1, LANES, 2 * ns), lambda j, g, c: (j, 0, 0)),
                  pl.BlockSpec((1, ns, LANES), lambda j, g, c: (j, 0, 0)),
                  pl.BlockSpec((1, ns, LANES), lambda j, g, c: (j, 0, 0)),
                  pl.BlockSpec((1, LANES), lambda j, g, c: (0, j)),
                  pl.BlockSpec((1, ns), lambda j, g, c: (0, j)),
                  pl.BlockSpec((1, ns), lambda j, g, c: (0, j))],
        out_specs=[rows, state, state],
        out_shape=[jax.ShapeDtypeStruct((batch, seq, d), F32),
                   jax.ShapeDtypeStruct((groups, SUBLANES, nblk * ns), F32),
                   jax.ShapeDtypeStruct((groups, SUBLANES, nblk * ns), F32)],
        scratch_shapes=[pltpu.VMEM((2 * ns // LANES, SUBLANES * tc, LANES), F32),
                        pltpu.VMEM((SUBLANES * tc, LANES), F32),
                        pltpu.VMEM((2 * ns // LANES, SUBLANES, LANES), F32)],
        compiler_params=_params(("arbitrary", "arbitrary", "arbitrary")),
        name="s5_prompt",
    )(u.reshape(batch, seq, d), b_exp, c_re_exp, c_im_exp, d_skip, a_re, a_im)
    return y.reshape(t, d), s_re, s_im


def _s5_step_kernel(u_ref, b_ref, cre_ref, cim_ref, d_ref, are_ref, aim_ref, s0re_ref, s0im_ref,
                    y_ref, sre_ref, sim_ref):
    ns = S5_BLOCK_STATE
    u = u_ref[...]
    bu = _dot(u.astype(BF16), b_ref[0])
    are, aim = are_ref[...], aim_ref[...]
    s0re, s0im = s0re_ref[...], s0im_ref[...]
    xre = bu[:, 0:ns] + (are * s0re - aim * s0im)
    xim = bu[:, ns:2 * ns] + (are * s0im + aim * s0re)
    sre_ref[...] = xre
    sim_ref[...] = xim
    y_ref[...] = _s5_output(xre, xim, u, cre_ref, cim_ref, d_ref)


def s5_step(u, b_exp, c_re_exp, c_im_exp, d_skip, a_re, a_im, s0_re, s0_im):
    b, d = u.shape
    nblk = d // LANES
    ns = S5_BLOCK_STATE
    return pl.pallas_call(
        _s5_step_kernel,
        grid=(nblk,),
        in_specs=[pl.BlockSpec((b, LANES), lambda j: (0, j)),
                  pl.BlockSpec((1, LANES, 2 * ns), lambda j: (j, 0, 0)),
                  pl.BlockSpec((1, ns, LANES), lambda j: (j, 0, 0)),
                  pl.BlockSpec((1, ns, LANES), lambda j: (j, 0, 0)),
                  pl.BlockSpec((1, LANES), lambda j: (0, j)),
                  pl.BlockSpec((1, ns), lambda j: (0, j)),
                  pl.BlockSpec((1, ns), lambda j: (0, j)),
                  pl.BlockSpec((b, ns), lambda j: (0, j)),
                  pl.BlockSpec((b, ns), lambda j: (0, j))],
        out_specs=[pl.BlockSpec((b, LANES), lambda j: (0, j)),
                   pl.BlockSpec((b, ns), lambda j: (0, j)),
                   pl.BlockSpec((b, ns), lambda j: (0, j))],
        out_shape=[jax.ShapeDtypeStruct((b, d), F32),
                   jax.ShapeDtypeStruct((b, nblk * ns), F32),
                   jax.ShapeDtypeStruct((b, nblk * ns), F32)],
        compiler_params=_params(("parallel",)),
        name="s5_step",
    )(u, b_exp, c_re_exp, c_im_exp, d_skip, a_re, a_im, s0_re, s0_im)


def _block_diag(per_group):
    ngroups, r, c = per_group.shape
    p = S5_GROUPS_PER_BLOCK
    eye = jnp.eye(p, dtype=per_group.dtype)
    tiles = per_group.reshape(ngroups // p, p, r, c)
    return jnp.einsum('bjrc,jk->bjrkc', tiles, eye).reshape(ngroups // p, p * r, p * c)


def kernel(x_prompt, x_sample, state_ssm, state_ssm_conv, state_hgrn, state_s5_re, state_s5_im, state_ffn_conv, norm_mix_pre, norm_mix_post, norm_ffn_pre, norm_ffn_post, ab_in_w, ssm_conv_w, ssm_conv_b, ssm_dt_bias, ssm_a_log, ssm_d, ssm_norm_w, hgrn_lb, hgrn_norm_w, ab_out_w, s5_lam_re, s5_lam_im, s5_log_step, s5_b_re, s5_b_im, s5_c_re, s5_c_im, s5_d, s5_glu_w, ffn_up_w, ffn_conv_w, ffn_conv_b, ffn_down_w):
    batch, seq, d = x_prompt.shape
    dec_batch = x_sample.shape[0]
    depth = norm_mix_pre.shape[0]
    conv_dim = ssm_conv_w.shape[2]
    nheads = ssm_a_log.shape[1]
    d_inner = nheads * SSM_HEAD_DIM
    hg_width = hgrn_lb.shape[1]
    ffn_dim = ffn_down_w.shape[1]
    tm = 256

    def row(v):
        return v.reshape(1, -1)

    def pad_lanes(v):
        return jnp.pad(v, ((0, 0), (0, LANES - v.shape[1])))

    groups = [(x_prompt.reshape(batch * seq, d), True), (x_sample.reshape(dec_batch, d), False)]
    xs = [g[0] for g in groups]
    normed = [None, None]
    up_w_all = ffn_up_w.astype(BF16)
    down_w_all = ffn_down_w.astype(BF16)
    out = {'ssm': [[], []], 'sconv': [[], []], 'hg': [[], []], 's5r': [[], []], 's5i': [[], []], 'fconv': [[], []]}

    for l in range(depth):
        e = l // 2
        if l % 2 == 0:
            w_in = ab_in_w[e]
            o_xbc = d_inner
            o_dt = o_xbc + conv_dim
            o_q = o_dt + nheads
            w_z = w_in[:, :o_xbc].astype(BF16)
            w_xbc = w_in[:, o_xbc:o_dt].astype(BF16)
            w_dt = pad_lanes(w_in[:, o_dt:o_q]).astype(BF16)
            w_qfig = w_in[:, o_q:].astype(BF16)
            w_out = ab_out_w[e].astype(BF16)
            dt_bias = pad_lanes(row(ssm_dt_bias[e]))
            a_log = pad_lanes(row(ssm_a_log[e]))
            d_skip = row(jnp.repeat(ssm_d[e], SSM_HEAD_DIM))
            for gi, (_, is_prompt) in enumerate(groups):
                x = xs[gi]
                t = x.shape[0]
                if is_prompt:
                    m_hg, hg_new = hgrn_prompt(x, row(norm_mix_pre[l]), w_qfig, w_out, hgrn_lb,
                                               row(hgrn_norm_w[e]), e, batch, seq)
                    xs[gi], ssm_new, xbc_tail = ssd_prompt(
                        x, row(norm_mix_pre[l]), w_z, w_xbc, w_dt, m_hg, w_out, row(norm_mix_post[l]),
                        ssm_conv_w[e], row(ssm_conv_b[e]), dt_bias, a_log, d_skip, row(ssm_norm_w[e]), batch, seq)
                    sconv_new = xbc_tail[:, SUBLANES - (SSM_CONV - 1):]
                else:
                    z, xbc, dt, qfig = norm_matmul(x, row(norm_mix_pre[l]), [w_z, w_xbc, w_dt, w_qfig], t)
                    hist = [state_ssm_conv[e][:, k] for k in range(SSM_CONV - 1)]
                    y, ssm_new = ssd_step(z, xbc, hist, dt, ssm_conv_w[e], row(ssm_conv_b[e]), dt_bias, a_log,
                                          d_skip, row(ssm_norm_w[e]), state_ssm[e])
                    o, hg_new = hgrn_step(qfig, hgrn_lb, row(hgrn_norm_w[e]), e, state_hgrn)
                    sconv_new = jnp.stack(hist[1:] + [xbc], axis=1)
                    xs[gi] = matmul_norm_residual([y, o], w_out, row(norm_mix_post[l]), x, t)
                out['ssm'][gi].append(ssm_new)
                out['hg'][gi].append(hg_new)
                out['sconv'][gi].append(sconv_new)
        else:
            ngroups = s5_lam_re.shape[1]
            a_re, a_im, bb_re, bb_im = s5_discretize(
                s5_lam_re[e], s5_lam_im[e], s5_log_step[e],
                jnp.swapaxes(s5_b_re[e], 1, 2), jnp.swapaxes(s5_b_im[e], 1, 2))
            a_re = a_re.reshape(1, ngroups * S5_STATE)
            a_im = a_im.reshape(1, ngroups * S5_STATE)
            b_exp = jnp.concatenate([_block_diag(bb_re), _block_diag(bb_im)], axis=2).astype(BF16)
            c_re_exp = _block_diag(jnp.swapaxes(s5_c_re[e], 1, 2)).astype(BF16)
            c_im_exp = _block_diag(jnp.swapaxes(s5_c_im[e], 1, 2)).astype(BF16)
            glu_w = s5_glu_w[e].astype(BF16)
            for gi, (_, is_prompt) in enumerate(groups):
                x = xs[gi]
                t = x.shape[0]
                u = normed[gi] if normed[gi] is not None else rmsnorm(x, row(norm_mix_pre[l]), min(tm, t))
                if is_prompt:
                    yg, s_re, s_im = s5_prompt(u, b_exp, c_re_exp, c_im_exp, row(s5_d[e]), a_re, a_im, batch, seq)
                else:
                    yg, s_re, s_im = s5_step(u, b_exp, c_re_exp, c_im_exp, row(s5_d[e]), a_re, a_im,
                                             state_s5_re[e].reshape(t, -1), state_s5_im[e].reshape(t, -1))
                out['s5r'][gi].append(s_re.reshape(-1, ngroups, S5_STATE))
                out['s5i'][gi].append(s_im.reshape(-1, ngroups, S5_STATE))
                xs[gi] = matmul_norm_residual([yg], glu_w, row(norm_mix_post[l]), x, min(tm, t), glu=True)

        for gi, (_, is_prompt) in enumerate(groups):
            x = xs[gi]
            t = x.shape[0]
            normed[gi] = None
            if is_prompt:
                g_next = row(norm_mix_pre[l + 1]) if l + 1 < depth and (l + 1) % 2 == 1 else None
                res = ffn_seq(x, row(norm_ffn_pre[l]), up_w_all, ffn_conv_w[l], row(ffn_conv_b[l]), down_w_all,
                              row(norm_ffn_post[l]), g_next, FFN_TILE, seq, l)
                xs[gi], tail = res[0], res[1]
                if g_next is not None:
                    normed[gi] = res[2]
                fconv_new = tail[:, SUBLANES - (FFN_CONV - 1):]
            else:
                (up,) = norm_matmul(x, row(norm_ffn_pre[l]), [up_w_all], t, layer=l)
                old = state_ffn_conv[l]
                xs[gi] = ffn_tail_step(up, old[:, 0], old[:, 1], ffn_conv_w[l], row(ffn_conv_b[l]), down_w_all,
                                       row(norm_ffn_post[l]), x, l)
                fconv_new = jnp.stack([old[:, 1], up], axis=1)
            out['fconv'][gi].append(fconv_new)

    y_prompt = xs[0].reshape(batch, seq, d)
    y_sample = xs[1].reshape(dec_batch, 1, d)
    states = []
    for gi in range(2):
        states += [jnp.stack(out[k][gi]) for k in ('ssm', 'sconv', 'hg', 's5r', 's5i', 'fconv')]
    return (y_prompt, y_sample, *states)
```

```python
import functools
import math

import jax
import jax.numpy as jnp
import numpy as np
from jax import lax
from jax.experimental import pallas as pl
from jax.experimental.pallas import tpu as pltpu

F32 = jnp.float32
BF16 = jnp.bfloat16
EPS = 1e-6

LANES = 128
SUBLANES = 8
VMEM_LIMIT_BYTES = 56 * 1024 * 1024

SSM_HEAD_DIM = 64
SSM_GROUPS = 2
SSM_STATE = 64
SSM_CONV = 4
SSM_CHUNK = 128
HG_DIM = 128
HG_CHUNK = 128
S5_GROUP = 16
S5_STATE = 64
S5_GROUPS_PER_BLOCK = LANES // S5_GROUP
S5_BLOCK_STATE = S5_GROUPS_PER_BLOCK * S5_STATE
FFN_CONV = 3


def _params(sem):
    return pltpu.CompilerParams(dimension_semantics=sem, vmem_limit_bytes=VMEM_LIMIT_BYTES)


def _resident(shape, layer=None):
    if layer is None:
        nd = len(shape)
        return pl.BlockSpec(shape, lambda *_: (0,) * nd, pipeline_mode=pl.Buffered(1))
    rest = tuple(shape[1:])
    return pl.BlockSpec((None,) + rest, lambda *_: (layer,) + (0,) * len(rest), pipeline_mode=pl.Buffered(1))


def _resident_rows(w, first, nrows):
    assert first % nrows == 0
    return pl.BlockSpec((nrows, w.shape[1]), lambda *_: (first // nrows, 0), pipeline_mode=pl.Buffered(1))


def _rms(x, w):
    return x * lax.rsqrt(jnp.mean(x * x, axis=-1, keepdims=True) + EPS) * w


def _silu(x):
    return x * jax.nn.sigmoid(x)


def _dot(a, b):
    return jnp.dot(a, b, preferred_element_type=F32)


def _dot_nt(a, b):
    return lax.dot_general(a, b, (((1,), (1,)), ((), ())), preferred_element_type=F32)


def _dot_tn(a, b):
    return lax.dot_general(a, b, (((0,), (0,)), ((), ())), preferred_element_type=F32)


def _norm_matmul_kernel(n_out, x_ref, g_ref, *refs):
    w_refs, o_refs = refs[:n_out], refs[n_out:]
    xn = _rms(x_ref[...], g_ref[...]).astype(BF16)
    for w_ref, o_ref in zip(w_refs, o_refs):
        n = w_ref.shape[1]
        for c0 in range(0, n, 512):
            c1 = min(n, c0 + 512)
            o_ref[:, c0:c1] = _dot(xn, w_ref[:, c0:c1])


def norm_matmul(x, g, ws, tm, layer=None):
    t, d = x.shape
    kern = functools.partial(_norm_matmul_kernel, len(ws))
    return pl.pallas_call(
        kern,
        grid=(t // tm,),
        in_specs=[pl.BlockSpec((tm, d), lambda i: (i, 0)), _resident((1, d))]
        + [_resident(w.shape, layer) for w in ws],
        out_specs=[pl.BlockSpec((tm, w.shape[-1]), lambda i: (i, 0)) for w in ws],
        out_shape=[jax.ShapeDtypeStruct((t, w.shape[-1]), F32) for w in ws],
        compiler_params=_params(("parallel",)),
        name="norm_matmul",
    )(x, g, *ws)


def _rmsnorm_kernel(x_ref, g_ref, o_ref):
    o_ref[...] = _rms(x_ref[...], g_ref[...])


def rmsnorm(x, g, tm):
    t, d = x.shape
    return pl.pallas_call(
        _rmsnorm_kernel,
        grid=(t // tm,),
        in_specs=[pl.BlockSpec((tm, d), lambda i: (i, 0)), _resident((1, d))],
        out_specs=pl.BlockSpec((tm, d), lambda i: (i, 0)),
        out_shape=jax.ShapeDtypeStruct((t, d), F32),
        compiler_params=_params(("parallel",)),
        name="rmsnorm",
    )(x, g)


def _matmul_norm_residual_kernel(n_in, glu, *refs):
    a_refs, w_refs = refs[:n_in], refs[n_in:2 * n_in]
    g_ref, x_ref, o_ref = refs[2 * n_in:]
    acc = None
    for a_ref, w_ref in zip(a_refs, w_refs):
        part = _dot(a_ref[...].astype(BF16), w_ref[...])
        acc = part if acc is None else acc + part
    if glu:
        d = acc.shape[1] // 2
        acc = acc[:, :d] * jax.nn.sigmoid(acc[:, d:])
    o_ref[...] = x_ref[...] + _rms(acc, g_ref[...])


def matmul_norm_residual(a_list, w, g, x, tm, glu=False):
    t, d = x.shape
    kern = functools.partial(_matmul_norm_residual_kernel, len(a_list), glu)
    firsts = [sum(a.shape[1] for a in a_list[:i]) for i in range(len(a_list))]
    return pl.pallas_call(
        kern,
        grid=(t // tm,),
        in_specs=[pl.BlockSpec((tm, a.shape[1]), lambda i: (i, 0)) for a in a_list]
        + [_resident_rows(w, first, a.shape[1]) for a, first in zip(a_list, firsts)]
        + [_resident((1, d)), pl.BlockSpec((tm, d), lambda i: (i, 0))],
        out_specs=pl.BlockSpec((tm, d), lambda i: (i, 0)),
        out_shape=jax.ShapeDtypeStruct((t, d), F32),
        compiler_params=_params(("parallel",)),
        name="matmul_norm_residual",
    )(*a_list, *([w] * len(a_list)), g, x)


FFN_CHUNK = 256
FFN_TILE = 256
GLU_TILE = 512


def _ffn_tail(up_ref, xm2_of, xm1_of, cw_ref, cb_ref, dw_ref, g_ref, x_ref, o_ref):
    f = dw_ref.shape[0]
    acc = jnp.zeros(o_ref.shape, F32)
    for c0 in range(0, f, FFN_CHUNK):
        c1 = c0 + FFN_CHUNK
        halves = []
        for off in (0, f):
            a, b = c0 + off, c1 + off
            halves.append(cb_ref[:, a:b] + xm2_of(a, b) * cw_ref[0:1, a:b]
                          + xm1_of(a, b) * cw_ref[1:2, a:b] + up_ref[:, a:b] * cw_ref[2:3, a:b])
        act = (halves[0] * _silu(halves[1])).astype(BF16)
        acc = acc + _dot(act, dw_ref[c0:c1, :])
    o_ref[...] = x_ref[...] + _rms(acc, g_ref[...])


def _ffn_seq_kernel(tiles_per_seq, emit_next, x_ref, g1_ref, uw_ref, cw_ref, cb_ref, dw_ref, g2_ref, *rest):
    if emit_next:
        g3_ref, o_ref, tail_ref, n_ref, carry_sc = rest
    else:
        o_ref, tail_ref, carry_sc = rest
    tm = x_ref.shape[0]
    f = dw_ref.shape[0]

    @pl.when(pl.program_id(0) % tiles_per_seq == 0)
    def _():
        carry_sc[...] = jnp.zeros_like(carry_sc)

    x = x_ref[...]
    xn = _rms(x, g1_ref[...]).astype(BF16)
    top_row = lax.broadcasted_iota(jnp.int32, (SUBLANES, FFN_CHUNK), 0)

    def earlier(up, prev, k):
        rolled = pltpu.roll(up, k, axis=0)
        top = jnp.where(top_row < k, pltpu.roll(prev, k, axis=0), rolled[:SUBLANES])
        return jnp.concatenate([top, rolled[SUBLANES:]], axis=0)

    def project(c0):
        return [_dot(xn, uw_ref[:, c0 + off:c0 + off + FFN_CHUNK]) for off in (0, f)]

    acc = jnp.zeros(o_ref.shape, F32)
    ups_next = project(0)
    act_prev = None
    for c0 in range(0, f, FFN_CHUNK):
        ups = ups_next
        if c0 + FFN_CHUNK < f:
            ups_next = project(c0 + FFN_CHUNK)
        if act_prev is not None:
            acc = acc + _dot(act_prev, dw_ref[c0 - FFN_CHUNK:c0, :])
        halves = []
        for up, off in zip(ups, (0, f)):
            a, b = c0 + off, c0 + off + FFN_CHUNK
            prev = carry_sc[:, a:b]
            carry_sc[:, a:b] = up[tm - SUBLANES:, :]
            tail_ref[0, :, a:b] = up[tm - SUBLANES:, :]
            halves.append(cb_ref[:, a:b] + earlier(up, prev, 2) * cw_ref[0:1, a:b]
                          + earlier(up, prev, 1) * cw_ref[1:2, a:b] + up * cw_ref[2:3, a:b])
        act_prev = (halves[0] * _silu(halves[1])).astype(BF16)
    acc = acc + _dot(act_prev, dw_ref[f - FFN_CHUNK:f, :])
    out = x + _rms(acc, g2_ref[...])
    o_ref[...] = out
    if emit_next:
        n_ref[...] = _rms(out, g3_ref[...])


def ffn_seq(x, g_pre, up_w, conv_w, conv_b, down_w, g_post, g_next, tm, seq, layer):
    t, d = x.shape
    f2 = up_w.shape[-1]
    tiles_per_seq = seq // tm
    emit_next = g_next is not None
    kern = functools.partial(_ffn_seq_kernel, tiles_per_seq, emit_next)
    tile = pl.BlockSpec((tm, d), lambda i: (i, 0))
    return pl.pallas_call(
        kern,
        grid=(t // tm,),
        in_specs=[tile, _resident((1, d)), _resident(up_w.shape, layer), _resident(conv_w.shape),
                  _resident((1, f2)), _resident(down_w.shape, layer), _resident((1, d))]
        + [_resident((1, d))] * emit_next,
        out_specs=[tile, pl.BlockSpec((1, SUBLANES, f2), lambda i: (i // tiles_per_seq, 0, 0))]
        + [tile] * emit_next,
        out_shape=[jax.ShapeDtypeStruct((t, d), F32), jax.ShapeDtypeStruct((t // seq, SUBLANES, f2), F32)]
        + [jax.ShapeDtypeStruct((t, d), F32)] * emit_next,
        scratch_shapes=[pltpu.VMEM((SUBLANES, f2), F32)],
        compiler_params=_params(("arbitrary",)),
        name="ffn_seq",
    )(x, g_pre, up_w, conv_w, conv_b, down_w, g_post, *([g_next] * emit_next))


def _ffn_step_kernel(up_ref, xm2_ref, xm1_ref, cw_ref, cb_ref, dw_ref, g_ref, x_ref, o_ref):
    _ffn_tail(up_ref, lambda a, b: xm2_ref[:, a:b], lambda a, b: xm1_ref[:, a:b],
              cw_ref, cb_ref, dw_ref, g_ref, x_ref, o_ref)


def ffn_tail_step(up, xm2, xm1, conv_w, conv_b, down_w, g, x, layer):
    t, d = x.shape
    f2 = up.shape[1]
    return pl.pallas_call(
        _ffn_step_kernel,
        grid=(1,),
        in_specs=[_resident((t, f2))] * 3
        + [_resident(conv_w.shape), _resident((1, f2)), _resident(down_w.shape, layer),
           _resident((1, d)), _resident((t, d))],
        out_specs=pl.BlockSpec((t, d), lambda i: (0, 0)),
        out_shape=jax.ShapeDtypeStruct((t, d), F32),
        compiler_params=_params(("arbitrary",)),
        name="ffn_tail_step",
    )(up, xm2, xm1, conv_w, conv_b, down_w, g, x)


def _ssd_gate_norm(y, xs, z, dsk_ref, nw_ref):
    y = y + dsk_ref[...] * xs
    return _rms(y * _silu(z), nw_ref[...])


SSM_SEQS_PER_STEP = 4


def _head_expansion(nheads, head_dim):
    e = np.zeros((LANES, nheads * head_dim), np.float32)
    for h in range(nheads):
        e[h, h * head_dim:(h + 1) * head_dim] = 1.0
    return e


def _expand_heads(v, exp_ref):
    hi = v.astype(BF16)
    lo = (v - hi.astype(F32)).astype(BF16)
    return _dot(hi, exp_ref[...]) + _dot(lo, exp_ref[...])


def _ssd_prompt_kernel(x_ref, gpre_ref, wz_ref, wxbc_ref, wdt_ref, other_ref, wout_ref, gpost_ref,
                       cw_ref, cb_ref, dtb_ref, alog_ref, dsk_ref, nw_ref, exp_ref,
                       xo_ref, hout_ref, histout_ref, hist_sc, h_sc, xp_sc, y_sc):
    c = pl.program_id(1)
    nchunks = pl.num_programs(1)
    nseq, cl = x_ref.shape[0], x_ref.shape[1]
    hn, d_inner = h_sc.shape[1], h_sc.shape[2]
    hp = SSM_HEAD_DIM
    nheads = d_inner // hp
    heads_per_group = nheads // SSM_GROUPS
    group_width = heads_per_group * hp
    seqs = range(nseq)

    @pl.when(c == 0)
    def _():
        hist_sc[...] = jnp.zeros_like(hist_sc)
        h_sc[...] = jnp.zeros_like(h_sc)

    x = [x_ref[s] for s in seqs]
    xn = [_rms(x[s], gpre_ref[...]).astype(BF16) for s in seqs]
    xs, bm, cm = [], [], []
    for s in seqs:
        raw = _dot(xn[s], wxbc_ref[...])
        xp_sc[s, 0:SUBLANES, :] = hist_sc[s]
        xp_sc[s, SUBLANES:, :] = raw
        hist_sc[s] = raw[cl - SUBLANES:, :]
        conv = cb_ref[...]
        for k in range(SSM_CONV):
            conv = conv + xp_sc[s, pl.ds(SUBLANES - (SSM_CONV - 1) + k, cl), :] * cw_ref[k:k + 1, :]
        act = _silu(conv)
        xs.append(act[:, :d_inner])
        bm.append(act[:, d_inner:d_inner + SSM_GROUPS * SSM_STATE])
        cm.append(act[:, d_inner + SSM_GROUPS * SSM_STATE:])

    ti = lax.broadcasted_iota(jnp.int32, (cl, cl), 0)
    si = lax.broadcasted_iota(jnp.int32, (cl, cl), 1)
    tril = ti >= si
    tril_f = tril.astype(F32)
    dt, cum_col = [], []
    for s in seqs:
        dt.append(jax.nn.softplus(_dot(xn[s], wdt_ref[...]) + dtb_ref[...]))
        da = dt[s] * (-jnp.exp(alog_ref[...]))
        cum_col.append(jnp.dot(tril_f, da, preferred_element_type=F32,
                               precision=lax.Precision.HIGHEST))
    cum_row = [cum_col[s].T for s in seqs]
    dt_row = [dt[s].T for s in seqs]

    from_start, to_end, chunk_decay = [], [], []
    for s in seqs:
        last = cum_col[s][cl - 1:cl, :]
        per_head = jnp.concatenate([jnp.exp(cum_col[s]), jnp.exp(last - cum_col[s]) * dt[s],
                                    jnp.broadcast_to(jnp.exp(last), (SUBLANES, LANES))], axis=0)
        wide = _expand_heads(per_head, exp_ref)
        from_start.append(wide[:cl])
        to_end.append(wide[cl:2 * cl])
        chunk_decay.append(wide[2 * cl:2 * cl + 1])

    z = [_dot(xn[s], wz_ref[...]) for s in seqs]
    xs16 = [xs[s].astype(BF16) for s in seqs]
    cbs = []
    for s in seqs:
        xw16 = (xs[s] * to_end[s]).astype(BF16)
        bm_t = bm[s].T
        st = h_sc[s]
        st16 = st.astype(BF16)
        cb_s, state_terms, updates = [], [], []
        for g in range(SSM_GROUPS):
            sl = slice(g * SSM_STATE, (g + 1) * SSM_STATE)
            gl = slice(g * group_width, (g + 1) * group_width)
            cm_g = cm[s][:, sl].astype(BF16)
            cb_s.append(_dot_nt(cm_g, bm[s][:, sl].astype(BF16)))
            state_terms.append(_dot(cm_g, st16[:, gl]))
            updates.append(_dot(bm_t[sl, :].astype(BF16), xw16[:, gl]))
        cbs.append(cb_s)
        h_sc[s] = st * chunk_decay[s] + jnp.concatenate(updates, axis=1)
        y_sc[s] = jnp.concatenate(state_terms, axis=1) * from_start[s]

    for h in range(nheads):
        hl = slice(h * hp, (h + 1) * hp)
        for s in seqs:
            seg = jnp.exp(jnp.where(tril, cum_col[s][:, h:h + 1] - cum_row[s][h:h + 1, :], -jnp.inf))
            wts = cbs[s][h // heads_per_group] * seg * dt_row[s][h:h + 1, :]
            y_sc[s, :, hl] = y_sc[s, :, hl] + _dot(wts.astype(BF16), xs16[s][:, hl])

    for s in seqs:
        y = _ssd_gate_norm(y_sc[s], xs[s], z[s], dsk_ref, nw_ref).astype(BF16)
        xo_ref[s] = x[s] + _rms(_dot(y, wout_ref[...]) + other_ref[s], gpost_ref[...])

    @pl.when(c == nchunks - 1)
    def _():
        pad = jnp.zeros((LANES - hn, LANES), F32)
        for s in seqs:
            histout_ref[s] = hist_sc[s]
            final = h_sc[s]
            for j in range(d_inner // LANES):
                tile_t = jnp.concatenate([final[:, j * LANES:(j + 1) * LANES], pad], axis=0).T
                for r in range(LANES // hp):
                    hout_ref[s, j * (LANES // hp) + r] = tile_t[r * hp:(r + 1) * hp, :hn]


def ssd_prompt(x, g_pre, w_z, w_xbc, w_dt, other, w_out, g_post, conv_w, conv_b, dt_bias, a_log, d_skip, norm_w,
               batch, seq):
    t, d = x.shape
    d_inner = w_z.shape[1]
    conv_dim = w_xbc.shape[1]
    nheads = d_inner // SSM_HEAD_DIM
    cl = SSM_CHUNK
    nchunks = seq // cl
    ns = SSM_SEQS_PER_STEP
    rows = pl.BlockSpec((ns, cl, d), lambda b, c: (b, c, 0))
    xo, hout, hist = pl.pallas_call(
        _ssd_prompt_kernel,
        grid=(batch // ns, nchunks),
        in_specs=[rows, _resident((1, d)), _resident(w_z.shape), _resident(w_xbc.shape),
                  _resident(w_dt.shape), rows, _resident_rows(w_out, 0, d_inner), _resident((1, d)),
                  _resident(conv_w.shape), _resident((1, conv_dim)), _resident((1, LANES)),
                  _resident((1, LANES)), _resident((1, d_inner)), _resident((1, d_inner)),
                  _resident((LANES, d_inner))],
        out_specs=[rows,
                   pl.BlockSpec((ns, nheads, SSM_HEAD_DIM, SSM_STATE), lambda b, c: (b, 0, 0, 0)),
                   pl.BlockSpec((ns, SUBLANES, conv_dim), lambda b, c: (b, 0, 0))],
        out_shape=[jax.ShapeDtypeStruct((batch, seq, d), F32),
                   jax.ShapeDtypeStruct((batch, nheads, SSM_HEAD_DIM, SSM_STATE), F32),
                   jax.ShapeDtypeStruct((batch, SUBLANES, conv_dim), F32)],
        scratch_shapes=[pltpu.VMEM((ns, SUBLANES, conv_dim), F32),
                        pltpu.VMEM((ns, SSM_STATE, d_inner), F32),
                        pltpu.VMEM((ns, cl + SUBLANES, conv_dim), F32),
                        pltpu.VMEM((ns, cl, d_inner), F32)],
        compiler_params=_params(("arbitrary", "arbitrary")),
        name="ssd_prompt",
    )(x.reshape(batch, seq, d), g_pre, w_z, w_xbc, w_dt, other.reshape(batch, seq, d), w_out, g_post,
      conv_w, conv_b, dt_bias, a_log, d_skip, norm_w, jnp.asarray(_head_expansion(nheads, SSM_HEAD_DIM), BF16))
    return xo.reshape(t, d), hout, hist


def _ssd_step_kernel(z_ref, xnew_ref, h0_ref, h1_ref, h2_ref, dt_ref, cw_ref, cb_ref, dtb_ref, alog_ref,
                     dsk_ref, nw_ref, st_ref, y_ref, stout_ref,
                     xs_sc, xst_sc, bmt_sc, cmt_sc, dtt_sc, dat_sc, yt_sc):
    h = pl.program_id(0)
    nheads = pl.num_programs(0)
    d_inner = xs_sc.shape[1]
    hp, hn = SSM_HEAD_DIM, SSM_STATE
    heads_per_group = d_inner // hp // SSM_GROUPS

    @pl.when(h == 0)
    def _():
        conv = (cb_ref[...] + h0_ref[...] * cw_ref[0:1, :] + h1_ref[...] * cw_ref[1:2, :]
                + h2_ref[...] * cw_ref[2:3, :] + xnew_ref[...] * cw_ref[3:4, :])
        act = _silu(conv)
        xs = act[:, :d_inner]
        dt = jax.nn.softplus(dt_ref[...] + dtb_ref[...])
        xs_sc[...] = xs
        xst_sc[...] = xs.T
        bmt_sc[...] = act[:, d_inner:d_inner + SSM_GROUPS * hn].T
        cmt_sc[...] = act[:, d_inner + SSM_GROUPS * hn:].T
        dtt_sc[...] = dt.T
        dat_sc[...] = jnp.exp(dt * (-jnp.exp(alog_ref[...]))).T

    g = h // heads_per_group
    bmt = bmt_sc[pl.ds(pl.multiple_of(g * hn, hn), hn), :]
    cmt = cmt_sc[pl.ds(pl.multiple_of(g * hn, hn), hn), :]
    da = dat_sc[pl.ds(h, 1), :]
    dtx = xst_sc[pl.ds(pl.multiple_of(h * hp, hp), hp), :] * dtt_sc[pl.ds(h, 1), :]
    rows_per_tile = LANES // hn
    for j in range(hp // rows_per_tile):
        tile = st_ref[:, j * LANES:(j + 1) * LANES].T
        news = []
        for r in range(rows_per_tile):
            p = j * rows_per_tile + r
            new = da * tile[r * hn:(r + 1) * hn, :] + dtx[p:p + 1, :] * bmt
            yt_sc[pl.ds(h * hp + p, 1), :] = jnp.sum(new * cmt, axis=0, keepdims=True)
            news.append(new)
        stout_ref[:, j * LANES:(j + 1) * LANES] = jnp.concatenate(news, axis=0).T

    @pl.when(h == nheads - 1)
    def _():
        y_ref[...] = _ssd_gate_norm(yt_sc[...].T, xs_sc[...], z_ref[...], dsk_ref, nw_ref).astype(y_ref.dtype)


def ssd_step(z, xnew, hist, dt, conv_w, conv_b, dt_bias, a_log, d_skip, norm_w, state):
    b, d_inner = z.shape
    conv_dim = xnew.shape[1]
    nheads = d_inner // SSM_HEAD_DIM
    per_head = SSM_HEAD_DIM * SSM_STATE
    full = lambda shape: pl.BlockSpec(shape, lambda h: (0, 0))
    y, st = pl.pallas_call(
        _ssd_step_kernel,
        grid=(nheads,),
        in_specs=[full((b, d_inner))] + [full((b, conv_dim))] * 4 + [full((b, LANES))]
        + [full(conv_w.shape), full((1, conv_dim)), full((1, LANES)), full((1, LANES)),
           full((1, d_inner)), full((1, d_inner)),
           pl.BlockSpec((b, per_head), lambda h: (0, h))],
        out_specs=[full((b, d_inner)), pl.BlockSpec((b, per_head), lambda h: (0, h))],
        out_shape=[jax.ShapeDtypeStruct((b, d_inner), BF16),
                   jax.ShapeDtypeStruct((b, nheads * per_head), F32)],
        scratch_shapes=[pltpu.VMEM((b, d_inner), F32), pltpu.VMEM((d_inner, b), F32),
                        pltpu.VMEM((SSM_GROUPS * SSM_STATE, b), F32),
                        pltpu.VMEM((SSM_GROUPS * SSM_STATE, b), F32),
                        pltpu.VMEM((LANES, b), F32), pltpu.VMEM((LANES, b), F32),
                        pltpu.VMEM((d_inner, b), F32)],
        compiler_params=_params(("arbitrary",)),
        name="ssd_step",
    )(z, xnew, hist[0], hist[1], hist[2], dt, conv_w, conv_b, dt_bias, a_log, d_skip, norm_w,
      state.reshape(b, nheads * per_head))
    return y, st.reshape(b, nheads, SSM_HEAD_DIM, SSM_STATE)


def _hgrn_lower_bound(lbp_ref, layer):
    raw = lbp_ref[...]
    e = jnp.exp(raw - jnp.max(raw, axis=0, keepdims=True))
    return jnp.sum(e[:layer + 1], axis=0, keepdims=True) / jnp.sum(e, axis=0, keepdims=True)


def _hgrn_gates(q_raw, f_raw, lb):
    q = _silu(q_raw)
    f = lb + (1.0 - lb) * jax.nn.sigmoid(f_raw)
    k = 1.0 - f
    return q, f, k


def _pair_levels(cl):
    t = np.arange(cl)[:, None]
    s = np.arange(cl)[None, :]
    lvl = np.floor(np.log2(np.maximum(t ^ s, 1))).astype(np.int32)
    return np.where(t > s, lvl, np.where(t == s, -1, -2)).astype(np.int32)


HG_HEADS_PER_DOT = 2
HG_GROUP = 8


def _hgrn_prompt_kernel(layer, x_ref, gpre_ref, win_ref, wout_ref, lvl_ref, lbp_ref, nw_ref,
                        m_ref, sout_ref, st_sc):
    c = pl.program_id(1)
    nchunks = pl.num_programs(1)
    cl = x_ref.shape[0]
    nheads = st_sc.shape[0]
    width = nheads * HG_DIM
    span = HG_HEADS_PER_DOT * HG_DIM

    @pl.when(c == 0)
    def _():
        st_sc[...] = jnp.zeros_like(st_sc)

    lb_all = _hgrn_lower_bound(lbp_ref, layer)
    row = lax.broadcasted_iota(jnp.int32, (cl, LANES), 0)
    lvl = lvl_ref[...]
    xn = _rms(x_ref[...], gpre_ref[...]).astype(BF16)

    def project(j):
        return [_dot(xn, win_ref[:, which * width + j * span:which * width + (j + 1) * span])
                for which in range(4)]

    acc = None
    for g0 in range(0, nheads, HG_GROUP):
        heads = range(g0, g0 + HG_GROUP)
        raw = [{}, {}, {}, {}]
        for j in range(g0 // HG_HEADS_PER_DOT, (g0 + HG_GROUP) // HG_HEADS_PER_DOT):
            for which, both in enumerate(project(j)):
                for hh in range(HG_HEADS_PER_DOT):
                    raw[which][j * HG_HEADS_PER_DOT + hh] = both[:, hh * HG_DIM:(hh + 1) * HG_DIM]
        q, k, cum = {}, {}, {}
        for h in heads:
            q[h], fh, k[h] = _hgrn_gates(raw[0][h], raw[1][h], lb_all[:, h * HG_DIM:(h + 1) * HG_DIM])
            total = jnp.log(fh)
            shift = 1
            while shift < cl:
                total = total + jnp.where(row >= shift, pltpu.roll(total, shift, axis=0), 0.0)
                shift *= 2
            cum[h] = total

        att = {h: jnp.where(lvl == -1, _dot_nt(q[h].astype(BF16), k[h].astype(BF16)), 0.0) for h in heads}
        last_of_block = dict(cum)
        blk = 1
        level = 0
        while blk < cl:
            odd = (row & blk) != 0
            for h in heads:
                expo = jnp.where(odd, cum[h] - pltpu.roll(last_of_block[h], blk, axis=0),
                                 last_of_block[h] - cum[h])
                scaled = (jnp.where(odd, q[h], k[h]) * jnp.exp(expo)).astype(BF16)
                att[h] = jnp.where(lvl == level, _dot_nt(scaled, scaled), att[h])
                last_of_block[h] = jnp.where(odd, last_of_block[h],
                                             pltpu.roll(last_of_block[h], cl - blk, axis=0))
            blk *= 2
            level += 1
        cum_last = last_of_block

        outs = []
        for h in heads:
            st = st_sc[h]
            v = raw[2][h]
            o = (_dot(att[h].astype(BF16), v.astype(BF16))
                 + _dot_nt((q[h] * jnp.exp(cum[h])).astype(BF16), st.astype(BF16)))
            k_end = (k[h] * jnp.exp(cum_last[h] - cum[h])).astype(BF16)
            st_sc[h] = st * jnp.exp(cum_last[h]) + _dot(v.T.astype(BF16), k_end)
            outs.append((_rms(o, nw_ref[...]) * _silu(raw[3][h])).astype(BF16))
        part = _dot(jnp.concatenate(outs, axis=1), wout_ref[g0 * HG_DIM:(g0 + HG_GROUP) * HG_DIM, :])
        acc = part if acc is None else acc + part
    m_ref[...] = acc

    @pl.when(c == nchunks - 1)
    def _():
        for h in range(nheads):
            sout_ref[0, h] = st_sc[h].T


def hgrn_prompt(x, g_pre, w_qfig, w_out, lb_param, norm_w, layer, batch, seq):
    t, d = x.shape
    width = w_qfig.shape[1] // 4
    nheads = width // HG_DIM
    cl = HG_CHUNK
    nchunks = seq // cl
    row = lambda b, c: (b * nchunks + c, 0)
    kern = functools.partial(_hgrn_prompt_kernel, layer)
    return pl.pallas_call(
        kern,
        grid=(batch, nchunks),
        in_specs=[pl.BlockSpec((cl, d), row), _resident((1, d)), _resident(w_qfig.shape),
                  _resident_rows(w_out, w_out.shape[0] - width, width),
                  _resident((cl, cl)), _resident(lb_param.shape), _resident((1, HG_DIM))],
        out_specs=[pl.BlockSpec((cl, d), row),
                   pl.BlockSpec((1, nheads, HG_DIM, HG_DIM), lambda b, c: (b, 0, 0, 0))],
        out_shape=[jax.ShapeDtypeStruct((t, d), F32),
                   jax.ShapeDtypeStruct((batch, nheads, HG_DIM, HG_DIM), F32)],
        scratch_shapes=[pltpu.VMEM((nheads, HG_DIM, HG_DIM), F32)],
        compiler_params=_params(("arbitrary", "arbitrary")),
        name="hgrn_prompt",
    )(x, g_pre, w_qfig, w_out, jnp.asarray(_pair_levels(cl)), lb_param, norm_w)


HG_STEP_TOKENS = 8


def _hgrn_step_kernel(layer, qfig_ref, lbp_ref, nw_ref, st_ref, o_ref, stout_ref):
    nheads = st_ref.shape[2]
    raw = lbp_ref[...]
    e = jnp.exp(raw - jnp.max(raw, axis=0, keepdims=True))
    lb = jnp.sum(e[:layer + 1], axis=0) / jnp.sum(e, axis=0)
    pad = jnp.zeros((HG_DIM - 2 * nheads, HG_DIM), F32)
    for bl in range(st_ref.shape[1]):
        t = qfig_ref[bl]
        q, f, k = _hgrn_gates(t[0:nheads], t[nheads:2 * nheads], lb)
        v = t[2 * nheads:3 * nheads]
        cols = jnp.concatenate([f, k, pad], axis=0).T
        q16 = q.astype(BF16)
        rows = []
        for h in range(nheads):
            f_col = cols[:, h:h + 1]
            k_col = cols[:, nheads + h:nheads + h + 1]
            new = f_col * st_ref[0, bl, h] + k_col * v[h:h + 1, :]
            stout_ref[bl, h] = new
            rows.append(_dot(q16, new.astype(BF16))[h:h + 1, :])
        o = jnp.concatenate(rows, axis=0)
        o_ref[bl] = _rms(o, nw_ref[...]) * _silu(t[3 * nheads:])


def hgrn_step(qfig, lb_param, norm_w, layer, state_all):
    b, w4 = qfig.shape
    width = w4 // 4
    nheads = width // HG_DIM
    bb = HG_STEP_TOKENS
    kern = functools.partial(_hgrn_step_kernel, layer)
    o, st = pl.pallas_call(
        kern,
        grid=(b // bb,),
        in_specs=[pl.BlockSpec((bb, 4 * nheads, HG_DIM), lambda i: (i, 0, 0)),
                  _resident((lb_param.shape[0], nheads, HG_DIM)), _resident((1, HG_DIM)),
                  pl.BlockSpec((1, bb, nheads, HG_DIM, HG_DIM), lambda i: (layer, i, 0, 0, 0))],
        out_specs=[pl.BlockSpec((bb, nheads, HG_DIM), lambda i: (i, 0, 0)),
                   pl.BlockSpec((bb, nheads, HG_DIM, HG_DIM), lambda i: (i, 0, 0, 0))],
        out_shape=[jax.ShapeDtypeStruct((b, nheads, HG_DIM), F32),
                   jax.ShapeDtypeStruct((b, nheads, HG_DIM, HG_DIM), F32)],
        compiler_params=_params(("parallel",)),
        name="hgrn_step",
    )(qfig.reshape(b, 4 * nheads, HG_DIM), lb_param.reshape(-1, nheads, HG_DIM), norm_w, state_all)
    return o.reshape(b, width), st


def _s5_discretize_kernel(lre_ref, lim_ref, step_ref, bre_ref, bim_ref, are_ref, aim_ref, bbre_ref, bbim_ref):
    lre, lim = lre_ref[...], lim_ref[...]
    step = jnp.exp(step_ref[...])
    mag = jnp.exp(lre * step)
    are = mag * jnp.cos(lim * step)
    aim = mag * jnp.sin(lim * step)
    den = lre * lre + lim * lim
    nr = are - 1.0
    cre = (nr * lre + aim * lim) / den
    cim = (aim * lre - nr * lim) / den
    are_ref[...] = are
    aim_ref[...] = aim
    bre, bim = bre_ref[...], bim_ref[...]
    bbre_ref[...] = cre * bre - cim * bim
    bbim_ref[...] = cre * bim + cim * bre


def s5_discretize(lam_re, lam_im, log_step, b_re_t, b_im_t):
    g, n = lam_re.shape
    c = b_re_t.shape[1]
    return pl.pallas_call(
        _s5_discretize_kernel,
        out_shape=[jax.ShapeDtypeStruct((g, 1, n), F32)] * 2 + [jax.ShapeDtypeStruct((g, c, n), F32)] * 2,
        name="s5_discretize",
    )(lam_re.reshape(g, 1, n), lam_im.reshape(g, 1, n), log_step.reshape(g, 1, 1), b_re_t, b_im_t)


def _s5_output(xre, xim, u, cre_ref, cim_ref, d_ref):
    y = _dot(xre.astype(BF16), cre_ref[0]) - _dot(xim.astype(BF16), cim_ref[0]) + d_ref[...] * u
    return jax.nn.gelu(y)


S5_ROWS = 256
S5_CHUNK = 256


def _s5_prompt_kernel(u_ref, b_ref, cre_ref, cim_ref, d_ref, are_ref, aim_ref,
                      y_ref, sre_ref, sim_ref, x_sc, ui_sc, st_sc):
    c = pl.program_id(2)
    nseq, tc = u_ref.shape[0], u_ref.shape[1]
    ns = S5_BLOCK_STATE
    nt = ns // LANES
    steps = S5_ROWS // nseq
    nslabs = tc // steps

    @pl.when(c == 0)
    def _():
        st_sc[...] = jnp.zeros_like(st_sc)

    def slab_rows(s):
        return slice(s * S5_ROWS, (s + 1) * S5_ROWS)

    def project(s):
        r = slab_rows(s)
        ui = jnp.concatenate([u_ref[:, i, :] for i in range(s * steps, (s + 1) * steps)], axis=0)
        ui_sc[r, :] = ui
        bu = _dot(ui.astype(BF16), b_ref[0])
        for k in range(2 * nt):
            x_sc[k, r, :] = bu[:, k * LANES:(k + 1) * LANES]

    def lane_tile(ref, k):
        return ref[:, k * LANES:(k + 1) * LANES]

    are = [jnp.broadcast_to(lane_tile(are_ref, k), (nseq, LANES)) for k in range(nt)]
    aim = [jnp.broadcast_to(lane_tile(aim_ref, k), (nseq, LANES)) for k in range(nt)]

    def scan_slab(s, carry):
        for i in range(s * steps, (s + 1) * steps):
            rows = slice(i * nseq, (i + 1) * nseq)
            new = []
            for k in range(nt):
                sre, sim = carry[k]
                nre = are[k] * sre - aim[k] * sim + x_sc[k, rows, :]
                nim = are[k] * sim + aim[k] * sre + x_sc[nt + k, rows, :]
                x_sc[k, rows, :] = nre
                x_sc[nt + k, rows, :] = nim
                new.append((nre, nim))
            carry = tuple(new)
        return carry

    def output(s):
        r = slab_rows(s)
        xre = jnp.concatenate([x_sc[k, r, :] for k in range(nt)], axis=1)
        xim = jnp.concatenate([x_sc[nt + k, r, :] for k in range(nt)], axis=1)
        y = _s5_output(xre, xim, ui_sc[r, :], cre_ref, cim_ref, d_ref)
        for ii in range(steps):
            y_ref[:, s * steps + ii, :] = y[ii * nseq:(ii + 1) * nseq, :]

    state = tuple((st_sc[k], st_sc[nt + k]) for k in range(nt))
    project(0)
    for s in range(nslabs):
        if s + 1 < nslabs:
            project(s + 1)
        if s > 0:
            output(s - 1)
        state = scan_slab(s, state)
    output(nslabs - 1)
    for k in range(nt):
        st_sc[k], st_sc[nt + k] = state[k]
        sre_ref[0, :, k * LANES:(k + 1) * LANES] = state[k][0]
        sim_ref[0, :, k * LANES:(k + 1) * LANES] = state[k][1]


def s5_prompt(u, b_exp, c_re_exp, c_im_exp, d_skip, a_re, a_im, batch, seq):
    t, d = u.shape
    nblk = d // LANES
    ns = S5_BLOCK_STATE
    tc = S5_CHUNK
    groups = batch // SUBLANES
    rows = pl.BlockSpec((SUBLANES, tc, LANES), lambda j, g, c: (g, c, j))
    state = pl.BlockSpec((1, SUBLANES, ns), lambda j, g, c: (g, 0, j))
    y, s_re, s_im = pl.pallas_call(
        _s5_prompt_kernel,
        grid=(nblk, groups, seq // tc),
        in_specs=[rows,
                  pl.BlockSpec((1, LANES, 2 * ns), lambda j, g, c: (j, 0, 0)),
                  pl.BlockSpec((1, ns, LANES), lambda j, g, c: (j, 0, 0)),
                  pl.BlockSpec((1, ns, LANES), lambda j, g, c: (j, 0, 0)),
                  pl.BlockSpec((1, LANES), lambda j, g, c: (0, j)),
                  pl.BlockSpec((1, ns), lambda j, g, c: (0, j)),
                  pl.BlockSpec((1, ns), lambda j, g, c: (0, j))],
        out_specs=[rows, state, state],
        out_shape=[jax.ShapeDtypeStruct((batch, seq, d), F32),
                   jax.ShapeDtypeStruct((groups, SUBLANES, nblk * ns), F32),
                   jax.ShapeDtypeStruct((groups, SUBLANES, nblk * ns), F32)],
        scratch_shapes=[pltpu.VMEM((2 * ns // LANES, SUBLANES * tc, LANES), F32),
                        pltpu.VMEM((SUBLANES * tc, LANES), F32),
                        pltpu.VMEM((2 * ns // LANES, SUBLANES, LANES), F32)],
        compiler_params=_params(("arbitrary", "arbitrary", "arbitrary")),
        name="s5_prompt",
    )(u.reshape(batch, seq, d), b_exp, c_re_exp, c_im_exp, d_skip, a_re, a_im)
    return y.reshape(t, d), s_re, s_im


def _s5_step_kernel(u_ref, b_ref, cre_ref, cim_ref, d_ref, are_ref, aim_ref, s0re_ref, s0im_ref,
                    y_ref, sre_ref, sim_ref):
    ns = S5_BLOCK_STATE
    u = u_ref[...]
    bu = _dot(u.astype(BF16), b_ref[0])
    are, aim = are_ref[...], aim_ref[...]
    s0re, s0im = s0re_ref[...], s0im_ref[...]
    xre = bu[:, 0:ns] + (are * s0re - aim * s0im)
    xim = bu[:, ns:2 * ns] + (are * s0im + aim * s0re)
    sre_ref[...] = xre
    sim_ref[...] = xim
    y_ref[...] = _s5_output(xre, xim, u, cre_ref, cim_ref, d_ref)


def s5_step(u, b_exp, c_re_exp, c_im_exp, d_skip, a_re, a_im, s0_re, s0_im):
    b, d = u.shape
    nblk = d // LANES
    ns = S5_BLOCK_STATE
    return pl.pallas_call(
        _s5_step_kernel,
        grid=(nblk,),
        in_specs=[pl.BlockSpec((b, LANES), lambda j: (0, j)),
                  pl.BlockSpec((1, LANES, 2 * ns), lambda j: (j, 0, 0)),
                  pl.BlockSpec((1, ns, LANES), lambda j: (j, 0, 0)),
                  pl.BlockSpec((1, ns, LANES), lambda j: (j, 0, 0)),
                  pl.BlockSpec((1, LANES), lambda j: (0, j)),
                  pl.BlockSpec((1, ns), lambda j: (0, j)),
                  pl.BlockSpec((1, ns), lambda j: (0, j)),
                  pl.BlockSpec((b, ns), lambda j: (0, j)),
                  pl.BlockSpec((b, ns), lambda j: (0, j))],
        out_specs=[pl.BlockSpec((b, LANES), lambda j: (0, j)),
                   pl.BlockSpec((b, ns), lambda j: (0, j)),
                   pl.BlockSpec((b, ns), lambda j: (0, j))],
        out_shape=[jax.ShapeDtypeStruct((b, d), F32),
                   jax.ShapeDtypeStruct((b, nblk * ns), F32),
                   jax.ShapeDtypeStruct((b, nblk * ns), F32)],
        compiler_params=_params(("parallel",)),
        name="s5_step",
    )(u, b_exp, c_re_exp, c_im_exp, d_skip, a_re, a_im, s0_re, s0_im)


def _block_diag(per_group):
    ngroups, r, c = per_group.shape
    p = S5_GROUPS_PER_BLOCK
    eye = jnp.eye(p, dtype=per_group.dtype)
    tiles = per_group.reshape(ngroups // p, p, r, c)
    return jnp.einsum('bjrc,jk->bjrkc', tiles, eye).reshape(ngroups // p, p * r, p * c)


def kernel(x_prompt, x_sample, state_ssm, state_ssm_conv, state_hgrn, state_s5_re, state_s5_im, state_ffn_conv, norm_mix_pre, norm_mix_post, norm_ffn_pre, norm_ffn_post, ab_in_w, ssm_conv_w, ssm_conv_b, ssm_dt_bias, ssm_a_log, ssm_d, ssm_norm_w, hgrn_lb, hgrn_norm_w, ab_out_w, s5_lam_re, s5_lam_im, s5_log_step, s5_b_re, s5_b_im, s5_c_re, s5_c_im, s5_d, s5_glu_w, ffn_up_w, ffn_conv_w, ffn_conv_b, ffn_down_w):
    batch, seq, d = x_prompt.shape
    dec_batch = x_sample.shape[0]
    depth = norm_mix_pre.shape[0]
    conv_dim = ssm_conv_w.shape[2]
    nheads = ssm_a_log.shape[1]
    d_inner = nheads * SSM_HEAD_DIM
    hg_width = hgrn_lb.shape[1]
    ffn_dim = ffn_down_w.shape[1]
    tm = 256

    def row(v):
        return v.reshape(1, -1)

    def pad_lanes(v):
        return jnp.pad(v, ((0, 0), (0, LANES - v.shape[1])))

    groups = [(x_prompt.reshape(batch * seq, d), True), (x_sample.reshape(dec_batch, d), False)]
    xs = [g[0] for g in groups]
    normed = [None, None]
    up_w_all = ffn_up_w.astype(BF16)
    down_w_all = ffn_down_w.astype(BF16)
    out = {'ssm': [[], []], 'sconv': [[], []], 'hg': [[], []], 's5r': [[], []], 's5i': [[], []], 'fconv': [[], []]}

    for l in range(depth):
        e = l // 2
        if l % 2 == 0:
            w_in = ab_in_w[e]
            o_xbc = d_inner
            o_dt = o_xbc + conv_dim
            o_q = o_dt + nheads
            w_z = w_in[:, :o_xbc].astype(BF16)
            w_xbc = w_in[:, o_xbc:o_dt].astype(BF16)
            w_dt = pad_lanes(w_in[:, o_dt:o_q]).astype(BF16)
            w_qfig = w_in[:, o_q:].astype(BF16)
            w_out = ab_out_w[e].astype(BF16)
            dt_bias = pad_lanes(row(ssm_dt_bias[e]))
            a_log = pad_lanes(row(ssm_a_log[e]))
            d_skip = row(jnp.repeat(ssm_d[e], SSM_HEAD_DIM))
            for gi, (_, is_prompt) in enumerate(groups):
                x = xs[gi]
                t = x.shape[0]
                if is_prompt:
                    m_hg, hg_new = hgrn_prompt(x, row(norm_mix_pre[l]), w_qfig, w_out, hgrn_lb,
                                               row(hgrn_norm_w[e]), e, batch, seq)
                    xs[gi], ssm_new, xbc_tail = ssd_prompt(
                        x, row(norm_mix_pre[l]), w_z, w_xbc, w_dt, m_hg, w_out, row(norm_mix_post[l]),
                        ssm_conv_w[e], row(ssm_conv_b[e]), dt_bias, a_log, d_skip, row(ssm_norm_w[e]), batch, seq)
                    sconv_new = xbc_tail[:, SUBLANES - (SSM_CONV - 1):]
                else:
                    z, xbc, dt, qfig = norm_matmul(x, row(norm_mix_pre[l]), [w_z, w_xbc, w_dt, w_qfig], t)
                    hist = [state_ssm_conv[e][:, k] for k in range(SSM_CONV - 1)]
                    y, ssm_new = ssd_step(z, xbc, hist, dt, ssm_conv_w[e], row(ssm_conv_b[e]), dt_bias, a_log,
                                          d_skip, row(ssm_norm_w[e]), state_ssm[e])
                    o, hg_new = hgrn_step(qfig, hgrn_lb, row(hgrn_norm_w[e]), e, state_hgrn)
                    sconv_new = jnp.stack(hist[1:] + [xbc], axis=1)
                    xs[gi] = matmul_norm_residual([y, o], w_out, row(norm_mix_post[l]), x, t)
                out['ssm'][gi].append(ssm_new)
                out['hg'][gi].append(hg_new)
                out['sconv'][gi].append(sconv_new)
        else:
            ngroups = s5_lam_re.shape[1]
            a_re, a_im, bb_re, bb_im = s5_discretize(
                s5_lam_re[e], s5_lam_im[e], s5_log_step[e],
                jnp.swapaxes(s5_b_re[e], 1, 2), jnp.swapaxes(s5_b_im[e], 1, 2))
            a_re = a_re.reshape(1, ngroups * S5_STATE)
            a_im = a_im.reshape(1, ngroups * S5_STATE)
            b_exp = jnp.concatenate([_block_diag(bb_re), _block_diag(bb_im)], axis=2).astype(BF16)
            c_re_exp = _block_diag(jnp.swapaxes(s5_c_re[e], 1, 2)).astype(BF16)
            c_im_exp = _block_diag(jnp.swapaxes(s5_c_im[e], 1, 2)).astype(BF16)
            glu_w = s5_glu_w[e].astype(BF16)
            for gi, (_, is_prompt) in enumerate(groups):
                x = xs[gi]
                t = x.shape[0]
                u = normed[gi] if normed[gi] is not None else rmsnorm(x, row(norm_mix_pre[l]), min(tm, t))
                if is_prompt:
                    yg, s_re, s_im = s5_prompt(u, b_exp, c_re_exp, c_im_exp, row(s5_d[e]), a_re, a_im, batch, seq)
                else:
                    yg, s_re, s_im = s5_step(u, b_exp, c_re_exp, c_im_exp, row(s5_d[e]), a_re, a_im,
                                             state_s5_re[e].reshape(t, -1), state_s5_im[e].reshape(t, -1))
                out['s5r'][gi].append(s_re.reshape(-1, ngroups, S5_STATE))
                out['s5i'][gi].append(s_im.reshape(-1, ngroups, S5_STATE))
                xs[gi] = matmul_norm_residual([yg], glu_w, row(norm_mix_post[l]), x, min(GLU_TILE, t), glu=True)

        for gi, (_, is_prompt) in enumerate(groups):
            x = xs[gi]
            t = x.shape[0]
            normed[gi] = None
            if is_prompt:
                g_next = row(norm_mix_pre[l + 1]) if l + 1 < depth and (l + 1) % 2 == 1 else None
                res = ffn_seq(x, row(norm_ffn_pre[l]), up_w_all, ffn_conv_w[l], row(ffn_conv_b[l]), down_w_all,
                              row(norm_ffn_post[l]), g_next, FFN_TILE, seq, l)
                xs[gi], tail = res[0], res[1]
                if g_next is not None:
                    normed[gi] = res[2]
                fconv_new = tail[:, SUBLANES - (FFN_CONV - 1):]
            else:
                (up,) = norm_matmul(x, row(norm_ffn_pre[l]), [up_w_all], t, layer=l)
                old = state_ffn_conv[l]
                xs[gi] = ffn_tail_step(up, old[:, 0], old[:, 1], ffn_conv_w[l], row(ffn_conv_b[l]), down_w_all,
                                       row(norm_ffn_post[l]), x, l)
                fconv_new = jnp.stack([old[:, 1], up], axis=1)
            out['fconv'][gi].append(fconv_new)

    y_prompt = xs[0].reshape(batch, seq, d)
    y_sample = xs[1].reshape(dec_batch, 1, d)
    states = []
    for gi in range(2):
        states += [jnp.stack(out[k][gi]) for k in ('ssm', 'sconv', 'hg', 's5r', 's5i', 'fconv')]
    return (y_prompt, y_sample, *states)
```

```python
import functools

import jax
import jax.numpy as jnp
import numpy as np
from jax import lax
from jax.experimental import pallas as pl
from jax.experimental.pallas import tpu as pltpu

F32 = jnp.float32
BF16 = jnp.bfloat16
EPS = 1e-6

LANES = 128
SUBLANES = 8
VMEM_LIMIT_BYTES = 56 * 1024 * 1024

SSM_HEAD_DIM = 64
SSM_GROUPS = 2
SSM_STATE = 64
SSM_CONV = 4
SSM_CHUNK = 128
HG_DIM = 128
HG_CHUNK = 128
S5_GROUP = 16
S5_STATE = 64
S5_GROUPS_PER_BLOCK = LANES // S5_GROUP
S5_BLOCK_STATE = S5_GROUPS_PER_BLOCK * S5_STATE
FFN_CONV = 3


def _params(sem):
    return pltpu.CompilerParams(dimension_semantics=sem, vmem_limit_bytes=VMEM_LIMIT_BYTES)


def _resident(shape, layer=None):
    if layer is None:
        nd = len(shape)
        return pl.BlockSpec(shape, lambda *_: (0,) * nd, pipeline_mode=pl.Buffered(1))
    rest = tuple(shape[1:])
    return pl.BlockSpec((None,) + rest, lambda *_: (layer,) + (0,) * len(rest), pipeline_mode=pl.Buffered(1))


def _resident_rows(w, first, nrows):
    assert first % nrows == 0
    return pl.BlockSpec((nrows, w.shape[1]), lambda *_: (first // nrows, 0), pipeline_mode=pl.Buffered(1))


def _rms(x, w):
    return x * lax.rsqrt(jnp.mean(x * x, axis=-1, keepdims=True) + EPS) * w


def _silu(x):
    return x * jax.nn.sigmoid(x)


def _dot(a, b):
    return jnp.dot(a, b, preferred_element_type=F32)


def _dot_nt(a, b):
    return lax.dot_general(a, b, (((1,), (1,)), ((), ())), preferred_element_type=F32)


def _dot_tn(a, b):
    return lax.dot_general(a, b, (((0,), (0,)), ((), ())), preferred_element_type=F32)


def _norm_matmul_kernel(n_out, x_ref, g_ref, *refs):
    w_refs, o_refs = refs[:n_out], refs[n_out:]
    xn = _rms(x_ref[...], g_ref[...]).astype(BF16)
    for w_ref, o_ref in zip(w_refs, o_refs):
        n = w_ref.shape[1]
        for c0 in range(0, n, 512):
            c1 = min(n, c0 + 512)
            o_ref[:, c0:c1] = _dot(xn, w_ref[:, c0:c1])


def norm_matmul(x, g, ws, tm, layer=None):
    t, d = x.shape
    kern = functools.partial(_norm_matmul_kernel, len(ws))
    return pl.pallas_call(
        kern,
        grid=(t // tm,),
        in_specs=[pl.BlockSpec((tm, d), lambda i: (i, 0)), _resident((1, d))]
        + [_resident(w.shape, layer) for w in ws],
        out_specs=[pl.BlockSpec((tm, w.shape[-1]), lambda i: (i, 0)) for w in ws],
        out_shape=[jax.ShapeDtypeStruct((t, w.shape[-1]), F32) for w in ws],
        compiler_params=_params(("parallel",)),
        name="norm_matmul",
    )(x, g, *ws)


def _rmsnorm_kernel(x_ref, g_ref, o_ref):
    o_ref[...] = _rms(x_ref[...], g_ref[...])


def rmsnorm(x, g, tm):
    t, d = x.shape
    return pl.pallas_call(
        _rmsnorm_kernel,
        grid=(t // tm,),
        in_specs=[pl.BlockSpec((tm, d), lambda i: (i, 0)), _resident((1, d))],
        out_specs=pl.BlockSpec((tm, d), lambda i: (i, 0)),
        out_shape=jax.ShapeDtypeStruct((t, d), F32),
        compiler_params=_params(("parallel",)),
        name="rmsnorm",
    )(x, g)


def _matmul_norm_residual_kernel(n_in, glu, *refs):
    a_refs, w_refs = refs[:n_in], refs[n_in:2 * n_in]
    g_ref, x_ref, o_ref = refs[2 * n_in:]
    acc = None
    for a_ref, w_ref in zip(a_refs, w_refs):
        part = _dot(a_ref[...].astype(BF16), w_ref[...])
        acc = part if acc is None else acc + part
    if glu:
        d = acc.shape[1] // 2
        acc = acc[:, :d] * jax.nn.sigmoid(acc[:, d:])
    o_ref[...] = x_ref[...] + _rms(acc, g_ref[...])


def matmul_norm_residual(a_list, w, g, x, tm, glu=False):
    t, d = x.shape
    kern = functools.partial(_matmul_norm_residual_kernel, len(a_list), glu)
    firsts = [sum(a.shape[1] for a in a_list[:i]) for i in range(len(a_list))]
    return pl.pallas_call(
        kern,
        grid=(t // tm,),
        in_specs=[pl.BlockSpec((tm, a.shape[1]), lambda i: (i, 0)) for a in a_list]
        + [_resident_rows(w, first, a.shape[1]) for a, first in zip(a_list, firsts)]
        + [_resident((1, d)), pl.BlockSpec((tm, d), lambda i: (i, 0))],
        out_specs=pl.BlockSpec((tm, d), lambda i: (i, 0)),
        out_shape=jax.ShapeDtypeStruct((t, d), F32),
        compiler_params=_params(("parallel",)),
        name="matmul_norm_residual",
    )(*a_list, *([w] * len(a_list)), g, x)


FFN_CHUNK = 256
FFN_TILE = 256
GLU_TILE = 1024


def _ffn_tail(up_ref, xm2_of, xm1_of, cw_ref, cb_ref, dw_ref, g_ref, x_ref, o_ref):
    f = dw_ref.shape[0]
    acc = jnp.zeros(o_ref.shape, F32)
    for c0 in range(0, f, FFN_CHUNK):
        c1 = c0 + FFN_CHUNK
        halves = []
        for off in (0, f):
            a, b = c0 + off, c1 + off
            halves.append(cb_ref[:, a:b] + xm2_of(a, b) * cw_ref[0:1, a:b]
                          + xm1_of(a, b) * cw_ref[1:2, a:b] + up_ref[:, a:b] * cw_ref[2:3, a:b])
        act = (halves[0] * _silu(halves[1])).astype(BF16)
        acc = acc + _dot(act, dw_ref[c0:c1, :])
    o_ref[...] = x_ref[...] + _rms(acc, g_ref[...])


def _ffn_seq_kernel(tiles_per_seq, emit_next, x_ref, g1_ref, uw_ref, cw_ref, cb_ref, dw_ref, g2_ref, *rest):
    if emit_next:
        g3_ref, o_ref, tail_ref, n_ref, carry_sc = rest
    else:
        o_ref, tail_ref, carry_sc = rest
    tm = x_ref.shape[0]
    f = dw_ref.shape[0]

    @pl.when(pl.program_id(0) % tiles_per_seq == 0)
    def _():
        carry_sc[...] = jnp.zeros_like(carry_sc)

    x = x_ref[...]
    xn = _rms(x, g1_ref[...]).astype(BF16)
    top_row = lax.broadcasted_iota(jnp.int32, (SUBLANES, FFN_CHUNK), 0)

    def earlier(up, prev, k):
        rolled = pltpu.roll(up, k, axis=0)
        top = jnp.where(top_row < k, pltpu.roll(prev, k, axis=0), rolled[:SUBLANES])
        return jnp.concatenate([top, rolled[SUBLANES:]], axis=0)

    def project(c0):
        return [_dot(xn, uw_ref[:, c0 + off:c0 + off + FFN_CHUNK]) for off in (0, f)]

    acc = jnp.zeros(o_ref.shape, F32)
    ups_next = project(0)
    act_prev = None
    for c0 in range(0, f, FFN_CHUNK):
        ups = ups_next
        if c0 + FFN_CHUNK < f:
            ups_next = project(c0 + FFN_CHUNK)
        if act_prev is not None:
            acc = acc + _dot(act_prev, dw_ref[c0 - FFN_CHUNK:c0, :])
        halves = []
        for up, off in zip(ups, (0, f)):
            a, b = c0 + off, c0 + off + FFN_CHUNK
            prev = carry_sc[:, a:b]
            carry_sc[:, a:b] = up[tm - SUBLANES:, :]
            tail_ref[0, :, a:b] = up[tm - SUBLANES:, :]
            halves.append(cb_ref[:, a:b] + earlier(up, prev, 2) * cw_ref[0:1, a:b]
                          + earlier(up, prev, 1) * cw_ref[1:2, a:b] + up * cw_ref[2:3, a:b])
        act_prev = (halves[0] * _silu(halves[1])).astype(BF16)
    acc = acc + _dot(act_prev, dw_ref[f - FFN_CHUNK:f, :])
    out = x + _rms(acc, g2_ref[...])
    o_ref[...] = out
    if emit_next:
        n_ref[...] = _rms(out, g3_ref[...])


def ffn_seq(x, g_pre, up_w, conv_w, conv_b, down_w, g_post, g_next, tm, seq, layer):
    t, d = x.shape
    f2 = up_w.shape[-1]
    tiles_per_seq = seq // tm
    emit_next = g_next is not None
    kern = functools.partial(_ffn_seq_kernel, tiles_per_seq, emit_next)
    tile = pl.BlockSpec((tm, d), lambda i: (i, 0))
    return pl.pallas_call(
        kern,
        grid=(t // tm,),
        in_specs=[tile, _resident((1, d)), _resident(up_w.shape, layer), _resident(conv_w.shape),
                  _resident((1, f2)), _resident(down_w.shape, layer), _resident((1, d))]
        + [_resident((1, d))] * emit_next,
        out_specs=[tile, pl.BlockSpec((1, SUBLANES, f2), lambda i: (i // tiles_per_seq, 0, 0))]
        + [tile] * emit_next,
        out_shape=[jax.ShapeDtypeStruct((t, d), F32), jax.ShapeDtypeStruct((t // seq, SUBLANES, f2), F32)]
        + [jax.ShapeDtypeStruct((t, d), F32)] * emit_next,
        scratch_shapes=[pltpu.VMEM((SUBLANES, f2), F32)],
        compiler_params=_params(("arbitrary",)),
        name="ffn_seq",
    )(x, g_pre, up_w, conv_w, conv_b, down_w, g_post, *([g_next] * emit_next))


def _ffn_step_kernel(up_ref, xm2_ref, xm1_ref, cw_ref, cb_ref, dw_ref, g_ref, x_ref, o_ref):
    _ffn_tail(up_ref, lambda a, b: xm2_ref[:, a:b], lambda a, b: xm1_ref[:, a:b],
              cw_ref, cb_ref, dw_ref, g_ref, x_ref, o_ref)


def ffn_tail_step(up, xm2, xm1, conv_w, conv_b, down_w, g, x, layer):
    t, d = x.shape
    f2 = up.shape[1]
    return pl.pallas_call(
        _ffn_step_kernel,
        grid=(1,),
        in_specs=[_resident((t, f2))] * 3
        + [_resident(conv_w.shape), _resident((1, f2)), _resident(down_w.shape, layer),
           _resident((1, d)), _resident((t, d))],
        out_specs=pl.BlockSpec((t, d), lambda i: (0, 0)),
        out_shape=jax.ShapeDtypeStruct((t, d), F32),
        compiler_params=_params(("arbitrary",)),
        name="ffn_tail_step",
    )(up, xm2, xm1, conv_w, conv_b, down_w, g, x)


def _ssd_gate_norm(y, xs, z, dsk_ref, nw_ref):
    y = y + dsk_ref[...] * xs
    return _rms(y * _silu(z), nw_ref[...])


SSM_SEQS_PER_STEP = 4


def _head_expansion(nheads, head_dim):
    e = np.zeros((LANES, nheads * head_dim), np.float32)
    for h in range(nheads):
        e[h, h * head_dim:(h + 1) * head_dim] = 1.0
    return e


def _expand_heads(v, exp_ref):
    hi = v.astype(BF16)
    lo = (v - hi.astype(F32)).astype(BF16)
    return _dot(hi, exp_ref[...]) + _dot(lo, exp_ref[...])


def _ssd_prompt_kernel(x_ref, gpre_ref, wz_ref, wxbc_ref, wdt_ref, other_ref, wout_ref, gpost_ref,
                       cw_ref, cb_ref, dtb_ref, alog_ref, dsk_ref, nw_ref, exp_ref,
                       xo_ref, hout_ref, histout_ref, hist_sc, h_sc, xp_sc, y_sc):
    c = pl.program_id(1)
    nchunks = pl.num_programs(1)
    nseq, cl = x_ref.shape[0], x_ref.shape[1]
    hn, d_inner = h_sc.shape[1], h_sc.shape[2]
    hp = SSM_HEAD_DIM
    nheads = d_inner // hp
    heads_per_group = nheads // SSM_GROUPS
    group_width = heads_per_group * hp
    seqs = range(nseq)

    @pl.when(c == 0)
    def _():
        hist_sc[...] = jnp.zeros_like(hist_sc)
        h_sc[...] = jnp.zeros_like(h_sc)

    x = [x_ref[s] for s in seqs]
    xn = [_rms(x[s], gpre_ref[...]).astype(BF16) for s in seqs]
    xs, bm, cm = [], [], []
    for s in seqs:
        raw = _dot(xn[s], wxbc_ref[...])
        xp_sc[s, 0:SUBLANES, :] = hist_sc[s]
        xp_sc[s, SUBLANES:, :] = raw
        hist_sc[s] = raw[cl - SUBLANES:, :]
        conv = cb_ref[...]
        for k in range(SSM_CONV):
            conv = conv + xp_sc[s, pl.ds(SUBLANES - (SSM_CONV - 1) + k, cl), :] * cw_ref[k:k + 1, :]
        act = _silu(conv)
        xs.append(act[:, :d_inner])
        bm.append(act[:, d_inner:d_inner + SSM_GROUPS * SSM_STATE])
        cm.append(act[:, d_inner + SSM_GROUPS * SSM_STATE:])

    ti = lax.broadcasted_iota(jnp.int32, (cl, cl), 0)
    si = lax.broadcasted_iota(jnp.int32, (cl, cl), 1)
    tril = ti >= si
    tril_f = tril.astype(F32)
    dt, cum_col = [], []
    for s in seqs:
        dt.append(jax.nn.softplus(_dot(xn[s], wdt_ref[...]) + dtb_ref[...]))
        da = dt[s] * (-jnp.exp(alog_ref[...]))
        cum_col.append(jnp.dot(tril_f, da, preferred_element_type=F32,
                               precision=lax.Precision.HIGHEST))
    cum_row = [cum_col[s].T for s in seqs]
    dt_row = [dt[s].T for s in seqs]

    from_start, to_end, chunk_decay = [], [], []
    for s in seqs:
        last = cum_col[s][cl - 1:cl, :]
        per_head = jnp.concatenate([jnp.exp(cum_col[s]), jnp.exp(last - cum_col[s]) * dt[s],
                                    jnp.broadcast_to(jnp.exp(last), (SUBLANES, LANES))], axis=0)
        wide = _expand_heads(per_head, exp_ref)
        from_start.append(wide[:cl])
        to_end.append(wide[cl:2 * cl])
        chunk_decay.append(wide[2 * cl:2 * cl + 1])

    z = [_dot(xn[s], wz_ref[...]) for s in seqs]
    xs16 = [xs[s].astype(BF16) for s in seqs]
    cbs = []
    for s in seqs:
        xw16 = (xs[s] * to_end[s]).astype(BF16)
        bm_t = bm[s].T
        st = h_sc[s]
        st16 = st.astype(BF16)
        cb_s, state_terms, updates = [], [], []
        for g in range(SSM_GROUPS):
            sl = slice(g * SSM_STATE, (g + 1) * SSM_STATE)
            gl = slice(g * group_width, (g + 1) * group_width)
            cm_g = cm[s][:, sl].astype(BF16)
            cb_s.append(_dot_nt(cm_g, bm[s][:, sl].astype(BF16)))
            state_terms.append(_dot(cm_g, st16[:, gl]))
            updates.append(_dot(bm_t[sl, :].astype(BF16), xw16[:, gl]))
        cbs.append(cb_s)
        h_sc[s] = st * chunk_decay[s] + jnp.concatenate(updates, axis=1)
        y_sc[s] = jnp.concatenate(state_terms, axis=1) * from_start[s]

    for h in range(nheads):
        hl = slice(h * hp, (h + 1) * hp)
        for s in seqs:
            seg = jnp.exp(jnp.where(tril, cum_col[s][:, h:h + 1] - cum_row[s][h:h + 1, :], -jnp.inf))
            wts = cbs[s][h // heads_per_group] * seg * dt_row[s][h:h + 1, :]
            y_sc[s, :, hl] = y_sc[s, :, hl] + _dot(wts.astype(BF16), xs16[s][:, hl])

    for s in seqs:
        y = _ssd_gate_norm(y_sc[s], xs[s], z[s], dsk_ref, nw_ref).astype(BF16)
        xo_ref[s] = x[s] + _rms(_dot(y, wout_ref[...]) + other_ref[s], gpost_ref[...])

    @pl.when(c == nchunks - 1)
    def _():
        pad = jnp.zeros((LANES - hn, LANES), F32)
        for s in seqs:
            histout_ref[s] = hist_sc[s]
            final = h_sc[s]
            for j in range(d_inner // LANES):
                tile_t = jnp.concatenate([final[:, j * LANES:(j + 1) * LANES], pad], axis=0).T
                for r in range(LANES // hp):
                    hout_ref[s, j * (LANES // hp) + r] = tile_t[r * hp:(r + 1) * hp, :hn]


def ssd_prompt(x, g_pre, w_z, w_xbc, w_dt, other, w_out, g_post, conv_w, conv_b, dt_bias, a_log, d_skip, norm_w,
               batch, seq):
    t, d = x.shape
    d_inner = w_z.shape[1]
    conv_dim = w_xbc.shape[1]
    nheads = d_inner // SSM_HEAD_DIM
    cl = SSM_CHUNK
    nchunks = seq // cl
    ns = SSM_SEQS_PER_STEP
    rows = pl.BlockSpec((ns, cl, d), lambda b, c: (b, c, 0))
    xo, hout, hist = pl.pallas_call(
        _ssd_prompt_kernel,
        grid=(batch // ns, nchunks),
        in_specs=[rows, _resident((1, d)), _resident(w_z.shape), _resident(w_xbc.shape),
                  _resident(w_dt.shape), rows, _resident_rows(w_out, 0, d_inner), _resident((1, d)),
                  _resident(conv_w.shape), _resident((1, conv_dim)), _resident((1, LANES)),
                  _resident((1, LANES)), _resident((1, d_inner)), _resident((1, d_inner)),
                  _resident((LANES, d_inner))],
        out_specs=[rows,
                   pl.BlockSpec((ns, nheads, SSM_HEAD_DIM, SSM_STATE), lambda b, c: (b, 0, 0, 0)),
                   pl.BlockSpec((ns, SUBLANES, conv_dim), lambda b, c: (b, 0, 0))],
        out_shape=[jax.ShapeDtypeStruct((batch, seq, d), F32),
                   jax.ShapeDtypeStruct((batch, nheads, SSM_HEAD_DIM, SSM_STATE), F32),
                   jax.ShapeDtypeStruct((batch, SUBLANES, conv_dim), F32)],
        scratch_shapes=[pltpu.VMEM((ns, SUBLANES, conv_dim), F32),
                        pltpu.VMEM((ns, SSM_STATE, d_inner), F32),
                        pltpu.VMEM((ns, cl + SUBLANES, conv_dim), F32),
                        pltpu.VMEM((ns, cl, d_inner), F32)],
        compiler_params=_params(("arbitrary", "arbitrary")),
        name="ssd_prompt",
    )(x.reshape(batch, seq, d), g_pre, w_z, w_xbc, w_dt, other.reshape(batch, seq, d), w_out, g_post,
      conv_w, conv_b, dt_bias, a_log, d_skip, norm_w, jnp.asarray(_head_expansion(nheads, SSM_HEAD_DIM), BF16))
    return xo.reshape(t, d), hout, hist


def _ssd_step_kernel(z_ref, xnew_ref, h0_ref, h1_ref, h2_ref, dt_ref, cw_ref, cb_ref, dtb_ref, alog_ref,
                     dsk_ref, nw_ref, st_ref, y_ref, stout_ref,
                     xs_sc, xst_sc, bmt_sc, cmt_sc, dtt_sc, dat_sc, yt_sc):
    h = pl.program_id(0)
    nheads = pl.num_programs(0)
    d_inner = xs_sc.shape[1]
    hp, hn = SSM_HEAD_DIM, SSM_STATE
    heads_per_group = d_inner // hp // SSM_GROUPS

    @pl.when(h == 0)
    def _():
        conv = (cb_ref[...] + h0_ref[...] * cw_ref[0:1, :] + h1_ref[...] * cw_ref[1:2, :]
                + h2_ref[...] * cw_ref[2:3, :] + xnew_ref[...] * cw_ref[3:4, :])
        act = _silu(conv)
        xs = act[:, :d_inner]
        dt = jax.nn.softplus(dt_ref[...] + dtb_ref[...])
        xs_sc[...] = xs
        xst_sc[...] = xs.T
        bmt_sc[...] = act[:, d_inner:d_inner + SSM_GROUPS * hn].T
        cmt_sc[...] = act[:, d_inner + SSM_GROUPS * hn:].T
        dtt_sc[...] = dt.T
        dat_sc[...] = jnp.exp(dt * (-jnp.exp(alog_ref[...]))).T

    g = h // heads_per_group
    bmt = bmt_sc[pl.ds(pl.multiple_of(g * hn, hn), hn), :]
    cmt = cmt_sc[pl.ds(pl.multiple_of(g * hn, hn), hn), :]
    da = dat_sc[pl.ds(h, 1), :]
    dtx = xst_sc[pl.ds(pl.multiple_of(h * hp, hp), hp), :] * dtt_sc[pl.ds(h, 1), :]
    rows_per_tile = LANES // hn
    for j in range(hp // rows_per_tile):
        tile = st_ref[:, j * LANES:(j + 1) * LANES].T
        news = []
        for r in range(rows_per_tile):
            p = j * rows_per_tile + r
            new = da * tile[r * hn:(r + 1) * hn, :] + dtx[p:p + 1, :] * bmt
            yt_sc[pl.ds(h * hp + p, 1), :] = jnp.sum(new * cmt, axis=0, keepdims=True)
            news.append(new)
        stout_ref[:, j * LANES:(j + 1) * LANES] = jnp.concatenate(news, axis=0).T

    @pl.when(h == nheads - 1)
    def _():
        y_ref[...] = _ssd_gate_norm(yt_sc[...].T, xs_sc[...], z_ref[...], dsk_ref, nw_ref).astype(y_ref.dtype)


def ssd_step(z, xnew, hist, dt, conv_w, conv_b, dt_bias, a_log, d_skip, norm_w, state):
    b, d_inner = z.shape
    conv_dim = xnew.shape[1]
    nheads = d_inner // SSM_HEAD_DIM
    per_head = SSM_HEAD_DIM * SSM_STATE
    full = lambda shape: pl.BlockSpec(shape, lambda h: (0, 0))
    y, st = pl.pallas_call(
        _ssd_step_kernel,
        grid=(nheads,),
        in_specs=[full((b, d_inner))] + [full((b, conv_dim))] * 4 + [full((b, LANES))]
        + [full(conv_w.shape), full((1, conv_dim)), full((1, LANES)), full((1, LANES)),
           full((1, d_inner)), full((1, d_inner)),
           pl.BlockSpec((b, per_head), lambda h: (0, h))],
        out_specs=[full((b, d_inner)), pl.BlockSpec((b, per_head), lambda h: (0, h))],
        out_shape=[jax.ShapeDtypeStruct((b, d_inner), BF16),
                   jax.ShapeDtypeStruct((b, nheads * per_head), F32)],
        scratch_shapes=[pltpu.VMEM((b, d_inner), F32), pltpu.VMEM((d_inner, b), F32),
                        pltpu.VMEM((SSM_GROUPS * SSM_STATE, b), F32),
                        pltpu.VMEM((SSM_GROUPS * SSM_STATE, b), F32),
                        pltpu.VMEM((LANES, b), F32), pltpu.VMEM((LANES, b), F32),
                        pltpu.VMEM((d_inner, b), F32)],
        compiler_params=_params(("arbitrary",)),
        name="ssd_step",
    )(z, xnew, hist[0], hist[1], hist[2], dt, conv_w, conv_b, dt_bias, a_log, d_skip, norm_w,
      state.reshape(b, nheads * per_head))
    return y, st.reshape(b, nheads, SSM_HEAD_DIM, SSM_STATE)


def _hgrn_lower_bound(lbp_ref, layer):
    raw = lbp_ref[...]
    e = jnp.exp(raw - jnp.max(raw, axis=0, keepdims=True))
    return jnp.sum(e[:layer + 1], axis=0, keepdims=True) / jnp.sum(e, axis=0, keepdims=True)


def _hgrn_gates(q_raw, f_raw, lb):
    q = _silu(q_raw)
    f = lb + (1.0 - lb) * jax.nn.sigmoid(f_raw)
    k = 1.0 - f
    return q, f, k


def _pair_levels(cl):
    t = np.arange(cl)[:, None]
    s = np.arange(cl)[None, :]
    lvl = np.floor(np.log2(np.maximum(t ^ s, 1))).astype(np.int32)
    return np.where(t > s, lvl, np.where(t == s, -1, -2)).astype(np.int32)


HG_HEADS_PER_DOT = 2
HG_GROUP = 8


def _hgrn_prompt_kernel(layer, x_ref, gpre_ref, win_ref, wout_ref, lvl_ref, lbp_ref, nw_ref,
                        m_ref, sout_ref, st_sc):
    c = pl.program_id(1)
    nchunks = pl.num_programs(1)
    cl = x_ref.shape[0]
    nheads = st_sc.shape[0]
    width = nheads * HG_DIM
    span = HG_HEADS_PER_DOT * HG_DIM

    @pl.when(c == 0)
    def _():
        st_sc[...] = jnp.zeros_like(st_sc)

    lb_all = _hgrn_lower_bound(lbp_ref, layer)
    row = lax.broadcasted_iota(jnp.int32, (cl, LANES), 0)
    lvl = lvl_ref[...]
    xn = _rms(x_ref[...], gpre_ref[...]).astype(BF16)

    def project(j):
        return [_dot(xn, win_ref[:, which * width + j * span:which * width + (j + 1) * span])
                for which in range(4)]

    acc = None
    for g0 in range(0, nheads, HG_GROUP):
        heads = range(g0, g0 + HG_GROUP)
        raw = [{}, {}, {}, {}]
        for j in range(g0 // HG_HEADS_PER_DOT, (g0 + HG_GROUP) // HG_HEADS_PER_DOT):
            for which, both in enumerate(project(j)):
                for hh in range(HG_HEADS_PER_DOT):
                    raw[which][j * HG_HEADS_PER_DOT + hh] = both[:, hh * HG_DIM:(hh + 1) * HG_DIM]
        q, k, cum = {}, {}, {}
        for h in heads:
            q[h], fh, k[h] = _hgrn_gates(raw[0][h], raw[1][h], lb_all[:, h * HG_DIM:(h + 1) * HG_DIM])
            total = jnp.log(fh)
            shift = 1
            while shift < cl:
                total = total + jnp.where(row >= shift, pltpu.roll(total, shift, axis=0), 0.0)
                shift *= 2
            cum[h] = total

        att = {h: jnp.where(lvl == -1, _dot_nt(q[h].astype(BF16), k[h].astype(BF16)), 0.0) for h in heads}
        last_of_block = dict(cum)
        blk = 1
        level = 0
        while blk < cl:
            odd = (row & blk) != 0
            for h in heads:
                expo = jnp.where(odd, cum[h] - pltpu.roll(last_of_block[h], blk, axis=0),
                                 last_of_block[h] - cum[h])
                scaled = (jnp.where(odd, q[h], k[h]) * jnp.exp(expo)).astype(BF16)
                att[h] = jnp.where(lvl == level, _dot_nt(scaled, scaled), att[h])
                last_of_block[h] = jnp.where(odd, last_of_block[h],
                                             pltpu.roll(last_of_block[h], cl - blk, axis=0))
            blk *= 2
            level += 1
        cum_last = last_of_block

        outs = []
        for h in heads:
            st = st_sc[h]
            v = raw[2][h]
            o = (_dot(att[h].astype(BF16), v.astype(BF16))
                 + _dot_nt((q[h] * jnp.exp(cum[h])).astype(BF16), st.astype(BF16)))
            k_end = (k[h] * jnp.exp(cum_last[h] - cum[h])).astype(BF16)
            st_sc[h] = st * jnp.exp(cum_last[h]) + _dot(v.T.astype(BF16), k_end)
            outs.append((_rms(o, nw_ref[...]) * _silu(raw[3][h])).astype(BF16))
        part = _dot(jnp.concatenate(outs, axis=1), wout_ref[g0 * HG_DIM:(g0 + HG_GROUP) * HG_DIM, :])
        acc = part if acc is None else acc + part
    m_ref[...] = acc

    @pl.when(c == nchunks - 1)
    def _():
        for h in range(nheads):
            sout_ref[0, h] = st_sc[h].T


def hgrn_prompt(x, g_pre, w_qfig, w_out, lb_param, norm_w, layer, batch, seq):
    t, d = x.shape
    width = w_qfig.shape[1] // 4
    nheads = width // HG_DIM
    cl = HG_CHUNK
    nchunks = seq // cl
    row = lambda b, c: (b * nchunks + c, 0)
    kern = functools.partial(_hgrn_prompt_kernel, layer)
    return pl.pallas_call(
        kern,
        grid=(batch, nchunks),
        in_specs=[pl.BlockSpec((cl, d), row), _resident((1, d)), _resident(w_qfig.shape),
                  _resident_rows(w_out, w_out.shape[0] - width, width),
                  _resident((cl, cl)), _resident(lb_param.shape), _resident((1, HG_DIM))],
        out_specs=[pl.BlockSpec((cl, d), row),
                   pl.BlockSpec((1, nheads, HG_DIM, HG_DIM), lambda b, c: (b, 0, 0, 0))],
        out_shape=[jax.ShapeDtypeStruct((t, d), F32),
                   jax.ShapeDtypeStruct((batch, nheads, HG_DIM, HG_DIM), F32)],
        scratch_shapes=[pltpu.VMEM((nheads, HG_DIM, HG_DIM), F32)],
        compiler_params=_params(("arbitrary", "arbitrary")),
        name="hgrn_prompt",
    )(x, g_pre, w_qfig, w_out, jnp.asarray(_pair_levels(cl)), lb_param, norm_w)


HG_STEP_TOKENS = 8


def _hgrn_step_kernel(layer, qfig_ref, lbp_ref, nw_ref, st_ref, o_ref, stout_ref):
    nheads = st_ref.shape[2]
    raw = lbp_ref[...]
    e = jnp.exp(raw - jnp.max(raw, axis=0, keepdims=True))
    lb = jnp.sum(e[:layer + 1], axis=0) / jnp.sum(e, axis=0)
    pad = jnp.zeros((HG_DIM - 2 * nheads, HG_DIM), F32)
    for bl in range(st_ref.shape[1]):
        t = qfig_ref[bl]
        q, f, k = _hgrn_gates(t[0:nheads], t[nheads:2 * nheads], lb)
        v = t[2 * nheads:3 * nheads]
        cols = jnp.concatenate([f, k, pad], axis=0).T
        q16 = q.astype(BF16)
        rows = []
        for h in range(nheads):
            f_col = cols[:, h:h + 1]
            k_col = cols[:, nheads + h:nheads + h + 1]
            new = f_col * st_ref[0, bl, h] + k_col * v[h:h + 1, :]
            stout_ref[bl, h] = new
            rows.append(_dot(q16, new.astype(BF16))[h:h + 1, :])
        o = jnp.concatenate(rows, axis=0)
        o_ref[bl] = _rms(o, nw_ref[...]) * _silu(t[3 * nheads:])


def hgrn_step(qfig, lb_param, norm_w, layer, state_all):
    b, w4 = qfig.shape
    width = w4 // 4
    nheads = width // HG_DIM
    bb = HG_STEP_TOKENS
    kern = functools.partial(_hgrn_step_kernel, layer)
    o, st = pl.pallas_call(
        kern,
        grid=(b // bb,),
        in_specs=[pl.BlockSpec((bb, 4 * nheads, HG_DIM), lambda i: (i, 0, 0)),
                  _resident((lb_param.shape[0], nheads, HG_DIM)), _resident((1, HG_DIM)),
                  pl.BlockSpec((1, bb, nheads, HG_DIM, HG_DIM), lambda i: (layer, i, 0, 0, 0))],
        out_specs=[pl.BlockSpec((bb, nheads, HG_DIM), lambda i: (i, 0, 0)),
                   pl.BlockSpec((bb, nheads, HG_DIM, HG_DIM), lambda i: (i, 0, 0, 0))],
        out_shape=[jax.ShapeDtypeStruct((b, nheads, HG_DIM), F32),
                   jax.ShapeDtypeStruct((b, nheads, HG_DIM, HG_DIM), F32)],
        compiler_params=_params(("parallel",)),
        name="hgrn_step",
    )(qfig.reshape(b, 4 * nheads, HG_DIM), lb_param.reshape(-1, nheads, HG_DIM), norm_w, state_all)
    return o.reshape(b, width), st


def _s5_discretize_kernel(lre_ref, lim_ref, step_ref, bre_ref, bim_ref, are_ref, aim_ref, bbre_ref, bbim_ref):
    lre, lim = lre_ref[...], lim_ref[...]
    step = jnp.exp(step_ref[...])
    mag = jnp.exp(lre * step)
    are = mag * jnp.cos(lim * step)
    aim = mag * jnp.sin(lim * step)
    den = lre * lre + lim * lim
    nr = are - 1.0
    cre = (nr * lre + aim * lim) / den
    cim = (aim * lre - nr * lim) / den
    are_ref[...] = are
    aim_ref[...] = aim
    bre, bim = bre_ref[...], bim_ref[...]
    bbre_ref[...] = cre * bre - cim * bim
    bbim_ref[...] = cre * bim + cim * bre


def s5_discretize(lam_re, lam_im, log_step, b_re_t, b_im_t):
    g, n = lam_re.shape
    c = b_re_t.shape[1]
    return pl.pallas_call(
        _s5_discretize_kernel,
        out_shape=[jax.ShapeDtypeStruct((g, 1, n), F32)] * 2 + [jax.ShapeDtypeStruct((g, c, n), F32)] * 2,
        name="s5_discretize",
    )(lam_re.reshape(g, 1, n), lam_im.reshape(g, 1, n), log_step.reshape(g, 1, 1), b_re_t, b_im_t)


def _s5_output(xre, xim, u, cre_ref, cim_ref, d_ref):
    y = _dot(xre.astype(BF16), cre_ref[0]) - _dot(xim.astype(BF16), cim_ref[0]) + d_ref[...] * u
    return jax.nn.gelu(y)


S5_ROWS = 256
S5_CHUNK = 512


def _s5_prompt_kernel(u_ref, b_ref, cre_ref, cim_ref, d_ref, are_ref, aim_ref,
                      y_ref, sre_ref, sim_ref, x_sc, ui_sc, st_sc):
    c = pl.program_id(2)
    nseq, tc = u_ref.shape[0], u_ref.shape[1]
    ns = S5_BLOCK_STATE
    nt = ns // LANES
    steps = S5_ROWS // nseq
    nslabs = tc // steps

    @pl.when(c == 0)
    def _():
        st_sc[...] = jnp.zeros_like(st_sc)

    def slab_rows(s):
        return slice(s * S5_ROWS, (s + 1) * S5_ROWS)

    def project(s):
        r = slab_rows(s)
        ui = jnp.concatenate([u_ref[:, i, :] for i in range(s * steps, (s + 1) * steps)], axis=0)
        ui_sc[r, :] = ui
        bu = _dot(ui.astype(BF16), b_ref[0])
        for k in range(2 * nt):
            x_sc[k, r, :] = bu[:, k * LANES:(k + 1) * LANES]

    def lane_tile(ref, k):
        return ref[:, k * LANES:(k + 1) * LANES]

    are = [jnp.broadcast_to(lane_tile(are_ref, k), (nseq, LANES)) for k in range(nt)]
    aim = [jnp.broadcast_to(lane_tile(aim_ref, k), (nseq, LANES)) for k in range(nt)]

    def scan_slab(s, carry):
        for i in range(s * steps, (s + 1) * steps):
            rows = slice(i * nseq, (i + 1) * nseq)
            new = []
            for k in range(nt):
                sre, sim = carry[k]
                nre = are[k] * sre - aim[k] * sim + x_sc[k, rows, :]
                nim = are[k] * sim + aim[k] * sre + x_sc[nt + k, rows, :]
                x_sc[k, rows, :] = nre
                x_sc[nt + k, rows, :] = nim
                new.append((nre, nim))
            carry = tuple(new)
        return carry

    def output(s):
        r = slab_rows(s)
        xre = jnp.concatenate([x_sc[k, r, :] for k in range(nt)], axis=1)
        xim = jnp.concatenate([x_sc[nt + k, r, :] for k in range(nt)], axis=1)
        y = _s5_output(xre, xim, ui_sc[r, :], cre_ref, cim_ref, d_ref)
        for ii in range(steps):
            y_ref[:, s * steps + ii, :] = y[ii * nseq:(ii + 1) * nseq, :]

    state = tuple((st_sc[k], st_sc[nt + k]) for k in range(nt))
    project(0)
    for s in range(nslabs):
        if s + 1 < nslabs:
            project(s + 1)
        if s > 0:
            output(s - 1)
        state = scan_slab(s, state)
    output(nslabs - 1)
    for k in range(nt):
        st_sc[k], st_sc[nt + k] = state[k]
        sre_ref[0, :, k * LANES:(k + 1) * LANES] = state[k][0]
        sim_ref[0, :, k * LANES:(k + 1) * LANES] = state[k][1]


def s5_prompt(u, b_exp, c_re_exp, c_im_exp, d_skip, a_re, a_im, batch, seq):
    t, d = u.shape
    nblk = d // LANES
    ns = S5_BLOCK_STATE
    tc = S5_CHUNK
    groups = batch // SUBLANES
    rows = pl.BlockSpec((SUBLANES, tc, LANES), lambda j, g, c: (g, c, j))
    state = pl.BlockSpec((1, SUBLANES, ns), lambda j, g, c: (g, 0, j))
    y, s_re, s_im = pl.pallas_call(
        _s5_prompt_kernel,
        grid=(nblk, groups, seq // tc),
        in_specs=[rows,
                  pl.BlockSpec((1, LANES, 2 * ns), lambda j, g, c: (j, 0, 0)),
                  pl.BlockSpec((1, ns, LANES), lambda j, g, c: (j, 0, 0)),
                  pl.BlockSpec((1, ns, LANES), lambda j, g, c: (j, 0, 0)),
                  pl.BlockSpec((1, LANES), lambda j, g, c: (0, j)),
                  pl.BlockSpec((1, ns), lambda j, g, c: (0, j)),
                  pl.BlockSpec((1, ns), lambda j, g, c: (0, j))],
        out_specs=[rows, state, state],
        out_shape=[jax.ShapeDtypeStruct((batch, seq, d), F32),
                   jax.ShapeDtypeStruct((groups, SUBLANES, nblk * ns), F32),
                   jax.ShapeDtypeStruct((groups, SUBLANES, nblk * ns), F32)],
        scratch_shapes=[pltpu.VMEM((2 * ns // LANES, SUBLANES * tc, LANES), F32),
                        pltpu.VMEM((SUBLANES * tc, LANES), F32),
                        pltpu.VMEM((2 * ns // LANES, SUBLANES, LANES), F32)],
        compiler_params=_params(("arbitrary", "arbitrary", "arbitrary")),
        name="s5_prompt",
    )(u.reshape(batch, seq, d), b_exp, c_re_exp, c_im_exp, d_skip, a_re, a_im)
    return y.reshape(t, d), s_re, s_im


def _s5_step_kernel(u_ref, b_ref, cre_ref, cim_ref, d_ref, are_ref, aim_ref, s0re_ref, s0im_ref,
                    y_ref, sre_ref, sim_ref):
    ns = S5_BLOCK_STATE
    u = u_ref[...]
    bu = _dot(u.astype(BF16), b_ref[0])
    are, aim = are_ref[...], aim_ref[...]
    s0re, s0im = s0re_ref[...], s0im_ref[...]
    xre = bu[:, 0:ns] + (are * s0re - aim * s0im)
    xim = bu[:, ns:2 * ns] + (are * s0im + aim * s0re)
    sre_ref[...] = xre
    sim_ref[...] = xim
    y_ref[...] = _s5_output(xre, xim, u, cre_ref, cim_ref, d_ref)


def s5_step(u, b_exp, c_re_exp, c_im_exp, d_skip, a_re, a_im, s0_re, s0_im):
    b, d = u.shape
    nblk = d // LANES
    ns = S5_BLOCK_STATE
    return pl.pallas_call(
        _s5_step_kernel,
        grid=(nblk,),
        in_specs=[pl.BlockSpec((b, LANES), lambda j: (0, j)),
                  pl.BlockSpec((1, LANES, 2 * ns), lambda j: (j, 0, 0)),
                  pl.BlockSpec((1, ns, LANES), lambda j: (j, 0, 0)),
                  pl.BlockSpec((1, ns, LANES), lambda j: (j, 0, 0)),
                  pl.BlockSpec((1, LANES), lambda j: (0, j)),
                  pl.BlockSpec((1, ns), lambda j: (0, j)),
                  pl.BlockSpec((1, ns), lambda j: (0, j)),
                  pl.BlockSpec((b, ns), lambda j: (0, j)),
                  pl.BlockSpec((b, ns), lambda j: (0, j))],
        out_specs=[pl.BlockSpec((b, LANES), lambda j: (0, j)),
                   pl.BlockSpec((b, ns), lambda j: (0, j)),
                   pl.BlockSpec((b, ns), lambda j: (0, j))],
        out_shape=[jax.ShapeDtypeStruct((b, d), F32),
                   jax.ShapeDtypeStruct((b, nblk * ns), F32),
                   jax.ShapeDtypeStruct((b, nblk * ns), F32)],
        compiler_params=_params(("parallel",)),
        name="s5_step",
    )(u, b_exp, c_re_exp, c_im_exp, d_skip, a_re, a_im, s0_re, s0_im)


def _block_diag(per_group):
    ngroups, r, c = per_group.shape
    p = S5_GROUPS_PER_BLOCK
    eye = jnp.eye(p, dtype=per_group.dtype)
    tiles = per_group.reshape(ngroups // p, p, r, c)
    return jnp.einsum('bjrc,jk->bjrkc', tiles, eye).reshape(ngroups // p, p * r, p * c)


def kernel(x_prompt, x_sample, state_ssm, state_ssm_conv, state_hgrn, state_s5_re, state_s5_im, state_ffn_conv, norm_mix_pre, norm_mix_post, norm_ffn_pre, norm_ffn_post, ab_in_w, ssm_conv_w, ssm_conv_b, ssm_dt_bias, ssm_a_log, ssm_d, ssm_norm_w, hgrn_lb, hgrn_norm_w, ab_out_w, s5_lam_re, s5_lam_im, s5_log_step, s5_b_re, s5_b_im, s5_c_re, s5_c_im, s5_d, s5_glu_w, ffn_up_w, ffn_conv_w, ffn_conv_b, ffn_down_w):
    batch, seq, d = x_prompt.shape
    dec_batch = x_sample.shape[0]
    depth = norm_mix_pre.shape[0]
    conv_dim = ssm_conv_w.shape[2]
    nheads = ssm_a_log.shape[1]
    d_inner = nheads * SSM_HEAD_DIM
    hg_width = hgrn_lb.shape[1]
    ffn_dim = ffn_down_w.shape[1]
    tm = 256

    def row(v):
        return v.reshape(1, -1)

    def pad_lanes(v):
        return jnp.pad(v, ((0, 0), (0, LANES - v.shape[1])))

    groups = [(x_prompt.reshape(batch * seq, d), True), (x_sample.reshape(dec_batch, d), False)]
    xs = [g[0] for g in groups]
    normed = [None, None]
    up_w_all = ffn_up_w.astype(BF16)
    down_w_all = ffn_down_w.astype(BF16)
    out = {'ssm': [[], []], 'sconv': [[], []], 'hg': [[], []], 's5r': [[], []], 's5i': [[], []], 'fconv': [[], []]}

    for l in range(depth):
        e = l // 2
        if l % 2 == 0:
            w_in = ab_in_w[e]
            o_xbc = d_inner
            o_dt = o_xbc + conv_dim
            o_q = o_dt + nheads
            w_z = w_in[:, :o_xbc].astype(BF16)
            w_xbc = w_in[:, o_xbc:o_dt].astype(BF16)
            w_dt = pad_lanes(w_in[:, o_dt:o_q]).astype(BF16)
            w_qfig = w_in[:, o_q:].astype(BF16)
            w_out = ab_out_w[e].astype(BF16)
            dt_bias = pad_lanes(row(ssm_dt_bias[e]))
            a_log = pad_lanes(row(ssm_a_log[e]))
            d_skip = row(jnp.repeat(ssm_d[e], SSM_HEAD_DIM))
            for gi, (_, is_prompt) in enumerate(groups):
                x = xs[gi]
                t = x.shape[0]
                if is_prompt:
                    m_hg, hg_new = hgrn_prompt(x, row(norm_mix_pre[l]), w_qfig, w_out, hgrn_lb,
                                               row(hgrn_norm_w[e]), e, batch, seq)
                    xs[gi], ssm_new, xbc_tail = ssd_prompt(
                        x, row(norm_mix_pre[l]), w_z, w_xbc, w_dt, m_hg, w_out, row(norm_mix_post[l]),
                        ssm_conv_w[e], row(ssm_conv_b[e]), dt_bias, a_log, d_skip, row(ssm_norm_w[e]), batch, seq)
                    sconv_new = xbc_tail[:, SUBLANES - (SSM_CONV - 1):]
                else:
                    z, xbc, dt, qfig = norm_matmul(x, row(norm_mix_pre[l]), [w_z, w_xbc, w_dt, w_qfig], t)
                    hist = [state_ssm_conv[e][:, k] for k in range(SSM_CONV - 1)]
                    y, ssm_new = ssd_step(z, xbc, hist, dt, ssm_conv_w[e], row(ssm_conv_b[e]), dt_bias, a_log,
                                          d_skip, row(ssm_norm_w[e]), state_ssm[e])
                    o, hg_new = hgrn_step(qfig, hgrn_lb, row(hgrn_norm_w[e]), e, state_hgrn)
                    sconv_new = jnp.stack(hist[1:] + [xbc], axis=1)
                    xs[gi] = matmul_norm_residual([y, o], w_out, row(norm_mix_post[l]), x, t)
                out['ssm'][gi].append(ssm_new)
                out['hg'][gi].append(hg_new)
                out['sconv'][gi].append(sconv_new)
        else:
            ngroups = s5_lam_re.shape[1]
            a_re, a_im, bb_re, bb_im = s5_discretize(
                s5_lam_re[e], s5_lam_im[e], s5_log_step[e],
                jnp.swapaxes(s5_b_re[e], 1, 2), jnp.swapaxes(s5_b_im[e], 1, 2))
            a_re = a_re.reshape(1, ngroups * S5_STATE)
            a_im = a_im.reshape(1, ngroups * S5_STATE)
            b_exp = jnp.concatenate([_block_diag(bb_re), _block_diag(bb_im)], axis=2).astype(BF16)
            c_re_exp = _block_diag(jnp.swapaxes(s5_c_re[e], 1, 2)).astype(BF16)
            c_im_exp = _block_diag(jnp.swapaxes(s5_c_im[e], 1, 2)).astype(BF16)
            glu_w = s5_glu_w[e].astype(BF16)
            for gi, (_, is_prompt) in enumerate(groups):
                x = xs[gi]
                t = x.shape[0]
                u = normed[gi] if normed[gi] is not None else rmsnorm(x, row(norm_mix_pre[l]), min(tm, t))
                if is_prompt:
                    yg, s_re, s_im = s5_prompt(u, b_exp, c_re_exp, c_im_exp, row(s5_d[e]), a_re, a_im, batch, seq)
                else:
                    yg, s_re, s_im = s5_step(u, b_exp, c_re_exp, c_im_exp, row(s5_d[e]), a_re, a_im,
                                             state_s5_re[e].reshape(t, -1), state_s5_im[e].reshape(t, -1))
                out['s5r'][gi].append(s_re.reshape(-1, ngroups, S5_STATE))
                out['s5i'][gi].append(s_im.reshape(-1, ngroups, S5_STATE))
                xs[gi] = matmul_norm_residual([yg], glu_w, row(norm_mix_post[l]), x, min(GLU_TILE, t), glu=True)

        for gi, (_, is_prompt) in enumerate(groups):
            x = xs[gi]
            t = x.shape[0]
            normed[gi] = None
            if is_prompt:
                g_next = row(norm_mix_pre[l + 1]) if l + 1 < depth and (l + 1) % 2 == 1 else None
                res = ffn_seq(x, row(norm_ffn_pre[l]), up_w_all, ffn_conv_w[l], row(ffn_conv_b[l]), down_w_all,
                              row(norm_ffn_post[l]), g_next, FFN_TILE, seq, l)
                xs[gi], tail = res[0], res[1]
                if g_next is not None:
                    normed[gi] = res[2]
                fconv_new = tail[:, SUBLANES - (FFN_CONV - 1):]
            else:
                (up,) = norm_matmul(x, row(norm_ffn_pre[l]), [up_w_all], t, layer=l)
                old = state_ffn_conv[l]
                xs[gi] = ffn_tail_step(up, old[:, 0], old[:, 1], ffn_conv_w[l], row(ffn_conv_b[l]), down_w_all,
                                       row(norm_ffn_post[l]), x, l)
                fconv_new = jnp.stack([old[:, 1], up], axis=1)
            out['fconv'][gi].append(fconv_new)

    y_prompt = xs[0].reshape(batch, seq, d)
    y_sample = xs[1].reshape(dec_batch, 1, d)
    states = []
    for gi in range(2):
        states += [jnp.stack(out[k][gi]) for k in ('ssm', 'sconv', 'hg', 's5r', 's5i', 'fconv')]
    return (y_prompt, y_sample, *states)
```
